```python
import math
import jax, jax.numpy as jnp
from jax import lax
import numpy as np

D_MODEL = 2048
BATCH = 1
SEQ = 8192
DEPTH = 4

BRANCH_WIDTH = D_MODEL // 2
N_BRANCHES = 3
MLSTM_HEADS = 4
MLSTM_HEAD_DIM = BRANCH_WIDTH // MLSTM_HEADS
MLSTM_CHUNK = 128
N_MLSTM_GATES = 2 * 2 * MLSTM_HEADS
HYENA_WIDTH = BRANCH_WIDTH
HYENA_BANDS = 16
HYENA_EMB = 2 * HYENA_BANDS + 1
HYENA_FILTER_HIDDEN = 64
S5_WIDTH = BRANCH_WIDTH
S5_GROUP = 16
S5_GROUPS = S5_WIDTH // S5_GROUP
S5_STATE = 64
PEER_HEADS = 8
PEER_KEYS = 128
PEER_TOPK = 16
PEER_QDIM = 256
PEER_HALF = PEER_QDIM // 2
N_EXPERTS = PEER_KEYS * PEER_KEYS
PEER_BLOCK = 128
IN_COLS = 4 * BRANCH_WIDTH + N_MLSTM_GATES + 3 * HYENA_WIDTH + S5_WIDTH + N_BRANCHES * D_MODEL
ALPHA = (2 * DEPTH) ** 0.25
BETA = (8 * DEPTH) ** -0.25
LN_EPS = 1e-5
F32 = jnp.float32

kernel_name = 'hybrid_mlstm_hyena_s5_peer_encoder'


def _layer_norm(x, g, b):
    xf = x.astype(F32)
    mu = xf.mean(-1, keepdims=True)
    var = jnp.square(xf - mu).mean(-1, keepdims=True)
    y = (xf - mu) * lax.rsqrt(var + LN_EPS)
    return (y * g.astype(F32) + b.astype(F32)).astype(x.dtype)


def _split_in(proj):
    sizes = (BRANCH_WIDTH,) * 4 + (N_MLSTM_GATES, 3 * HYENA_WIDTH, S5_WIDTH, N_BRANCHES * D_MODEL)
    idx = np.cumsum(sizes)[:-1].tolist()
    return jnp.split(proj, idx, axis=-1)


def _mlstm_chunk_step(carry, inp):
    c_state, n_state, m_state = carry
    q, k, v, log_i, log_f = inp
    L = q.shape[2]
    b = jnp.cumsum(log_f, axis=-1)
    tri = jnp.tril(jnp.ones((L, L), dtype=bool))
    log_d = jnp.where(tri, b[..., :, None] - b[..., None, :] + log_i[..., None, :], -jnp.inf)
    m_inter = b + m_state[..., None]
    m_t = jnp.maximum(m_inter, log_d.max(-1))
    inter = jnp.exp(m_inter - m_t)
    s = jnp.einsum('bhtd,bhsd->bhts', q, k) * jnp.exp(log_d - m_t[..., None])
    num = jnp.einsum('bhts,bhse->bhte', s, v) + inter[..., None] * jnp.einsum('bhtd,bhde->bhte', q, c_state)
    den = s.sum(-1) + inter * jnp.einsum('bhtd,bhd->bht', q, n_state)
    h = num / jnp.maximum(jnp.abs(den), jnp.exp(-m_t))[..., None]
    b_last = b[..., -1]
    log_w = b_last[..., None] - b + log_i
    m_new = jnp.maximum(b_last + m_state, log_w.max(-1))
    w = jnp.exp(log_w - m_new[..., None])
    decay = jnp.exp(b_last + m_state - m_new)
    c_new = decay[..., None, None] * c_state + jnp.einsum('bhs,bhsd,bhse->bhde', w, k, v)
    n_new = decay[..., None] * n_state + jnp.einsum('bhs,bhsd->bhd', w, k)
    return (c_new, n_new, m_new), h


def _mlstm_scan(q, k, v, log_i, log_f):
    bsz, nh, s, d = q.shape
    nc = s // MLSTM_CHUNK

    def chunks(t):
        t = t.reshape(bsz, nh, nc, MLSTM_CHUNK, *t.shape[3:])
        return jnp.moveaxis(t, 2, 0)

    init = (jnp.zeros((bsz, nh, d, d), F32), jnp.zeros((bsz, nh, d), F32), jnp.zeros((bsz, nh), F32))
    _, h = lax.scan(_mlstm_chunk_step, init, (chunks(q), chunks(k), chunks(v), chunks(log_i), chunks(log_f)))
    return jnp.moveaxis(h, 0, 2).reshape(bsz, nh, s, d)


def _mlstm_branch(q, k, v, o, gates, gate_bias, norm_gain):
    bsz, s, _ = q.shape

    def heads(t):
        return t.astype(F32).reshape(bsz, s, MLSTM_HEADS, MLSTM_HEAD_DIM).transpose(0, 2, 1, 3)

    qh, kh, vh = heads(q), heads(k) * (MLSTM_HEAD_DIM ** -0.5), heads(v)
    g = gates.astype(F32).reshape(bsz, s, 2, 2, MLSTM_HEADS) + gate_bias.astype(F32)
    g = jnp.moveaxis(g, 1, -1)
    log_i = g[:, :, 0]
    log_f = jax.nn.log_sigmoid(g[:, :, 1])
    h_fwd = _mlstm_scan(qh, kh, vh, log_i[:, 0], log_f[:, 0])

    def flip(t):
        return jnp.flip(t, axis=2)

    h_bwd = flip(_mlstm_scan(flip(qh), flip(kh), flip(vh), flip(log_i[:, 1]), flip(log_f[:, 1])))
    h = (h_fwd + h_bwd).transpose(0, 2, 1, 3)
    h = jax.nn.sigmoid(o.astype(F32)).reshape(bsz, s, MLSTM_HEADS, MLSTM_HEAD_DIM) * h
    mu = h.mean(-1, keepdims=True)
    var = jnp.square(h - mu).mean(-1, keepdims=True)
    h = ((h - mu) * lax.rsqrt(var + LN_EPS)).reshape(bsz, s, BRANCH_WIDTH) * norm_gain.astype(F32)
    return h.astype(q.dtype)


def _centred_short_conv(x, w, b):
    s = x.shape[1]
    xp = jnp.pad(x, ((0, 0), (1, 1), (0, 0)))
    return xp[:, :s] * w[0] + xp[:, 1:s + 1] * w[1] + xp[:, 2:] * w[2] + b


def _hyena_filters(seq_len, w1, b1, w2, b2, freq, w3, decay):
    pos = jnp.arange(seq_len, dtype=F32)
    t = pos / (seq_len - 1)
    bands = jnp.linspace(1e-4, HYENA_BANDS - 1, HYENA_BANDS, dtype=F32)
    ang = 2.0 * math.pi * pos[:, None] * bands[None, :] / seq_len
    feat = jnp.concatenate([t[:, None], jnp.cos(ang), -jnp.sin(ang)], axis=-1)
    freq = freq.astype(F32)
    h = jnp.sin(freq[0] * (feat @ w1.astype(F32) + b1.astype(F32)))
    h = jnp.sin(freq[1] * (h @ w2.astype(F32) + b2.astype(F32)))
    filt = (h @ w3.astype(F32)).reshape(seq_len, 2, HYENA_WIDTH)
    filt = filt * jnp.exp(-t[:, None, None] * jnp.abs(decay.astype(F32))[None])
    return filt / (jnp.sum(jnp.abs(filt), axis=0, keepdims=True) + 1e-6)


def _hyena_branch(p, conv_w, conv_b, w1, b1, w2, b2, freq, w3, decay, skip):
    seq_len = p.shape[1]
    u = _centred_short_conv(p, conv_w, conv_b)
    x0, x1, v = jnp.split(u, 3, axis=-1)
    z = (x1 * v).astype(F32)
    filt = _hyena_filters(seq_len, w1, b1, w2, b2, freq, w3, decay)
    two_sided = jnp.concatenate([filt[:, 0], jnp.zeros_like(filt[:1, 0]), jnp.flip(filt[1:, 1], axis=0)], axis=0)
    z_f = jnp.fft.rfft(z, n=2 * seq_len, axis=1)
    h_f = jnp.fft.rfft(two_sided, axis=0)
    y = jnp.fft.irfft(z_f * h_f[None], n=2 * seq_len, axis=1)[:, :seq_len]
    y = y + skip.astype(F32) * z
    return x0 * y.astype(x0.dtype)


def _ssm_combine(e1, e2):
    a1, b1 = e1
    a2, b2 = e2
    return a2 * a1, a2 * b1 + b2


def _s5_branch(u, lam_re, lam_im, log_step, b_re, b_im, c_re, c_im, skip):
    bsz, s, _ = u.shape
    ug = u.astype(F32).reshape(bsz, s, S5_GROUPS, S5_GROUP)
    lam = lax.complex(lam_re.astype(F32), lam_im.astype(F32))
    step = jnp.exp(log_step.astype(F32))[..., None]
    a_bar = jnp.exp(lam * step)
    b_mat = lax.complex(b_re.astype(F32), b_im.astype(F32))
    b_bar = ((a_bar - 1.0) / lam)[..., None] * b_mat
    c_mat = lax.complex(c_re.astype(F32), c_im.astype(F32))

    def run(d, reverse):
        bu = jnp.einsum('gpn,bsgn->bsgp', b_bar[d], ug)
        a = jnp.broadcast_to(a_bar[d], bu.shape)
        _, states = lax.associative_scan(_ssm_combine, (a, bu), reverse=reverse, axis=1)
        return jnp.einsum('gnp,bsgp->bsgn', c_mat[d], states).real

    y = run(0, False) + run(1, True) + skip.astype(F32).reshape(S5_GROUPS, S5_GROUP) * ug
    return y.reshape(bsz, s, S5_WIDTH).astype(u.dtype)


def _mixer(x, w_in, mlstm_gate_bias, mlstm_norm_gain, w_mlstm_out,
           hyena_conv_w, hyena_conv_b, hyena_w1, hyena_b1, hyena_w2, hyena_b2, hyena_freq, hyena_w3,
           hyena_decay, hyena_skip, w_hyena_out,
           s5_lambda_re, s5_lambda_im, s5_log_step, s5_b_re, s5_b_im, s5_c_re, s5_c_im, s5_skip, w_s5_glu,
           w_out):
    bsz, s, d = x.shape
    q, k, v, o, mg, hp, su, gate_pre = _split_in(x @ w_in)
    out_a = _mlstm_branch(q, k, v, o, mg, mlstm_gate_bias, mlstm_norm_gain) @ w_mlstm_out
    out_b = _hyena_branch(hp, hyena_conv_w, hyena_conv_b, hyena_w1, hyena_b1, hyena_w2, hyena_b2,
                          hyena_freq, hyena_w3, hyena_decay, hyena_skip) @ w_hyena_out
    s5y = _s5_branch(su, s5_lambda_re, s5_lambda_im, s5_log_step, s5_b_re, s5_b_im, s5_c_re, s5_c_im, s5_skip)
    glu_a, glu_g = jnp.split(s5y @ w_s5_glu, 2, axis=-1)
    out_c = glu_a * jax.nn.sigmoid(glu_g)
    g = jax.nn.sigmoid(gate_pre).reshape(bsz, s, N_BRANCHES, d)
    merged = g[:, :, 0] * out_a + g[:, :, 1] * out_b + g[:, :, 2] * out_c
    return merged @ w_out


def _peer(x, w_q, subkeys, u_tab, v_tab):
    bsz, s, d = x.shape
    t = x.reshape(bsz * s, d)
    n_tok = t.shape[0]
    q = (t @ w_q).astype(F32).reshape(n_tok, PEER_HEADS, 2, PEER_HALF)
    scores = jnp.einsum('thcd,hckd->thck', q, subkeys.astype(F32))
    top_v, top_i = lax.top_k(scores, PEER_TOPK)
    cand = top_v[:, :, 0, :, None] + top_v[:, :, 1, None, :]
    best_v, best_c = lax.top_k(cand.reshape(n_tok, PEER_HEADS, PEER_TOPK * PEER_TOPK), PEER_TOPK)
    i1 = jnp.take_along_axis(top_i[:, :, 0], best_c // PEER_TOPK, axis=-1)
    i2 = jnp.take_along_axis(top_i[:, :, 1], best_c % PEER_TOPK, axis=-1)
    expert = (i1 * PEER_KEYS + i2).reshape(n_tok, PEER_HEADS * PEER_TOPK)
    gate = jax.nn.softmax(best_v, axis=-1).reshape(n_tok, PEER_HEADS * PEER_TOPK)

    def block(args):
        xb, eb, gb = args
        act = jax.nn.gelu(jnp.einsum('td,ted->te', xb, u_tab[eb]).astype(F32))
        return jnp.einsum('te,ted->td', (act * gb).astype(xb.dtype), v_tab[eb])

    nb = n_tok // PEER_BLOCK
    out = lax.map(block, (t.reshape(nb, PEER_BLOCK, d),
                          expert.reshape(nb, PEER_BLOCK, -1),
                          gate.reshape(nb, PEER_BLOCK, -1)))
    return out.reshape(bsz, s, d)


def setup_inputs(seed: int = 0) -> dict:
    key = jax.random.key(seed)
    ks = iter(jax.random.split(key, 48))

    def nrm(shape, scale):
        return jax.random.normal(next(ks), shape, F32) * scale

    L = DEPTH
    ig = nrm((L, 2, MLSTM_HEADS), 0.1)
    fg = jnp.linspace(3.0, 6.0, MLSTM_HEADS, dtype=F32) + nrm((L, 2, MLSTM_HEADS), 0.1)
    min_decay = math.log(1e-2) / 1.5
    max_decay = math.log(1e-2) / 0.3
    deltas = jnp.linspace(min_decay, max_decay, HYENA_WIDTH, dtype=F32)
    n_idx = jnp.arange(S5_STATE, dtype=F32)
    return {
        'x': nrm((BATCH, SEQ, D_MODEL), 1.0),
        'w_in': nrm((L, D_MODEL, IN_COLS), D_MODEL ** -0.5),
        'mlstm_gate_bias': jnp.stack([ig, fg], axis=2),
        'mlstm_norm_gain': 1.0 + nrm((L, BRANCH_WIDTH), 0.02),
        'w_mlstm_out': nrm((L, BRANCH_WIDTH, D_MODEL), BRANCH_WIDTH ** -0.5),
        'hyena_conv_w': nrm((L, 3, 3 * HYENA_WIDTH), 3 ** -0.5),
        'hyena_conv_b': nrm((L, 3 * HYENA_WIDTH), 0.02),
        'hyena_w1': nrm((L, HYENA_EMB, HYENA_FILTER_HIDDEN), HYENA_EMB ** -0.5),
        'hyena_b1': nrm((L, HYENA_FILTER_HIDDEN), 0.02),
        'hyena_w2': nrm((L, HYENA_FILTER_HIDDEN, HYENA_FILTER_HIDDEN), HYENA_FILTER_HIDDEN ** -0.5),
        'hyena_b2': nrm((L, HYENA_FILTER_HIDDEN), 0.02),
        'hyena_freq': 1.0 + nrm((L, 2, HYENA_FILTER_HIDDEN), 0.02),
        'hyena_w3': nrm((L, HYENA_FILTER_HIDDEN, 2 * HYENA_WIDTH), HYENA_FILTER_HIDDEN ** -0.5),
        'hyena_decay': deltas[None, None] * (1.0 + nrm((L, 2, HYENA_WIDTH), 0.02)),
        'hyena_skip': nrm((L, HYENA_WIDTH), 1.0),
        'w_hyena_out': nrm((L, HYENA_WIDTH, D_MODEL), HYENA_WIDTH ** -0.5),
        's5_lambda_re': -0.5 + nrm((L, 2, S5_GROUPS, S5_STATE), 0.01),
        's5_lambda_im': math.pi * n_idx + nrm((L, 2, S5_GROUPS, S5_STATE), 0.01),
        's5_log_step': jax.random.uniform(next(ks), (L, 2, S5_GROUPS), F32, math.log(1e-3), math.log(1e-1)),
        's5_b_re': nrm((L, 2, S5_GROUPS, S5_STATE, S5_GROUP), (2 * S5_GROUP) ** -0.5),
        's5_b_im': nrm((L, 2, S5_GROUPS, S5_STATE, S5_GROUP), (2 * S5_GROUP) ** -0.5),
        's5_c_re': nrm((L, 2, S5_GROUPS, S5_GROUP, S5_STATE), (2 * S5_STATE) ** -0.5),
        's5_c_im': nrm((L, 2, S5_GROUPS, S5_GROUP, S5_STATE), (2 * S5_STATE) ** -0.5),
        's5_skip': nrm((L, S5_WIDTH), 1.0),
        'w_s5_glu': nrm((L, S5_WIDTH, 2 * D_MODEL), S5_WIDTH ** -0.5),
        'w_out': nrm((L, D_MODEL, D_MODEL), D_MODEL ** -0.5 * BETA),
        'ln1_g': 1.0 + nrm((L, D_MODEL), 0.02),
        'ln1_b': nrm((L, D_MODEL), 0.02),
        'peer_w_q': nrm((L, D_MODEL, PEER_HEADS * PEER_QDIM), D_MODEL ** -0.5),
        'peer_subkeys': nrm((L, PEER_HEADS, 2, PEER_KEYS, PEER_HALF), PEER_HALF ** -0.5),
        'peer_u': nrm((L, N_EXPERTS, D_MODEL), D_MODEL ** -0.5),
        'peer_v': nrm((L, N_EXPERTS, D_MODEL), (PEER_HEADS * PEER_TOPK) ** -0.5 * BETA),
        'ln2_g': 1.0 + nrm((L, D_MODEL), 0.02),
        'ln2_b': nrm((L, D_MODEL), 0.02),
    }


def reference(x, w_in, mlstm_gate_bias, mlstm_norm_gain, w_mlstm_out,
              hyena_conv_w, hyena_conv_b, hyena_w1, hyena_b1, hyena_w2, hyena_b2, hyena_freq, hyena_w3,
              hyena_decay, hyena_skip, w_hyena_out,
              s5_lambda_re, s5_lambda_im, s5_log_step, s5_b_re, s5_b_im, s5_c_re, s5_c_im, s5_skip, w_s5_glu,
              w_out, ln1_g, ln1_b, peer_w_q, peer_subkeys, peer_u, peer_v, ln2_g, ln2_b):
    for l in range(DEPTH):
        mix = _mixer(x, w_in[l], mlstm_gate_bias[l], mlstm_norm_gain[l], w_mlstm_out[l],
                     hyena_conv_w[l], hyena_conv_b[l], hyena_w1[l], hyena_b1[l], hyena_w2[l], hyena_b2[l],
                     hyena_freq[l], hyena_w3[l], hyena_decay[l], hyena_skip[l], w_hyena_out[l],
                     s5_lambda_re[l], s5_lambda_im[l], s5_log_step[l], s5_b_re[l], s5_b_im[l],
                     s5_c_re[l], s5_c_im[l], s5_skip[l], w_s5_glu[l], w_out[l])
        x = _layer_norm(ALPHA * x + mix, ln1_g[l], ln1_b[l])
        ffn = _peer(x, peer_w_q[l], peer_subkeys[l], peer_u[l], peer_v[l])
        x = _layer_norm(ALPHA * x + ffn, ln2_g[l], ln2_b[l])
    return x
```

```python
import functools
import math

import numpy as np
import jax
import jax.numpy as jnp
from jax import lax
from jax.experimental import pallas as pl
from jax.experimental.pallas import tpu as pltpu

F32 = jnp.float32
BF16 = jnp.bfloat16

D_MODEL = 2048
SEQ = 8192
DEPTH = 4
BRANCH = 1024
HEADS = 4
HEAD_DIM = 256
CHUNK = 128
N_GATES = 16
HY_HID = 64
HY_EMB_PAD = 128
S5_GROUP = 16
S5_GROUPS = 64
S5_STATE = 64
S5_T = 16
S5_CHUNKS = SEQ // S5_T
S5_LOG_CHUNKS = 9
PEER_HEADS = 8
PEER_KEYS = 128
PEER_TOPK = 16
N_EXPERTS = PEER_KEYS * PEER_KEYS
ALPHA = (2 * DEPTH) ** 0.25
LN_EPS = 1e-5
FFT_N = 2 * SEQ
FFT_R = 128
OFF_GATES = 4 * BRANCH
OFF_HYENA = OFF_GATES + N_GATES
OFF_S5 = OFF_HYENA + 3 * BRANCH
OFF_MIXG = OFF_S5 + BRANCH

VMEM_LIMIT = 56 * 1024 * 1024


def _cparams(sem):
    return pltpu.CompilerParams(dimension_semantics=sem, vmem_limit_bytes=VMEM_LIMIT)


def _split_bf16(a):
    hi = a.astype(BF16)
    lo = (a - hi.astype(F32)).astype(BF16)
    return hi, lo


def _dot3(a, b_hi, b_lo):
    a_hi, a_lo = _split_bf16(a)
    acc = jnp.dot(a_hi, b_hi, preferred_element_type=F32)
    acc += jnp.dot(a_lo, b_hi, preferred_element_type=F32)
    acc += jnp.dot(a_hi, b_lo, preferred_element_type=F32)
    return acc


def _dot3_both(a, b):
    b_hi, b_lo = _split_bf16(b)
    return _dot3(a, b_hi, b_lo)


def _mm_kernel(a_ref, b_ref, o_ref):
    o_ref[...] = jnp.dot(a_ref[...].astype(BF16), b_ref[...].astype(BF16),
                         preferred_element_type=F32).astype(o_ref.dtype)


def _mm(a, w, l, col_off, n_cols, *, tm, tn, name):
    m, k = a.shape
    cb = col_off // tn
    assert col_off % tn == 0 and n_cols % tn == 0 and m % tm == 0
    return pl.pallas_call(
        _mm_kernel,
        grid=(m // tm, n_cols // tn),
        in_specs=[pl.BlockSpec((tm, k), lambda i, j: (i, 0)),
                  pl.BlockSpec((None, k, tn), lambda i, j: (l, 0, cb + j))],
        out_specs=pl.BlockSpec((tm, tn), lambda i, j: (i, j)),
        out_shape=jax.ShapeDtypeStruct((m, n_cols), F32),
        compiler_params=_cparams(("parallel", "parallel")),
        name=name,
    )(a, w)


def _mm_nt_kernel(w_ref, a_ref, o_ref):
    o_ref[...] = lax.dot_general(w_ref[...].astype(BF16), a_ref[...].astype(BF16),
                                 (((1,), (1,)), ((), ())), preferred_element_type=F32)


def _mm_nt(wt, l, a, *, tr, tm, name):
    _, r, k = wt.shape
    m = a.shape[0]
    return pl.pallas_call(
        _mm_nt_kernel,
        grid=(r // tr, m // tm),
        in_specs=[pl.BlockSpec((None, tr, k), lambda i, j: (l, i, 0)),
                  pl.BlockSpec((tm, k), lambda i, j: (j, 0))],
        out_specs=pl.BlockSpec((tr, tm), lambda i, j: (i, j)),
        out_shape=jax.ShapeDtypeStruct((r, m), F32),
        compiler_params=_cparams(("parallel", "parallel")),
        name=name,
    )(wt, a)


def _log_sigmoid(x):
    return jnp.minimum(x, 0.0) - jnp.log1p(jnp.exp(-jnp.abs(x)))


def _mlstm_kernel(q_ref, k_ref, v_ref, li_r_ref, lf_r_ref, li_c_ref, lf_c_ref, bi_ref, bf_ref,
                  o_ref, c_sc, n_sc, m_sc):
    d = pl.program_id(0)
    c = pl.program_id(2)

    @pl.when(c == 0)
    def _():
        c_sc[...] = jnp.zeros_like(c_sc)
        n_sc[...] = jnp.zeros_like(n_sc)
        m_sc[...] = jnp.zeros_like(m_sc)

    n = CHUNK
    b_i = bi_ref[...]
    b_f = bf_ref[...]
    li_r = li_r_ref[...] + b_i
    li_c = li_c_ref[...] + b_i
    lf_r = _log_sigmoid(lf_r_ref[...] + b_f)
    lf_c = _log_sigmoid(lf_c_ref[...] + b_f)

    row = lax.broadcasted_iota(jnp.int32, (n, n), 0)
    col = lax.broadcasted_iota(jnp.int32, (n, n), 1)
    sgn = 1 - 2 * d
    valid = ((row - col) * sgn) >= 0
    valid_t = ((col - row) * sgn) >= 0
    b_col = jnp.sum(jnp.where(valid, lf_r, 0.0), axis=1, keepdims=True)
    b_row = jnp.sum(jnp.where(valid_t, lf_c, 0.0), axis=0, keepdims=True)
    total = jnp.sum(lf_r, axis=1, keepdims=True)

    m_prev = m_sc[...]
    m_inter = b_col + m_prev
    log_d = jnp.where(valid, b_col - b_row + li_r, -jnp.inf)
    m_t = jnp.maximum(m_inter, jnp.max(log_d, axis=1, keepdims=True))
    inter = jnp.exp(m_inter - m_t)
    dmat = jnp.exp(log_d - m_t)

    scale = HEAD_DIM ** -0.5
    q = q_ref[...]
    k = k_ref[...] * scale
    qb = q.astype(BF16)
    kb = k.astype(BF16)
    vb = v_ref[...].astype(BF16)
    s = lax.dot_general(qb, kb, (((1,), (1,)), ((), ())), preferred_element_type=F32) * dmat
    num = jnp.dot(s.astype(BF16), vb, preferred_element_type=F32)
    num += inter * jnp.dot(qb, c_sc[...].astype(BF16), preferred_element_type=F32)
    qn = jnp.sum(q * n_sc[...], axis=1, keepdims=True)
    den = jnp.sum(s, axis=1, keepdims=True) + inter * qn
    o_ref[...] = num / jnp.maximum(jnp.abs(den), jnp.exp(-m_t))

    log_w = total - b_col + li_c
    m_new = jnp.maximum(total + m_prev, jnp.max(log_w, axis=0, keepdims=True))
    w = jnp.exp(log_w - m_new)
    decay = jnp.exp(total + m_prev - m_new)
    kw = k * w
    c_sc[...] = decay * c_sc[...] + lax.dot_general(
        kw.astype(BF16), vb, (((0,), (0,)), ((), ())), preferred_element_type=F32)
    n_sc[...] = decay * n_sc[...] + jnp.sum(kw, axis=0, keepdims=True)
    m_sc[...] = m_new


def _mlstm(qkvo, gates, gate_bias, l):
    nc = SEQ // CHUNK
    g_rows = gates.T.reshape(N_GATES, 1, SEQ)
    g_cols = gates.T.reshape(N_GATES, SEQ, 1)
    bias = gate_bias.reshape(DEPTH, N_GATES, 1, 1)

    def chunk(d, c):
        return c + d * (nc - 1 - 2 * c)

    def gi(d, h):
        return d * 8 + h

    def gf(d, h):
        return d * 8 + 4 + h

    return pl.pallas_call(
        _mlstm_kernel,
        grid=(2, HEADS, nc),
        in_specs=[
            pl.BlockSpec((CHUNK, HEAD_DIM), lambda d, h, c: (chunk(d, c), h)),
            pl.BlockSpec((CHUNK, HEAD_DIM), lambda d, h, c: (chunk(d, c), HEADS + h)),
            pl.BlockSpec((CHUNK, HEAD_DIM), lambda d, h, c: (chunk(d, c), 2 * HEADS + h)),
            pl.BlockSpec((None, 1, CHUNK), lambda d, h, c: (gi(d, h), 0, chunk(d, c))),
            pl.BlockSpec((None, 1, CHUNK), lambda d, h, c: (gf(d, h), 0, chunk(d, c))),
            pl.BlockSpec((None, CHUNK, 1), lambda d, h, c: (gi(d, h), chunk(d, c), 0)),
            pl.BlockSpec((None, CHUNK, 1), lambda d, h, c: (gf(d, h), chunk(d, c), 0)),
            pl.BlockSpec((None, None, 1, 1), lambda d, h, c: (l, gi(d, h), 0, 0)),
            pl.BlockSpec((None, None, 1, 1), lambda d, h, c: (l, gf(d, h), 0, 0)),
        ],
        out_specs=pl.BlockSpec((None, CHUNK, HEAD_DIM), lambda d, h, c: (d, chunk(d, c), h)),
        out_shape=jax.ShapeDtypeStruct((2, SEQ, BRANCH), F32),
        scratch_shapes=[pltpu.VMEM((HEAD_DIM, HEAD_DIM), F32),
                        pltpu.VMEM((1, HEAD_DIM), F32),
                        pltpu.VMEM((1, 1), F32)],
        compiler_params=_cparams(("parallel", "parallel", "arbitrary")),
        name="mlstm_scan",
    )(qkvo, qkvo, qkvo, g_rows, g_rows, g_cols, g_cols, bias, bias)


def _mlstm_post_kernel(h_ref, o_ref, gain_ref, out_ref):
    h = h_ref[0] + h_ref[1]
    h = jax.nn.sigmoid(o_ref[...]) * h
    for hd in range(HEADS):
        sl = slice(hd * HEAD_DIM, (hd + 1) * HEAD_DIM)
        hh = h[:, sl]
        mu = jnp.mean(hh, axis=1, keepdims=True)
        var = jnp.mean(jnp.square(hh - mu), axis=1, keepdims=True)
        out_ref[:, sl] = ((hh - mu) * lax.rsqrt(var + LN_EPS) * gain_ref[:, sl]).astype(out_ref.dtype)


def _mlstm_post(hdir, qkvo, gain, l, *, tm=512):
    return pl.pallas_call(
        _mlstm_post_kernel,
        grid=(SEQ // tm,),
        in_specs=[pl.BlockSpec((2, tm, BRANCH), lambda i: (0, i, 0)),
                  pl.BlockSpec((tm, BRANCH), lambda i: (i, 3)),
                  pl.BlockSpec((None, 1, BRANCH), lambda i: (l, 0, 0))],
        out_specs=pl.BlockSpec((tm, BRANCH), lambda i: (i, 0)),
        out_shape=jax.ShapeDtypeStruct((SEQ, BRANCH), BF16),
        compiler_params=_cparams(("parallel",)),
        name="mlstm_post",
    )(hdir, qkvo, gain.reshape(DEPTH, 1, BRANCH))


@functools.lru_cache(maxsize=None)
def _fft_constants():
    r = FFT_R
    idx = np.arange(r)
    th = 2.0 * np.pi * np.outer(idx, idx) / r
    cs, sn = np.cos(th), np.sin(th)
    ph = 2.0 * np.pi * np.outer(idx, idx) / FFT_N
    tr, ti = np.cos(ph), np.sin(ph)
    f1 = np.concatenate([cs, -sn], axis=1)
    f1[r // 2:] = 0.0
    m3 = np.block([[cs, -sn], [sn, cs]])
    m4 = np.block([[cs, sn], [-sn, cs]])
    m6 = np.concatenate([cs, -sn], axis=0) / FFT_N
    m6[:, r // 2:] = 0.0
    consts = {}
    for name, mat in (("f1", f1), ("m3", m3), ("m4", m4), ("m6", m6)):
        m32 = jnp.asarray(mat, F32)
        hi = m32.astype(BF16)
        lo = (m32 - hi.astype(F32)).astype(BF16)
        consts[name] = (hi, lo)
    consts["tw"] = (jnp.asarray(tr, F32), jnp.asarray(ti, F32))
    return consts


def _spectrum(sig_ref, ct, t_buf, a_buf, b_buf, f1_hi, f1_lo, m3_hi, m3_lo, twr, twi):
    r = FFT_R
    zeros = jnp.zeros((r // 2, r), F32)

    def tr_in(c, carry):
        t_buf[c] = jnp.concatenate([sig_ref[c], zeros], axis=0).T
        return carry

    lax.fori_loop(0, ct, tr_in, 0)

    def stage1(c, carry):
        a_buf[pl.ds(pl.multiple_of(c * r, r), r), :] = _dot3(t_buf[c], f1_hi, f1_lo)
        return carry

    lax.fori_loop(0, ct, stage1, 0)

    def twiddle(c, carry):
        rows = pl.ds(pl.multiple_of(c * r, r), r)
        a = a_buf[rows, :]
        ar, ai = a[:, :r], a[:, r:]
        b_buf[rows, :r] = (ar * twr + ai * twi).T
        b_buf[rows, r:] = (ai * twr - ar * twi).T
        return carry

    lax.fori_loop(0, ct, twiddle, 0)

    def stage2(c, carry):
        rows = pl.ds(pl.multiple_of(c * r, r), r)
        a_buf[rows, :] = _dot3(b_buf[rows, :], m3_hi, m3_lo)
        return carry

    lax.fori_loop(0, ct, stage2, 0)


def _hy_mlp_kernel(feat_ref, w1_ref, b1_ref, w2_ref, b2_ref, fr_ref, out_ref):
    h = _dot3_both(w1_ref[...], feat_ref[...]) + b1_ref[...]
    h = jnp.sin(fr_ref[0] * h)
    h = _dot3_both(w2_ref[...], h) + b2_ref[...]
    out_ref[...] = jnp.sin(fr_ref[1] * h)


def _hy_filter_kernel(w3_ref, h_ref, dec_ref, t_ref, out_ref):
    d = pl.program_id(0)
    filt = _dot3_both(w3_ref[...], h_ref[...])
    filt = filt * jnp.exp(-t_ref[...] * jnp.abs(dec_ref[...]))
    filt = filt / (jnp.sum(jnp.abs(filt), axis=1, keepdims=True) + 1e-6)
    lane = lax.broadcasted_iota(jnp.int32, filt.shape, 1)
    out_ref[...] = jnp.where((lane == 0) & (d == 1), 0.0, filt)


def _hy_spec_kernel(fa_ref, fb_ref, f1h, f1l, m3h, m3l, twr_ref, twi_ref, out_ref,
                    t_buf, a_buf, b_buf, *, ct):
    args = (f1h[...], f1l[...], m3h[...], m3l[...], twr_ref[...], twi_ref[...])
    r = FFT_R
    _spectrum(fa_ref, ct, t_buf, a_buf, b_buf, *args)

    def put(c, carry):
        out_ref[c] = a_buf[pl.ds(pl.multiple_of(c * r, r), r), :]
        return carry

    lax.fori_loop(0, ct, put, 0)
    _spectrum(fb_ref, ct, t_buf, a_buf, b_buf, *args)

    def add_conj(c, carry):
        sb = a_buf[pl.ds(pl.multiple_of(c * r, r), r), :]
        sa = out_ref[c]
        out_ref[c] = jnp.concatenate([sa[:, :r] + sb[:, :r], sa[:, r:] - sb[:, r:]], axis=1)
        return carry

    lax.fori_loop(0, ct, add_conj, 0)


def _shift_prev(p, n1, n2):
    r1 = pltpu.roll(p, 1, 2)
    r2 = pltpu.roll(r1, 1, 1)
    return jnp.where(n2 == 0, jnp.where(n1 == 0, 0.0, r2), r1)


def _shift_next(p, n1, n2):
    s1, s2 = p.shape[1], p.shape[2]
    r1 = pltpu.roll(p, s2 - 1, 2)
    r2 = pltpu.roll(r1, s1 - 1, 1)
    return jnp.where(n2 == s2 - 1, jnp.where(n1 == s1 - 1, 0.0, r2), r1)


def _hy_conv_kernel(p0_ref, p1_ref, p2_ref, cw_ref, cb_ref, skip_ref, h_ref,
                    f1h, f1l, m3h, m3l, m4h, m4l, m6h, m6l, twr_ref, twi_ref, twrt_ref, twit_ref,
                    out_ref, z_buf, t_buf, a_buf, b_buf, *, ct):
    r = FFT_R
    shp = (ct, SEQ // r, r)
    n1 = lax.broadcasted_iota(jnp.int32, shp, 1)
    n2 = lax.broadcasted_iota(jnp.int32, shp, 2)

    def short_conv(p_ref, g):
        p = p_ref[...]
        return (_shift_prev(p, n1, n2) * cw_ref[3 * g] + p * cw_ref[3 * g + 1]
                + _shift_next(p, n1, n2) * cw_ref[3 * g + 2] + cb_ref[g])

    z_buf[...] = short_conv(p1_ref, 1) * short_conv(p2_ref, 2)
    _spectrum(z_buf, ct, t_buf, a_buf, b_buf, f1h[...], f1l[...], m3h[...], m3l[...],
              twr_ref[...], twi_ref[...])

    m4_hi, m4_lo = m4h[...], m4l[...]
    m6_hi, m6_lo = m6h[...], m6l[...]
    twr_t, twi_t = twrt_ref[...], twit_ref[...]

    def inverse(c, carry):
        rows = pl.ds(pl.multiple_of(c * r, r), r)
        x = a_buf[rows, :]
        h = h_ref[c]
        xr, xi = x[:, :r], x[:, r:]
        hr, hi = h[:, :r], h[:, r:]
        y = jnp.concatenate([xr * hr - xi * hi, xr * hi + xi * hr], axis=1)
        b = _dot3(y, m4_hi, m4_lo)
        br, bi = b[:, :r], b[:, r:]
        bt = jnp.concatenate([(br * twr_t - bi * twi_t).T, (br * twi_t + bi * twr_t).T], axis=1)
        yt = _dot3(bt, m6_hi, m6_lo)
        y_c = yt.T[: SEQ // r, :]
        z_c = z_buf[c]
        t_buf[c, : SEQ // r, :] = y_c + skip_ref[c] * z_c
        return carry

    lax.fori_loop(0, ct, inverse, 0)
    out_ref[...] = short_conv(p0_ref, 0) * t_buf[:, : SEQ // r, :]


def _hyena(hp_t, l, conv_w, conv_b, w1, b1, w2, b2, freq, w3, decay, skip, feat_t, t_row, *, ct=32):
    r = FFT_R
    n1 = SEQ // r
    k = _fft_constants()
    (f1h, f1l), (m3h, m3l), (m4h, m4l), (m6h, m6l) = k["f1"], k["m3"], k["m4"], k["m6"]
    twr, twi = k["tw"]

    def full(a):
        return pl.BlockSpec(a.shape, lambda *_: (0,) * a.ndim)

    h2 = pl.pallas_call(
        _hy_mlp_kernel,
        grid=(1,),
        in_specs=[full(feat_t), pl.BlockSpec((None, HY_HID, feat_t.shape[0]), lambda i: (l, 0, 0)),
                  pl.BlockSpec((None, HY_HID, 1), lambda i: (l, 0, 0)),
                  pl.BlockSpec((None, HY_HID, HY_HID), lambda i: (l, 0, 0)),
                  pl.BlockSpec((None, HY_HID, 1), lambda i: (l, 0, 0)),
                  pl.BlockSpec((None, 2, HY_HID, 1), lambda i: (l, 0, 0, 0))],
        out_specs=pl.BlockSpec((HY_HID, SEQ), lambda i: (0, 0)),
        out_shape=jax.ShapeDtypeStruct((HY_HID, SEQ), F32),
        compiler_params=pltpu.CompilerParams(vmem_limit_bytes=VMEM_LIMIT),
        name="hyena_filter_mlp",
    )(feat_t, jnp.pad(jnp.swapaxes(w1, 1, 2), ((0, 0), (0, 0), (0, feat_t.shape[0] - w1.shape[1]))),
      b1[:, :, None], jnp.swapaxes(w2, 1, 2), b2[:, :, None], freq[:, :, :, None])

    rt = 128
    filt = pl.pallas_call(
        _hy_filter_kernel,
        grid=(2, BRANCH // rt),
        in_specs=[pl.BlockSpec((None, rt, HY_HID), lambda d, i: (l, d * (BRANCH // rt) + i, 0)),
                  pl.BlockSpec((HY_HID, SEQ), lambda d, i: (0, 0)),
                  pl.BlockSpec((None, None, rt, 1), lambda d, i: (l, d, i, 0)),
                  pl.BlockSpec((1, SEQ), lambda d, i: (0, 0))],
        out_specs=pl.BlockSpec((None, rt, SEQ), lambda d, i: (d, i, 0)),
        out_shape=jax.ShapeDtypeStruct((2, BRANCH, SEQ), F32),
        compiler_params=_cparams(("parallel", "parallel")),
        name="hyena_filter",
    )(jnp.swapaxes(w3, 1, 2), h2, decay[:, :, :, None], t_row)
    filt = filt.reshape(2, BRANCH, n1, r)

    const_specs = [full(a) for a in (f1h, f1l, m3h, m3l)]
    scratch = [pltpu.VMEM((ct, r, r), F32), pltpu.VMEM((ct * r, 2 * r), F32), pltpu.VMEM((ct * r, 2 * r), F32)]
    spec = pl.pallas_call(
        functools.partial(_hy_spec_kernel, ct=ct),
        grid=(BRANCH // ct,),
        in_specs=[pl.BlockSpec((None, ct, n1, r), lambda i: (0, i, 0, 0)),
                  pl.BlockSpec((None, ct, n1, r), lambda i: (1, i, 0, 0)),
                  *const_specs, full(twr), full(twi)],
        out_specs=pl.BlockSpec((ct, r, 2 * r), lambda i: (i, 0, 0)),
        out_shape=jax.ShapeDtypeStruct((BRANCH, r, 2 * r), F32),
        scratch_shapes=scratch,
        compiler_params=_cparams(("parallel",)),
        name="hyena_filter_spectrum",
    )(filt, filt, f1h, f1l, m3h, m3l, twr, twi)

    hp3 = hp_t.reshape(3 * BRANCH, n1, r)
    nb = BRANCH // ct
    cw = conv_w.reshape(DEPTH, 3, 3, BRANCH).transpose(0, 2, 1, 3).reshape(DEPTH, 9, BRANCH, 1, 1)
    cb = conv_b.reshape(DEPTH, 3, BRANCH, 1, 1)
    consts = (f1h, f1l, m3h, m3l, m4h, m4l, m6h, m6l, twr, twi, twr.T, twi.T)
    return pl.pallas_call(
        functools.partial(_hy_conv_kernel, ct=ct),
        grid=(nb,),
        in_specs=[pl.BlockSpec((ct, n1, r), lambda i: (i, 0, 0)),
                  pl.BlockSpec((ct, n1, r), lambda i: (nb + i, 0, 0)),
                  pl.BlockSpec((ct, n1, r), lambda i: (2 * nb + i, 0, 0)),
                  pl.BlockSpec((None, 9, ct, 1, 1), lambda i: (l, 0, i, 0, 0)),
                  pl.BlockSpec((None, 3, ct, 1, 1), lambda i: (l, 0, i, 0, 0)),
                  pl.BlockSpec((None, ct, 1, 1), lambda i: (l, i, 0, 0)),
                  pl.BlockSpec((ct, r, 2 * r), lambda i: (i, 0, 0)),
                  *[full(a) for a in consts]],
        out_specs=pl.BlockSpec((ct, n1, r), lambda i: (i, 0, 0)),
        out_shape=jax.ShapeDtypeStruct((BRANCH, n1, r), F32),
        scratch_shapes=[pltpu.VMEM((ct, n1, r), F32)] + scratch,
        compiler_params=_cparams(("parallel",)),
        name="hyena_conv",
    )(hp3, hp3, hp3, cw, cb, skip.reshape(DEPTH, BRANCH, 1, 1), spec, *consts).reshape(BRANCH, SEQ)


def _s5_chunk_matrices(lam_re, lam_im, log_step, b_re, b_im, c_re, c_im):
    t = S5_T
    lam = lax.complex(lam_re, lam_im)
    step = jnp.exp(log_step)[..., None]
    lam_dt = lam * step
    b_bar = ((jnp.exp(lam_dt) - 1.0) / lam)[..., None] * lax.complex(b_re, b_im)
    c_mat = lax.complex(c_re, c_im)
    pw = jnp.exp(lam_dt[None] * jnp.arange(t + 1, dtype=F32)[:, None, None, None])
    kern = jnp.einsum("dgvp,mdgp,dgpn->mdgvn", c_mat, pw[:t], b_bar, precision="highest").real
    tau = jnp.arange(t)
    lag_f = tau[:, None] - tau[None, :]
    kf = jnp.where((lag_f >= 0)[:, :, None, None, None], kern[jnp.clip(lag_f, 0, t - 1), 0], 0.0)
    kb = jnp.where((lag_f <= 0)[:, :, None, None, None], kern[jnp.clip(-lag_f, 0, t - 1), 1], 0.0)
    w_intra = jnp.transpose(kf + kb, (2, 1, 4, 0, 3)).reshape(S5_GROUPS, t * S5_GROUP, t * S5_GROUP)
    in_f = pw[t - 1 - tau, 0][..., None] * b_bar[0][None]
    in_b = pw[tau, 1][..., None] * b_bar[1][None]

    def in_mat(m):
        m = jnp.transpose(m, (1, 0, 3, 2)).reshape(S5_GROUPS, t * S5_GROUP, S5_STATE)
        return jnp.concatenate([m.real, m.imag], axis=-1)

    w_in = jnp.concatenate([in_mat(in_f), in_mat(in_b)], axis=-1)
    out_f = c_mat[0][None] * pw[tau + 1, 0][:, :, None, :]
    out_b = c_mat[1][None] * pw[t - tau, 1][:, :, None, :]

    def out_mat(m):
        m = jnp.transpose(m, (1, 3, 0, 2)).reshape(S5_GROUPS, S5_STATE, t * S5_GROUP)
        return jnp.concatenate([m.real, -m.imag], axis=1)

    w_out = jnp.concatenate([out_mat(out_f), out_mat(out_b)], axis=1)
    a_pow = jnp.exp(lam_dt[None] * (t * 2.0 ** jnp.arange(S5_LOG_CHUNKS, dtype=F32))[:, None, None, None])
    a_pow = jnp.transpose(a_pow, (2, 1, 0, 3))
    a_re = jnp.concatenate([a_pow.real, a_pow.real], axis=-1)
    a_im = jnp.concatenate([-a_pow.imag, a_pow.imag], axis=-1)
    return w_intra, w_in, w_out, a_re, a_im


def _s5_kernel(u_ref, wi_ref, win_ref, wout_ref, are_ref, aim_ref, skip_ref, y_ref):
    nc = S5_CHUNKS
    half = S5_STATE
    u = u_ref[...]
    ub = u.astype(BF16)
    y = jnp.dot(ub, wi_ref[...].astype(BF16), preferred_element_type=F32) + skip_ref[...] * u
    s_in = jnp.dot(ub, win_ref[...].astype(BF16), preferred_element_type=F32)
    row = lax.broadcasted_iota(jnp.int32, (nc, 2 * half), 0)

    def scan(s, d):
        for step in range(S5_LOG_CHUNKS):
            sh = 1 << step
            if d == 0:
                moved = jnp.where(row >= sh, pltpu.roll(s, sh, 0), 0.0)
            else:
                moved = jnp.where(row < nc - sh, pltpu.roll(s, nc - sh, 0), 0.0)
            swapped = pltpu.roll(moved, half, 1)
            s = s + are_ref[d, step:step + 1, :] * moved + aim_ref[d, step:step + 1, :] * swapped
        if d == 0:
            return jnp.where(row >= 1, pltpu.roll(s, 1, 0), 0.0)
        return jnp.where(row < nc - 1, pltpu.roll(s, nc - 1, 0), 0.0)

    e = jnp.concatenate([scan(s_in[:, :2 * half], 0), scan(s_in[:, 2 * half:], 1)], axis=1)
    y_ref[...] = y + jnp.dot(e.astype(BF16), wout_ref[...].astype(BF16), preferred_element_type=F32)


def _s5(su, mats, skip, l):
    w_intra, w_in, w_out, a_re, a_im = mats
    t = S5_T
    gw = t * S5_GROUP
    u = su.reshape(S5_CHUNKS, t, S5_GROUPS, S5_GROUP).transpose(2, 0, 1, 3).reshape(S5_GROUPS, S5_CHUNKS, gw)
    skip_t = jnp.tile(skip.reshape(DEPTH, S5_GROUPS, 1, S5_GROUP), (1, 1, t, 1)).reshape(DEPTH, S5_GROUPS, 1, gw)
    y = pl.pallas_call(
        _s5_kernel,
        grid=(S5_GROUPS,),
        in_specs=[pl.BlockSpec((None, S5_CHUNKS, gw), lambda g: (g, 0, 0)),
                  pl.BlockSpec((None, gw, gw), lambda g: (g, 0, 0)),
                  pl.BlockSpec((None, gw, 4 * S5_STATE), lambda g: (g, 0, 0)),
                  pl.BlockSpec((None, 4 * S5_STATE, gw), lambda g: (g, 0, 0)),
                  pl.BlockSpec((None, 2, S5_LOG_CHUNKS, 2 * S5_STATE), lambda g: (g, 0, 0, 0)),
                  pl.BlockSpec((None, 2, S5_LOG_CHUNKS, 2 * S5_STATE), lambda g: (g, 0, 0, 0)),
                  pl.BlockSpec((None, None, 1, gw), lambda g: (l, g, 0, 0))],
        out_specs=pl.BlockSpec((None, S5_CHUNKS, gw), lambda g: (g, 0, 0)),
        out_shape=jax.ShapeDtypeStruct((S5_GROUPS, S5_CHUNKS, gw), F32),
        compiler_params=_cparams(("parallel",)),
        name="s5_scan",
    )(u, w_intra, w_in, w_out, a_re, a_im, skip_t)
    return y.reshape(S5_GROUPS, S5_CHUNKS, t, S5_GROUP).transpose(1, 2, 0, 3).reshape(SEQ, BRANCH)


def _merge_kernel(hm_ref, hyt_ref, s5_ref, wa_ref, wb_ref, wga_ref, wgg_ref, g0_ref, g1_ref, g2_ref,
                  out_ref, hy_sc):
    @pl.when(pl.program_id(1) == 0)
    def _():
        hy_sc[...] = hyt_ref[...].T.astype(BF16)

    def proj(a, w_ref):
        return jnp.dot(a, w_ref[...].astype(BF16), preferred_element_type=F32)

    s5b = s5_ref[...].astype(BF16)
    out_a = proj(hm_ref[...], wa_ref)
    out_b = proj(hy_sc[...], wb_ref)
    out_c = proj(s5b, wga_ref) * jax.nn.sigmoid(proj(s5b, wgg_ref))
    merged = (jax.nn.sigmoid(g0_ref[...]) * out_a + jax.nn.sigmoid(g1_ref[...]) * out_b
              + jax.nn.sigmoid(g2_ref[...]) * out_c)
    out_ref[...] = merged.astype(out_ref.dtype)


def _merge(hm, hy_t, s5y, rest, w_a, w_b, w_glu, l, *, tm=512, tn=512):
    nj = D_MODEL // tn
    goff = BRANCH // tn
    return pl.pallas_call(
        _merge_kernel,
        grid=(SEQ // tm, nj),
        in_specs=[pl.BlockSpec((tm, BRANCH), lambda i, j: (i, 0)),
                  pl.BlockSpec((BRANCH, tm), lambda i, j: (0, i)),
                  pl.BlockSpec((tm, BRANCH), lambda i, j: (i, 0)),
                  pl.BlockSpec((None, BRANCH, tn), lambda i, j: (l, 0, j)),
                  pl.BlockSpec((None, BRANCH, tn), lambda i, j: (l, 0, j)),
                  pl.BlockSpec((None, BRANCH, tn), lambda i, j: (l, 0, j)),
                  pl.BlockSpec((None, BRANCH, tn), lambda i, j: (l, 0, nj + j)),
                  pl.BlockSpec((tm, tn), lambda i, j: (i, goff + j)),
                  pl.BlockSpec((tm, tn), lambda i, j: (i, goff + nj + j)),
                  pl.BlockSpec((tm, tn), lambda i, j: (i, goff + 2 * nj + j))],
        out_specs=pl.BlockSpec((tm, tn), lambda i, j: (i, j)),
        out_shape=jax.ShapeDtypeStruct((SEQ, D_MODEL), BF16),
        scratch_shapes=[pltpu.VMEM((tm, BRANCH), BF16)],
        compiler_params=_cparams(("parallel", "arbitrary")),
        name="branch_merge",
    )(hm, hy_t, s5y, w_a, w_b, w_glu, w_glu, rest, rest, rest)


def _layer_norm_rows(y, g, b):
    mu = jnp.mean(y, axis=1, keepdims=True)
    var = jnp.mean(jnp.square(y - mu), axis=1, keepdims=True)
    return (y - mu) * lax.rsqrt(var + LN_EPS) * g + b


def _res_ln_kernel(x_ref, f_ref, g_ref, b_ref, o_ref, ob_ref, *, transposed):
    f = f_ref[...]
    if transposed:
        f = f.T
    y = _layer_norm_rows(ALPHA * x_ref[...] + f, g_ref[...], b_ref[...])
    o_ref[...] = y
    ob_ref[...] = y.astype(BF16)


def _res_ln(x, f, g, b, l, *, transposed, tm=256):
    f_spec = (pl.BlockSpec((D_MODEL, tm), lambda i: (0, i)) if transposed
              else pl.BlockSpec((tm, D_MODEL), lambda i: (i, 0)))
    n_tok = x.shape[0]
    return pl.pallas_call(
        functools.partial(_res_ln_kernel, transposed=transposed),
        grid=(n_tok // tm,),
        in_specs=[pl.BlockSpec((tm, D_MODEL), lambda i: (i, 0)), f_spec,
                  pl.BlockSpec((None, 1, D_MODEL), lambda i: (l, 0, 0)),
                  pl.BlockSpec((None, 1, D_MODEL), lambda i: (l, 0, 0))],
        out_specs=[pl.BlockSpec((tm, D_MODEL), lambda i: (i, 0)),
                   pl.BlockSpec((tm, D_MODEL), lambda i: (i, 0))],
        out_shape=[jax.ShapeDtypeStruct((n_tok, D_MODEL), F32), jax.ShapeDtypeStruct((n_tok, D_MODEL), BF16)],
        compiler_params=_cparams(("parallel",)),
        name="residual_layernorm",
    )(x, f, g.reshape(DEPTH, 1, D_MODEL), b.reshape(DEPTH, 1, D_MODEL))


def _extract_sorted(s, n, out_ref, base):
    for r in range(n):
        m = jnp.max(s, axis=0, keepdims=True)
        out_ref[base + r:base + r + 1, :] = m
        s = jnp.where(s == m, -jnp.inf, s)


def _peer_select_kernel(q_ref, keys_ref, s1_ref, s2_ref, e1_ref, e2_ref, tau_ref, top_sc, kth_sc):
    k = PEER_TOPK
    qb = q_ref[...].astype(BF16)
    half = PEER_KEYS
    for h in range(PEER_HEADS):
        sc = []
        for c in range(2):
            qs = qb[:, (2 * h + c) * half:(2 * h + c + 1) * half]
            s = lax.dot_general(keys_ref[h, c].astype(BF16), qs, (((1,), (1,)), ((), ())),
                                preferred_element_type=F32)
            sc.append(s)
            _extract_sorted(s, k, top_sc, c * k)
        a = top_sc[0:k, :]
        b = top_sc[k:2 * k, :]
        sub = lax.broadcasted_iota(jnp.int32, (8, a.shape[1]), 0)
        groups = [a[0:1] + b[0:8], a[0:1] + b[8:16], a[1:2] + b[0:8]]
        for i in range(2, 8):
            groups.append(jnp.where(sub < k // (i + 1), a[i:i + 1] + b[0:8], -jnp.inf))
        groups.append(a[8:16] + b[0:1])
        cand = jnp.concatenate(groups, axis=0)
        _extract_sorted(cand, k, kth_sc, 0)
        tau = kth_sc[k - 1:k, :]
        top = a[0:1] + b[0:1]
        z = jnp.sum(jnp.where(cand >= tau, jnp.exp(cand - top), 0.0), axis=0, keepdims=True)
        s1_ref[h] = sc[0]
        s2_ref[h] = sc[1]
        e1_ref[h] = jnp.exp(sc[0] - a[0:1])
        e2_ref[h] = jnp.exp(sc[1] - b[0:1]) / z
        tau_ref[h:h + 1, :] = tau


def _peer_select(q, keys, l, *, tt=512):
    n_tok = q.shape[0]
    big = jax.ShapeDtypeStruct((PEER_HEADS, PEER_KEYS, n_tok), F32)
    big_spec = pl.BlockSpec((PEER_HEADS, PEER_KEYS, tt), lambda i: (0, 0, i))
    return pl.pallas_call(
        _peer_select_kernel,
        grid=(n_tok // tt,),
        in_specs=[pl.BlockSpec((tt, D_MODEL), lambda i: (i, 0)),
                  pl.BlockSpec((None, PEER_HEADS, 2, PEER_KEYS, PEER_KEYS), lambda i: (l, 0, 0, 0, 0))],
        out_specs=[big_spec, big_spec, big_spec, big_spec,
                   pl.BlockSpec((PEER_HEADS, tt), lambda i: (0, i))],
        out_shape=[big, big, big, big, jax.ShapeDtypeStruct((PEER_HEADS, n_tok), F32)],
        scratch_shapes=[pltpu.VMEM((2 * PEER_TOPK, tt), F32), pltpu.VMEM((PEER_TOPK, tt), F32)],
        compiler_params=_cparams(("parallel",)),
        name="peer_select",
    )(q, keys)


def _peer_dense_kernel(x_ref, u_ref, vt_ref, s1_ref, s2_ref, e1_ref, e2_ref, tau_ref, out_ref, g_sc,
                       *, rows):
    e = pl.program_id(1)

    @pl.when(e == 0)
    def _():
        out_ref[...] = jnp.zeros_like(out_ref)

    act = lax.dot_general(u_ref[...], x_ref[...], (((1,), (1,)), ((), ())),
                          preferred_element_type=F32)
    kk = PEER_KEYS
    for i in range(rows):
        w = jnp.zeros((kk, act.shape[1]), F32)
        for h in range(PEER_HEADS):
            pair = s1_ref[h, i:i + 1, :] + s2_ref[h]
            val = e1_ref[h, i:i + 1, :] * e2_ref[h]
            w = w + jnp.where(pair >= tau_ref[h:h + 1, :], val, 0.0)
        g = jax.nn.gelu(act[i * kk:(i + 1) * kk, :]) * w
        g_sc[i * kk:(i + 1) * kk, :] = g.astype(BF16)
    out_ref[...] += jnp.dot(vt_ref[...], g_sc[...], preferred_element_type=F32)


def _peer_dense(x_bf, u_bf, vt_bf, sel, *, tt=512, rows=8):
    s1, s2, e1, e2, tau = sel
    te = rows * PEER_KEYS
    n_tok = x_bf.shape[0]
    row_spec = pl.BlockSpec((PEER_HEADS, rows, tt), lambda i, e: (0, e, i))
    all_spec = pl.BlockSpec((PEER_HEADS, PEER_KEYS, tt), lambda i, e: (0, 0, i))
    return pl.pallas_call(
        functools.partial(_peer_dense_kernel, rows=rows),
        grid=(n_tok // tt, N_EXPERTS // te),
        in_specs=[pl.BlockSpec((tt, D_MODEL), lambda i, e: (i, 0)),
                  pl.BlockSpec((te, D_MODEL), lambda i, e: (e, 0)),
                  pl.BlockSpec((D_MODEL, te), lambda i, e: (0, e)),
                  row_spec, all_spec, row_spec, all_spec,
                  pl.BlockSpec((PEER_HEADS, tt), lambda i, e: (0, i))],
        out_specs=pl.BlockSpec((D_MODEL, tt), lambda i, e: (0, i)),
        out_shape=jax.ShapeDtypeStruct((D_MODEL, n_tok), F32),
        scratch_shapes=[pltpu.VMEM((te, tt), BF16)],
        compiler_params=_cparams(("parallel", "arbitrary")),
        name="peer_dense",
    )(x_bf, u_bf, vt_bf, s1, s2, e1, e2, tau)


def _hyena_features():
    pos = jnp.arange(SEQ, dtype=F32)
    t = pos / (SEQ - 1)
    bands = jnp.linspace(1e-4, 16 - 1, 16, dtype=F32)
    ang = 2.0 * math.pi * pos[:, None] * bands[None, :] / SEQ
    feat = jnp.concatenate([t[:, None], jnp.cos(ang), -jnp.sin(ang)], axis=-1)
    feat_t = jnp.pad(feat.T, ((0, HY_EMB_PAD - feat.shape[1]), (0, 0)))
    return feat_t, t[None, :]


def kernel(x, w_in, mlstm_gate_bias, mlstm_norm_gain, w_mlstm_out, hyena_conv_w, hyena_conv_b, hyena_w1,
           hyena_b1, hyena_w2, hyena_b2, hyena_freq, hyena_w3, hyena_decay, hyena_skip, w_hyena_out,
           s5_lambda_re, s5_lambda_im, s5_log_step, s5_b_re, s5_b_im, s5_c_re, s5_c_im, s5_skip, w_s5_glu,
           w_out, ln1_g, ln1_b, peer_w_q, peer_subkeys, peer_u, peer_v, ln2_g, ln2_b):
    xf = x.reshape(SEQ, D_MODEL)
    xb = xf.astype(BF16)
    feat_t, t_row = _hyena_features()
    w_gates = w_in[:, :, OFF_GATES:OFF_HYENA]
    w_hy_t = jnp.swapaxes(w_in[:, :, OFF_HYENA:OFF_S5], 1, 2)
    w_rest = w_in[:, :, OFF_HYENA:]
    for l in range(DEPTH):
        qkvo = _mm(xb, w_in, l, 0, 4 * BRANCH, tm=1024, tn=512, name="proj_qkvo")
        gates = _mm(xb, w_gates, l, 0, N_GATES, tm=1024, tn=N_GATES, name="proj_gates")
        hp_t = _mm_nt(w_hy_t, l, xb, tr=512, tm=1024, name="proj_hyena_t")
        rest = _mm(xb, w_rest, l, 3 * BRANCH, BRANCH + 3 * D_MODEL, tm=1024, tn=512, name="proj_s5_mixgates")

        hdir = _mlstm(qkvo, gates, mlstm_gate_bias, l)
        hm = _mlstm_post(hdir, qkvo, mlstm_norm_gain, l)
        hy_t = _hyena(hp_t, l, hyena_conv_w, hyena_conv_b, hyena_w1, hyena_b1, hyena_w2, hyena_b2,
                      hyena_freq, hyena_w3, hyena_decay, hyena_skip, feat_t, t_row)
        mats = _s5_chunk_matrices(s5_lambda_re[l], s5_lambda_im[l], s5_log_step[l], s5_b_re[l], s5_b_im[l],
                                  s5_c_re[l], s5_c_im[l])
        s5y = _s5(rest[:, :BRANCH], mats, s5_skip, l)
        merged = _merge(hm, hy_t, s5y, rest, w_mlstm_out, w_hyena_out, w_s5_glu, l)
        mix = _mm(merged, w_out, l, 0, D_MODEL, tm=1024, tn=512, name="proj_out")
        xf, xb = _res_ln(xf, mix, ln1_g, ln1_b, l, transposed=False)

        q = _mm(xb, peer_w_q, l, 0, PEER_HEADS * 2 * PEER_KEYS, tm=1024, tn=512, name="peer_query")
        sel = _peer_select(q, peer_subkeys, l)
        ffn_t = _peer_dense(xb, peer_u[l].astype(BF16), peer_v[l].T.astype(BF16), sel)
        xf, xb = _res_ln(xf, ffn_t, ln2_g, ln2_b, l, transposed=True)
    return xf.reshape(1, SEQ, D_MODEL)
```

```python
import functools
import math

import numpy as np
import jax
import jax.numpy as jnp
from jax import lax
from jax.experimental import pallas as pl
from jax.experimental.pallas import tpu as pltpu

F32 = jnp.float32
BF16 = jnp.bfloat16

D_MODEL = 2048
SEQ = 8192
DEPTH = 4
BRANCH = 1024
HEADS = 4
HEAD_DIM = 256
CHUNK = 128
N_GATES = 16
HY_HID = 64
HY_EMB_PAD = 128
S5_GROUP = 16
S5_GROUPS = 64
S5_STATE = 64
S5_T = 16
S5_CHUNKS = SEQ // S5_T
S5_LOG_CHUNKS = 9
PEER_HEADS = 8
PEER_KEYS = 128
PEER_TOPK = 16
N_EXPERTS = PEER_KEYS * PEER_KEYS
ALPHA = (2 * DEPTH) ** 0.25
LN_EPS = 1e-5
FFT_N = 2 * SEQ
FFT_R = 128
OFF_GATES = 4 * BRANCH
OFF_HYENA = OFF_GATES + N_GATES
OFF_S5 = OFF_HYENA + 3 * BRANCH
OFF_MIXG = OFF_S5 + BRANCH

VMEM_LIMIT = 56 * 1024 * 1024


def _cparams(sem):
    return pltpu.CompilerParams(dimension_semantics=sem, vmem_limit_bytes=VMEM_LIMIT)


def _split_bf16(a):
    hi = a.astype(BF16)
    lo = (a - hi.astype(F32)).astype(BF16)
    return hi, lo


def _dot3(a, b_hi, b_lo):
    a_hi, a_lo = _split_bf16(a)
    acc = jnp.dot(a_hi, b_hi, preferred_element_type=F32)
    acc += jnp.dot(a_lo, b_hi, preferred_element_type=F32)
    acc += jnp.dot(a_hi, b_lo, preferred_element_type=F32)
    return acc


def _dot3_both(a, b):
    b_hi, b_lo = _split_bf16(b)
    return _dot3(a, b_hi, b_lo)


def _mm_kernel(a_ref, b_ref, o_ref):
    o_ref[...] = jnp.dot(a_ref[...].astype(BF16), b_ref[...].astype(BF16),
                         preferred_element_type=F32).astype(o_ref.dtype)


def _mm(a, w, l, col_off, n_cols, *, tm, tn, name):
    m, k = a.shape
    cb = col_off // tn
    assert col_off % tn == 0 and n_cols % tn == 0 and m % tm == 0
    return pl.pallas_call(
        _mm_kernel,
        grid=(m // tm, n_cols // tn),
        in_specs=[pl.BlockSpec((tm, k), lambda i, j: (i, 0)),
                  pl.BlockSpec((None, k, tn), lambda i, j: (l, 0, cb + j))],
        out_specs=pl.BlockSpec((tm, tn), lambda i, j: (i, j)),
        out_shape=jax.ShapeDtypeStruct((m, n_cols), F32),
        compiler_params=_cparams(("parallel", "parallel")),
        name=name,
    )(a, w)


def _mm_nt_kernel(w_ref, a_ref, o_ref):
    o_ref[...] = lax.dot_general(w_ref[...].astype(BF16), a_ref[...].astype(BF16),
                                 (((1,), (1,)), ((), ())), preferred_element_type=F32)


def _mm_nt(wt, l, a, *, tr, tm, name):
    _, r, k = wt.shape
    m = a.shape[0]
    return pl.pallas_call(
        _mm_nt_kernel,
        grid=(r // tr, m // tm),
        in_specs=[pl.BlockSpec((None, tr, k), lambda i, j: (l, i, 0)),
                  pl.BlockSpec((tm, k), lambda i, j: (j, 0))],
        out_specs=pl.BlockSpec((tr, tm), lambda i, j: (i, j)),
        out_shape=jax.ShapeDtypeStruct((r, m), F32),
        compiler_params=_cparams(("parallel", "parallel")),
        name=name,
    )(wt, a)


def _log_sigmoid(x):
    return jnp.minimum(x, 0.0) - jnp.log1p(jnp.exp(-jnp.abs(x)))


def _mlstm_kernel(q_ref, k_ref, v_ref, li_r_ref, lf_r_ref, li_c_ref, lf_c_ref, bi_ref, bf_ref,
                  o_ref, c_sc, n_sc, m_sc):
    d = pl.program_id(0)
    c = pl.program_id(2)

    @pl.when(c == 0)
    def _():
        c_sc[...] = jnp.zeros_like(c_sc)
        n_sc[...] = jnp.zeros_like(n_sc)
        m_sc[...] = jnp.zeros_like(m_sc)

    n = CHUNK
    b_i = bi_ref[...]
    b_f = bf_ref[...]
    li_r = li_r_ref[...] + b_i
    li_c = li_c_ref[...] + b_i
    lf_r = _log_sigmoid(lf_r_ref[...] + b_f)
    lf_c = _log_sigmoid(lf_c_ref[...] + b_f)

    row = lax.broadcasted_iota(jnp.int32, (n, n), 0)
    col = lax.broadcasted_iota(jnp.int32, (n, n), 1)
    sgn = 1 - 2 * d
    valid = ((row - col) * sgn) >= 0
    valid_t = ((col - row) * sgn) >= 0
    b_col = jnp.sum(jnp.where(valid, lf_r, 0.0), axis=1, keepdims=True)
    b_row = jnp.sum(jnp.where(valid_t, lf_c, 0.0), axis=0, keepdims=True)
    total = jnp.sum(lf_r, axis=1, keepdims=True)

    m_prev = m_sc[...]
    m_inter = b_col + m_prev
    log_d = jnp.where(valid, b_col - b_row + li_r, -jnp.inf)
    m_t = jnp.maximum(m_inter, jnp.max(log_d, axis=1, keepdims=True))
    inter = jnp.exp(m_inter - m_t)
    dmat = jnp.exp(log_d - m_t)

    scale = HEAD_DIM ** -0.5
    q = q_ref[...]
    k = k_ref[...] * scale
    qb = q.astype(BF16)
    kb = k.astype(BF16)
    vb = v_ref[...].astype(BF16)
    s = lax.dot_general(qb, kb, (((1,), (1,)), ((), ())), preferred_element_type=F32) * dmat
    num = jnp.dot(s.astype(BF16), vb, preferred_element_type=F32)
    num += inter * jnp.dot(qb, c_sc[...].astype(BF16), preferred_element_type=F32)
    qn = jnp.sum(q * n_sc[...], axis=1, keepdims=True)
    den = jnp.sum(s, axis=1, keepdims=True) + inter * qn
    o_ref[...] = num / jnp.maximum(jnp.abs(den), jnp.exp(-m_t))

    log_w = total - b_col + li_c
    m_new = jnp.maximum(total + m_prev, jnp.max(log_w, axis=0, keepdims=True))
    w = jnp.exp(log_w - m_new)
    decay = jnp.exp(total + m_prev - m_new)
    kw = k * w
    c_sc[...] = decay * c_sc[...] + lax.dot_general(
        kw.astype(BF16), vb, (((0,), (0,)), ((), ())), preferred_element_type=F32)
    n_sc[...] = decay * n_sc[...] + jnp.sum(kw, axis=0, keepdims=True)
    m_sc[...] = m_new


def _mlstm(qkvo, gates, gate_bias, l):
    nc = SEQ // CHUNK
    g_rows = gates.T.reshape(N_GATES, 1, SEQ)
    g_cols = gates.T.reshape(N_GATES, SEQ, 1)
    bias = gate_bias.reshape(DEPTH, N_GATES, 1, 1)

    def chunk(d, c):
        return c + d * (nc - 1 - 2 * c)

    def gi(d, h):
        return d * 8 + h

    def gf(d, h):
        return d * 8 + 4 + h

    return pl.pallas_call(
        _mlstm_kernel,
        grid=(2, HEADS, nc),
        in_specs=[
            pl.BlockSpec((CHUNK, HEAD_DIM), lambda d, h, c: (chunk(d, c), h)),
            pl.BlockSpec((CHUNK, HEAD_DIM), lambda d, h, c: (chunk(d, c), HEADS + h)),
            pl.BlockSpec((CHUNK, HEAD_DIM), lambda d, h, c: (chunk(d, c), 2 * HEADS + h)),
            pl.BlockSpec((None, 1, CHUNK), lambda d, h, c: (gi(d, h), 0, chunk(d, c))),
            pl.BlockSpec((None, 1, CHUNK), lambda d, h, c: (gf(d, h), 0, chunk(d, c))),
            pl.BlockSpec((None, CHUNK, 1), lambda d, h, c: (gi(d, h), chunk(d, c), 0)),
            pl.BlockSpec((None, CHUNK, 1), lambda d, h, c: (gf(d, h), chunk(d, c), 0)),
            pl.BlockSpec((None, None, 1, 1), lambda d, h, c: (l, gi(d, h), 0, 0)),
            pl.BlockSpec((None, None, 1, 1), lambda d, h, c: (l, gf(d, h), 0, 0)),
        ],
        out_specs=pl.BlockSpec((None, CHUNK, HEAD_DIM), lambda d, h, c: (d, chunk(d, c), h)),
        out_shape=jax.ShapeDtypeStruct((2, SEQ, BRANCH), F32),
        scratch_shapes=[pltpu.VMEM((HEAD_DIM, HEAD_DIM), F32),
                        pltpu.VMEM((1, HEAD_DIM), F32),
                        pltpu.VMEM((1, 1), F32)],
        compiler_params=_cparams(("parallel", "parallel", "arbitrary")),
        name="mlstm_scan",
    )(qkvo, qkvo, qkvo, g_rows, g_rows, g_cols, g_cols, bias, bias)


def _mlstm_post_kernel(h_ref, o_ref, gain_ref, out_ref):
    h = h_ref[0] + h_ref[1]
    h = jax.nn.sigmoid(o_ref[...]) * h
    for hd in range(HEADS):
        sl = slice(hd * HEAD_DIM, (hd + 1) * HEAD_DIM)
        hh = h[:, sl]
        mu = jnp.mean(hh, axis=1, keepdims=True)
        var = jnp.mean(jnp.square(hh - mu), axis=1, keepdims=True)
        out_ref[:, sl] = ((hh - mu) * lax.rsqrt(var + LN_EPS) * gain_ref[:, sl]).astype(out_ref.dtype)


def _mlstm_post(hdir, qkvo, gain, l, *, tm=512):
    return pl.pallas_call(
        _mlstm_post_kernel,
        grid=(SEQ // tm,),
        in_specs=[pl.BlockSpec((2, tm, BRANCH), lambda i: (0, i, 0)),
                  pl.BlockSpec((tm, BRANCH), lambda i: (i, 3)),
                  pl.BlockSpec((None, 1, BRANCH), lambda i: (l, 0, 0))],
        out_specs=pl.BlockSpec((tm, BRANCH), lambda i: (i, 0)),
        out_shape=jax.ShapeDtypeStruct((SEQ, BRANCH), BF16),
        compiler_params=_cparams(("parallel",)),
        name="mlstm_post",
    )(hdir, qkvo, gain.reshape(DEPTH, 1, BRANCH))


HY_CT = 32
HY_RB = 512


@functools.lru_cache(maxsize=None)
def _fft_constants():
    r = FFT_R
    idx = np.arange(r)
    th = 2.0 * np.pi * np.outer(idx, idx) / r
    cs, sn = np.cos(th), np.sin(th)
    ph = 2.0 * np.pi * np.outer(idx, idx) / FFT_N
    f1 = np.concatenate([cs, -sn], axis=1)
    f1[r // 2:] = 0.0
    m3 = np.block([[cs, -sn], [sn, cs]])
    m4 = np.block([[cs, sn], [-sn, cs]])
    m6 = np.concatenate([cs, -sn], axis=0) / FFT_N
    m6[:, r // 2:] = 0.0
    consts = {name: jnp.asarray(mat, F32).astype(BF16)
              for name, mat in (("f1", f1), ("m3", m3), ("m4", m4), ("m6", m6))}
    consts["tw"] = (jnp.asarray(np.cos(ph), F32), jnp.asarray(np.sin(ph), F32))
    return consts


def _rows(i, n):
    return pl.ds(pl.multiple_of(i * n, n), n)


def _for_each(n, body, unroll=1):
    def step(i, carry):
        body(i)
        return carry

    lax.fori_loop(0, n, step, 0, unroll=unroll)


def _fwd_stage1(sig_ref, ct, t_buf, a_buf, b_buf, f1_ref, twr_ref, twi_ref):
    r = FFT_R
    zeros = jnp.zeros((r // 2, r), F32)

    def tr_in(c):
        t_buf[_rows(c, r), :] = jnp.concatenate([sig_ref[c], zeros], axis=0).T.astype(BF16)

    _for_each(ct, tr_in, unroll=4)

    def stage1(b):
        rows = _rows(b, HY_RB)
        a_buf[rows, :] = jnp.dot(t_buf[rows, :], f1_ref[...], preferred_element_type=F32)

    _for_each(ct * r // HY_RB, stage1)

    def twiddle(c):
        rows = _rows(c, r)
        a = a_buf[rows, :]
        ar, ai = a[:, :r], a[:, r:]
        twr, twi = twr_ref[...], twi_ref[...]
        b_buf[rows, :r] = (ar * twr + ai * twi).T.astype(BF16)
        b_buf[rows, r:] = (ai * twr - ar * twi).T.astype(BF16)

    _for_each(ct, twiddle, unroll=4)


def _hy_mlp_kernel(feat_ref, w1_ref, b1_ref, w2_ref, b2_ref, fr_ref, out_ref):
    h = _dot3_both(w1_ref[...], feat_ref[...]) + b1_ref[...]
    h = jnp.sin(fr_ref[0] * h)
    h = _dot3_both(w2_ref[...], h) + b2_ref[...]
    out_ref[...] = jnp.sin(fr_ref[1] * h)


def _hy_filter_kernel(w3_ref, h_ref, dec_ref, t_ref, out_ref):
    d = pl.program_id(0)
    filt = _dot3_both(w3_ref[...], h_ref[...])
    filt = filt * jnp.exp(-t_ref[...] * jnp.abs(dec_ref[...]))
    filt = filt / (jnp.sum(jnp.abs(filt), axis=1, keepdims=True) + 1e-6)
    lane = lax.broadcasted_iota(jnp.int32, filt.shape, 1)
    out_ref[...] = jnp.where((lane == 0) & (d == 1), 0.0, filt)


def _hy_spec_kernel(fa_ref, fb_ref, f1_ref, m3_ref, twr_ref, twi_ref, out_ref, t_buf, a_buf, b_buf, *, ct):
    r = FFT_R
    nblk = ct * r // HY_RB
    _fwd_stage1(fa_ref, ct, t_buf, a_buf, b_buf, f1_ref, twr_ref, twi_ref)

    def put(b):
        rows = _rows(b, HY_RB)
        out_ref[rows, :] = jnp.dot(b_buf[rows, :], m3_ref[...], preferred_element_type=F32)

    _for_each(nblk, put)
    _fwd_stage1(fb_ref, ct, t_buf, a_buf, b_buf, f1_ref, twr_ref, twi_ref)

    def add_conj(b):
        rows = _rows(b, HY_RB)
        sb = jnp.dot(b_buf[rows, :], m3_ref[...], preferred_element_type=F32)
        sa = out_ref[rows, :]
        out_ref[rows, :] = jnp.concatenate([sa[:, :r] + sb[:, :r], sa[:, r:] - sb[:, r:]], axis=1)

    _for_each(nblk, add_conj)


def _shift_prev(p, n1, n2):
    r1 = pltpu.roll(p, 1, 2)
    r2 = pltpu.roll(r1, 1, 1)
    return jnp.where(n2 == 0, jnp.where(n1 == 0, 0.0, r2), r1)


def _shift_next(p, n1, n2):
    s1, s2 = p.shape[1], p.shape[2]
    r1 = pltpu.roll(p, s2 - 1, 2)
    r2 = pltpu.roll(r1, s1 - 1, 1)
    return jnp.where(n2 == s2 - 1, jnp.where(n1 == s1 - 1, 0.0, r2), r1)


def _hy_conv_kernel(p0_ref, p1_ref, p2_ref, cw_ref, cb_ref, skip_ref, h_ref,
                    f1_ref, m3_ref, m4_ref, m6_ref, twr_ref, twi_ref,
                    out_ref, z_buf, t_buf, a_buf, b_buf, *, ct):
    r = FFT_R
    n1_len = SEQ // r
    nblk = ct * r // HY_RB
    shp = (ct, n1_len, r)
    n1 = lax.broadcasted_iota(jnp.int32, shp, 1)
    n2 = lax.broadcasted_iota(jnp.int32, shp, 2)

    def short_conv(p_ref, g):
        p = p_ref[...]
        return (_shift_prev(p, n1, n2) * cw_ref[3 * g] + p * cw_ref[3 * g + 1]
                + _shift_next(p, n1, n2) * cw_ref[3 * g + 2] + cb_ref[g])

    z_buf[...] = short_conv(p1_ref, 1) * short_conv(p2_ref, 2)
    _fwd_stage1(z_buf, ct, t_buf, a_buf, b_buf, f1_ref, twr_ref, twi_ref)

    def freq_domain(b):
        rows = _rows(b, HY_RB)
        x = jnp.dot(b_buf[rows, :], m3_ref[...], preferred_element_type=F32)
        h = h_ref[rows, :]
        xr, xi = x[:, :r], x[:, r:]
        hr, hi = h[:, :r], h[:, r:]
        y = jnp.concatenate([xr * hr - xi * hi, xr * hi + xi * hr], axis=1).astype(BF16)
        a_buf[rows, :] = jnp.dot(y, m4_ref[...], preferred_element_type=F32)

    _for_each(nblk, freq_domain)

    def inv_twiddle(c):
        rows = _rows(c, r)
        b = a_buf[rows, :]
        br, bi = b[:, :r], b[:, r:]
        twr, twi = twr_ref[...], twi_ref[...]
        b_buf[rows, :r] = (br * twr - bi * twi).T.astype(BF16)
        b_buf[rows, r:] = (br * twi + bi * twr).T.astype(BF16)

    _for_each(ct, inv_twiddle, unroll=4)

    def stage6(b):
        rows = _rows(b, HY_RB)
        a_buf[rows, :r] = jnp.dot(b_buf[rows, :], m6_ref[...], preferred_element_type=F32)

    _for_each(nblk, stage6)

    def finish(c):
        y_c = a_buf[_rows(c, r), :r].T[:n1_len, :]
        out_ref[c] = y_c + skip_ref[c] * z_buf[c]

    _for_each(ct, finish, unroll=4)
    out_ref[...] = short_conv(p0_ref, 0) * out_ref[...]


def _hyena(hp_t, l, conv_w, conv_b, w1, b1, w2, b2, freq, w3, decay, skip, feat_t, t_row):
    r = FFT_R
    n1 = SEQ // r
    ct = HY_CT
    k = _fft_constants()
    f1, m3, m4, m6 = k["f1"], k["m3"], k["m4"], k["m6"]
    twr, twi = k["tw"]

    def full(a):
        return pl.BlockSpec(a.shape, lambda *_: (0,) * a.ndim)

    h2 = pl.pallas_call(
        _hy_mlp_kernel,
        grid=(1,),
        in_specs=[full(feat_t), pl.BlockSpec((None, HY_HID, feat_t.shape[0]), lambda i: (l, 0, 0)),
                  pl.BlockSpec((None, HY_HID, 1), lambda i: (l, 0, 0)),
                  pl.BlockSpec((None, HY_HID, HY_HID), lambda i: (l, 0, 0)),
                  pl.BlockSpec((None, HY_HID, 1), lambda i: (l, 0, 0)),
                  pl.BlockSpec((None, 2, HY_HID, 1), lambda i: (l, 0, 0, 0))],
        out_specs=pl.BlockSpec((HY_HID, SEQ), lambda i: (0, 0)),
        out_shape=jax.ShapeDtypeStruct((HY_HID, SEQ), F32),
        compiler_params=pltpu.CompilerParams(vmem_limit_bytes=VMEM_LIMIT),
        name="hyena_filter_mlp",
    )(feat_t, jnp.pad(jnp.swapaxes(w1, 1, 2), ((0, 0), (0, 0), (0, feat_t.shape[0] - w1.shape[1]))),
      b1[:, :, None], jnp.swapaxes(w2, 1, 2), b2[:, :, None], freq[:, :, :, None])

    rt = 128
    filt = pl.pallas_call(
        _hy_filter_kernel,
        grid=(2, BRANCH // rt),
        in_specs=[pl.BlockSpec((None, rt, HY_HID), lambda d, i: (l, d * (BRANCH // rt) + i, 0)),
                  pl.BlockSpec((HY_HID, SEQ), lambda d, i: (0, 0)),
                  pl.BlockSpec((None, None, rt, 1), lambda d, i: (l, d, i, 0)),
                  pl.BlockSpec((1, SEQ), lambda d, i: (0, 0))],
        out_specs=pl.BlockSpec((None, rt, SEQ), lambda d, i: (d, i, 0)),
        out_shape=jax.ShapeDtypeStruct((2, BRANCH, SEQ), F32),
        compiler_params=_cparams(("parallel", "parallel")),
        name="hyena_filter",
    )(jnp.swapaxes(w3, 1, 2), h2, decay[:, :, :, None], t_row)
    filt = filt.reshape(2, BRANCH, n1, r)

    scratch = [pltpu.VMEM((ct * r, r), BF16), pltpu.VMEM((ct * r, 2 * r), F32), pltpu.VMEM((ct * r, 2 * r), BF16)]
    spec = pl.pallas_call(
        functools.partial(_hy_spec_kernel, ct=ct),
        grid=(BRANCH // ct,),
        in_specs=[pl.BlockSpec((None, ct, n1, r), lambda i: (0, i, 0, 0)),
                  pl.BlockSpec((None, ct, n1, r), lambda i: (1, i, 0, 0)),
                  full(f1), full(m3), full(twr), full(twi)],
        out_specs=pl.BlockSpec((ct * r, 2 * r), lambda i: (i, 0)),
        out_shape=jax.ShapeDtypeStruct((BRANCH * r, 2 * r), F32),
        scratch_shapes=scratch,
        compiler_params=_cparams(("parallel",)),
        name="hyena_filter_spectrum",
    )(filt, filt, f1, m3, twr, twi)

    hp3 = hp_t.reshape(3 * BRANCH, n1, r)
    nb = BRANCH // ct
    cw = conv_w.reshape(DEPTH, 3, 3, BRANCH).transpose(0, 2, 1, 3).reshape(DEPTH, 9, BRANCH, 1, 1)
    cb = conv_b.reshape(DEPTH, 3, BRANCH, 1, 1)
    consts = (f1, m3, m4, m6, twr, twi)
    return pl.pallas_call(
        functools.partial(_hy_conv_kernel, ct=ct),
        grid=(nb,),
        in_specs=[pl.BlockSpec((ct, n1, r), lambda i: (i, 0, 0)),
                  pl.BlockSpec((ct, n1, r), lambda i: (nb + i, 0, 0)),
                  pl.BlockSpec((ct, n1, r), lambda i: (2 * nb + i, 0, 0)),
                  pl.BlockSpec((None, 9, ct, 1, 1), lambda i: (l, 0, i, 0, 0)),
                  pl.BlockSpec((None, 3, ct, 1, 1), lambda i: (l, 0, i, 0, 0)),
                  pl.BlockSpec((None, ct, 1, 1), lambda i: (l, i, 0, 0)),
                  pl.BlockSpec((ct * r, 2 * r), lambda i: (i, 0)),
                  *[full(a) for a in consts]],
        out_specs=pl.BlockSpec((ct, n1, r), lambda i: (i, 0, 0)),
        out_shape=jax.ShapeDtypeStruct((BRANCH, n1, r), F32),
        scratch_shapes=[pltpu.VMEM((ct, n1, r), F32)] + scratch,
        compiler_params=_cparams(("parallel",)),
        name="hyena_conv",
    )(hp3, hp3, hp3, cw, cb, skip.reshape(DEPTH, BRANCH, 1, 1), spec, *consts).reshape(BRANCH, SEQ)


def _s5_chunk_matrices(lam_re, lam_im, log_step, b_re, b_im, c_re, c_im):
    t = S5_T
    lam = lax.complex(lam_re, lam_im)
    step = jnp.exp(log_step)[..., None]
    lam_dt = lam * step
    b_bar = ((jnp.exp(lam_dt) - 1.0) / lam)[..., None] * lax.complex(b_re, b_im)
    c_mat = lax.complex(c_re, c_im)
    pw = jnp.exp(lam_dt[None] * jnp.arange(t + 1, dtype=F32)[:, None, None, None])
    kern = jnp.einsum("dgvp,mdgp,dgpn->mdgvn", c_mat, pw[:t], b_bar, precision="highest").real
    tau = jnp.arange(t)
    lag_f = tau[:, None] - tau[None, :]
    kf = jnp.where((lag_f >= 0)[:, :, None, None, None], kern[jnp.clip(lag_f, 0, t - 1), 0], 0.0)
    kb = jnp.where((lag_f <= 0)[:, :, None, None, None], kern[jnp.clip(-lag_f, 0, t - 1), 1], 0.0)
    w_intra = jnp.transpose(kf + kb, (2, 1, 4, 0, 3)).reshape(S5_GROUPS, t * S5_GROUP, t * S5_GROUP)
    in_f = pw[t - 1 - tau, 0][..., None] * b_bar[0][None]
    in_b = pw[tau, 1][..., None] * b_bar[1][None]

    def in_mat(m):
        m = jnp.transpose(m, (1, 0, 3, 2)).reshape(S5_GROUPS, t * S5_GROUP, S5_STATE)
        return jnp.concatenate([m.real, m.imag], axis=-1)

    w_in = jnp.concatenate([in_mat(in_f), in_mat(in_b)], axis=-1)
    out_f = c_mat[0][None] * pw[tau + 1, 0][:, :, None, :]
    out_b = c_mat[1][None] * pw[t - tau, 1][:, :, None, :]

    def out_mat(m):
        m = jnp.transpose(m, (1, 3, 0, 2)).reshape(S5_GROUPS, S5_STATE, t * S5_GROUP)
        return jnp.concatenate([m.real, -m.imag], axis=1)

    w_out = jnp.concatenate([out_mat(out_f), out_mat(out_b)], axis=1)
    a_pow = jnp.exp(lam_dt[None] * (t * 2.0 ** jnp.arange(S5_LOG_CHUNKS, dtype=F32))[:, None, None, None])
    a_pow = jnp.transpose(a_pow, (2, 1, 0, 3))
    a_re = jnp.concatenate([a_pow.real, a_pow.real], axis=-1)
    a_im = jnp.concatenate([-a_pow.imag, a_pow.imag], axis=-1)
    return w_intra, w_in, w_out, a_re, a_im


def _s5_kernel(u_ref, wi_ref, win_ref, wout_ref, are_ref, aim_ref, skip_ref, y_ref):
    nc = S5_CHUNKS
    half = S5_STATE
    u = u_ref[...]
    ub = u.astype(BF16)
    y = jnp.dot(ub, wi_ref[...].astype(BF16), preferred_element_type=F32) + skip_ref[...] * u
    s_in = jnp.dot(ub, win_ref[...].astype(BF16), preferred_element_type=F32)
    row = lax.broadcasted_iota(jnp.int32, (nc, 2 * half), 0)

    def scan(s, d):
        for step in range(S5_LOG_CHUNKS):
            sh = 1 << step
            if d == 0:
                moved = jnp.where(row >= sh, pltpu.roll(s, sh, 0), 0.0)
            else:
                moved = jnp.where(row < nc - sh, pltpu.roll(s, nc - sh, 0), 0.0)
            swapped = pltpu.roll(moved, half, 1)
            s = s + are_ref[d, step:step + 1, :] * moved + aim_ref[d, step:step + 1, :] * swapped
        if d == 0:
            return jnp.where(row >= 1, pltpu.roll(s, 1, 0), 0.0)
        return jnp.where(row < nc - 1, pltpu.roll(s, nc - 1, 0), 0.0)

    e = jnp.concatenate([scan(s_in[:, :2 * half], 0), scan(s_in[:, 2 * half:], 1)], axis=1)
    y_ref[...] = y + jnp.dot(e.astype(BF16), wout_ref[...].astype(BF16), preferred_element_type=F32)


def _s5(su, mats, skip, l):
    w_intra, w_in, w_out, a_re, a_im = mats
    t = S5_T
    gw = t * S5_GROUP
    u = su.reshape(S5_CHUNKS, t, S5_GROUPS, S5_GROUP).transpose(2, 0, 1, 3).reshape(S5_GROUPS, S5_CHUNKS, gw)
    skip_t = jnp.tile(skip.reshape(DEPTH, S5_GROUPS, 1, S5_GROUP), (1, 1, t, 1)).reshape(DEPTH, S5_GROUPS, 1, gw)
    y = pl.pallas_call(
        _s5_kernel,
        grid=(S5_GROUPS,),
        in_specs=[pl.BlockSpec((None, S5_CHUNKS, gw), lambda g: (g, 0, 0)),
                  pl.BlockSpec((None, gw, gw), lambda g: (g, 0, 0)),
                  pl.BlockSpec((None, gw, 4 * S5_STATE), lambda g: (g, 0, 0)),
                  pl.BlockSpec((None, 4 * S5_STATE, gw), lambda g: (g, 0, 0)),
                  pl.BlockSpec((None, 2, S5_LOG_CHUNKS, 2 * S5_STATE), lambda g: (g, 0, 0, 0)),
                  pl.BlockSpec((None, 2, S5_LOG_CHUNKS, 2 * S5_STATE), lambda g: (g, 0, 0, 0)),
                  pl.BlockSpec((None, None, 1, gw), lambda g: (l, g, 0, 0))],
        out_specs=pl.BlockSpec((None, S5_CHUNKS, gw), lambda g: (g, 0, 0)),
        out_shape=jax.ShapeDtypeStruct((S5_GROUPS, S5_CHUNKS, gw), F32),
        compiler_params=_cparams(("parallel",)),
        name="s5_scan",
    )(u, w_intra, w_in, w_out, a_re, a_im, skip_t)
    return y.reshape(S5_GROUPS, S5_CHUNKS, t, S5_GROUP).transpose(1, 2, 0, 3).reshape(SEQ, BRANCH)


def _merge_kernel(hm_ref, hyt_ref, s5_ref, wa_ref, wb_ref, wga_ref, wgg_ref, g0_ref, g1_ref, g2_ref,
                  out_ref, hy_sc):
    @pl.when(pl.program_id(1) == 0)
    def _():
        hy_sc[...] = hyt_ref[...].T.astype(BF16)

    def proj(a, w_ref):
        return jnp.dot(a, w_ref[...].astype(BF16), preferred_element_type=F32)

    s5b = s5_ref[...].astype(BF16)
    out_a = proj(hm_ref[...], wa_ref)
    out_b = proj(hy_sc[...], wb_ref)
    out_c = proj(s5b, wga_ref) * jax.nn.sigmoid(proj(s5b, wgg_ref))
    merged = (jax.nn.sigmoid(g0_ref[...]) * out_a + jax.nn.sigmoid(g1_ref[...]) * out_b
              + jax.nn.sigmoid(g2_ref[...]) * out_c)
    out_ref[...] = merged.astype(out_ref.dtype)


def _merge(hm, hy_t, s5y, rest, w_a, w_b, w_glu, l, *, tm=512, tn=512):
    nj = D_MODEL // tn
    goff = BRANCH // tn
    return pl.pallas_call(
        _merge_kernel,
        grid=(SEQ // tm, nj),
        in_specs=[pl.BlockSpec((tm, BRANCH), lambda i, j: (i, 0)),
                  pl.BlockSpec((BRANCH, tm), lambda i, j: (0, i)),
                  pl.BlockSpec((tm, BRANCH), lambda i, j: (i, 0)),
                  pl.BlockSpec((None, BRANCH, tn), lambda i, j: (l, 0, j)),
                  pl.BlockSpec((None, BRANCH, tn), lambda i, j: (l, 0, j)),
                  pl.BlockSpec((None, BRANCH, tn), lambda i, j: (l, 0, j)),
                  pl.BlockSpec((None, BRANCH, tn), lambda i, j: (l, 0, nj + j)),
                  pl.BlockSpec((tm, tn), lambda i, j: (i, goff + j)),
                  pl.BlockSpec((tm, tn), lambda i, j: (i, goff + nj + j)),
                  pl.BlockSpec((tm, tn), lambda i, j: (i, goff + 2 * nj + j))],
        out_specs=pl.BlockSpec((tm, tn), lambda i, j: (i, j)),
        out_shape=jax.ShapeDtypeStruct((SEQ, D_MODEL), BF16),
        scratch_shapes=[pltpu.VMEM((tm, BRANCH), BF16)],
        compiler_params=_cparams(("parallel", "arbitrary")),
        name="branch_merge",
    )(hm, hy_t, s5y, w_a, w_b, w_glu, w_glu, rest, rest, rest)


def _layer_norm_rows(y, g, b):
    mu = jnp.mean(y, axis=1, keepdims=True)
    var = jnp.mean(jnp.square(y - mu), axis=1, keepdims=True)
    return (y - mu) * lax.rsqrt(var + LN_EPS) * g + b


def _res_ln_kernel(x_ref, f_ref, g_ref, b_ref, o_ref, ob_ref, *, transposed):
    f = f_ref[...]
    if transposed:
        f = f.T
    y = _layer_norm_rows(ALPHA * x_ref[...] + f, g_ref[...], b_ref[...])
    o_ref[...] = y
    ob_ref[...] = y.astype(BF16)


def _res_ln(x, f, g, b, l, *, transposed, tm=256):
    f_spec = (pl.BlockSpec((D_MODEL, tm), lambda i: (0, i)) if transposed
              else pl.BlockSpec((tm, D_MODEL), lambda i: (i, 0)))
    n_tok = x.shape[0]
    return pl.pallas_call(
        functools.partial(_res_ln_kernel, transposed=transposed),
        grid=(n_tok // tm,),
        in_specs=[pl.BlockSpec((tm, D_MODEL), lambda i: (i, 0)), f_spec,
                  pl.BlockSpec((None, 1, D_MODEL), lambda i: (l, 0, 0)),
                  pl.BlockSpec((None, 1, D_MODEL), lambda i: (l, 0, 0))],
        out_specs=[pl.BlockSpec((tm, D_MODEL), lambda i: (i, 0)),
                   pl.BlockSpec((tm, D_MODEL), lambda i: (i, 0))],
        out_shape=[jax.ShapeDtypeStruct((n_tok, D_MODEL), F32), jax.ShapeDtypeStruct((n_tok, D_MODEL), BF16)],
        compiler_params=_cparams(("parallel",)),
        name="residual_layernorm",
    )(x, f, g.reshape(DEPTH, 1, D_MODEL), b.reshape(DEPTH, 1, D_MODEL))


def _extract_sorted(s, n, out_ref, base):
    for r in range(n):
        m = jnp.max(s, axis=0, keepdims=True)
        out_ref[base + r:base + r + 1, :] = m
        s = jnp.where(s == m, -jnp.inf, s)


def _peer_select_kernel(q_ref, keys_ref, th_ref, s2_ref, e1_ref, e2_ref, top_sc, kth_sc):
    k = PEER_TOPK
    qb = q_ref[...].astype(BF16)
    half = PEER_KEYS
    for h in range(PEER_HEADS):
        sc = []
        for c in range(2):
            qs = qb[:, (2 * h + c) * half:(2 * h + c + 1) * half]
            s = lax.dot_general(keys_ref[h, c].astype(BF16), qs, (((1,), (1,)), ((), ())),
                                preferred_element_type=F32)
            sc.append(s)
            _extract_sorted(s, k, top_sc, c * k)
        a = top_sc[0:k, :]
        b = top_sc[k:2 * k, :]
        sub = lax.broadcasted_iota(jnp.int32, (8, a.shape[1]), 0)
        groups = [a[0:1] + b[0:8], a[0:1] + b[8:16], a[1:2] + b[0:8]]
        for i in range(2, 8):
            groups.append(jnp.where(sub < k // (i + 1), a[i:i + 1] + b[0:8], -jnp.inf))
        groups.append(a[8:16] + b[0:1])
        cand = jnp.concatenate(groups, axis=0)
        _extract_sorted(cand, k, kth_sc, 0)
        tau = kth_sc[k - 1:k, :]
        top = a[0:1] + b[0:1]
        z = jnp.sum(jnp.where(cand >= tau, jnp.exp(cand - top), 0.0), axis=0, keepdims=True)
        th = jnp.full_like(sc[0], jnp.inf)
        for j in range(k):
            bj = top_sc[k + j:k + j + 1, :]
            th = jnp.where(sc[0] + bj >= tau, bj, th)
        th_ref[h] = th
        s2_ref[h] = sc[1]
        e1_ref[h] = jnp.exp(sc[0] - a[0:1])
        e2_ref[h] = jnp.exp(sc[1] - b[0:1]) / z


def _peer_select(q, keys, l, *, tt=128):
    n_tok = q.shape[0]
    big = jax.ShapeDtypeStruct((PEER_HEADS, PEER_KEYS, n_tok), F32)
    big_spec = pl.BlockSpec((PEER_HEADS, PEER_KEYS, tt), lambda i: (0, 0, i))
    return pl.pallas_call(
        _peer_select_kernel,
        grid=(n_tok // tt,),
        in_specs=[pl.BlockSpec((tt, D_MODEL), lambda i: (i, 0)),
                  pl.BlockSpec((None, PEER_HEADS, 2, PEER_KEYS, PEER_KEYS), lambda i: (l, 0, 0, 0, 0))],
        out_specs=[big_spec, big_spec, big_spec, big_spec],
        out_shape=[big, big, big, big],
        scratch_shapes=[pltpu.VMEM((2 * PEER_TOPK, tt), F32), pltpu.VMEM((PEER_TOPK, tt), F32)],
        compiler_params=_cparams(("parallel",)),
        name="peer_select",
    )(q, keys)


def _peer_dense_kernel(x_ref, u_ref, vt_ref, th_ref, s2_ref, e1_ref, e2_ref, out_ref, act_sc, g_sc, *, rows):
    e = pl.program_id(1)

    @pl.when(e == 0)
    def _():
        out_ref[...] = jnp.zeros_like(out_ref)

    act_sc[...] = lax.dot_general(u_ref[...], x_ref[...], (((1,), (1,)), ((), ())),
                                  preferred_element_type=F32)
    kk = PEER_KEYS
    for tc in range(act_sc.shape[1] // kk):
        lanes = slice(tc * kk, (tc + 1) * kk)
        for i in range(rows):
            w = jnp.zeros((kk, kk), F32)
            for h in range(PEER_HEADS):
                hit = s2_ref[h, :, lanes] >= th_ref[h, i:i + 1, lanes]
                w = w + jnp.where(hit, e2_ref[h, :, lanes] * e1_ref[h, i:i + 1, lanes], 0.0)
            g = jax.nn.gelu(act_sc[i * kk:(i + 1) * kk, lanes]) * w
            g_sc[i * kk:(i + 1) * kk, lanes] = g.astype(BF16)
    out_ref[...] += jnp.dot(vt_ref[...], g_sc[...], preferred_element_type=F32)


def _peer_dense(x_bf, u_bf, vt_bf, sel, *, tt=512, rows=8):
    th, s2, e1, e2 = sel
    te = rows * PEER_KEYS
    n_tok = x_bf.shape[0]
    row_spec = pl.BlockSpec((PEER_HEADS, rows, tt), lambda i, e: (0, e, i))
    all_spec = pl.BlockSpec((PEER_HEADS, PEER_KEYS, tt), lambda i, e: (0, 0, i))
    return pl.pallas_call(
        functools.partial(_peer_dense_kernel, rows=rows),
        grid=(n_tok // tt, N_EXPERTS // te),
        in_specs=[pl.BlockSpec((tt, D_MODEL), lambda i, e: (i, 0)),
                  pl.BlockSpec((te, D_MODEL), lambda i, e: (e, 0)),
                  pl.BlockSpec((D_MODEL, te), lambda i, e: (0, e)),
                  row_spec, all_spec, row_spec, all_spec],
        out_specs=pl.BlockSpec((D_MODEL, tt), lambda i, e: (0, i)),
        out_shape=jax.ShapeDtypeStruct((D_MODEL, n_tok), F32),
        scratch_shapes=[pltpu.VMEM((te, tt), F32), pltpu.VMEM((te, tt), BF16)],
        compiler_params=_cparams(("parallel", "arbitrary")),
        name="peer_dense",
    )(x_bf, u_bf, vt_bf, th, s2, e1, e2)


def _hyena_features():
    pos = jnp.arange(SEQ, dtype=F32)
    t = pos / (SEQ - 1)
    bands = jnp.linspace(1e-4, 16 - 1, 16, dtype=F32)
    ang = 2.0 * math.pi * pos[:, None] * bands[None, :] / SEQ
    feat = jnp.concatenate([t[:, None], jnp.cos(ang), -jnp.sin(ang)], axis=-1)
    feat_t = jnp.pad(feat.T, ((0, HY_EMB_PAD - feat.shape[1]), (0, 0)))
    return feat_t, t[None, :]


def kernel(x, w_in, mlstm_gate_bias, mlstm_norm_gain, w_mlstm_out, hyena_conv_w, hyena_conv_b, hyena_w1,
           hyena_b1, hyena_w2, hyena_b2, hyena_freq, hyena_w3, hyena_decay, hyena_skip, w_hyena_out,
           s5_lambda_re, s5_lambda_im, s5_log_step, s5_b_re, s5_b_im, s5_c_re, s5_c_im, s5_skip, w_s5_glu,
           w_out, ln1_g, ln1_b, peer_w_q, peer_subkeys, peer_u, peer_v, ln2_g, ln2_b):
    xf = x.reshape(SEQ, D_MODEL)
    xb = xf.astype(BF16)
    feat_t, t_row = _hyena_features()
    w_gates = w_in[:, :, OFF_GATES:OFF_HYENA]
    w_hy_t = jnp.swapaxes(w_in[:, :, OFF_HYENA:OFF_S5], 1, 2)
    w_rest = w_in[:, :, OFF_HYENA:]
    for l in range(DEPTH):
        qkvo = _mm(xb, w_in, l, 0, 4 * BRANCH, tm=1024, tn=512, name="proj_qkvo")
        gates = _mm(xb, w_gates, l, 0, N_GATES, tm=1024, tn=N_GATES, name="proj_gates")
        hp_t = _mm_nt(w_hy_t, l, xb, tr=512, tm=1024, name="proj_hyena_t")
        rest = _mm(xb, w_rest, l, 3 * BRANCH, BRANCH + 3 * D_MODEL, tm=1024, tn=512, name="proj_s5_mixgates")

        hdir = _mlstm(qkvo, gates, mlstm_gate_bias, l)
        hm = _mlstm_post(hdir, qkvo, mlstm_norm_gain, l)
        hy_t = _hyena(hp_t, l, hyena_conv_w, hyena_conv_b, hyena_w1, hyena_b1, hyena_w2, hyena_b2,
                      hyena_freq, hyena_w3, hyena_decay, hyena_skip, feat_t, t_row)
        mats = _s5_chunk_matrices(s5_lambda_re[l], s5_lambda_im[l], s5_log_step[l], s5_b_re[l], s5_b_im[l],
                                  s5_c_re[l], s5_c_im[l])
        s5y = _s5(rest[:, :BRANCH], mats, s5_skip, l)
        merged = _merge(hm, hy_t, s5y, rest, w_mlstm_out, w_hyena_out, w_s5_glu, l)
        mix = _mm(merged, w_out, l, 0, D_MODEL, tm=1024, tn=512, name="proj_out")
        xf, xb = _res_ln(xf, mix, ln1_g, ln1_b, l, transposed=False)

        q = _mm(xb, peer_w_q, l, 0, PEER_HEADS * 2 * PEER_KEYS, tm=1024, tn=512, name="peer_query")
        sel = _peer_select(q, peer_subkeys, l)
        ffn_t = _peer_dense(xb, peer_u[l].astype(BF16), peer_v[l].T.astype(BF16), sel)
        xf, xb = _res_ln(xf, ffn_t, ln2_g, ln2_b, l, transposed=True)
    return xf.reshape(1, SEQ, D_MODEL)
```

```python
import functools
import math

import numpy as np
import jax
import jax.numpy as jnp
from jax import lax
from jax.experimental import pallas as pl
from jax.experimental.pallas import tpu as pltpu

F32 = jnp.float32
BF16 = jnp.bfloat16

D_MODEL = 2048
SEQ = 8192
DEPTH = 4
BRANCH = 1024
HEADS = 4
HEAD_DIM = 256
CHUNK = 128
N_GATES = 16
HY_HID = 64
HY_EMB_PAD = 128
S5_GROUP = 16
S5_GROUPS = 64
S5_STATE = 64
S5_T = 16
S5_CHUNKS = SEQ // S5_T
S5_LOG_CHUNKS = 9
PEER_HEADS = 8
PEER_KEYS = 128
PEER_TOPK = 16
N_EXPERTS = PEER_KEYS * PEER_KEYS
ALPHA = (2 * DEPTH) ** 0.25
LN_EPS = 1e-5
FFT_N = 2 * SEQ
FFT_R = 128
OFF_GATES = 4 * BRANCH
OFF_HYENA = OFF_GATES + N_GATES
OFF_S5 = OFF_HYENA + 3 * BRANCH
OFF_MIXG = OFF_S5 + BRANCH

VMEM_LIMIT = 56 * 1024 * 1024


def _cparams(sem):
    return pltpu.CompilerParams(dimension_semantics=sem, vmem_limit_bytes=VMEM_LIMIT)


def _split_bf16(a):
    hi = a.astype(BF16)
    lo = (a - hi.astype(F32)).astype(BF16)
    return hi, lo


def _dot3(a, b_hi, b_lo):
    a_hi, a_lo = _split_bf16(a)
    acc = jnp.dot(a_hi, b_hi, preferred_element_type=F32)
    acc += jnp.dot(a_lo, b_hi, preferred_element_type=F32)
    acc += jnp.dot(a_hi, b_lo, preferred_element_type=F32)
    return acc


def _dot3_both(a, b):
    b_hi, b_lo = _split_bf16(b)
    return _dot3(a, b_hi, b_lo)


def _mm_kernel(a_ref, b_ref, o_ref):
    o_ref[...] = jnp.dot(a_ref[...].astype(BF16), b_ref[...].astype(BF16),
                         preferred_element_type=F32).astype(o_ref.dtype)


def _mm(a, w, l, col_off, n_cols, *, tm, tn, name):
    m, k = a.shape
    cb = col_off // tn
    assert col_off % tn == 0 and n_cols % tn == 0 and m % tm == 0
    return pl.pallas_call(
        _mm_kernel,
        grid=(m // tm, n_cols // tn),
        in_specs=[pl.BlockSpec((tm, k), lambda i, j: (i, 0)),
                  pl.BlockSpec((None, k, tn), lambda i, j: (l, 0, cb + j))],
        out_specs=pl.BlockSpec((tm, tn), lambda i, j: (i, j)),
        out_shape=jax.ShapeDtypeStruct((m, n_cols), F32),
        compiler_params=_cparams(("parallel", "parallel")),
        name=name,
    )(a, w)


def _mm_nt_kernel(w_ref, a_ref, o_ref):
    o_ref[...] = lax.dot_general(w_ref[...].astype(BF16), a_ref[...].astype(BF16),
                                 (((1,), (1,)), ((), ())), preferred_element_type=F32)


def _mm_nt(wt, l, a, *, tr, tm, name):
    _, r, k = wt.shape
    m = a.shape[0]
    return pl.pallas_call(
        _mm_nt_kernel,
        grid=(r // tr, m // tm),
        in_specs=[pl.BlockSpec((None, tr, k), lambda i, j: (l, i, 0)),
                  pl.BlockSpec((tm, k), lambda i, j: (j, 0))],
        out_specs=pl.BlockSpec((tr, tm), lambda i, j: (i, j)),
        out_shape=jax.ShapeDtypeStruct((r, m), F32),
        compiler_params=_cparams(("parallel", "parallel")),
        name=name,
    )(wt, a)


def _log_sigmoid(x):
    return jnp.minimum(x, 0.0) - jnp.log1p(jnp.exp(-jnp.abs(x)))


def _mlstm_chunk(d, q, k, v, li_r, lf_r, li_c, lf_c, c_sc, n_sc, m_sc):
    n = CHUNK
    row = lax.broadcasted_iota(jnp.int32, (n, n), 0)
    col = lax.broadcasted_iota(jnp.int32, (n, n), 1)
    valid = (row >= col) if d == 0 else (row <= col)
    valid_t = (col >= row) if d == 0 else (col <= row)
    b_col = jnp.sum(jnp.where(valid, lf_r, 0.0), axis=1, keepdims=True)
    b_row = jnp.sum(jnp.where(valid_t, lf_c, 0.0), axis=0, keepdims=True)
    total = jnp.sum(lf_r, axis=1, keepdims=True)

    m_prev = m_sc[...]
    m_inter = b_col + m_prev
    log_d = jnp.where(valid, b_col - b_row + li_r, -jnp.inf)
    m_t = jnp.maximum(m_inter, jnp.max(log_d, axis=1, keepdims=True))
    inter = jnp.exp(m_inter - m_t)
    dmat = jnp.exp(log_d - m_t)

    k = k * (HEAD_DIM ** -0.5)
    qb = q.astype(BF16)
    kb = k.astype(BF16)
    vb = v.astype(BF16)
    s = lax.dot_general(qb, kb, (((1,), (1,)), ((), ())), preferred_element_type=F32) * dmat
    num = jnp.dot(s.astype(BF16), vb, preferred_element_type=F32)
    num += inter * jnp.dot(qb, c_sc[...].astype(BF16), preferred_element_type=F32)
    qn = jnp.sum(q * n_sc[...], axis=1, keepdims=True)
    den = jnp.sum(s, axis=1, keepdims=True) + inter * qn
    h_out = num / jnp.maximum(jnp.abs(den), jnp.exp(-m_t))

    log_w = total - b_col + li_c
    m_new = jnp.maximum(total + m_prev, jnp.max(log_w, axis=0, keepdims=True))
    w = jnp.exp(log_w - m_new)
    decay = jnp.exp(total + m_prev - m_new)
    kw = k * w
    c_sc[...] = decay * c_sc[...] + lax.dot_general(
        kw.astype(BF16), vb, (((0,), (0,)), ((), ())), preferred_element_type=F32)
    n_sc[...] = decay * n_sc[...] + jnp.sum(kw, axis=0, keepdims=True)
    m_sc[...] = m_new
    return h_out


def _mlstm_kernel(qkv_f_ref, qkv_b_ref, gr_f_ref, gr_b_ref, gc_f_ref, gc_b_ref, bias_ref,
                  o_f_ref, o_b_ref, c_sc, n_sc, m_sc):
    @pl.when(pl.program_id(0) == 0)
    def _():
        c_sc[...] = jnp.zeros_like(c_sc)
        n_sc[...] = jnp.zeros_like(n_sc)
        m_sc[...] = jnp.zeros_like(m_sc)

    for d, (qkv_ref, gr_ref, gc_ref, o_ref) in enumerate(((qkv_f_ref, gr_f_ref, gc_f_ref, o_f_ref),
                                                         (qkv_b_ref, gr_b_ref, gc_b_ref, o_b_ref))):
        for h in range(HEADS):
            cols = slice(h * HEAD_DIM, (h + 1) * HEAD_DIM)
            b_i = bias_ref[d * 2 * HEADS + h]
            b_f = bias_ref[d * 2 * HEADS + HEADS + h]
            chain = d * HEADS + h
            o_ref[:, cols] = _mlstm_chunk(
                d,
                qkv_ref[:, cols],
                qkv_ref[:, BRANCH + h * HEAD_DIM:BRANCH + (h + 1) * HEAD_DIM],
                qkv_ref[:, 2 * BRANCH + h * HEAD_DIM:2 * BRANCH + (h + 1) * HEAD_DIM],
                gr_ref[h] + b_i, _log_sigmoid(gr_ref[HEADS + h] + b_f),
                gc_ref[h] + b_i, _log_sigmoid(gc_ref[HEADS + h] + b_f),
                c_sc.at[chain], n_sc.at[chain], m_sc.at[chain])


def _mlstm(qkvo, gates, gate_bias, l):
    nc = SEQ // CHUNK
    g_rows = gates.T.reshape(N_GATES, 1, SEQ)
    g_cols = gates.T.reshape(N_GATES, SEQ, 1)
    bias = gate_bias.reshape(DEPTH, N_GATES, 1, 1)
    half = N_GATES // 2
    chains = 2 * HEADS
    out = jax.ShapeDtypeStruct((SEQ, BRANCH), F32)
    return pl.pallas_call(
        _mlstm_kernel,
        grid=(nc,),
        in_specs=[
            pl.BlockSpec((CHUNK, 4 * BRANCH), lambda c: (c, 0)),
            pl.BlockSpec((CHUNK, 4 * BRANCH), lambda c: (nc - 1 - c, 0)),
            pl.BlockSpec((half, 1, CHUNK), lambda c: (0, 0, c)),
            pl.BlockSpec((half, 1, CHUNK), lambda c: (1, 0, nc - 1 - c)),
            pl.BlockSpec((half, CHUNK, 1), lambda c: (0, c, 0)),
            pl.BlockSpec((half, CHUNK, 1), lambda c: (1, nc - 1 - c, 0)),
            pl.BlockSpec((None, N_GATES, 1, 1), lambda c: (l, 0, 0, 0)),
        ],
        out_specs=[pl.BlockSpec((CHUNK, BRANCH), lambda c: (c, 0)),
                   pl.BlockSpec((CHUNK, BRANCH), lambda c: (nc - 1 - c, 0))],
        out_shape=[out, out],
        scratch_shapes=[pltpu.VMEM((chains, HEAD_DIM, HEAD_DIM), F32),
                        pltpu.VMEM((chains, 1, HEAD_DIM), F32),
                        pltpu.VMEM((chains, 1, 1), F32)],
        compiler_params=_cparams(("arbitrary",)),
        name="mlstm_scan",
    )(qkvo, qkvo, g_rows, g_rows, g_cols, g_cols, bias)


def _mlstm_post_kernel(hf_ref, hb_ref, o_ref, gain_ref, out_ref):
    h = hf_ref[...] + hb_ref[...]
    h = jax.nn.sigmoid(o_ref[...]) * h
    for hd in range(HEADS):
        sl = slice(hd * HEAD_DIM, (hd + 1) * HEAD_DIM)
        hh = h[:, sl]
        mu = jnp.mean(hh, axis=1, keepdims=True)
        var = jnp.mean(jnp.square(hh - mu), axis=1, keepdims=True)
        out_ref[:, sl] = ((hh - mu) * lax.rsqrt(var + LN_EPS) * gain_ref[:, sl]).astype(out_ref.dtype)


def _mlstm_post(hdir, qkvo, gain, l, *, tm=512):
    return pl.pallas_call(
        _mlstm_post_kernel,
        grid=(SEQ // tm,),
        in_specs=[pl.BlockSpec((tm, BRANCH), lambda i: (i, 0)),
                  pl.BlockSpec((tm, BRANCH), lambda i: (i, 0)),
                  pl.BlockSpec((tm, BRANCH), lambda i: (i, 3)),
                  pl.BlockSpec((None, 1, BRANCH), lambda i: (l, 0, 0))],
        out_specs=pl.BlockSpec((tm, BRANCH), lambda i: (i, 0)),
        out_shape=jax.ShapeDtypeStruct((SEQ, BRANCH), BF16),
        compiler_params=_cparams(("parallel",)),
        name="mlstm_post",
    )(hdir[0], hdir[1], qkvo, gain.reshape(DEPTH, 1, BRANCH))


HY_CT = 32
HY_RB = 512


@functools.lru_cache(maxsize=None)
def _fft_constants():
    r = FFT_R
    idx = np.arange(r)
    th = 2.0 * np.pi * np.outer(idx, idx) / r
    cs, sn = np.cos(th), np.sin(th)
    ph = 2.0 * np.pi * np.outer(idx, idx) / FFT_N
    f1 = np.concatenate([cs, -sn], axis=1)
    f1[r // 2:] = 0.0
    m3 = np.block([[cs, -sn], [sn, cs]])
    m4 = np.block([[cs, sn], [-sn, cs]])
    m6 = np.concatenate([cs, -sn], axis=0) / FFT_N
    m6[:, r // 2:] = 0.0
    consts = {name: jnp.asarray(mat, F32).astype(BF16)
              for name, mat in (("f1", f1), ("m3", m3), ("m4", m4), ("m6", m6))}
    consts["tw"] = (jnp.asarray(np.cos(ph), F32), jnp.asarray(np.sin(ph), F32))
    return consts


def _rows(i, n):
    return pl.ds(pl.multiple_of(i * n, n), n)


def _for_each(n, body, unroll=1):
    def step(i, carry):
        body(i)
        return carry

    lax.fori_loop(0, n, step, 0, unroll=unroll)


def _fwd_stage1(sig_ref, ct, t_buf, a_buf, b_buf, f1_ref, twr_ref, twi_ref):
    r = FFT_R
    zeros = jnp.zeros((r // 2, r), F32)

    def tr_in(c):
        t_buf[_rows(c, r), :] = jnp.concatenate([sig_ref[c], zeros], axis=0).T.astype(BF16)

    _for_each(ct, tr_in, unroll=4)

    def stage1(b):
        rows = _rows(b, HY_RB)
        a_buf[rows, :] = jnp.dot(t_buf[rows, :], f1_ref[...], preferred_element_type=F32)

    _for_each(ct * r // HY_RB, stage1)

    def twiddle(c):
        rows = _rows(c, r)
        a = a_buf[rows, :]
        ar, ai = a[:, :r], a[:, r:]
        twr, twi = twr_ref[...], twi_ref[...]
        b_buf[rows, :r] = (ar * twr + ai * twi).T.astype(BF16)
        b_buf[rows, r:] = (ai * twr - ar * twi).T.astype(BF16)

    _for_each(ct, twiddle, unroll=4)


def _hy_mlp_kernel(feat_ref, w1_ref, b1_ref, w2_ref, b2_ref, fr_ref, out_ref):
    h = _dot3_both(w1_ref[...], feat_ref[...]) + b1_ref[...]
    h = jnp.sin(fr_ref[0] * h)
    h = _dot3_both(w2_ref[...], h) + b2_ref[...]
    out_ref[...] = jnp.sin(fr_ref[1] * h)


def _hy_filter_kernel(w3_ref, h_ref, dec_ref, t_ref, out_ref):
    d = pl.program_id(0)
    filt = _dot3_both(w3_ref[...], h_ref[...])
    filt = filt * jnp.exp(-t_ref[...] * jnp.abs(dec_ref[...]))
    filt = filt / (jnp.sum(jnp.abs(filt), axis=1, keepdims=True) + 1e-6)
    lane = lax.broadcasted_iota(jnp.int32, filt.shape, 1)
    out_ref[...] = jnp.where((lane == 0) & (d == 1), 0.0, filt)


def _hy_spec_kernel(fa_ref, fb_ref, f1_ref, m3_ref, twr_ref, twi_ref, out_ref, t_buf, a_buf, b_buf, *, ct):
    r = FFT_R
    nblk = ct * r // HY_RB
    _fwd_stage1(fa_ref, ct, t_buf, a_buf, b_buf, f1_ref, twr_ref, twi_ref)

    def put(b):
        rows = _rows(b, HY_RB)
        out_ref[rows, :] = jnp.dot(b_buf[rows, :], m3_ref[...], preferred_element_type=F32)

    _for_each(nblk, put)
    _fwd_stage1(fb_ref, ct, t_buf, a_buf, b_buf, f1_ref, twr_ref, twi_ref)

    def add_conj(b):
        rows = _rows(b, HY_RB)
        sb = jnp.dot(b_buf[rows, :], m3_ref[...], preferred_element_type=F32)
        sa = out_ref[rows, :]
        out_ref[rows, :] = jnp.concatenate([sa[:, :r] + sb[:, :r], sa[:, r:] - sb[:, r:]], axis=1)

    _for_each(nblk, add_conj)


def _shift_prev(p, n1, n2):
    r1 = pltpu.roll(p, 1, 2)
    r2 = pltpu.roll(r1, 1, 1)
    return jnp.where(n2 == 0, jnp.where(n1 == 0, 0.0, r2), r1)


def _shift_next(p, n1, n2):
    s1, s2 = p.shape[1], p.shape[2]
    r1 = pltpu.roll(p, s2 - 1, 2)
    r2 = pltpu.roll(r1, s1 - 1, 1)
    return jnp.where(n2 == s2 - 1, jnp.where(n1 == s1 - 1, 0.0, r2), r1)


def _hy_conv_kernel(p0_ref, p1_ref, p2_ref, cw_ref, cb_ref, skip_ref, h_ref,
                    f1_ref, m3_ref, m4_ref, m6_ref, twr_ref, twi_ref,
                    out_ref, z_buf, t_buf, a_buf, b_buf, *, ct):
    r = FFT_R
    n1_len = SEQ // r
    nblk = ct * r // HY_RB
    shp = (ct, n1_len, r)
    n1 = lax.broadcasted_iota(jnp.int32, shp, 1)
    n2 = lax.broadcasted_iota(jnp.int32, shp, 2)

    def short_conv(p_ref, g):
        p = p_ref[...]
        return (_shift_prev(p, n1, n2) * cw_ref[3 * g] + p * cw_ref[3 * g + 1]
                + _shift_next(p, n1, n2) * cw_ref[3 * g + 2] + cb_ref[g])

    z_buf[...] = short_conv(p1_ref, 1) * short_conv(p2_ref, 2)
    _fwd_stage1(z_buf, ct, t_buf, a_buf, b_buf, f1_ref, twr_ref, twi_ref)

    def freq_domain(b):
        rows = _rows(b, HY_RB)
        x = jnp.dot(b_buf[rows, :], m3_ref[...], preferred_element_type=F32)
        h = h_ref[rows, :]
        xr, xi = x[:, :r], x[:, r:]
        hr, hi = h[:, :r], h[:, r:]
        y = jnp.concatenate([xr * hr - xi * hi, xr * hi + xi * hr], axis=1).astype(BF16)
        a_buf[rows, :] = jnp.dot(y, m4_ref[...], preferred_element_type=F32)

    _for_each(nblk, freq_domain)

    def inv_twiddle(c):
        rows = _rows(c, r)
        b = a_buf[rows, :]
        br, bi = b[:, :r], b[:, r:]
        twr, twi = twr_ref[...], twi_ref[...]
        b_buf[rows, :r] = (br * twr - bi * twi).T.astype(BF16)
        b_buf[rows, r:] = (br * twi + bi * twr).T.astype(BF16)

    _for_each(ct, inv_twiddle, unroll=4)

    def stage6(b):
        rows = _rows(b, HY_RB)
        a_buf[rows, :r] = jnp.dot(b_buf[rows, :], m6_ref[...], preferred_element_type=F32)

    _for_each(nblk, stage6)

    def finish(c):
        y_c = a_buf[_rows(c, r), :r].T[:n1_len, :]
        out_ref[c] = y_c + skip_ref[c] * z_buf[c]

    _for_each(ct, finish, unroll=4)
    out_ref[...] = short_conv(p0_ref, 0) * out_ref[...]


def _hyena(hp_t, l, conv_w, conv_b, w1, b1, w2, b2, freq, w3, decay, skip, feat_t, t_row):
    r = FFT_R
    n1 = SEQ // r
    ct = HY_CT
    k = _fft_constants()
    f1, m3, m4, m6 = k["f1"], k["m3"], k["m4"], k["m6"]
    twr, twi = k["tw"]

    def full(a):
        return pl.BlockSpec(a.shape, lambda *_: (0,) * a.ndim)

    h2 = pl.pallas_call(
        _hy_mlp_kernel,
        grid=(1,),
        in_specs=[full(feat_t), pl.BlockSpec((None, HY_HID, feat_t.shape[0]), lambda i: (l, 0, 0)),
                  pl.BlockSpec((None, HY_HID, 1), lambda i: (l, 0, 0)),
                  pl.BlockSpec((None, HY_HID, HY_HID), lambda i: (l, 0, 0)),
                  pl.BlockSpec((None, HY_HID, 1), lambda i: (l, 0, 0)),
                  pl.BlockSpec((None, 2, HY_HID, 1), lambda i: (l, 0, 0, 0))],
        out_specs=pl.BlockSpec((HY_HID, SEQ), lambda i: (0, 0)),
        out_shape=jax.ShapeDtypeStruct((HY_HID, SEQ), F32),
        compiler_params=pltpu.CompilerParams(vmem_limit_bytes=VMEM_LIMIT),
        name="hyena_filter_mlp",
    )(feat_t, jnp.pad(jnp.swapaxes(w1, 1, 2), ((0, 0), (0, 0), (0, feat_t.shape[0] - w1.shape[1]))),
      b1[:, :, None], jnp.swapaxes(w2, 1, 2), b2[:, :, None], freq[:, :, :, None])

    rt = 128
    filt = pl.pallas_call(
        _hy_filter_kernel,
        grid=(2, BRANCH // rt),
        in_specs=[pl.BlockSpec((None, rt, HY_HID), lambda d, i: (l, d * (BRANCH // rt) + i, 0)),
                  pl.BlockSpec((HY_HID, SEQ), lambda d, i: (0, 0)),
                  pl.BlockSpec((None, None, rt, 1), lambda d, i: (l, d, i, 0)),
                  pl.BlockSpec((1, SEQ), lambda d, i: (0, 0))],
        out_specs=pl.BlockSpec((None, rt, SEQ), lambda d, i: (d, i, 0)),
        out_shape=jax.ShapeDtypeStruct((2, BRANCH, SEQ), F32),
        compiler_params=_cparams(("parallel", "parallel")),
        name="hyena_filter",
    )(jnp.swapaxes(w3, 1, 2), h2, decay[:, :, :, None], t_row)
    filt = filt.reshape(2, BRANCH, n1, r)

    scratch = [pltpu.VMEM((ct * r, r), BF16), pltpu.VMEM((ct * r, 2 * r), F32), pltpu.VMEM((ct * r, 2 * r), BF16)]
    spec = pl.pallas_call(
        functools.partial(_hy_spec_kernel, ct=ct),
        grid=(BRANCH // ct,),
        in_specs=[pl.BlockSpec((None, ct, n1, r), lambda i: (0, i, 0, 0)),
                  pl.BlockSpec((None, ct, n1, r), lambda i: (1, i, 0, 0)),
                  full(f1), full(m3), full(twr), full(twi)],
        out_specs=pl.BlockSpec((ct * r, 2 * r), lambda i: (i, 0)),
        out_shape=jax.ShapeDtypeStruct((BRANCH * r, 2 * r), F32),
        scratch_shapes=scratch,
        compiler_params=_cparams(("parallel",)),
        name="hyena_filter_spectrum",
    )(filt, filt, f1, m3, twr, twi)

    hp3 = hp_t.reshape(3 * BRANCH, n1, r)
    nb = BRANCH // ct
    cw = conv_w.reshape(DEPTH, 3, 3, BRANCH).transpose(0, 2, 1, 3).reshape(DEPTH, 9, BRANCH, 1, 1)
    cb = conv_b.reshape(DEPTH, 3, BRANCH, 1, 1)
    consts = (f1, m3, m4, m6, twr, twi)
    return pl.pallas_call(
        functools.partial(_hy_conv_kernel, ct=ct),
        grid=(nb,),
        in_specs=[pl.BlockSpec((ct, n1, r), lambda i: (i, 0, 0)),
                  pl.BlockSpec((ct, n1, r), lambda i: (nb + i, 0, 0)),
                  pl.BlockSpec((ct, n1, r), lambda i: (2 * nb + i, 0, 0)),
                  pl.BlockSpec((None, 9, ct, 1, 1), lambda i: (l, 0, i, 0, 0)),
                  pl.BlockSpec((None, 3, ct, 1, 1), lambda i: (l, 0, i, 0, 0)),
                  pl.BlockSpec((None, ct, 1, 1), lambda i: (l, i, 0, 0)),
                  pl.BlockSpec((ct * r, 2 * r), lambda i: (i, 0)),
                  *[full(a) for a in consts]],
        out_specs=pl.BlockSpec((ct, n1, r), lambda i: (i, 0, 0)),
        out_shape=jax.ShapeDtypeStruct((BRANCH, n1, r), F32),
        scratch_shapes=[pltpu.VMEM((ct, n1, r), F32)] + scratch,
        compiler_params=_cparams(("parallel",)),
        name="hyena_conv",
    )(hp3, hp3, hp3, cw, cb, skip.reshape(DEPTH, BRANCH, 1, 1), spec, *consts).reshape(BRANCH, SEQ)


def _s5_chunk_matrices(lam_re, lam_im, log_step, b_re, b_im, c_re, c_im):
    t = S5_T
    lam = lax.complex(lam_re, lam_im)
    step = jnp.exp(log_step)[..., None]
    lam_dt = lam * step
    b_bar = ((jnp.exp(lam_dt) - 1.0) / lam)[..., None] * lax.complex(b_re, b_im)
    c_mat = lax.complex(c_re, c_im)
    pw = jnp.exp(lam_dt[None] * jnp.arange(t + 1, dtype=F32)[:, None, None, None])
    kern = jnp.einsum("dgvp,mdgp,dgpn->mdgvn", c_mat, pw[:t], b_bar, precision="highest").real
    zero = jnp.zeros_like(kern[0, 0])

    def lag_block(tau_out, j_in):
        fwd = kern[tau_out - j_in, 0] if tau_out >= j_in else zero
        bwd = kern[j_in - tau_out, 1] if j_in >= tau_out else zero
        return fwd + bwd

    kfb = jnp.stack([jnp.stack([lag_block(a, j) for j in range(t)]) for a in range(t)])
    w_intra = jnp.transpose(kfb, (2, 1, 4, 0, 3)).reshape(S5_GROUPS, t * S5_GROUP, t * S5_GROUP)
    in_f = pw[:t, 0][::-1][..., None] * b_bar[0][None]
    in_b = pw[:t, 1][..., None] * b_bar[1][None]

    def in_mat(m):
        m = jnp.transpose(m, (1, 0, 3, 2)).reshape(S5_GROUPS, t * S5_GROUP, S5_STATE)
        return jnp.concatenate([m.real, m.imag], axis=-1)

    w_in = jnp.concatenate([in_mat(in_f), in_mat(in_b)], axis=-1)
    out_f = c_mat[0][None] * pw[1:, 0][:, :, None, :]
    out_b = c_mat[1][None] * pw[1:, 1][::-1][:, :, None, :]

    def out_mat(m):
        m = jnp.transpose(m, (1, 3, 0, 2)).reshape(S5_GROUPS, S5_STATE, t * S5_GROUP)
        return jnp.concatenate([m.real, -m.imag], axis=1)

    w_out = jnp.concatenate([out_mat(out_f), out_mat(out_b)], axis=1)
    a_pow = jnp.exp(lam_dt[None] * (t * 2.0 ** jnp.arange(S5_LOG_CHUNKS, dtype=F32))[:, None, None, None])
    a_pow = jnp.transpose(a_pow, (2, 1, 0, 3))
    a_re = jnp.concatenate([a_pow.real, a_pow.real], axis=-1)
    a_im = jnp.concatenate([-a_pow.imag, a_pow.imag], axis=-1)
    return w_intra, w_in, w_out, a_re, a_im


def _s5_kernel(u_ref, wi_ref, win_ref, wout_ref, are_ref, aim_ref, skip_ref, y_ref):
    nc = S5_CHUNKS
    half = S5_STATE
    u = u_ref[...]
    ub = u.astype(BF16)
    y = jnp.dot(ub, wi_ref[...].astype(BF16), preferred_element_type=F32) + skip_ref[...] * u
    s_in = jnp.dot(ub, win_ref[...].astype(BF16), preferred_element_type=F32)
    row = lax.broadcasted_iota(jnp.int32, (nc, 2 * half), 0)

    def scan(s, d):
        for step in range(S5_LOG_CHUNKS):
            sh = 1 << step
            if d == 0:
                moved = jnp.where(row >= sh, pltpu.roll(s, sh, 0), 0.0)
            else:
                moved = jnp.where(row < nc - sh, pltpu.roll(s, nc - sh, 0), 0.0)
            swapped = pltpu.roll(moved, half, 1)
            s = s + are_ref[d, step:step + 1, :] * moved + aim_ref[d, step:step + 1, :] * swapped
        if d == 0:
            return jnp.where(row >= 1, pltpu.roll(s, 1, 0), 0.0)
        return jnp.where(row < nc - 1, pltpu.roll(s, nc - 1, 0), 0.0)

    e = jnp.concatenate([scan(s_in[:, :2 * half], 0), scan(s_in[:, 2 * half:], 1)], axis=1)
    y_ref[...] = y + jnp.dot(e.astype(BF16), wout_ref[...].astype(BF16), preferred_element_type=F32)


def _s5(su, mats, skip, l):
    w_intra, w_in, w_out, a_re, a_im = mats
    t = S5_T
    gw = t * S5_GROUP
    u = su.reshape(S5_CHUNKS, t, S5_GROUPS, S5_GROUP).transpose(2, 0, 1, 3).reshape(S5_GROUPS, S5_CHUNKS, gw)
    skip_t = jnp.tile(skip.reshape(DEPTH, S5_GROUPS, 1, S5_GROUP), (1, 1, t, 1)).reshape(DEPTH, S5_GROUPS, 1, gw)
    y = pl.pallas_call(
        _s5_kernel,
        grid=(S5_GROUPS,),
        in_specs=[pl.BlockSpec((None, S5_CHUNKS, gw), lambda g: (g, 0, 0)),
                  pl.BlockSpec((None, gw, gw), lambda g: (g, 0, 0)),
                  pl.BlockSpec((None, gw, 4 * S5_STATE), lambda g: (g, 0, 0)),
                  pl.BlockSpec((None, 4 * S5_STATE, gw), lambda g: (g, 0, 0)),
                  pl.BlockSpec((None, 2, S5_LOG_CHUNKS, 2 * S5_STATE), lambda g: (g, 0, 0, 0)),
                  pl.BlockSpec((None, 2, S5_LOG_CHUNKS, 2 * S5_STATE), lambda g: (g, 0, 0, 0)),
                  pl.BlockSpec((None, None, 1, gw), lambda g: (l, g, 0, 0))],
        out_specs=pl.BlockSpec((None, S5_CHUNKS, gw), lambda g: (g, 0, 0)),
        out_shape=jax.ShapeDtypeStruct((S5_GROUPS, S5_CHUNKS, gw), F32),
        compiler_params=_cparams(("parallel",)),
        name="s5_scan",
    )(u, w_intra, w_in, w_out, a_re, a_im, skip_t)
    return y.reshape(S5_GROUPS, S5_CHUNKS, t, S5_GROUP).transpose(1, 2, 0, 3).reshape(SEQ, BRANCH)


def _merge_kernel(hm_ref, hyt_ref, s5_ref, wa_ref, wb_ref, wga_ref, wgg_ref, g0_ref, g1_ref, g2_ref,
                  out_ref, hy_sc):
    @pl.when(pl.program_id(1) == 0)
    def _():
        hy_sc[...] = hyt_ref[...].T.astype(BF16)

    def proj(a, w_ref):
        return jnp.dot(a, w_ref[...].astype(BF16), preferred_element_type=F32)

    s5b = s5_ref[...].astype(BF16)
    out_a = proj(hm_ref[...], wa_ref)
    out_b = proj(hy_sc[...], wb_ref)
    out_c = proj(s5b, wga_ref) * jax.nn.sigmoid(proj(s5b, wgg_ref))
    merged = (jax.nn.sigmoid(g0_ref[...]) * out_a + jax.nn.sigmoid(g1_ref[...]) * out_b
              + jax.nn.sigmoid(g2_ref[...]) * out_c)
    out_ref[...] = merged.astype(out_ref.dtype)


def _merge(hm, hy_t, s5y, rest, w_a, w_b, w_glu, l, *, tm=512, tn=512):
    nj = D_MODEL // tn
    goff = BRANCH // tn
    return pl.pallas_call(
        _merge_kernel,
        grid=(SEQ // tm, nj),
        in_specs=[pl.BlockSpec((tm, BRANCH), lambda i, j: (i, 0)),
                  pl.BlockSpec((BRANCH, tm), lambda i, j: (0, i)),
                  pl.BlockSpec((tm, BRANCH), lambda i, j: (i, 0)),
                  pl.BlockSpec((None, BRANCH, tn), lambda i, j: (l, 0, j)),
                  pl.BlockSpec((None, BRANCH, tn), lambda i, j: (l, 0, j)),
                  pl.BlockSpec((None, BRANCH, tn), lambda i, j: (l, 0, j)),
                  pl.BlockSpec((None, BRANCH, tn), lambda i, j: (l, 0, nj + j)),
                  pl.BlockSpec((tm, tn), lambda i, j: (i, goff + j)),
                  pl.BlockSpec((tm, tn), lambda i, j: (i, goff + nj + j)),
                  pl.BlockSpec((tm, tn), lambda i, j: (i, goff + 2 * nj + j))],
        out_specs=pl.BlockSpec((tm, tn), lambda i, j: (i, j)),
        out_shape=jax.ShapeDtypeStruct((SEQ, D_MODEL), BF16),
        scratch_shapes=[pltpu.VMEM((tm, BRANCH), BF16)],
        compiler_params=_cparams(("parallel", "arbitrary")),
        name="branch_merge",
    )(hm, hy_t, s5y, w_a, w_b, w_glu, w_glu, rest, rest, rest)


def _layer_norm_rows(y, g, b):
    mu = jnp.mean(y, axis=1, keepdims=True)
    var = jnp.mean(jnp.square(y - mu), axis=1, keepdims=True)
    return (y - mu) * lax.rsqrt(var + LN_EPS) * g + b


def _res_ln_kernel(x_ref, f_ref, g_ref, b_ref, o_ref, ob_ref, *, transposed):
    f = f_ref[...]
    if transposed:
        f = f.T
    y = _layer_norm_rows(ALPHA * x_ref[...] + f, g_ref[...], b_ref[...])
    o_ref[...] = y
    ob_ref[...] = y.astype(BF16)


def _res_ln(x, f, g, b, l, *, transposed, tm=256):
    f_spec = (pl.BlockSpec((D_MODEL, tm), lambda i: (0, i)) if transposed
              else pl.BlockSpec((tm, D_MODEL), lambda i: (i, 0)))
    n_tok = x.shape[0]
    return pl.pallas_call(
        functools.partial(_res_ln_kernel, transposed=transposed),
        grid=(n_tok // tm,),
        in_specs=[pl.BlockSpec((tm, D_MODEL), lambda i: (i, 0)), f_spec,
                  pl.BlockSpec((None, 1, D_MODEL), lambda i: (l, 0, 0)),
                  pl.BlockSpec((None, 1, D_MODEL), lambda i: (l, 0, 0))],
        out_specs=[pl.BlockSpec((tm, D_MODEL), lambda i: (i, 0)),
                   pl.BlockSpec((tm, D_MODEL), lambda i: (i, 0))],
        out_shape=[jax.ShapeDtypeStruct((n_tok, D_MODEL), F32), jax.ShapeDtypeStruct((n_tok, D_MODEL), BF16)],
        compiler_params=_cparams(("parallel",)),
        name="residual_layernorm",
    )(x, f, g.reshape(DEPTH, 1, D_MODEL), b.reshape(DEPTH, 1, D_MODEL))


PEER_NO_RANK = 64.0


def _extract_sorted(s, n, out_ref, base, want_rank=False):
    rank = jnp.full_like(s, PEER_NO_RANK) if want_rank else None
    for r in range(n):
        m = jnp.max(s, axis=0, keepdims=True)
        out_ref[base + r:base + r + 1, :] = m
        hit = s == m
        if want_rank:
            rank = jnp.where(hit, float(r), rank)
        s = jnp.where(hit, -jnp.inf, s)
    return rank


def _peer_select_kernel(q_ref, keys_ref, cnt_ref, r2_ref, e1_ref, e2_ref, top_sc, kth_sc):
    k = PEER_TOPK
    qb = q_ref[...].astype(BF16)
    half = PEER_KEYS
    for h in range(PEER_HEADS):
        sc = []
        rank2 = None
        for c in range(2):
            qs = qb[:, (2 * h + c) * half:(2 * h + c + 1) * half]
            s = lax.dot_general(keys_ref[h, c].astype(BF16), qs, (((1,), (1,)), ((), ())),
                                preferred_element_type=F32)
            sc.append(s)
            rank2 = _extract_sorted(s, k, top_sc, c * k, want_rank=(c == 1))
        a = top_sc[0:k, :]
        b = top_sc[k:2 * k, :]
        sub = lax.broadcasted_iota(jnp.int32, (8, a.shape[1]), 0)
        groups = [a[0:1] + b[0:8], a[0:1] + b[8:16], a[1:2] + b[0:8]]
        for i in range(2, 8):
            groups.append(jnp.where(sub < k // (i + 1), a[i:i + 1] + b[0:8], -jnp.inf))
        groups.append(a[8:16] + b[0:1])
        cand = jnp.concatenate(groups, axis=0)
        _extract_sorted(cand, k, kth_sc, 0)
        tau = kth_sc[k - 1:k, :]
        top = a[0:1] + b[0:1]
        z = jnp.sum(jnp.where(cand >= tau, jnp.exp(cand - top), 0.0), axis=0, keepdims=True)
        cnt = jnp.zeros_like(sc[0])
        for j in range(k):
            bj = top_sc[k + j:k + j + 1, :]
            cnt = jnp.where(sc[0] + bj >= tau, j + 1.0, cnt)
        cnt_ref[h] = cnt
        r2_ref[h] = rank2.astype(BF16)
        e1_ref[h] = jnp.exp(sc[0] - a[0:1])
        e2_ref[h] = (jnp.exp(sc[1] - b[0:1]) / z).astype(BF16)


def _peer_select(q, keys, l, *, tt=128):
    n_tok = q.shape[0]
    big = jax.ShapeDtypeStruct((PEER_HEADS, PEER_KEYS, n_tok), F32)
    big_bf = jax.ShapeDtypeStruct((PEER_HEADS, PEER_KEYS, n_tok), BF16)
    big_spec = pl.BlockSpec((PEER_HEADS, PEER_KEYS, tt), lambda i: (0, 0, i))
    return pl.pallas_call(
        _peer_select_kernel,
        grid=(n_tok // tt,),
        in_specs=[pl.BlockSpec((tt, D_MODEL), lambda i: (i, 0)),
                  pl.BlockSpec((None, PEER_HEADS, 2, PEER_KEYS, PEER_KEYS), lambda i: (l, 0, 0, 0, 0))],
        out_specs=[big_spec, big_spec, big_spec, big_spec],
        out_shape=[big, big_bf, big, big_bf],
        scratch_shapes=[pltpu.VMEM((2 * PEER_TOPK, tt), F32), pltpu.VMEM((PEER_TOPK, tt), F32)],
        compiler_params=_cparams(("parallel",)),
        name="peer_select",
    )(q, keys)


def _peer_dense_kernel(x_ref, u_ref, vt_ref, cnt_ref, r2_ref, e1_ref, e2_ref, out_ref, act_sc, g_sc, *, rows):
    e = pl.program_id(1)

    @pl.when(e == 0)
    def _():
        out_ref[...] = jnp.zeros_like(out_ref)

    act_sc[...] = lax.dot_general(u_ref[...], x_ref[...], (((1,), (1,)), ((), ())),
                                  preferred_element_type=F32)
    kk = PEER_KEYS
    for tc in range(act_sc.shape[1] // kk):
        lanes = slice(tc * kk, (tc + 1) * kk)
        for i in range(rows):
            w = jnp.zeros((kk, kk), BF16)
            for h in range(PEER_HEADS):
                hit = r2_ref[h, :, lanes] < cnt_ref[h, i:i + 1, lanes].astype(BF16)
                val = e2_ref[h, :, lanes] * e1_ref[h, i:i + 1, lanes].astype(BF16)
                w = w + jnp.where(hit, val, jnp.zeros_like(val))
            g = jax.nn.gelu(act_sc[i * kk:(i + 1) * kk, lanes]) * w.astype(F32)
            g_sc[i * kk:(i + 1) * kk, lanes] = g.astype(BF16)
    out_ref[...] += jnp.dot(vt_ref[...], g_sc[...], preferred_element_type=F32)


def _peer_dense(x_bf, u_bf, vt_bf, sel, *, tt=512, rows=8):
    cnt, r2, e1, e2 = sel
    te = rows * PEER_KEYS
    n_tok = x_bf.shape[0]
    row_spec = pl.BlockSpec((PEER_HEADS, rows, tt), lambda i, e: (0, e, i))
    all_spec = pl.BlockSpec((PEER_HEADS, PEER_KEYS, tt), lambda i, e: (0, 0, i))
    return pl.pallas_call(
        functools.partial(_peer_dense_kernel, rows=rows),
        grid=(n_tok // tt, N_EXPERTS // te),
        in_specs=[pl.BlockSpec((tt, D_MODEL), lambda i, e: (i, 0)),
                  pl.BlockSpec((te, D_MODEL), lambda i, e: (e, 0)),
                  pl.BlockSpec((D_MODEL, te), lambda i, e: (0, e)),
                  row_spec, all_spec, row_spec, all_spec],
        out_specs=pl.BlockSpec((D_MODEL, tt), lambda i, e: (0, i)),
        out_shape=jax.ShapeDtypeStruct((D_MODEL, n_tok), F32),
        scratch_shapes=[pltpu.VMEM((te, tt), F32), pltpu.VMEM((te, tt), BF16)],
        compiler_params=_cparams(("parallel", "arbitrary")),
        name="peer_dense",
    )(x_bf, u_bf, vt_bf, cnt, r2, e1, e2)


def _hyena_features():
    pos = jnp.arange(SEQ, dtype=F32)
    t = pos / (SEQ - 1)
    bands = jnp.linspace(1e-4, 16 - 1, 16, dtype=F32)
    ang = 2.0 * math.pi * pos[:, None] * bands[None, :] / SEQ
    feat = jnp.concatenate([t[:, None], jnp.cos(ang), -jnp.sin(ang)], axis=-1)
    feat_t = jnp.pad(feat.T, ((0, HY_EMB_PAD - feat.shape[1]), (0, 0)))
    return feat_t, t[None, :]


def kernel(x, w_in, mlstm_gate_bias, mlstm_norm_gain, w_mlstm_out, hyena_conv_w, hyena_conv_b, hyena_w1,
           hyena_b1, hyena_w2, hyena_b2, hyena_freq, hyena_w3, hyena_decay, hyena_skip, w_hyena_out,
           s5_lambda_re, s5_lambda_im, s5_log_step, s5_b_re, s5_b_im, s5_c_re, s5_c_im, s5_skip, w_s5_glu,
           w_out, ln1_g, ln1_b, peer_w_q, peer_subkeys, peer_u, peer_v, ln2_g, ln2_b):
    xf = x.reshape(SEQ, D_MODEL)
    xb = xf.astype(BF16)
    feat_t, t_row = _hyena_features()
    w_gates = w_in[:, :, OFF_GATES:OFF_HYENA]
    w_hy_t = jnp.swapaxes(w_in[:, :, OFF_HYENA:OFF_S5], 1, 2)
    w_rest = w_in[:, :, OFF_HYENA:]
    for l in range(DEPTH):
        qkvo = _mm(xb, w_in, l, 0, 4 * BRANCH, tm=1024, tn=512, name="proj_qkvo")
        gates = _mm(xb, w_gates, l, 0, N_GATES, tm=1024, tn=N_GATES, name="proj_gates")
        hp_t = _mm_nt(w_hy_t, l, xb, tr=512, tm=1024, name="proj_hyena_t")
        rest = _mm(xb, w_rest, l, 3 * BRANCH, BRANCH + 3 * D_MODEL, tm=1024, tn=512, name="proj_s5_mixgates")

        hdir = _mlstm(qkvo, gates, mlstm_gate_bias, l)
        hm = _mlstm_post(hdir, qkvo, mlstm_norm_gain, l)
        hy_t = _hyena(hp_t, l, hyena_conv_w, hyena_conv_b, hyena_w1, hyena_b1, hyena_w2, hyena_b2,
                      hyena_freq, hyena_w3, hyena_decay, hyena_skip, feat_t, t_row)
        mats = _s5_chunk_matrices(s5_lambda_re[l], s5_lambda_im[l], s5_log_step[l], s5_b_re[l], s5_b_im[l],
                                  s5_c_re[l], s5_c_im[l])
        s5y = _s5(rest[:, :BRANCH], mats, s5_skip, l)
        merged = _merge(hm, hy_t, s5y, rest, w_mlstm_out, w_hyena_out, w_s5_glu, l)
        mix = _mm(merged, w_out, l, 0, D_MODEL, tm=1024, tn=512, name="proj_out")
        xf, xb = _res_ln(xf, mix, ln1_g, ln1_b, l, transposed=False)

        q = _mm(xb, peer_w_q, l, 0, PEER_HEADS * 2 * PEER_KEYS, tm=1024, tn=512, name="peer_query")
        sel = _peer_select(q, peer_subkeys, l)
        ffn_t = _peer_dense(xb, peer_u[l].astype(BF16), peer_v[l].T.astype(BF16), sel)
        xf, xb = _res_ln(xf, ffn_t, ln2_g, ln2_b, l, transposed=True)
    return xf.reshape(1, SEQ, D_MODEL)
```

```python
import functools
import math

import numpy as np
import jax
import jax.numpy as jnp
from jax import lax
from jax.experimental import pallas as pl
from jax.experimental.pallas import tpu as pltpu

F32 = jnp.float32
BF16 = jnp.bfloat16

D_MODEL = 2048
SEQ = 8192
DEPTH = 4
BRANCH = 1024
HEADS = 4
HEAD_DIM = 256
CHUNK = 128
N_GATES = 16
HY_HID = 64
HY_EMB_PAD = 128
S5_GROUP = 16
S5_GROUPS = 64
S5_STATE = 64
S5_T = 16
S5_CHUNKS = SEQ // S5_T
S5_LOG_CHUNKS = 9
PEER_HEADS = 8
PEER_KEYS = 128
PEER_TOPK = 16
N_EXPERTS = PEER_KEYS * PEER_KEYS
ALPHA = (2 * DEPTH) ** 0.25
LN_EPS = 1e-5
FFT_N = 2 * SEQ
FFT_R = 128
OFF_GATES = 4 * BRANCH
OFF_HYENA = OFF_GATES + N_GATES
OFF_S5 = OFF_HYENA + 3 * BRANCH
OFF_MIXG = OFF_S5 + BRANCH

VMEM_LIMIT = 56 * 1024 * 1024


def _cparams(sem):
    return pltpu.CompilerParams(dimension_semantics=sem, vmem_limit_bytes=VMEM_LIMIT)


def _split_bf16(a):
    hi = a.astype(BF16)
    lo = (a - hi.astype(F32)).astype(BF16)
    return hi, lo


def _dot3(a, b_hi, b_lo):
    a_hi, a_lo = _split_bf16(a)
    acc = jnp.dot(a_hi, b_hi, preferred_element_type=F32)
    acc += jnp.dot(a_lo, b_hi, preferred_element_type=F32)
    acc += jnp.dot(a_hi, b_lo, preferred_element_type=F32)
    return acc


def _dot3_both(a, b):
    b_hi, b_lo = _split_bf16(b)
    return _dot3(a, b_hi, b_lo)


def _mm_kernel(a_ref, b_ref, o_ref):
    o_ref[...] = jnp.dot(a_ref[...].astype(BF16), b_ref[...].astype(BF16),
                         preferred_element_type=F32).astype(o_ref.dtype)


def _mm(a, w, l, col_off, n_cols, *, tm, tn, name):
    m, k = a.shape
    cb = col_off // tn
    assert col_off % tn == 0 and n_cols % tn == 0 and m % tm == 0
    return pl.pallas_call(
        _mm_kernel,
        grid=(m // tm, n_cols // tn),
        in_specs=[pl.BlockSpec((tm, k), lambda i, j: (i, 0)),
                  pl.BlockSpec((None, k, tn), lambda i, j: (l, 0, cb + j))],
        out_specs=pl.BlockSpec((tm, tn), lambda i, j: (i, j)),
        out_shape=jax.ShapeDtypeStruct((m, n_cols), F32),
        compiler_params=_cparams(("parallel", "parallel")),
        name=name,
    )(a, w)


def _mm_nt_kernel(w_ref, a_ref, o_ref):
    o_ref[...] = lax.dot_general(w_ref[...].astype(BF16), a_ref[...].astype(BF16),
                                 (((1,), (1,)), ((), ())), preferred_element_type=F32)


def _mm_nt(wt, l, a, *, tr, tm, name):
    _, r, k = wt.shape
    m = a.shape[0]
    return pl.pallas_call(
        _mm_nt_kernel,
        grid=(r // tr, m // tm),
        in_specs=[pl.BlockSpec((None, tr, k), lambda i, j: (l, i, 0)),
                  pl.BlockSpec((tm, k), lambda i, j: (j, 0))],
        out_specs=pl.BlockSpec((tr, tm), lambda i, j: (i, j)),
        out_shape=jax.ShapeDtypeStruct((r, m), F32),
        compiler_params=_cparams(("parallel", "parallel")),
        name=name,
    )(wt, a)


def _log_sigmoid(x):
    return jnp.minimum(x, 0.0) - jnp.log1p(jnp.exp(-jnp.abs(x)))


def _mlstm_chunk(d, q, k, v, li_r, lf_r, li_c, lf_c, c_sc, n_sc, m_sc):
    n = CHUNK
    row = lax.broadcasted_iota(jnp.int32, (n, n), 0)
    col = lax.broadcasted_iota(jnp.int32, (n, n), 1)
    valid = (row >= col) if d == 0 else (row <= col)
    valid_t = (col >= row) if d == 0 else (col <= row)
    b_col = jnp.sum(jnp.where(valid, lf_r, 0.0), axis=1, keepdims=True)
    b_row = jnp.sum(jnp.where(valid_t, lf_c, 0.0), axis=0, keepdims=True)
    total = jnp.sum(lf_r, axis=1, keepdims=True)

    m_prev = m_sc[...]
    m_inter = b_col + m_prev
    log_d = jnp.where(valid, b_col - b_row + li_r, -jnp.inf)
    m_t = jnp.maximum(m_inter, jnp.max(log_d, axis=1, keepdims=True))
    inter = jnp.exp(m_inter - m_t)
    dmat = jnp.exp(log_d - m_t)

    k = k * (HEAD_DIM ** -0.5)
    qb = q.astype(BF16)
    kb = k.astype(BF16)
    vb = v.astype(BF16)
    s = lax.dot_general(qb, kb, (((1,), (1,)), ((), ())), preferred_element_type=F32) * dmat
    num = jnp.dot(s.astype(BF16), vb, preferred_element_type=F32)
    num += inter * jnp.dot(qb, c_sc[...].astype(BF16), preferred_element_type=F32)
    qn = jnp.sum(q * n_sc[...], axis=1, keepdims=True)
    den = jnp.sum(s, axis=1, keepdims=True) + inter * qn
    h_out = num / jnp.maximum(jnp.abs(den), jnp.exp(-m_t))

    log_w = total - b_col + li_c
    m_new = jnp.maximum(total + m_prev, jnp.max(log_w, axis=0, keepdims=True))
    w = jnp.exp(log_w - m_new)
    decay = jnp.exp(total + m_prev - m_new)
    kw = k * w
    c_sc[...] = decay * c_sc[...] + lax.dot_general(
        kw.astype(BF16), vb, (((0,), (0,)), ((), ())), preferred_element_type=F32)
    n_sc[...] = decay * n_sc[...] + jnp.sum(kw, axis=0, keepdims=True)
    m_sc[...] = m_new
    return h_out


def _mlstm_kernel(qkv_f_ref, qkv_b_ref, gr_f_ref, gr_b_ref, gc_f_ref, gc_b_ref, bias_ref,
                  o_f_ref, o_b_ref, c_sc, n_sc, m_sc):
    @pl.when(pl.program_id(0) == 0)
    def _():
        c_sc[...] = jnp.zeros_like(c_sc)
        n_sc[...] = jnp.zeros_like(n_sc)
        m_sc[...] = jnp.zeros_like(m_sc)

    for d, (qkv_ref, gr_ref, gc_ref, o_ref) in enumerate(((qkv_f_ref, gr_f_ref, gc_f_ref, o_f_ref),
                                                         (qkv_b_ref, gr_b_ref, gc_b_ref, o_b_ref))):
        for h in range(HEADS):
            cols = slice(h * HEAD_DIM, (h + 1) * HEAD_DIM)
            b_i = bias_ref[d * 2 * HEADS + h]
            b_f = bias_ref[d * 2 * HEADS + HEADS + h]
            chain = d * HEADS + h
            o_ref[:, cols] = _mlstm_chunk(
                d,
                qkv_ref[:, cols],
                qkv_ref[:, BRANCH + h * HEAD_DIM:BRANCH + (h + 1) * HEAD_DIM],
                qkv_ref[:, 2 * BRANCH + h * HEAD_DIM:2 * BRANCH + (h + 1) * HEAD_DIM],
                gr_ref[h] + b_i, _log_sigmoid(gr_ref[HEADS + h] + b_f),
                gc_ref[h] + b_i, _log_sigmoid(gc_ref[HEADS + h] + b_f),
                c_sc.at[chain], n_sc.at[chain], m_sc.at[chain])


def _mlstm(qkvo, gates, gate_bias, l):
    nc = SEQ // CHUNK
    g_rows = gates.T.reshape(N_GATES, 1, SEQ)
    g_cols = gates.T.reshape(N_GATES, SEQ, 1)
    bias = gate_bias.reshape(DEPTH, N_GATES, 1, 1)
    half = N_GATES // 2
    chains = 2 * HEADS
    out = jax.ShapeDtypeStruct((SEQ, BRANCH), F32)
    return pl.pallas_call(
        _mlstm_kernel,
        grid=(nc,),
        in_specs=[
            pl.BlockSpec((CHUNK, 4 * BRANCH), lambda c: (c, 0)),
            pl.BlockSpec((CHUNK, 4 * BRANCH), lambda c: (nc - 1 - c, 0)),
            pl.BlockSpec((half, 1, CHUNK), lambda c: (0, 0, c)),
            pl.BlockSpec((half, 1, CHUNK), lambda c: (1, 0, nc - 1 - c)),
            pl.BlockSpec((half, CHUNK, 1), lambda c: (0, c, 0)),
            pl.BlockSpec((half, CHUNK, 1), lambda c: (1, nc - 1 - c, 0)),
            pl.BlockSpec((None, N_GATES, 1, 1), lambda c: (l, 0, 0, 0)),
        ],
        out_specs=[pl.BlockSpec((CHUNK, BRANCH), lambda c: (c, 0)),
                   pl.BlockSpec((CHUNK, BRANCH), lambda c: (nc - 1 - c, 0))],
        out_shape=[out, out],
        scratch_shapes=[pltpu.VMEM((chains, HEAD_DIM, HEAD_DIM), F32),
                        pltpu.VMEM((chains, 1, HEAD_DIM), F32),
                        pltpu.VMEM((chains, 1, 1), F32)],
        compiler_params=_cparams(("arbitrary",)),
        name="mlstm_scan",
    )(qkvo, qkvo, g_rows, g_rows, g_cols, g_cols, bias)


def _mlstm_post_kernel(hf_ref, hb_ref, o_ref, gain_ref, out_ref):
    h = hf_ref[...] + hb_ref[...]
    h = jax.nn.sigmoid(o_ref[...]) * h
    for hd in range(HEADS):
        sl = slice(hd * HEAD_DIM, (hd + 1) * HEAD_DIM)
        hh = h[:, sl]
        mu = jnp.mean(hh, axis=1, keepdims=True)
        var = jnp.mean(jnp.square(hh - mu), axis=1, keepdims=True)
        out_ref[:, sl] = ((hh - mu) * lax.rsqrt(var + LN_EPS) * gain_ref[:, sl]).astype(out_ref.dtype)


def _mlstm_post(hdir, qkvo, gain, l, *, tm=512):
    return pl.pallas_call(
        _mlstm_post_kernel,
        grid=(SEQ // tm,),
        in_specs=[pl.BlockSpec((tm, BRANCH), lambda i: (i, 0)),
                  pl.BlockSpec((tm, BRANCH), lambda i: (i, 0)),
                  pl.BlockSpec((tm, BRANCH), lambda i: (i, 3)),
                  pl.BlockSpec((None, 1, BRANCH), lambda i: (l, 0, 0))],
        out_specs=pl.BlockSpec((tm, BRANCH), lambda i: (i, 0)),
        out_shape=jax.ShapeDtypeStruct((SEQ, BRANCH), BF16),
        compiler_params=_cparams(("parallel",)),
        name="mlstm_post",
    )(hdir[0], hdir[1], qkvo, gain.reshape(DEPTH, 1, BRANCH))


HY_CT = 32
HY_RB = 512


@functools.lru_cache(maxsize=None)
def _fft_constants():
    r = FFT_R
    idx = np.arange(r)
    th = 2.0 * np.pi * np.outer(idx, idx) / r
    cs, sn = np.cos(th), np.sin(th)
    ph = 2.0 * np.pi * np.outer(idx, idx) / FFT_N
    f1 = np.concatenate([cs, -sn], axis=1)
    f1[r // 2:] = 0.0
    m3 = np.block([[cs, -sn], [sn, cs]])
    m4 = np.block([[cs, sn], [-sn, cs]])
    m6 = np.concatenate([cs, -sn], axis=0) / FFT_N
    m6[:, r // 2:] = 0.0
    consts = {name: jnp.asarray(mat, F32).astype(BF16)
              for name, mat in (("f1", f1), ("m3", m3), ("m4", m4), ("m6", m6))}
    consts["tw"] = (jnp.asarray(np.cos(ph), F32), jnp.asarray(np.sin(ph), F32))
    return consts


def _rows(i, n):
    return pl.ds(pl.multiple_of(i * n, n), n)


def _for_each(n, body, unroll=1):
    def step(i, carry):
        body(i)
        return carry

    lax.fori_loop(0, n, step, 0, unroll=unroll)


def _fwd_stage1(sig_ref, ct, t_buf, a_buf, b_buf, f1_ref, twr_ref, twi_ref):
    r = FFT_R
    zeros = jnp.zeros((r // 2, r), F32)

    def tr_in(c):
        t_buf[_rows(c, r), :] = jnp.concatenate([sig_ref[c], zeros], axis=0).T.astype(BF16)

    _for_each(ct, tr_in, unroll=4)

    def stage1(b):
        rows = _rows(b, HY_RB)
        a_buf[rows, :] = jnp.dot(t_buf[rows, :], f1_ref[...], preferred_element_type=F32)

    _for_each(ct * r // HY_RB, stage1)

    def twiddle(c):
        rows = _rows(c, r)
        a = a_buf[rows, :]
        ar, ai = a[:, :r], a[:, r:]
        twr, twi = twr_ref[...], twi_ref[...]
        b_buf[rows, :r] = (ar * twr + ai * twi).T.astype(BF16)
        b_buf[rows, r:] = (ai * twr - ar * twi).T.astype(BF16)

    _for_each(ct, twiddle, unroll=4)


def _hy_mlp_kernel(feat_ref, w1_ref, b1_ref, w2_ref, b2_ref, fr_ref, out_ref):
    h = _dot3_both(w1_ref[...], feat_ref[...]) + b1_ref[...]
    h = jnp.sin(fr_ref[0] * h)
    h = _dot3_both(w2_ref[...], h) + b2_ref[...]
    out_ref[...] = jnp.sin(fr_ref[1] * h)


def _hy_filter_kernel(w3_ref, h_ref, dec_ref, t_ref, out_ref):
    d = pl.program_id(0)
    filt = _dot3_both(w3_ref[...], h_ref[...])
    filt = filt * jnp.exp(-t_ref[...] * jnp.abs(dec_ref[...]))
    filt = filt / (jnp.sum(jnp.abs(filt), axis=1, keepdims=True) + 1e-6)
    lane = lax.broadcasted_iota(jnp.int32, filt.shape, 1)
    out_ref[...] = jnp.where((lane == 0) & (d == 1), 0.0, filt)


def _hy_spec_kernel(fa_ref, fb_ref, f1_ref, m3_ref, twr_ref, twi_ref, out_ref, t_buf, a_buf, b_buf, *, ct):
    r = FFT_R
    nblk = ct * r // HY_RB
    _fwd_stage1(fa_ref, ct, t_buf, a_buf, b_buf, f1_ref, twr_ref, twi_ref)

    def put(b):
        rows = _rows(b, HY_RB)
        out_ref[rows, :] = jnp.dot(b_buf[rows, :], m3_ref[...], preferred_element_type=F32)

    _for_each(nblk, put)
    _fwd_stage1(fb_ref, ct, t_buf, a_buf, b_buf, f1_ref, twr_ref, twi_ref)

    def add_conj(b):
        rows = _rows(b, HY_RB)
        sb = jnp.dot(b_buf[rows, :], m3_ref[...], preferred_element_type=F32)
        sa = out_ref[rows, :]
        out_ref[rows, :] = jnp.concatenate([sa[:, :r] + sb[:, :r], sa[:, r:] - sb[:, r:]], axis=1)

    _for_each(nblk, add_conj)


def _shift_prev(p, n1, n2):
    r1 = pltpu.roll(p, 1, 2)
    r2 = pltpu.roll(r1, 1, 1)
    return jnp.where(n2 == 0, jnp.where(n1 == 0, 0.0, r2), r1)


def _shift_next(p, n1, n2):
    s1, s2 = p.shape[1], p.shape[2]
    r1 = pltpu.roll(p, s2 - 1, 2)
    r2 = pltpu.roll(r1, s1 - 1, 1)
    return jnp.where(n2 == s2 - 1, jnp.where(n1 == s1 - 1, 0.0, r2), r1)


def _hy_conv_kernel(p0_ref, p1_ref, p2_ref, cw_ref, cb_ref, skip_ref, h_ref,
                    f1_ref, m3_ref, m4_ref, m6_ref, twr_ref, twi_ref,
                    out_ref, z_buf, t_buf, a_buf, b_buf, *, ct):
    r = FFT_R
    n1_len = SEQ // r
    nblk = ct * r // HY_RB
    shp = (ct, n1_len, r)
    n1 = lax.broadcasted_iota(jnp.int32, shp, 1)
    n2 = lax.broadcasted_iota(jnp.int32, shp, 2)

    def short_conv(p_ref, g):
        p = p_ref[...]
        return (_shift_prev(p, n1, n2) * cw_ref[3 * g] + p * cw_ref[3 * g + 1]
                + _shift_next(p, n1, n2) * cw_ref[3 * g + 2] + cb_ref[g])

    z_buf[...] = short_conv(p1_ref, 1) * short_conv(p2_ref, 2)
    _fwd_stage1(z_buf, ct, t_buf, a_buf, b_buf, f1_ref, twr_ref, twi_ref)

    def freq_domain(b):
        rows = _rows(b, HY_RB)
        x = jnp.dot(b_buf[rows, :], m3_ref[...], preferred_element_type=F32)
        h = h_ref[rows, :]
        xr, xi = x[:, :r], x[:, r:]
        hr, hi = h[:, :r], h[:, r:]
        y = jnp.concatenate([xr * hr - xi * hi, xr * hi + xi * hr], axis=1).astype(BF16)
        a_buf[rows, :] = jnp.dot(y, m4_ref[...], preferred_element_type=F32)

    _for_each(nblk, freq_domain)

    def inv_twiddle(c):
        rows = _rows(c, r)
        b = a_buf[rows, :]
        br, bi = b[:, :r], b[:, r:]
        twr, twi = twr_ref[...], twi_ref[...]
        b_buf[rows, :r] = (br * twr - bi * twi).T.astype(BF16)
        b_buf[rows, r:] = (br * twi + bi * twr).T.astype(BF16)

    _for_each(ct, inv_twiddle, unroll=4)

    def stage6(b):
        rows = _rows(b, HY_RB)
        a_buf[rows, :r] = jnp.dot(b_buf[rows, :], m6_ref[...], preferred_element_type=F32)

    _for_each(nblk, stage6)

    def finish(c):
        y_c = a_buf[_rows(c, r), :r].T[:n1_len, :]
        out_ref[c] = y_c + skip_ref[c] * z_buf[c]

    _for_each(ct, finish, unroll=4)
    out_ref[...] = short_conv(p0_ref, 0) * out_ref[...]


def _hyena(hp_t, l, conv_w, conv_b, w1, b1, w2, b2, freq, w3, decay, skip, feat_t, t_row):
    r = FFT_R
    n1 = SEQ // r
    ct = HY_CT
    k = _fft_constants()
    f1, m3, m4, m6 = k["f1"], k["m3"], k["m4"], k["m6"]
    twr, twi = k["tw"]

    def full(a):
        return pl.BlockSpec(a.shape, lambda *_: (0,) * a.ndim)

    h2 = pl.pallas_call(
        _hy_mlp_kernel,
        grid=(1,),
        in_specs=[full(feat_t), pl.BlockSpec((None, HY_HID, feat_t.shape[0]), lambda i: (l, 0, 0)),
                  pl.BlockSpec((None, HY_HID, 1), lambda i: (l, 0, 0)),
                  pl.BlockSpec((None, HY_HID, HY_HID), lambda i: (l, 0, 0)),
                  pl.BlockSpec((None, HY_HID, 1), lambda i: (l, 0, 0)),
                  pl.BlockSpec((None, 2, HY_HID, 1), lambda i: (l, 0, 0, 0))],
        out_specs=pl.BlockSpec((HY_HID, SEQ), lambda i: (0, 0)),
        out_shape=jax.ShapeDtypeStruct((HY_HID, SEQ), F32),
        compiler_params=pltpu.CompilerParams(vmem_limit_bytes=VMEM_LIMIT),
        name="hyena_filter_mlp",
    )(feat_t, jnp.pad(jnp.swapaxes(w1, 1, 2), ((0, 0), (0, 0), (0, feat_t.shape[0] - w1.shape[1]))),
      b1[:, :, None], jnp.swapaxes(w2, 1, 2), b2[:, :, None], freq[:, :, :, None])

    rt = 128
    filt = pl.pallas_call(
        _hy_filter_kernel,
        grid=(2, BRANCH // rt),
        in_specs=[pl.BlockSpec((None, rt, HY_HID), lambda d, i: (l, d * (BRANCH // rt) + i, 0)),
                  pl.BlockSpec((HY_HID, SEQ), lambda d, i: (0, 0)),
                  pl.BlockSpec((None, None, rt, 1), lambda d, i: (l, d, i, 0)),
                  pl.BlockSpec((1, SEQ), lambda d, i: (0, 0))],
        out_specs=pl.BlockSpec((None, rt, SEQ), lambda d, i: (d, i, 0)),
        out_shape=jax.ShapeDtypeStruct((2, BRANCH, SEQ), F32),
        compiler_params=_cparams(("parallel", "parallel")),
        name="hyena_filter",
    )(jnp.swapaxes(w3, 1, 2), h2, decay[:, :, :, None], t_row)
    filt = filt.reshape(2, BRANCH, n1, r)

    scratch = [pltpu.VMEM((ct * r, r), BF16), pltpu.VMEM((ct * r, 2 * r), F32), pltpu.VMEM((ct * r, 2 * r), BF16)]
    spec = pl.pallas_call(
        functools.partial(_hy_spec_kernel, ct=ct),
        grid=(BRANCH // ct,),
        in_specs=[pl.BlockSpec((None, ct, n1, r), lambda i: (0, i, 0, 0)),
                  pl.BlockSpec((None, ct, n1, r), lambda i: (1, i, 0, 0)),
                  full(f1), full(m3), full(twr), full(twi)],
        out_specs=pl.BlockSpec((ct * r, 2 * r), lambda i: (i, 0)),
        out_shape=jax.ShapeDtypeStruct((BRANCH * r, 2 * r), F32),
        scratch_shapes=scratch,
        compiler_params=_cparams(("parallel",)),
        name="hyena_filter_spectrum",
    )(filt, filt, f1, m3, twr, twi)

    hp3 = hp_t.reshape(3 * BRANCH, n1, r)
    nb = BRANCH // ct
    cw = conv_w.reshape(DEPTH, 3, 3, BRANCH).transpose(0, 2, 1, 3).reshape(DEPTH, 9, BRANCH, 1, 1)
    cb = conv_b.reshape(DEPTH, 3, BRANCH, 1, 1)
    consts = (f1, m3, m4, m6, twr, twi)
    return pl.pallas_call(
        functools.partial(_hy_conv_kernel, ct=ct),
        grid=(nb,),
        in_specs=[pl.BlockSpec((ct, n1, r), lambda i: (i, 0, 0)),
                  pl.BlockSpec((ct, n1, r), lambda i: (nb + i, 0, 0)),
                  pl.BlockSpec((ct, n1, r), lambda i: (2 * nb + i, 0, 0)),
                  pl.BlockSpec((None, 9, ct, 1, 1), lambda i: (l, 0, i, 0, 0)),
                  pl.BlockSpec((None, 3, ct, 1, 1), lambda i: (l, 0, i, 0, 0)),
                  pl.BlockSpec((None, ct, 1, 1), lambda i: (l, i, 0, 0)),
                  pl.BlockSpec((ct * r, 2 * r), lambda i: (i, 0)),
                  *[full(a) for a in consts]],
        out_specs=pl.BlockSpec((ct, n1, r), lambda i: (i, 0, 0)),
        out_shape=jax.ShapeDtypeStruct((BRANCH, n1, r), F32),
        scratch_shapes=[pltpu.VMEM((ct, n1, r), F32)] + scratch,
        compiler_params=_cparams(("parallel",)),
        name="hyena_conv",
    )(hp3, hp3, hp3, cw, cb, skip.reshape(DEPTH, BRANCH, 1, 1), spec, *consts).reshape(BRANCH, SEQ)


def _s5_chunk_matrices(lam_re, lam_im, log_step, b_re, b_im, c_re, c_im):
    t = S5_T
    lam = lax.complex(lam_re, lam_im)
    step = jnp.exp(log_step)[..., None]
    lam_dt = lam * step
    b_bar = ((jnp.exp(lam_dt) - 1.0) / lam)[..., None] * lax.complex(b_re, b_im)
    c_mat = lax.complex(c_re, c_im)
    pw = jnp.exp(lam_dt[None] * jnp.arange(t + 1, dtype=F32)[:, None, None, None])
    kern = jnp.einsum("dgvp,mdgp,dgpn->mdgvn", c_mat, pw[:t], b_bar, precision="highest").real
    lag = np.arange(t)[:, None] - np.arange(t)[None, :]
    pick = np.stack([lag[:, :, None] == np.arange(t), -lag[:, :, None] == np.arange(t)], axis=-1)
    kfb = jnp.einsum("ajmd,mdgvn->ajgvn", jnp.asarray(pick, F32), kern, precision="highest")
    w_intra = jnp.transpose(kfb, (2, 1, 4, 0, 3)).reshape(S5_GROUPS, t * S5_GROUP, t * S5_GROUP)
    in_f = pw[:t, 0][::-1][..., None] * b_bar[0][None]
    in_b = pw[:t, 1][..., None] * b_bar[1][None]

    def in_mat(m):
        m = jnp.transpose(m, (1, 0, 3, 2)).reshape(S5_GROUPS, t * S5_GROUP, S5_STATE)
        return jnp.concatenate([m.real, m.imag], axis=-1)

    w_in = jnp.concatenate([in_mat(in_f), in_mat(in_b)], axis=-1)
    out_f = c_mat[0][None] * pw[1:, 0][:, :, None, :]
    out_b = c_mat[1][None] * pw[1:, 1][::-1][:, :, None, :]

    def out_mat(m):
        m = jnp.transpose(m, (1, 3, 0, 2)).reshape(S5_GROUPS, S5_STATE, t * S5_GROUP)
        return jnp.concatenate([m.real, -m.imag], axis=1)

    w_out = jnp.concatenate([out_mat(out_f), out_mat(out_b)], axis=1)
    a_pow = jnp.exp(lam_dt[None] * (t * 2.0 ** jnp.arange(S5_LOG_CHUNKS, dtype=F32))[:, None, None, None])
    a_pow = jnp.transpose(a_pow, (2, 1, 0, 3))
    a_re = jnp.concatenate([a_pow.real, a_pow.real], axis=-1)
    a_im = jnp.concatenate([-a_pow.imag, a_pow.imag], axis=-1)
    return w_intra, w_in, w_out, a_re, a_im


def _s5_kernel(u_ref, wi_ref, win_ref, wout_ref, are_ref, aim_ref, skip_ref, y_ref):
    nc = S5_CHUNKS
    half = S5_STATE
    u = u_ref[...]
    ub = u.astype(BF16)
    y = jnp.dot(ub, wi_ref[...].astype(BF16), preferred_element_type=F32) + skip_ref[...] * u
    s_in = jnp.dot(ub, win_ref[...].astype(BF16), preferred_element_type=F32)
    row = lax.broadcasted_iota(jnp.int32, (nc, 2 * half), 0)

    def scan(s, d):
        for step in range(S5_LOG_CHUNKS):
            sh = 1 << step
            if d == 0:
                moved = jnp.where(row >= sh, pltpu.roll(s, sh, 0), 0.0)
            else:
                moved = jnp.where(row < nc - sh, pltpu.roll(s, nc - sh, 0), 0.0)
            swapped = pltpu.roll(moved, half, 1)
            s = s + are_ref[d, step:step + 1, :] * moved + aim_ref[d, step:step + 1, :] * swapped
        if d == 0:
            return jnp.where(row >= 1, pltpu.roll(s, 1, 0), 0.0)
        return jnp.where(row < nc - 1, pltpu.roll(s, nc - 1, 0), 0.0)

    e = jnp.concatenate([scan(s_in[:, :2 * half], 0), scan(s_in[:, 2 * half:], 1)], axis=1)
    y_ref[...] = y + jnp.dot(e.astype(BF16), wout_ref[...].astype(BF16), preferred_element_type=F32)


def _s5(su, mats, skip, l):
    w_intra, w_in, w_out, a_re, a_im = mats
    t = S5_T
    gw = t * S5_GROUP
    u = su.reshape(S5_CHUNKS, t, S5_GROUPS, S5_GROUP).transpose(2, 0, 1, 3).reshape(S5_GROUPS, S5_CHUNKS, gw)
    skip_t = jnp.tile(skip.reshape(DEPTH, S5_GROUPS, 1, S5_GROUP), (1, 1, t, 1)).reshape(DEPTH, S5_GROUPS, 1, gw)
    y = pl.pallas_call(
        _s5_kernel,
        grid=(S5_GROUPS,),
        in_specs=[pl.BlockSpec((None, S5_CHUNKS, gw), lambda g: (g, 0, 0)),
                  pl.BlockSpec((None, None, gw, gw), lambda g: (l, g, 0, 0)),
                  pl.BlockSpec((None, None, gw, 4 * S5_STATE), lambda g: (l, g, 0, 0)),
                  pl.BlockSpec((None, None, 4 * S5_STATE, gw), lambda g: (l, g, 0, 0)),
                  pl.BlockSpec((None, None, 2, S5_LOG_CHUNKS, 2 * S5_STATE), lambda g: (l, g, 0, 0, 0)),
                  pl.BlockSpec((None, None, 2, S5_LOG_CHUNKS, 2 * S5_STATE), lambda g: (l, g, 0, 0, 0)),
                  pl.BlockSpec((None, None, 1, gw), lambda g: (l, g, 0, 0))],
        out_specs=pl.BlockSpec((None, S5_CHUNKS, gw), lambda g: (g, 0, 0)),
        out_shape=jax.ShapeDtypeStruct((S5_GROUPS, S5_CHUNKS, gw), F32),
        compiler_params=_cparams(("parallel",)),
        name="s5_scan",
    )(u, w_intra, w_in, w_out, a_re, a_im, skip_t)
    return y.reshape(S5_GROUPS, S5_CHUNKS, t, S5_GROUP).transpose(1, 2, 0, 3).reshape(SEQ, BRANCH)


def _merge_kernel(hm_ref, hyt_ref, s5_ref, wa_ref, wb_ref, wga_ref, wgg_ref, g0_ref, g1_ref, g2_ref,
                  out_ref, hy_sc):
    @pl.when(pl.program_id(1) == 0)
    def _():
        hy_sc[...] = hyt_ref[...].T.astype(BF16)

    def proj(a, w_ref):
        return jnp.dot(a, w_ref[...].astype(BF16), preferred_element_type=F32)

    s5b = s5_ref[...].astype(BF16)
    out_a = proj(hm_ref[...], wa_ref)
    out_b = proj(hy_sc[...], wb_ref)
    out_c = proj(s5b, wga_ref) * jax.nn.sigmoid(proj(s5b, wgg_ref))
    merged = (jax.nn.sigmoid(g0_ref[...]) * out_a + jax.nn.sigmoid(g1_ref[...]) * out_b
              + jax.nn.sigmoid(g2_ref[...]) * out_c)
    out_ref[...] = merged.astype(out_ref.dtype)


def _merge(hm, hy_t, s5y, rest, w_a, w_b, w_glu, l, *, tm=512, tn=512):
    nj = D_MODEL // tn
    goff = BRANCH // tn
    return pl.pallas_call(
        _merge_kernel,
        grid=(SEQ // tm, nj),
        in_specs=[pl.BlockSpec((tm, BRANCH), lambda i, j: (i, 0)),
                  pl.BlockSpec((BRANCH, tm), lambda i, j: (0, i)),
                  pl.BlockSpec((tm, BRANCH), lambda i, j: (i, 0)),
                  pl.BlockSpec((None, BRANCH, tn), lambda i, j: (l, 0, j)),
                  pl.BlockSpec((None, BRANCH, tn), lambda i, j: (l, 0, j)),
                  pl.BlockSpec((None, BRANCH, tn), lambda i, j: (l, 0, j)),
                  pl.BlockSpec((None, BRANCH, tn), lambda i, j: (l, 0, nj + j)),
                  pl.BlockSpec((tm, tn), lambda i, j: (i, goff + j)),
                  pl.BlockSpec((tm, tn), lambda i, j: (i, goff + nj + j)),
                  pl.BlockSpec((tm, tn), lambda i, j: (i, goff + 2 * nj + j))],
        out_specs=pl.BlockSpec((tm, tn), lambda i, j: (i, j)),
        out_shape=jax.ShapeDtypeStruct((SEQ, D_MODEL), BF16),
        scratch_shapes=[pltpu.VMEM((tm, BRANCH), BF16)],
        compiler_params=_cparams(("parallel", "arbitrary")),
        name="branch_merge",
    )(hm, hy_t, s5y, w_a, w_b, w_glu, w_glu, rest, rest, rest)


def _layer_norm_rows(y, g, b):
    mu = jnp.mean(y, axis=1, keepdims=True)
    var = jnp.mean(jnp.square(y - mu), axis=1, keepdims=True)
    return (y - mu) * lax.rsqrt(var + LN_EPS) * g + b


def _res_ln_kernel(x_ref, f_ref, g_ref, b_ref, o_ref, ob_ref, *, transposed):
    f = f_ref[...]
    if transposed:
        f = f.T
    y = _layer_norm_rows(ALPHA * x_ref[...] + f, g_ref[...], b_ref[...])
    o_ref[...] = y
    ob_ref[...] = y.astype(BF16)


def _res_ln(x, f, g, b, l, *, transposed, tm=256):
    f_spec = (pl.BlockSpec((D_MODEL, tm), lambda i: (0, i)) if transposed
              else pl.BlockSpec((tm, D_MODEL), lambda i: (i, 0)))
    n_tok = x.shape[0]
    return pl.pallas_call(
        functools.partial(_res_ln_kernel, transposed=transposed),
        grid=(n_tok // tm,),
        in_specs=[pl.BlockSpec((tm, D_MODEL), lambda i: (i, 0)), f_spec,
                  pl.BlockSpec((None, 1, D_MODEL), lambda i: (l, 0, 0)),
                  pl.BlockSpec((None, 1, D_MODEL), lambda i: (l, 0, 0))],
        out_specs=[pl.BlockSpec((tm, D_MODEL), lambda i: (i, 0)),
                   pl.BlockSpec((tm, D_MODEL), lambda i: (i, 0))],
        out_shape=[jax.ShapeDtypeStruct((n_tok, D_MODEL), F32), jax.ShapeDtypeStruct((n_tok, D_MODEL), BF16)],
        compiler_params=_cparams(("parallel",)),
        name="residual_layernorm",
    )(x, f, g.reshape(DEPTH, 1, D_MODEL), b.reshape(DEPTH, 1, D_MODEL))


PEER_NO_RANK = 64.0


def _extract_sorted(s, n, out_ref, base, want_rank=False):
    rank = jnp.full_like(s, PEER_NO_RANK) if want_rank else None
    for r in range(n):
        m = jnp.max(s, axis=0, keepdims=True)
        out_ref[base + r:base + r + 1, :] = m
        hit = s == m
        if want_rank:
            rank = jnp.where(hit, float(r), rank)
        s = jnp.where(hit, -jnp.inf, s)
    return rank


def _peer_select_kernel(q_ref, keys_ref, cnt_ref, r2_ref, e1_ref, e2_ref, top_sc, kth_sc):
    k = PEER_TOPK
    qb = q_ref[...].astype(BF16)
    half = PEER_KEYS
    for h in range(PEER_HEADS):
        sc = []
        rank2 = None
        for c in range(2):
            qs = qb[:, (2 * h + c) * half:(2 * h + c + 1) * half]
            s = lax.dot_general(keys_ref[h, c].astype(BF16), qs, (((1,), (1,)), ((), ())),
                                preferred_element_type=F32)
            sc.append(s)
            rank2 = _extract_sorted(s, k, top_sc, c * k, want_rank=(c == 1))
        a = top_sc[0:k, :]
        b = top_sc[k:2 * k, :]
        sub = lax.broadcasted_iota(jnp.int32, (8, a.shape[1]), 0)
        groups = [a[0:1] + b[0:8], a[0:1] + b[8:16], a[1:2] + b[0:8]]
        for i in range(2, 8):
            groups.append(jnp.where(sub < k // (i + 1), a[i:i + 1] + b[0:8], -jnp.inf))
        groups.append(a[8:16] + b[0:1])
        cand = jnp.concatenate(groups, axis=0)
        _extract_sorted(cand, k, kth_sc, 0)
        tau = kth_sc[k - 1:k, :]
        top = a[0:1] + b[0:1]
        z = jnp.sum(jnp.where(cand >= tau, jnp.exp(cand - top), 0.0), axis=0, keepdims=True)
        cnt = jnp.zeros_like(sc[0])
        for j in range(k):
            bj = top_sc[k + j:k + j + 1, :]
            cnt = jnp.where(sc[0] + bj >= tau, j + 1.0, cnt)
        cnt_ref[h] = cnt
        r2_ref[h] = rank2.astype(BF16)
        e1_ref[h] = jnp.exp(sc[0] - a[0:1])
        e2_ref[h] = (jnp.exp(sc[1] - b[0:1]) / z).astype(BF16)


def _peer_select(q, keys, l, *, tt=128):
    n_tok = q.shape[0]
    big = jax.ShapeDtypeStruct((PEER_HEADS, PEER_KEYS, n_tok), F32)
    big_bf = jax.ShapeDtypeStruct((PEER_HEADS, PEER_KEYS, n_tok), BF16)
    big_spec = pl.BlockSpec((PEER_HEADS, PEER_KEYS, tt), lambda i: (0, 0, i))
    return pl.pallas_call(
        _peer_select_kernel,
        grid=(n_tok // tt,),
        in_specs=[pl.BlockSpec((tt, D_MODEL), lambda i: (i, 0)),
                  pl.BlockSpec((None, PEER_HEADS, 2, PEER_KEYS, PEER_KEYS), lambda i: (l, 0, 0, 0, 0))],
        out_specs=[big_spec, big_spec, big_spec, big_spec],
        out_shape=[big, big_bf, big, big_bf],
        scratch_shapes=[pltpu.VMEM((2 * PEER_TOPK, tt), F32), pltpu.VMEM((PEER_TOPK, tt), F32)],
        compiler_params=_cparams(("parallel",)),
        name="peer_select",
    )(q, keys)


def _peer_dense_kernel(x_ref, u_ref, vt_ref, cnt_ref, r2_ref, e1_ref, e2_ref, out_ref, act_sc, g_sc, *, rows):
    e = pl.program_id(1)

    @pl.when(e == 0)
    def _():
        out_ref[...] = jnp.zeros_like(out_ref)

    act_sc[...] = lax.dot_general(u_ref[...], x_ref[...], (((1,), (1,)), ((), ())),
                                  preferred_element_type=F32)
    kk = PEER_KEYS
    for tc in range(act_sc.shape[1] // kk):
        lanes = slice(tc * kk, (tc + 1) * kk)
        for i in range(rows):
            w = jnp.zeros((kk, kk), BF16)
            for h in range(PEER_HEADS):
                hit = r2_ref[h, :, lanes] < cnt_ref[h, i:i + 1, lanes].astype(BF16)
                val = e2_ref[h, :, lanes] * e1_ref[h, i:i + 1, lanes].astype(BF16)
                w = w + jnp.where(hit, val, jnp.zeros_like(val))
            g = jax.nn.gelu(act_sc[i * kk:(i + 1) * kk, lanes]) * w.astype(F32)
            g_sc[i * kk:(i + 1) * kk, lanes] = g.astype(BF16)
    out_ref[...] += jnp.dot(vt_ref[...], g_sc[...], preferred_element_type=F32)


def _peer_dense(x_bf, u_bf, v, sel, *, tt=512, rows=8):
    cnt, r2, e1, e2 = sel
    te = rows * PEER_KEYS
    n_tok = x_bf.shape[0]
    vt_bf = v.reshape(N_EXPERTS // te, te, D_MODEL).transpose(0, 2, 1).astype(BF16)
    row_spec = pl.BlockSpec((PEER_HEADS, rows, tt), lambda i, e: (0, e, i))
    all_spec = pl.BlockSpec((PEER_HEADS, PEER_KEYS, tt), lambda i, e: (0, 0, i))
    return pl.pallas_call(
        functools.partial(_peer_dense_kernel, rows=rows),
        grid=(n_tok // tt, N_EXPERTS // te),
        in_specs=[pl.BlockSpec((tt, D_MODEL), lambda i, e: (i, 0)),
                  pl.BlockSpec((te, D_MODEL), lambda i, e: (e, 0)),
                  pl.BlockSpec((None, D_MODEL, te), lambda i, e: (e, 0, 0)),
                  row_spec, all_spec, row_spec, all_spec],
        out_specs=pl.BlockSpec((D_MODEL, tt), lambda i, e: (0, i)),
        out_shape=jax.ShapeDtypeStruct((D_MODEL, n_tok), F32),
        scratch_shapes=[pltpu.VMEM((te, tt), F32), pltpu.VMEM((te, tt), BF16)],
        compiler_params=_cparams(("parallel", "arbitrary")),
        name="peer_dense",
    )(x_bf, u_bf, vt_bf, cnt, r2, e1, e2)


def _hyena_features():
    pos = jnp.arange(SEQ, dtype=F32)
    t = pos / (SEQ - 1)
    bands = jnp.linspace(1e-4, 16 - 1, 16, dtype=F32)
    ang = 2.0 * math.pi * pos[:, None] * bands[None, :] / SEQ
    feat = jnp.concatenate([t[:, None], jnp.cos(ang), -jnp.sin(ang)], axis=-1)
    feat_t = jnp.pad(feat.T, ((0, HY_EMB_PAD - feat.shape[1]), (0, 0)))
    return feat_t, t[None, :]


def kernel(x, w_in, mlstm_gate_bias, mlstm_norm_gain, w_mlstm_out, hyena_conv_w, hyena_conv_b, hyena_w1,
           hyena_b1, hyena_w2, hyena_b2, hyena_freq, hyena_w3, hyena_decay, hyena_skip, w_hyena_out,
           s5_lambda_re, s5_lambda_im, s5_log_step, s5_b_re, s5_b_im, s5_c_re, s5_c_im, s5_skip, w_s5_glu,
           w_out, ln1_g, ln1_b, peer_w_q, peer_subkeys, peer_u, peer_v, ln2_g, ln2_b):
    xf = x.reshape(SEQ, D_MODEL)
    xb = xf.astype(BF16)
    feat_t, t_row = _hyena_features()
    w_gates = w_in[:, :, OFF_GATES:OFF_HYENA]
    w_hy_t = jnp.swapaxes(w_in[:, :, OFF_HYENA:OFF_S5], 1, 2).astype(BF16)
    w_rest = w_in[:, :, OFF_S5:].astype(BF16)
    s5_mats = jax.vmap(_s5_chunk_matrices)(s5_lambda_re, s5_lambda_im, s5_log_step, s5_b_re, s5_b_im,
                                           s5_c_re, s5_c_im)
    for l in range(DEPTH):
        qkvo = _mm(xb, w_in, l, 0, 4 * BRANCH, tm=1024, tn=512, name="proj_qkvo")
        gates = _mm(xb, w_gates, l, 0, N_GATES, tm=1024, tn=N_GATES, name="proj_gates")
        hp_t = _mm_nt(w_hy_t, l, xb, tr=512, tm=1024, name="proj_hyena_t")
        rest = _mm(xb, w_rest, l, 0, BRANCH + 3 * D_MODEL, tm=1024, tn=512, name="proj_s5_mixgates")

        hdir = _mlstm(qkvo, gates, mlstm_gate_bias, l)
        hm = _mlstm_post(hdir, qkvo, mlstm_norm_gain, l)
        hy_t = _hyena(hp_t, l, hyena_conv_w, hyena_conv_b, hyena_w1, hyena_b1, hyena_w2, hyena_b2,
                      hyena_freq, hyena_w3, hyena_decay, hyena_skip, feat_t, t_row)
        s5y = _s5(rest[:, :BRANCH], s5_mats, s5_skip, l)
        merged = _merge(hm, hy_t, s5y, rest, w_mlstm_out, w_hyena_out, w_s5_glu, l)
        mix = _mm(merged, w_out, l, 0, D_MODEL, tm=1024, tn=512, name="proj_out")
        xf, xb = _res_ln(xf, mix, ln1_g, ln1_b, l, transposed=False)

        q = _mm(xb, peer_w_q, l, 0, PEER_HEADS * 2 * PEER_KEYS, tm=1024, tn=512, name="peer_query")
        sel = _peer_select(q, peer_subkeys, l)
        ffn_t = _peer_dense(xb, peer_u[l].astype(BF16), peer_v[l], sel)
        xf, xb = _res_ln(xf, ffn_t, ln2_g, ln2_b, l, transposed=True)
    return xf.reshape(1, SEQ, D_MODEL)
```

```python
import functools
import math

import numpy as np
import jax
import jax.numpy as jnp
from jax import lax
from jax.experimental import pallas as pl
from jax.experimental.pallas import tpu as pltpu

F32 = jnp.float32
BF16 = jnp.bfloat16

D_MODEL = 2048
SEQ = 8192
DEPTH = 4
BRANCH = 1024
HEADS = 4
HEAD_DIM = 256
CHUNK = 128
N_GATES = 16
HY_HID = 64
HY_EMB_PAD = 128
S5_GROUP = 16
S5_GROUPS = 64
S5_STATE = 64
S5_T = 16
S5_CHUNKS = SEQ // S5_T
S5_LOG_CHUNKS = 9
PEER_HEADS = 8
PEER_KEYS = 128
PEER_TOPK = 16
N_EXPERTS = PEER_KEYS * PEER_KEYS
ALPHA = (2 * DEPTH) ** 0.25
LN_EPS = 1e-5
FFT_N = 2 * SEQ
FFT_R = 128
OFF_GATES = 4 * BRANCH
OFF_HYENA = OFF_GATES + N_GATES
OFF_S5 = OFF_HYENA + 3 * BRANCH
OFF_MIXG = OFF_S5 + BRANCH

VMEM_LIMIT = 56 * 1024 * 1024


def _cparams(sem):
    return pltpu.CompilerParams(dimension_semantics=sem, vmem_limit_bytes=VMEM_LIMIT)


def _split_bf16(a):
    hi = a.astype(BF16)
    lo = (a - hi.astype(F32)).astype(BF16)
    return hi, lo


def _dot3(a, b_hi, b_lo):
    a_hi, a_lo = _split_bf16(a)
    acc = jnp.dot(a_hi, b_hi, preferred_element_type=F32)
    acc += jnp.dot(a_lo, b_hi, preferred_element_type=F32)
    acc += jnp.dot(a_hi, b_lo, preferred_element_type=F32)
    return acc


def _dot3_both(a, b):
    b_hi, b_lo = _split_bf16(b)
    return _dot3(a, b_hi, b_lo)


def _mm_kernel(a_ref, b_ref, o_ref):
    o_ref[...] = jnp.dot(a_ref[...].astype(BF16), b_ref[...].astype(BF16),
                         preferred_element_type=F32).astype(o_ref.dtype)


def _mm(a, w, l, col_off, n_cols, *, tm, tn, name):
    m, k = a.shape
    cb = col_off // tn
    assert col_off % tn == 0 and n_cols % tn == 0 and m % tm == 0
    return pl.pallas_call(
        _mm_kernel,
        grid=(m // tm, n_cols // tn),
        in_specs=[pl.BlockSpec((tm, k), lambda i, j: (i, 0)),
                  pl.BlockSpec((None, k, tn), lambda i, j: (l, 0, cb + j))],
        out_specs=pl.BlockSpec((tm, tn), lambda i, j: (i, j)),
        out_shape=jax.ShapeDtypeStruct((m, n_cols), F32),
        compiler_params=_cparams(("parallel", "parallel")),
        name=name,
    )(a, w)


def _mm_nt_kernel(w_ref, a_ref, o_ref):
    o_ref[...] = lax.dot_general(w_ref[...].astype(BF16), a_ref[...].astype(BF16),
                                 (((1,), (1,)), ((), ())), preferred_element_type=F32).astype(o_ref.dtype)


def _mm_nt(wt, l, a, *, tr, tm, name):
    _, r, k = wt.shape
    m = a.shape[0]
    return pl.pallas_call(
        _mm_nt_kernel,
        grid=(r // tr, m // tm),
        in_specs=[pl.BlockSpec((None, tr, k), lambda i, j: (l, i, 0)),
                  pl.BlockSpec((tm, k), lambda i, j: (j, 0))],
        out_specs=pl.BlockSpec((tr, tm), lambda i, j: (i, j)),
        out_shape=jax.ShapeDtypeStruct((r, m), BF16),
        compiler_params=_cparams(("parallel", "parallel")),
        name=name,
    )(wt, a)


def _log_sigmoid(x):
    return jnp.minimum(x, 0.0) - jnp.log1p(jnp.exp(-jnp.abs(x)))


def _mlstm_chunk(d, q, k, v, li_r, lf_r, li_c, lf_c, c_sc, n_sc, m_sc):
    n = CHUNK
    row = lax.broadcasted_iota(jnp.int32, (n, n), 0)
    col = lax.broadcasted_iota(jnp.int32, (n, n), 1)
    valid = (row >= col) if d == 0 else (row <= col)
    valid_t = (col >= row) if d == 0 else (col <= row)
    b_col = jnp.sum(jnp.where(valid, lf_r, 0.0), axis=1, keepdims=True)
    b_row = jnp.sum(jnp.where(valid_t, lf_c, 0.0), axis=0, keepdims=True)
    total = jnp.sum(lf_r, axis=1, keepdims=True)

    m_prev = m_sc[...]
    m_inter = b_col + m_prev
    log_d = jnp.where(valid, b_col - b_row + li_r, -jnp.inf)
    m_t = jnp.maximum(m_inter, jnp.max(log_d, axis=1, keepdims=True))
    inter = jnp.exp(m_inter - m_t)
    dmat = jnp.exp(log_d - m_t)

    k = k * (HEAD_DIM ** -0.5)
    qb = q.astype(BF16)
    kb = k.astype(BF16)
    vb = v.astype(BF16)
    s = lax.dot_general(qb, kb, (((1,), (1,)), ((), ())), preferred_element_type=F32) * dmat
    num = jnp.dot(s.astype(BF16), vb, preferred_element_type=F32)
    num += inter * jnp.dot(qb, c_sc[...].astype(BF16), preferred_element_type=F32)
    qn = jnp.sum(q * n_sc[...], axis=1, keepdims=True)
    den = jnp.sum(s, axis=1, keepdims=True) + inter * qn
    h_out = num / jnp.maximum(jnp.abs(den), jnp.exp(-m_t))

    log_w = total - b_col + li_c
    m_new = jnp.maximum(total + m_prev, jnp.max(log_w, axis=0, keepdims=True))
    w = jnp.exp(log_w - m_new)
    decay = jnp.exp(total + m_prev - m_new)
    kw = k * w
    c_sc[...] = decay * c_sc[...] + lax.dot_general(
        kw.astype(BF16), vb, (((0,), (0,)), ((), ())), preferred_element_type=F32)
    n_sc[...] = decay * n_sc[...] + jnp.sum(kw, axis=0, keepdims=True)
    m_sc[...] = m_new
    return h_out


def _mlstm_kernel(qkv_f_ref, qkv_b_ref, gr_f_ref, gr_b_ref, gc_f_ref, gc_b_ref, bias_ref,
                  o_f_ref, o_b_ref, c_sc, n_sc, m_sc):
    @pl.when(pl.program_id(0) == 0)
    def _():
        c_sc[...] = jnp.zeros_like(c_sc)
        n_sc[...] = jnp.zeros_like(n_sc)
        m_sc[...] = jnp.zeros_like(m_sc)

    for d, (qkv_ref, gr_ref, gc_ref, o_ref) in enumerate(((qkv_f_ref, gr_f_ref, gc_f_ref, o_f_ref),
                                                         (qkv_b_ref, gr_b_ref, gc_b_ref, o_b_ref))):
        for h in range(HEADS):
            cols = slice(h * HEAD_DIM, (h + 1) * HEAD_DIM)
            b_i = bias_ref[d * 2 * HEADS + h]
            b_f = bias_ref[d * 2 * HEADS + HEADS + h]
            chain = d * HEADS + h
            o_ref[:, cols] = _mlstm_chunk(
                d,
                qkv_ref[:, cols],
                qkv_ref[:, BRANCH + h * HEAD_DIM:BRANCH + (h + 1) * HEAD_DIM],
                qkv_ref[:, 2 * BRANCH + h * HEAD_DIM:2 * BRANCH + (h + 1) * HEAD_DIM],
                gr_ref[h] + b_i, _log_sigmoid(gr_ref[HEADS + h] + b_f),
                gc_ref[h] + b_i, _log_sigmoid(gc_ref[HEADS + h] + b_f),
                c_sc.at[chain], n_sc.at[chain], m_sc.at[chain])


def _mlstm(qkvo, gates, gate_bias, l):
    nc = SEQ // CHUNK
    g_rows = gates.T.reshape(N_GATES, 1, SEQ)
    g_cols = gates.T.reshape(N_GATES, SEQ, 1)
    bias = gate_bias.reshape(DEPTH, N_GATES, 1, 1)
    half = N_GATES // 2
    chains = 2 * HEADS
    out = jax.ShapeDtypeStruct((SEQ, BRANCH), F32)
    return pl.pallas_call(
        _mlstm_kernel,
        grid=(nc,),
        in_specs=[
            pl.BlockSpec((CHUNK, 4 * BRANCH), lambda c: (c, 0)),
            pl.BlockSpec((CHUNK, 4 * BRANCH), lambda c: (nc - 1 - c, 0)),
            pl.BlockSpec((half, 1, CHUNK), lambda c: (0, 0, c)),
            pl.BlockSpec((half, 1, CHUNK), lambda c: (1, 0, nc - 1 - c)),
            pl.BlockSpec((half, CHUNK, 1), lambda c: (0, c, 0)),
            pl.BlockSpec((half, CHUNK, 1), lambda c: (1, nc - 1 - c, 0)),
            pl.BlockSpec((None, N_GATES, 1, 1), lambda c: (l, 0, 0, 0)),
        ],
        out_specs=[pl.BlockSpec((CHUNK, BRANCH), lambda c: (c, 0)),
                   pl.BlockSpec((CHUNK, BRANCH), lambda c: (nc - 1 - c, 0))],
        out_shape=[out, out],
        scratch_shapes=[pltpu.VMEM((chains, HEAD_DIM, HEAD_DIM), F32),
                        pltpu.VMEM((chains, 1, HEAD_DIM), F32),
                        pltpu.VMEM((chains, 1, 1), F32)],
        compiler_params=_cparams(("arbitrary",)),
        name="mlstm_scan",
    )(qkvo, qkvo, g_rows, g_rows, g_cols, g_cols, bias)


def _mlstm_post_kernel(hf_ref, hb_ref, o_ref, gain_ref, out_ref):
    h = hf_ref[...] + hb_ref[...]
    h = jax.nn.sigmoid(o_ref[...]) * h
    for hd in range(HEADS):
        sl = slice(hd * HEAD_DIM, (hd + 1) * HEAD_DIM)
        hh = h[:, sl]
        mu = jnp.mean(hh, axis=1, keepdims=True)
        var = jnp.mean(jnp.square(hh - mu), axis=1, keepdims=True)
        out_ref[:, sl] = ((hh - mu) * lax.rsqrt(var + LN_EPS) * gain_ref[:, sl]).astype(out_ref.dtype)


def _mlstm_post(hdir, qkvo, gain, l, *, tm=512):
    return pl.pallas_call(
        _mlstm_post_kernel,
        grid=(SEQ // tm,),
        in_specs=[pl.BlockSpec((tm, BRANCH), lambda i: (i, 0)),
                  pl.BlockSpec((tm, BRANCH), lambda i: (i, 0)),
                  pl.BlockSpec((tm, BRANCH), lambda i: (i, 3)),
                  pl.BlockSpec((None, 1, BRANCH), lambda i: (l, 0, 0))],
        out_specs=pl.BlockSpec((tm, BRANCH), lambda i: (i, 0)),
        out_shape=jax.ShapeDtypeStruct((SEQ, BRANCH), BF16),
        compiler_params=_cparams(("parallel",)),
        name="mlstm_post",
    )(hdir[0], hdir[1], qkvo, gain.reshape(DEPTH, 1, BRANCH))


HY_CT = 32
HY_RB = 512


@functools.lru_cache(maxsize=None)
def _fft_constants():
    r = FFT_R
    idx = np.arange(r)
    th = 2.0 * np.pi * np.outer(idx, idx) / r
    cs, sn = np.cos(th), np.sin(th)
    ph = 2.0 * np.pi * np.outer(idx, idx) / FFT_N
    f1 = np.concatenate([cs, -sn], axis=1)
    m3 = np.block([[cs, -sn], [sn, cs]])
    m4 = np.block([[cs, sn], [-sn, cs]])
    m6 = np.concatenate([cs, -sn], axis=0) / FFT_N
    m6[:, r // 2:] = 0.0
    consts = {name: jnp.asarray(mat, F32).astype(BF16)
              for name, mat in (("f1", f1), ("m3", m3), ("m4", m4), ("m6", m6))}
    consts["tw"] = (jnp.asarray(np.cos(ph), F32), jnp.asarray(np.sin(ph), F32))
    return consts


def _rows(i, n):
    return pl.ds(pl.multiple_of(i * n, n), n)


def _for_each(n, body, unroll=1):
    def step(i, carry):
        body(i)
        return carry

    lax.fori_loop(0, n, step, 0, unroll=unroll)


def _fwd_stage1(sig_ref, hi_ref, ct, t_buf, a_buf, b_buf, f1_ref, twr_ref, twi_ref):
    r = FFT_R
    zeros = jnp.zeros((r // 2, r), F32)

    def tr_in(c):
        hi = zeros if hi_ref is None else hi_ref[c]
        t_buf[_rows(c, r), :] = jnp.concatenate([sig_ref[c], hi], axis=0).T.astype(BF16)

    _for_each(ct, tr_in, unroll=4)

    def stage1(b):
        rows = _rows(b, HY_RB)
        a_buf[rows, :] = jnp.dot(t_buf[rows, :], f1_ref[...], preferred_element_type=F32)

    _for_each(ct * r // HY_RB, stage1)

    def twiddle(c):
        rows = _rows(c, r)
        a = a_buf[rows, :]
        ar, ai = a[:, :r], a[:, r:]
        twr, twi = twr_ref[...], twi_ref[...]
        b_buf[rows, :r] = (ar * twr + ai * twi).T.astype(BF16)
        b_buf[rows, r:] = (ai * twr - ar * twi).T.astype(BF16)

    _for_each(ct, twiddle, unroll=4)


def _hy_mlp_kernel(feat_ref, w1_ref, b1_ref, w2_ref, b2_ref, fr_ref, out_ref):
    h = _dot3_both(w1_ref[...], feat_ref[...]) + b1_ref[...]
    h = jnp.sin(fr_ref[0] * h)
    h = _dot3_both(w2_ref[...], h) + b2_ref[...]
    out_ref[...] = jnp.sin(fr_ref[1] * h)


def _hy_filter_kernel(w3_ref, h_ref, dec_ref, t_ref, out_ref):
    d = pl.program_id(0)
    filt = _dot3_both(w3_ref[...], h_ref[...])
    filt = filt * jnp.exp(-t_ref[...] * jnp.abs(dec_ref[...]))
    filt = filt / (jnp.sum(jnp.abs(filt), axis=1, keepdims=True) + 1e-6)
    lane = lax.broadcasted_iota(jnp.int32, filt.shape, 1)
    out_ref[...] = jnp.where((lane == 0) & (d == 1), 0.0, filt)


def _hy_spec_kernel(fa_ref, fb_ref, f1_ref, m3_ref, twr_ref, twi_ref, out_ref, t_buf, a_buf, b_buf, *, ct):
    r = FFT_R
    nblk = ct * r // HY_RB
    _fwd_stage1(fa_ref, fb_ref, ct, t_buf, a_buf, b_buf, f1_ref, twr_ref, twi_ref)

    def put(b):
        rows = _rows(b, HY_RB)
        out_ref[rows, :] = jnp.dot(b_buf[rows, :], m3_ref[...], preferred_element_type=F32)

    _for_each(nblk, put)


def _shift_prev(p, n1, n2):
    r1 = pltpu.roll(p, 1, 2)
    r2 = pltpu.roll(r1, 1, 1)
    return jnp.where(n2 == 0, jnp.where(n1 == 0, 0.0, r2), r1)


def _shift_next(p, n1, n2):
    s1, s2 = p.shape[1], p.shape[2]
    r1 = pltpu.roll(p, s2 - 1, 2)
    r2 = pltpu.roll(r1, s1 - 1, 1)
    return jnp.where(n2 == s2 - 1, jnp.where(n1 == s1 - 1, 0.0, r2), r1)


def _hy_conv_kernel(p0_ref, p1_ref, p2_ref, cw_ref, cb_ref, skip_ref, h_ref,
                    f1_ref, m3_ref, m4_ref, m6_ref, twr_ref, twi_ref,
                    out_ref, z_buf, t_buf, a_buf, b_buf, *, ct):
    r = FFT_R
    n1_len = SEQ // r
    nblk = ct * r // HY_RB
    shp = (ct, n1_len, r)
    n1 = lax.broadcasted_iota(jnp.int32, shp, 1)
    n2 = lax.broadcasted_iota(jnp.int32, shp, 2)

    def short_conv(p_ref, g):
        p = p_ref[...].astype(F32)
        return (_shift_prev(p, n1, n2) * cw_ref[3 * g] + p * cw_ref[3 * g + 1]
                + _shift_next(p, n1, n2) * cw_ref[3 * g + 2] + cb_ref[g])

    z_buf[...] = short_conv(p1_ref, 1) * short_conv(p2_ref, 2)
    _fwd_stage1(z_buf, None, ct, t_buf, a_buf, b_buf, f1_ref, twr_ref, twi_ref)

    def freq_domain(b):
        rows = _rows(b, HY_RB)
        x = jnp.dot(b_buf[rows, :], m3_ref[...], preferred_element_type=F32)
        h = h_ref[rows, :]
        xr, xi = x[:, :r], x[:, r:]
        hr, hi = h[:, :r], h[:, r:]
        y = jnp.concatenate([xr * hr - xi * hi, xr * hi + xi * hr], axis=1).astype(BF16)
        a_buf[rows, :] = jnp.dot(y, m4_ref[...], preferred_element_type=F32)

    _for_each(nblk, freq_domain)

    def inv_twiddle(c):
        rows = _rows(c, r)
        b = a_buf[rows, :]
        br, bi = b[:, :r], b[:, r:]
        twr, twi = twr_ref[...], twi_ref[...]
        b_buf[rows, :r] = (br * twr - bi * twi).T.astype(BF16)
        b_buf[rows, r:] = (br * twi + bi * twr).T.astype(BF16)

    _for_each(ct, inv_twiddle, unroll=4)

    def stage6(b):
        rows = _rows(b, HY_RB)
        a_buf[rows, :r] = jnp.dot(b_buf[rows, :], m6_ref[...], preferred_element_type=F32)

    _for_each(nblk, stage6)

    def finish(c):
        y_c = a_buf[_rows(c, r), :r].T[:n1_len, :]
        out_ref[c] = y_c + skip_ref[c] * z_buf[c]

    _for_each(ct, finish, unroll=4)
    out_ref[...] = short_conv(p0_ref, 0) * out_ref[...]


def _hyena(hp_t, l, conv_w, conv_b, w1, b1, w2, b2, freq, w3, decay, skip, feat_t, t_row):
    r = FFT_R
    n1 = SEQ // r
    ct = HY_CT
    k = _fft_constants()
    f1, m3, m4, m6 = k["f1"], k["m3"], k["m4"], k["m6"]
    twr, twi = k["tw"]

    def full(a):
        return pl.BlockSpec(a.shape, lambda *_: (0,) * a.ndim)

    h2 = pl.pallas_call(
        _hy_mlp_kernel,
        grid=(2,),
        in_specs=[pl.BlockSpec((feat_t.shape[0], SEQ), lambda i: (0, i)),
                  pl.BlockSpec((None, HY_HID, feat_t.shape[0]), lambda i: (l, 0, 0)),
                  pl.BlockSpec((None, HY_HID, 1), lambda i: (l, 0, 0)),
                  pl.BlockSpec((None, HY_HID, HY_HID), lambda i: (l, 0, 0)),
                  pl.BlockSpec((None, HY_HID, 1), lambda i: (l, 0, 0)),
                  pl.BlockSpec((None, 2, HY_HID, 1), lambda i: (l, 0, 0, 0))],
        out_specs=pl.BlockSpec((HY_HID, SEQ), lambda i: (0, i)),
        out_shape=jax.ShapeDtypeStruct((HY_HID, 2 * SEQ), F32),
        compiler_params=_cparams(("parallel",)),
        name="hyena_filter_mlp",
    )(feat_t, jnp.pad(jnp.swapaxes(w1, 1, 2), ((0, 0), (0, 0), (0, feat_t.shape[0] - w1.shape[1]))),
      b1[:, :, None], jnp.swapaxes(w2, 1, 2), b2[:, :, None], freq[:, :, :, None])

    rt = 128
    filt = pl.pallas_call(
        _hy_filter_kernel,
        grid=(2, BRANCH // rt),
        in_specs=[pl.BlockSpec((None, rt, HY_HID), lambda d, i: (l, d * (BRANCH // rt) + i, 0)),
                  pl.BlockSpec((HY_HID, SEQ), lambda d, i: (0, d)),
                  pl.BlockSpec((None, None, rt, 1), lambda d, i: (l, d, i, 0)),
                  pl.BlockSpec((1, SEQ), lambda d, i: (0, d))],
        out_specs=pl.BlockSpec((None, rt, SEQ), lambda d, i: (d, i, 0)),
        out_shape=jax.ShapeDtypeStruct((2, BRANCH, SEQ), F32),
        compiler_params=_cparams(("parallel", "parallel")),
        name="hyena_filter",
    )(jnp.swapaxes(w3, 1, 2), h2, decay[:, :, :, None], t_row)
    filt = filt.reshape(2, BRANCH, n1, r)

    scratch = [pltpu.VMEM((ct * r, r), BF16), pltpu.VMEM((ct * r, 2 * r), F32), pltpu.VMEM((ct * r, 2 * r), BF16)]
    spec = pl.pallas_call(
        functools.partial(_hy_spec_kernel, ct=ct),
        grid=(BRANCH // ct,),
        in_specs=[pl.BlockSpec((None, ct, n1, r), lambda i: (0, i, 0, 0)),
                  pl.BlockSpec((None, ct, n1, r), lambda i: (1, i, 0, 0)),
                  full(f1), full(m3), full(twr), full(twi)],
        out_specs=pl.BlockSpec((ct * r, 2 * r), lambda i: (i, 0)),
        out_shape=jax.ShapeDtypeStruct((BRANCH * r, 2 * r), F32),
        scratch_shapes=scratch,
        compiler_params=_cparams(("parallel",)),
        name="hyena_filter_spectrum",
    )(filt, filt, f1, m3, twr, twi)

    hp3 = hp_t.reshape(3 * BRANCH, n1, r)
    nb = BRANCH // ct
    cw = conv_w.reshape(DEPTH, 3, 3, BRANCH).transpose(0, 2, 1, 3).reshape(DEPTH, 9, BRANCH, 1, 1)
    cb = conv_b.reshape(DEPTH, 3, BRANCH, 1, 1)
    consts = (f1, m3, m4, m6, twr, twi)
    return pl.pallas_call(
        functools.partial(_hy_conv_kernel, ct=ct),
        grid=(nb,),
        in_specs=[pl.BlockSpec((ct, n1, r), lambda i: (i, 0, 0)),
                  pl.BlockSpec((ct, n1, r), lambda i: (nb + i, 0, 0)),
                  pl.BlockSpec((ct, n1, r), lambda i: (2 * nb + i, 0, 0)),
                  pl.BlockSpec((None, 9, ct, 1, 1), lambda i: (l, 0, i, 0, 0)),
                  pl.BlockSpec((None, 3, ct, 1, 1), lambda i: (l, 0, i, 0, 0)),
                  pl.BlockSpec((None, ct, 1, 1), lambda i: (l, i, 0, 0)),
                  pl.BlockSpec((ct * r, 2 * r), lambda i: (i, 0)),
                  *[full(a) for a in consts]],
        out_specs=pl.BlockSpec((ct, n1, r), lambda i: (i, 0, 0)),
        out_shape=jax.ShapeDtypeStruct((BRANCH, n1, r), F32),
        scratch_shapes=[pltpu.VMEM((ct, n1, r), F32)] + scratch,
        compiler_params=_cparams(("parallel",)),
        name="hyena_conv",
    )(hp3, hp3, hp3, cw, cb, skip.reshape(DEPTH, BRANCH, 1, 1), spec, *consts).reshape(BRANCH, SEQ)


def _s5_chunk_matrices(lam_re, lam_im, log_step, b_re, b_im, c_re, c_im):
    t = S5_T
    lam = lax.complex(lam_re, lam_im)
    step = jnp.exp(log_step)[..., None]
    lam_dt = lam * step
    b_bar = ((jnp.exp(lam_dt) - 1.0) / lam)[..., None] * lax.complex(b_re, b_im)
    c_mat = lax.complex(c_re, c_im)
    pw = jnp.exp(lam_dt[None] * jnp.arange(t + 1, dtype=F32)[:, None, None, None])
    kern = jnp.einsum("dgvp,mdgp,dgpn->mdgvn", c_mat, pw[:t], b_bar, precision="highest").real
    lag = np.arange(t)[:, None] - np.arange(t)[None, :]
    pick = np.stack([lag[:, :, None] == np.arange(t), -lag[:, :, None] == np.arange(t)], axis=-1)
    kfb = jnp.einsum("ajmd,mdgvn->ajgvn", jnp.asarray(pick, F32), kern, precision="highest")
    w_intra = jnp.transpose(kfb, (2, 1, 4, 0, 3)).reshape(S5_GROUPS, t * S5_GROUP, t * S5_GROUP)
    in_f = pw[:t, 0][::-1][..., None] * b_bar[0][None]
    in_b = pw[:t, 1][..., None] * b_bar[1][None]

    def in_mat(m):
        m = jnp.transpose(m, (1, 0, 3, 2)).reshape(S5_GROUPS, t * S5_GROUP, S5_STATE)
        return jnp.concatenate([m.real, m.imag], axis=-1)

    w_in = jnp.concatenate([in_mat(in_f), in_mat(in_b)], axis=-1)
    out_f = c_mat[0][None] * pw[1:, 0][:, :, None, :]
    out_b = c_mat[1][None] * pw[1:, 1][::-1][:, :, None, :]

    def out_mat(m):
        m = jnp.transpose(m, (1, 3, 0, 2)).reshape(S5_GROUPS, S5_STATE, t * S5_GROUP)
        return jnp.concatenate([m.real, -m.imag], axis=1)

    w_out = jnp.concatenate([out_mat(out_f), out_mat(out_b)], axis=1)
    a_pow = jnp.exp(lam_dt[None] * (t * 2.0 ** jnp.arange(S5_LOG_CHUNKS, dtype=F32))[:, None, None, None])
    a_pow = jnp.transpose(a_pow, (2, 1, 0, 3))
    a_re = jnp.concatenate([a_pow.real, a_pow.real], axis=-1)
    a_im = jnp.concatenate([-a_pow.imag, a_pow.imag], axis=-1)
    return w_intra, w_in, w_out, a_re, a_im


def _s5_kernel(u_ref, wi_ref, win_ref, wout_ref, are_ref, aim_ref, skip_ref, y_ref):
    nc = S5_CHUNKS
    half = S5_STATE
    u = u_ref[...]
    ub = u.astype(BF16)
    y = jnp.dot(ub, wi_ref[...].astype(BF16), preferred_element_type=F32) + skip_ref[...] * u
    s_in = jnp.dot(ub, win_ref[...].astype(BF16), preferred_element_type=F32)
    row = lax.broadcasted_iota(jnp.int32, (nc, 2 * half), 0)

    def scan(s, d):
        for step in range(S5_LOG_CHUNKS):
            sh = 1 << step
            if d == 0:
                moved = jnp.where(row >= sh, pltpu.roll(s, sh, 0), 0.0)
            else:
                moved = jnp.where(row < nc - sh, pltpu.roll(s, nc - sh, 0), 0.0)
            swapped = pltpu.roll(moved, half, 1)
            s = s + are_ref[d, step:step + 1, :] * moved + aim_ref[d, step:step + 1, :] * swapped
        if d == 0:
            return jnp.where(row >= 1, pltpu.roll(s, 1, 0), 0.0)
        return jnp.where(row < nc - 1, pltpu.roll(s, nc - 1, 0), 0.0)

    e = jnp.concatenate([scan(s_in[:, :2 * half], 0), scan(s_in[:, 2 * half:], 1)], axis=1)
    y_ref[...] = y + jnp.dot(e.astype(BF16), wout_ref[...].astype(BF16), preferred_element_type=F32)


def _s5(su, mats, skip, l):
    w_intra, w_in, w_out, a_re, a_im = mats
    t = S5_T
    gw = t * S5_GROUP
    u = su.reshape(S5_CHUNKS, t, S5_GROUPS, S5_GROUP).transpose(2, 0, 1, 3).reshape(S5_GROUPS, S5_CHUNKS, gw)
    skip_t = jnp.tile(skip.reshape(DEPTH, S5_GROUPS, 1, S5_GROUP), (1, 1, t, 1)).reshape(DEPTH, S5_GROUPS, 1, gw)
    y = pl.pallas_call(
        _s5_kernel,
        grid=(S5_GROUPS,),
        in_specs=[pl.BlockSpec((None, S5_CHUNKS, gw), lambda g: (g, 0, 0)),
                  pl.BlockSpec((None, None, gw, gw), lambda g: (l, g, 0, 0)),
                  pl.BlockSpec((None, None, gw, 4 * S5_STATE), lambda g: (l, g, 0, 0)),
                  pl.BlockSpec((None, None, 4 * S5_STATE, gw), lambda g: (l, g, 0, 0)),
                  pl.BlockSpec((None, None, 2, S5_LOG_CHUNKS, 2 * S5_STATE), lambda g: (l, g, 0, 0, 0)),
                  pl.BlockSpec((None, None, 2, S5_LOG_CHUNKS, 2 * S5_STATE), lambda g: (l, g, 0, 0, 0)),
                  pl.BlockSpec((None, None, 1, gw), lambda g: (l, g, 0, 0))],
        out_specs=pl.BlockSpec((None, S5_CHUNKS, gw), lambda g: (g, 0, 0)),
        out_shape=jax.ShapeDtypeStruct((S5_GROUPS, S5_CHUNKS, gw), F32),
        compiler_params=_cparams(("parallel",)),
        name="s5_scan",
    )(u, w_intra, w_in, w_out, a_re, a_im, skip_t)
    return y.reshape(S5_GROUPS, S5_CHUNKS, t, S5_GROUP).transpose(1, 2, 0, 3).reshape(SEQ, BRANCH)


def _merge_kernel(hm_ref, hyt_ref, s5_ref, wa_ref, wb_ref, wga_ref, wgg_ref, g0_ref, g1_ref, g2_ref,
                  out_ref, hy_sc):
    @pl.when(pl.program_id(1) == 0)
    def _():
        hy_sc[...] = hyt_ref[...].T.astype(BF16)

    def proj(a, w_ref):
        return jnp.dot(a, w_ref[...].astype(BF16), preferred_element_type=F32)

    s5b = s5_ref[...].astype(BF16)
    out_a = proj(hm_ref[...], wa_ref)
    out_b = proj(hy_sc[...], wb_ref)
    out_c = proj(s5b, wga_ref) * jax.nn.sigmoid(proj(s5b, wgg_ref))
    merged = (jax.nn.sigmoid(g0_ref[...]) * out_a + jax.nn.sigmoid(g1_ref[...]) * out_b
              + jax.nn.sigmoid(g2_ref[...]) * out_c)
    out_ref[...] = merged.astype(out_ref.dtype)


def _merge(hm, hy_t, s5y, rest, w_a, w_b, w_glu, l, *, tm=512, tn=512):
    nj = D_MODEL // tn
    goff = BRANCH // tn
    return pl.pallas_call(
        _merge_kernel,
        grid=(SEQ // tm, nj),
        in_specs=[pl.BlockSpec((tm, BRANCH), lambda i, j: (i, 0)),
                  pl.BlockSpec((BRANCH, tm), lambda i, j: (0, i)),
                  pl.BlockSpec((tm, BRANCH), lambda i, j: (i, 0)),
                  pl.BlockSpec((None, BRANCH, tn), lambda i, j: (l, 0, j)),
                  pl.BlockSpec((None, BRANCH, tn), lambda i, j: (l, 0, j)),
                  pl.BlockSpec((None, BRANCH, tn), lambda i, j: (l, 0, j)),
                  pl.BlockSpec((None, BRANCH, tn), lambda i, j: (l, 0, nj + j)),
                  pl.BlockSpec((tm, tn), lambda i, j: (i, goff + j)),
                  pl.BlockSpec((tm, tn), lambda i, j: (i, goff + nj + j)),
                  pl.BlockSpec((tm, tn), lambda i, j: (i, goff + 2 * nj + j))],
        out_specs=pl.BlockSpec((tm, tn), lambda i, j: (i, j)),
        out_shape=jax.ShapeDtypeStruct((SEQ, D_MODEL), BF16),
        scratch_shapes=[pltpu.VMEM((tm, BRANCH), BF16)],
        compiler_params=_cparams(("parallel", "arbitrary")),
        name="branch_merge",
    )(hm, hy_t, s5y, w_a, w_b, w_glu, w_glu, rest, rest, rest)


def _layer_norm_rows(y, g, b):
    mu = jnp.mean(y, axis=1, keepdims=True)
    var = jnp.mean(jnp.square(y - mu), axis=1, keepdims=True)
    return (y - mu) * lax.rsqrt(var + LN_EPS) * g + b


def _res_ln_kernel(x_ref, f_ref, g_ref, b_ref, o_ref, ob_ref, *, transposed):
    f = f_ref[...]
    if transposed:
        f = f.T
    y = _layer_norm_rows(ALPHA * x_ref[...] + f, g_ref[...], b_ref[...])
    o_ref[...] = y
    ob_ref[...] = y.astype(BF16)


def _res_ln(x, f, g, b, l, *, transposed, tm=256):
    f_spec = (pl.BlockSpec((D_MODEL, tm), lambda i: (0, i)) if transposed
              else pl.BlockSpec((tm, D_MODEL), lambda i: (i, 0)))
    n_tok = x.shape[0]
    return pl.pallas_call(
        functools.partial(_res_ln_kernel, transposed=transposed),
        grid=(n_tok // tm,),
        in_specs=[pl.BlockSpec((tm, D_MODEL), lambda i: (i, 0)), f_spec,
                  pl.BlockSpec((None, 1, D_MODEL), lambda i: (l, 0, 0)),
                  pl.BlockSpec((None, 1, D_MODEL), lambda i: (l, 0, 0))],
        out_specs=[pl.BlockSpec((tm, D_MODEL), lambda i: (i, 0)),
                   pl.BlockSpec((tm, D_MODEL), lambda i: (i, 0))],
        out_shape=[jax.ShapeDtypeStruct((n_tok, D_MODEL), F32), jax.ShapeDtypeStruct((n_tok, D_MODEL), BF16)],
        compiler_params=_cparams(("parallel",)),
        name="residual_layernorm",
    )(x, f, g.reshape(DEPTH, 1, D_MODEL), b.reshape(DEPTH, 1, D_MODEL))


PEER_NO_RANK = 64.0


def _extract_sorted(s, n, out_ref, base, want_rank=False):
    rank = jnp.full_like(s, PEER_NO_RANK) if want_rank else None
    for r in range(n):
        m = jnp.max(s, axis=0, keepdims=True)
        out_ref[base + r:base + r + 1, :] = m
        hit = s == m
        if want_rank:
            rank = jnp.where(hit, float(r), rank)
        s = jnp.where(hit, -jnp.inf, s)
    return rank


def _peer_select_kernel(q_ref, keys_ref, cnt_ref, r2_ref, e1_ref, e2_ref, top_sc, kth_sc):
    k = PEER_TOPK
    qb = q_ref[...].astype(BF16)
    half = PEER_KEYS
    for h in range(PEER_HEADS):
        sc = []
        rank2 = None
        for c in range(2):
            qs = qb[:, (2 * h + c) * half:(2 * h + c + 1) * half]
            s = lax.dot_general(keys_ref[h, c].astype(BF16), qs, (((1,), (1,)), ((), ())),
                                preferred_element_type=F32)
            sc.append(s)
            rank2 = _extract_sorted(s, k, top_sc, c * k, want_rank=(c == 1))
        a = top_sc[0:k, :]
        b = top_sc[k:2 * k, :]
        sub = lax.broadcasted_iota(jnp.int32, (8, a.shape[1]), 0)
        groups = [a[0:1] + b[0:8], a[0:1] + b[8:16], a[1:2] + b[0:8]]
        for i in range(2, 8):
            groups.append(jnp.where(sub < k // (i + 1), a[i:i + 1] + b[0:8], -jnp.inf))
        groups.append(a[8:16] + b[0:1])
        cand = jnp.concatenate(groups, axis=0)
        _extract_sorted(cand, k, kth_sc, 0)
        tau = kth_sc[k - 1:k, :]
        top = a[0:1] + b[0:1]
        z = jnp.sum(jnp.where(cand >= tau, jnp.exp(cand - top), 0.0), axis=0, keepdims=True)
        cnt = jnp.zeros_like(sc[0])
        for j in range(k):
            bj = top_sc[k + j:k + j + 1, :]
            cnt = jnp.where(sc[0] + bj >= tau, j + 1.0, cnt)
        cnt_ref[h] = cnt
        r2_ref[h] = rank2.astype(BF16)
        e1_ref[h] = jnp.exp(sc[0] - a[0:1])
        e2_ref[h] = (jnp.exp(sc[1] - b[0:1]) / z).astype(BF16)


def _peer_select(q, keys, l, *, tt=128):
    n_tok = q.shape[0]
    big = jax.ShapeDtypeStruct((PEER_HEADS, PEER_KEYS, n_tok), F32)
    big_bf = jax.ShapeDtypeStruct((PEER_HEADS, PEER_KEYS, n_tok), BF16)
    big_spec = pl.BlockSpec((PEER_HEADS, PEER_KEYS, tt), lambda i: (0, 0, i))
    return pl.pallas_call(
        _peer_select_kernel,
        grid=(n_tok // tt,),
        in_specs=[pl.BlockSpec((tt, D_MODEL), lambda i: (i, 0)),
                  pl.BlockSpec((None, PEER_HEADS, 2, PEER_KEYS, PEER_KEYS), lambda i: (l, 0, 0, 0, 0))],
        out_specs=[big_spec, big_spec, big_spec, big_spec],
        out_shape=[big, big_bf, big, big_bf],
        scratch_shapes=[pltpu.VMEM((2 * PEER_TOPK, tt), F32), pltpu.VMEM((PEER_TOPK, tt), F32)],
        compiler_params=_cparams(("parallel",)),
        name="peer_select",
    )(q, keys)


def _peer_dense_kernel(x_ref, u_ref, vt_ref, cnt_ref, r2_ref, e1_ref, e2_ref, out_ref, act_sc, g_sc, *, rows):
    e = pl.program_id(1)

    @pl.when(e == 0)
    def _():
        out_ref[...] = jnp.zeros_like(out_ref)

    act_sc[...] = lax.dot_general(u_ref[...], x_ref[...], (((1,), (1,)), ((), ())),
                                  preferred_element_type=F32)
    kk = PEER_KEYS
    for tc in range(act_sc.shape[1] // kk):
        lanes = slice(tc * kk, (tc + 1) * kk)
        for i in range(rows):
            w = jnp.zeros((kk, kk), BF16)
            for h in range(PEER_HEADS):
                hit = r2_ref[h, :, lanes] < cnt_ref[h, i:i + 1, lanes].astype(BF16)
                val = e2_ref[h, :, lanes] * e1_ref[h, i:i + 1, lanes].astype(BF16)
                w = w + jnp.where(hit, val, jnp.zeros_like(val))
            g = jax.nn.gelu(act_sc[i * kk:(i + 1) * kk, lanes]) * w.astype(F32)
            g_sc[i * kk:(i + 1) * kk, lanes] = g.astype(BF16)
    out_ref[...] += jnp.dot(vt_ref[...], g_sc[...], preferred_element_type=F32)


def _transpose_tile_kernel(v_ref, o_ref):
    o_ref[...] = v_ref[...].T.astype(o_ref.dtype)


def _peer_value_tiles(v, l, te):
    return pl.pallas_call(
        _transpose_tile_kernel,
        grid=(N_EXPERTS // te,),
        in_specs=[pl.BlockSpec((None, te, D_MODEL), lambda e: (l, e, 0))],
        out_specs=pl.BlockSpec((None, D_MODEL, te), lambda e: (e, 0, 0)),
        out_shape=jax.ShapeDtypeStruct((N_EXPERTS // te, D_MODEL, te), BF16),
        compiler_params=_cparams(("parallel",)),
        name="peer_value_tiles",
    )(v)


def _peer_dense(x_bf, u_bf, v, l, sel, *, tt=512, rows=8):
    cnt, r2, e1, e2 = sel
    te = rows * PEER_KEYS
    n_tok = x_bf.shape[0]
    vt_bf = _peer_value_tiles(v, l, te)
    row_spec = pl.BlockSpec((PEER_HEADS, rows, tt), lambda i, e: (0, e, i))
    all_spec = pl.BlockSpec((PEER_HEADS, PEER_KEYS, tt), lambda i, e: (0, 0, i))
    return pl.pallas_call(
        functools.partial(_peer_dense_kernel, rows=rows),
        grid=(n_tok // tt, N_EXPERTS // te),
        in_specs=[pl.BlockSpec((tt, D_MODEL), lambda i, e: (i, 0)),
                  pl.BlockSpec((te, D_MODEL), lambda i, e: (e, 0)),
                  pl.BlockSpec((None, D_MODEL, te), lambda i, e: (e, 0, 0)),
                  row_spec, all_spec, row_spec, all_spec],
        out_specs=pl.BlockSpec((D_MODEL, tt), lambda i, e: (0, i)),
        out_shape=jax.ShapeDtypeStruct((D_MODEL, n_tok), F32),
        scratch_shapes=[pltpu.VMEM((te, tt), F32), pltpu.VMEM((te, tt), BF16)],
        compiler_params=_cparams(("parallel", "arbitrary")),
        name="peer_dense",
    )(x_bf, u_bf, vt_bf, cnt, r2, e1, e2)


def _hyena_features():
    pos = jnp.arange(SEQ, dtype=F32)
    t = pos / (SEQ - 1)
    bands = jnp.linspace(1e-4, 16 - 1, 16, dtype=F32)
    ang = 2.0 * math.pi * pos[:, None] * bands[None, :] / SEQ
    feat = jnp.concatenate([t[:, None], jnp.cos(ang), -jnp.sin(ang)], axis=-1)
    feat_t = jnp.pad(feat.T, ((0, HY_EMB_PAD - feat.shape[1]), (0, 0)))
    t_row = t[None, :]

    def reversed_time(a):
        return jnp.roll(a[:, ::-1], 1, axis=1)

    return (jnp.concatenate([feat_t, reversed_time(feat_t)], axis=1),
            jnp.concatenate([t_row, reversed_time(t_row)], axis=1))


def kernel(x, w_in, mlstm_gate_bias, mlstm_norm_gain, w_mlstm_out, hyena_conv_w, hyena_conv_b, hyena_w1,
           hyena_b1, hyena_w2, hyena_b2, hyena_freq, hyena_w3, hyena_decay, hyena_skip, w_hyena_out,
           s5_lambda_re, s5_lambda_im, s5_log_step, s5_b_re, s5_b_im, s5_c_re, s5_c_im, s5_skip, w_s5_glu,
           w_out, ln1_g, ln1_b, peer_w_q, peer_subkeys, peer_u, peer_v, ln2_g, ln2_b):
    xf = x.reshape(SEQ, D_MODEL)
    xb = xf.astype(BF16)
    feat_t, t_row = _hyena_features()
    w_gates = w_in[:, :, OFF_GATES:OFF_HYENA]
    w_hy_t = jnp.swapaxes(w_in[:, :, OFF_HYENA:OFF_S5], 1, 2).astype(BF16)
    w_rest = w_in[:, :, OFF_S5:].astype(BF16)
    s5_mats = jax.vmap(_s5_chunk_matrices)(s5_lambda_re, s5_lambda_im, s5_log_step, s5_b_re, s5_b_im,
                                           s5_c_re, s5_c_im)
    for l in range(DEPTH):
        qkvo = _mm(xb, w_in, l, 0, 4 * BRANCH, tm=1024, tn=512, name="proj_qkvo")
        gates = _mm(xb, w_gates, l, 0, N_GATES, tm=1024, tn=N_GATES, name="proj_gates")
        hp_t = _mm_nt(w_hy_t, l, xb, tr=512, tm=1024, name="proj_hyena_t")
        rest = _mm(xb, w_rest, l, 0, BRANCH + 3 * D_MODEL, tm=1024, tn=512, name="proj_s5_mixgates")

        hdir = _mlstm(qkvo, gates, mlstm_gate_bias, l)
        hm = _mlstm_post(hdir, qkvo, mlstm_norm_gain, l)
        hy_t = _hyena(hp_t, l, hyena_conv_w, hyena_conv_b, hyena_w1, hyena_b1, hyena_w2, hyena_b2,
                      hyena_freq, hyena_w3, hyena_decay, hyena_skip, feat_t, t_row)
        s5y = _s5(rest[:, :BRANCH], s5_mats, s5_skip, l)
        merged = _merge(hm, hy_t, s5y, rest, w_mlstm_out, w_hyena_out, w_s5_glu, l)
        mix = _mm(merged, w_out, l, 0, D_MODEL, tm=1024, tn=512, name="proj_out")
        xf, xb = _res_ln(xf, mix, ln1_g, ln1_b, l, transposed=False)

        q = _mm(xb, peer_w_q, l, 0, PEER_HEADS * 2 * PEER_KEYS, tm=1024, tn=512, name="peer_query")
        sel = _peer_select(q, peer_subkeys, l)
        ffn_t = _peer_dense(xb, peer_u[l].astype(BF16), peer_v, l, sel)
        xf, xb = _res_ln(xf, ffn_t, ln2_g, ln2_b, l, transposed=True)
    return xf.reshape(1, SEQ, D_MODEL)
```

```python
import functools
import math

import numpy as np
import jax
import jax.numpy as jnp
from jax import lax
from jax.experimental import pallas as pl
from jax.experimental.pallas import tpu as pltpu

F32 = jnp.float32
BF16 = jnp.bfloat16

D_MODEL = 2048
SEQ = 8192
DEPTH = 4
BRANCH = 1024
HEADS = 4
HEAD_DIM = 256
CHUNK = 128
N_GATES = 16
HY_HID = 64
HY_EMB_PAD = 128
S5_GROUP = 16
S5_GROUPS = 64
S5_STATE = 64
S5_T = 16
S5_CHUNKS = SEQ // S5_T
S5_LOG_CHUNKS = 9
PEER_HEADS = 8
PEER_KEYS = 128
PEER_TOPK = 16
N_EXPERTS = PEER_KEYS * PEER_KEYS
ALPHA = (2 * DEPTH) ** 0.25
LN_EPS = 1e-5
FFT_N = 2 * SEQ
FFT_R = 128
OFF_GATES = 4 * BRANCH
OFF_HYENA = OFF_GATES + N_GATES
OFF_S5 = OFF_HYENA + 3 * BRANCH
OFF_MIXG = OFF_S5 + BRANCH

VMEM_LIMIT = 56 * 1024 * 1024


def _cparams(sem):
    return pltpu.CompilerParams(dimension_semantics=sem, vmem_limit_bytes=VMEM_LIMIT)


def _split_bf16(a):
    hi = a.astype(BF16)
    lo = (a - hi.astype(F32)).astype(BF16)
    return hi, lo


def _dot3(a, b_hi, b_lo):
    a_hi, a_lo = _split_bf16(a)
    acc = jnp.dot(a_hi, b_hi, preferred_element_type=F32)
    acc += jnp.dot(a_lo, b_hi, preferred_element_type=F32)
    acc += jnp.dot(a_hi, b_lo, preferred_element_type=F32)
    return acc


def _dot3_both(a, b):
    b_hi, b_lo = _split_bf16(b)
    return _dot3(a, b_hi, b_lo)


def _mm_kernel(a_ref, b_ref, o_ref):
    o_ref[...] = jnp.dot(a_ref[...].astype(BF16), b_ref[...].astype(BF16),
                         preferred_element_type=F32).astype(o_ref.dtype)


def _mm(a, w, l, col_off, n_cols, *, tm, tn, name):
    m, k = a.shape
    cb = col_off // tn
    assert col_off % tn == 0 and n_cols % tn == 0 and m % tm == 0
    return pl.pallas_call(
        _mm_kernel,
        grid=(m // tm, n_cols // tn),
        in_specs=[pl.BlockSpec((tm, k), lambda i, j: (i, 0)),
                  pl.BlockSpec((None, k, tn), lambda i, j: (l, 0, cb + j))],
        out_specs=pl.BlockSpec((tm, tn), lambda i, j: (i, j)),
        out_shape=jax.ShapeDtypeStruct((m, n_cols), F32),
        compiler_params=_cparams(("parallel", "parallel")),
        name=name,
    )(a, w)


def _mm_tn_kernel(w_ref, a_ref, o_ref, wt_sc):
    @pl.when(pl.program_id(1) == 0)
    def _():
        wt_sc[...] = w_ref[...].astype(F32).T.astype(BF16)

    o_ref[...] = lax.dot_general(wt_sc[...], a_ref[...], (((1,), (1,)), ((), ())),
                                 preferred_element_type=F32).astype(o_ref.dtype)


def _mm_tn(w, l, n_rows, a, *, tr, tm, name):
    k = w.shape[1]
    m = a.shape[0]
    return pl.pallas_call(
        _mm_tn_kernel,
        grid=(n_rows // tr, m // tm),
        in_specs=[pl.BlockSpec((None, k, tr), lambda i, j: (l, 0, i)),
                  pl.BlockSpec((tm, k), lambda i, j: (j, 0))],
        out_specs=pl.BlockSpec((tr, tm), lambda i, j: (i, j)),
        out_shape=jax.ShapeDtypeStruct((n_rows, m), BF16),
        scratch_shapes=[pltpu.VMEM((tr, k), BF16)],
        compiler_params=_cparams(("parallel", "arbitrary")),
        name=name,
    )(w, a)


def _w_tail_kernel(cur_ref, nxt_ref, o_ref, *, shift):
    o_ref[...] = jnp.concatenate([cur_ref[:, shift:], nxt_ref[:, :shift]], axis=1).astype(o_ref.dtype)


def _w_in_tail(w_in, *, tn=512):
    lane = 128
    depth, k, n_all = w_in.shape
    n_out = n_all - OFF_HYENA
    base = OFF_GATES // tn
    return pl.pallas_call(
        functools.partial(_w_tail_kernel, shift=N_GATES),
        grid=(depth, n_out // tn),
        in_specs=[pl.BlockSpec((None, k, tn), lambda l, j: (l, 0, base + j)),
                  pl.BlockSpec((None, k, lane), lambda l, j: (l, 0, (base + j + 1) * (tn // lane)))],
        out_specs=pl.BlockSpec((None, k, tn), lambda l, j: (l, 0, j)),
        out_shape=jax.ShapeDtypeStruct((depth, k, n_out), BF16),
        compiler_params=_cparams(("parallel", "parallel")),
        name="w_in_tail",
    )(w_in, w_in)


def _log_sigmoid(x):
    return jnp.minimum(x, 0.0) - jnp.log1p(jnp.exp(-jnp.abs(x)))


def _mlstm_chunk(d, q, k, v, li_r, lf_r, li_c, lf_c, c_sc, n_sc, m_sc):
    n = CHUNK
    row = lax.broadcasted_iota(jnp.int32, (n, n), 0)
    col = lax.broadcasted_iota(jnp.int32, (n, n), 1)
    valid = (row >= col) if d == 0 else (row <= col)
    valid_t = (col >= row) if d == 0 else (col <= row)
    b_col = jnp.sum(jnp.where(valid, lf_r, 0.0), axis=1, keepdims=True)
    b_row = jnp.sum(jnp.where(valid_t, lf_c, 0.0), axis=0, keepdims=True)
    total = jnp.sum(lf_r, axis=1, keepdims=True)

    m_prev = m_sc[...]
    m_inter = b_col + m_prev
    log_d = jnp.where(valid, b_col - b_row + li_r, -jnp.inf)
    m_t = jnp.maximum(m_inter, jnp.max(log_d, axis=1, keepdims=True))
    inter = jnp.exp(m_inter - m_t)
    dmat = jnp.exp(log_d - m_t)

    k = k * (HEAD_DIM ** -0.5)
    qb = q.astype(BF16)
    kb = k.astype(BF16)
    vb = v.astype(BF16)
    s = lax.dot_general(qb, kb, (((1,), (1,)), ((), ())), preferred_element_type=F32) * dmat
    num = jnp.dot(s.astype(BF16), vb, preferred_element_type=F32)
    num += inter * jnp.dot(qb, c_sc[...].astype(BF16), preferred_element_type=F32)
    qn = jnp.sum(q * n_sc[...], axis=1, keepdims=True)
    den = jnp.sum(s, axis=1, keepdims=True) + inter * qn
    h_out = num / jnp.maximum(jnp.abs(den), jnp.exp(-m_t))

    log_w = total - b_col + li_c
    m_new = jnp.maximum(total + m_prev, jnp.max(log_w, axis=0, keepdims=True))
    w = jnp.exp(log_w - m_new)
    decay = jnp.exp(total + m_prev - m_new)
    kw = k * w
    c_sc[...] = decay * c_sc[...] + lax.dot_general(
        kw.astype(BF16), vb, (((0,), (0,)), ((), ())), preferred_element_type=F32)
    n_sc[...] = decay * n_sc[...] + jnp.sum(kw, axis=0, keepdims=True)
    m_sc[...] = m_new
    return h_out


def _mlstm_kernel(qkv_f_ref, qkv_b_ref, gr_f_ref, gr_b_ref, gc_f_ref, gc_b_ref, bias_ref,
                  o_f_ref, o_b_ref, c_sc, n_sc, m_sc):
    @pl.when(pl.program_id(0) == 0)
    def _():
        c_sc[...] = jnp.zeros_like(c_sc)
        n_sc[...] = jnp.zeros_like(n_sc)
        m_sc[...] = jnp.zeros_like(m_sc)

    for d, (qkv_ref, gr_ref, gc_ref, o_ref) in enumerate(((qkv_f_ref, gr_f_ref, gc_f_ref, o_f_ref),
                                                         (qkv_b_ref, gr_b_ref, gc_b_ref, o_b_ref))):
        for h in range(HEADS):
            cols = slice(h * HEAD_DIM, (h + 1) * HEAD_DIM)
            b_i = bias_ref[d * 2 * HEADS + h]
            b_f = bias_ref[d * 2 * HEADS + HEADS + h]
            chain = d * HEADS + h
            o_ref[:, cols] = _mlstm_chunk(
                d,
                qkv_ref[:, cols],
                qkv_ref[:, BRANCH + h * HEAD_DIM:BRANCH + (h + 1) * HEAD_DIM],
                qkv_ref[:, 2 * BRANCH + h * HEAD_DIM:2 * BRANCH + (h + 1) * HEAD_DIM],
                gr_ref[h] + b_i, _log_sigmoid(gr_ref[HEADS + h] + b_f),
                gc_ref[h] + b_i, _log_sigmoid(gc_ref[HEADS + h] + b_f),
                c_sc.at[chain], n_sc.at[chain], m_sc.at[chain])


def _mlstm(qkvo, gates, gate_bias, l):
    nc = SEQ // CHUNK
    g_rows = gates.T.reshape(N_GATES, 1, SEQ)
    g_cols = gates.T.reshape(N_GATES, SEQ, 1)
    bias = gate_bias.reshape(DEPTH, N_GATES, 1, 1)
    half = N_GATES // 2
    chains = 2 * HEADS
    out = jax.ShapeDtypeStruct((SEQ, BRANCH), F32)
    return pl.pallas_call(
        _mlstm_kernel,
        grid=(nc,),
        in_specs=[
            pl.BlockSpec((CHUNK, 4 * BRANCH), lambda c: (c, 0)),
            pl.BlockSpec((CHUNK, 4 * BRANCH), lambda c: (nc - 1 - c, 0)),
            pl.BlockSpec((half, 1, CHUNK), lambda c: (0, 0, c)),
            pl.BlockSpec((half, 1, CHUNK), lambda c: (1, 0, nc - 1 - c)),
            pl.BlockSpec((half, CHUNK, 1), lambda c: (0, c, 0)),
            pl.BlockSpec((half, CHUNK, 1), lambda c: (1, nc - 1 - c, 0)),
            pl.BlockSpec((None, N_GATES, 1, 1), lambda c: (l, 0, 0, 0)),
        ],
        out_specs=[pl.BlockSpec((CHUNK, BRANCH), lambda c: (c, 0)),
                   pl.BlockSpec((CHUNK, BRANCH), lambda c: (nc - 1 - c, 0))],
        out_shape=[out, out],
        scratch_shapes=[pltpu.VMEM((chains, HEAD_DIM, HEAD_DIM), F32),
                        pltpu.VMEM((chains, 1, HEAD_DIM), F32),
                        pltpu.VMEM((chains, 1, 1), F32)],
        compiler_params=_cparams(("arbitrary",)),
        name="mlstm_scan",
    )(qkvo, qkvo, g_rows, g_rows, g_cols, g_cols, bias)


def _mlstm_post_kernel(hf_ref, hb_ref, o_ref, gain_ref, out_ref):
    h = hf_ref[...] + hb_ref[...]
    h = jax.nn.sigmoid(o_ref[...]) * h
    for hd in range(HEADS):
        sl = slice(hd * HEAD_DIM, (hd + 1) * HEAD_DIM)
        hh = h[:, sl]
        mu = jnp.mean(hh, axis=1, keepdims=True)
        var = jnp.mean(jnp.square(hh - mu), axis=1, keepdims=True)
        out_ref[:, sl] = ((hh - mu) * lax.rsqrt(var + LN_EPS) * gain_ref[:, sl]).astype(out_ref.dtype)


def _mlstm_post(hdir, qkvo, gain, l, *, tm=512):
    return pl.pallas_call(
        _mlstm_post_kernel,
        grid=(SEQ // tm,),
        in_specs=[pl.BlockSpec((tm, BRANCH), lambda i: (i, 0)),
                  pl.BlockSpec((tm, BRANCH), lambda i: (i, 0)),
                  pl.BlockSpec((tm, BRANCH), lambda i: (i, 3)),
                  pl.BlockSpec((None, 1, BRANCH), lambda i: (l, 0, 0))],
        out_specs=pl.BlockSpec((tm, BRANCH), lambda i: (i, 0)),
        out_shape=jax.ShapeDtypeStruct((SEQ, BRANCH), BF16),
        compiler_params=_cparams(("parallel",)),
        name="mlstm_post",
    )(hdir[0], hdir[1], qkvo, gain.reshape(DEPTH, 1, BRANCH))


HY_CT = 32
HY_RB = 512


@functools.lru_cache(maxsize=None)
def _fft_constants():
    r = FFT_R
    idx = np.arange(r)
    th = 2.0 * np.pi * np.outer(idx, idx) / r
    cs, sn = np.cos(th), np.sin(th)
    ph = 2.0 * np.pi * np.outer(idx, idx) / FFT_N
    f1 = np.concatenate([cs, -sn], axis=1)
    m3 = np.block([[cs, -sn], [sn, cs]])
    m4 = np.block([[cs, sn], [-sn, cs]])
    m6 = np.concatenate([cs, -sn], axis=0) / FFT_N
    m6[:, r // 2:] = 0.0
    consts = {name: jnp.asarray(mat, F32).astype(BF16)
              for name, mat in (("f1", f1), ("m3", m3), ("m4", m4), ("m6", m6))}
    consts["tw"] = (jnp.asarray(np.cos(ph), F32), jnp.asarray(np.sin(ph), F32))
    return consts


def _rows(i, n):
    return pl.ds(pl.multiple_of(i * n, n), n)


def _for_each(n, body, unroll=1):
    def step(i, carry):
        body(i)
        return carry

    lax.fori_loop(0, n, step, 0, unroll=unroll)


def _fwd_stage1(sig_ref, hi_ref, ct, t_buf, a_buf, b_buf, f1_ref, twr_ref, twi_ref):
    r = FFT_R
    zeros = jnp.zeros((r // 2, r), F32)

    def tr_in(c):
        hi = zeros if hi_ref is None else hi_ref[c]
        t_buf[_rows(c, r), :] = jnp.concatenate([sig_ref[c], hi], axis=0).T.astype(BF16)

    _for_each(ct, tr_in, unroll=4)

    def stage1(b):
        rows = _rows(b, HY_RB)
        a_buf[rows, :] = jnp.dot(t_buf[rows, :], f1_ref[...], preferred_element_type=F32)

    _for_each(ct * r // HY_RB, stage1)

    def twiddle(c):
        rows = _rows(c, r)
        a = a_buf[rows, :]
        ar, ai = a[:, :r], a[:, r:]
        twr, twi = twr_ref[...], twi_ref[...]
        b_buf[rows, :r] = (ar * twr + ai * twi).T.astype(BF16)
        b_buf[rows, r:] = (ai * twr - ar * twi).T.astype(BF16)

    _for_each(ct, twiddle, unroll=4)


def _hy_mlp_kernel(feat_ref, w1_ref, b1_ref, w2_ref, b2_ref, fr_ref, out_ref):
    h = _dot3_both(w1_ref[...], feat_ref[...]) + b1_ref[...]
    h = jnp.sin(fr_ref[0] * h)
    h = _dot3_both(w2_ref[...], h) + b2_ref[...]
    out_ref[...] = jnp.sin(fr_ref[1] * h)


def _hy_filter_kernel(w3_ref, h_ref, dec_ref, t_ref, out_ref):
    d = pl.program_id(0)
    filt = _dot3_both(w3_ref[...], h_ref[...])
    filt = filt * jnp.exp(-t_ref[...] * jnp.abs(dec_ref[...]))
    filt = filt / (jnp.sum(jnp.abs(filt), axis=1, keepdims=True) + 1e-6)
    lane = lax.broadcasted_iota(jnp.int32, filt.shape, 1)
    out_ref[...] = jnp.where((lane == 0) & (d == 1), 0.0, filt)


def _hy_spec_kernel(fa_ref, fb_ref, f1_ref, m3_ref, twr_ref, twi_ref, out_ref, t_buf, a_buf, b_buf, *, ct):
    r = FFT_R
    nblk = ct * r // HY_RB
    _fwd_stage1(fa_ref, fb_ref, ct, t_buf, a_buf, b_buf, f1_ref, twr_ref, twi_ref)

    def put(b):
        rows = _rows(b, HY_RB)
        out_ref[rows, :] = jnp.dot(b_buf[rows, :], m3_ref[...], preferred_element_type=F32)

    _for_each(nblk, put)


def _shift_prev(p, n1, n2):
    r1 = pltpu.roll(p, 1, 2)
    r2 = pltpu.roll(r1, 1, 1)
    return jnp.where(n2 == 0, jnp.where(n1 == 0, 0.0, r2), r1)


def _shift_next(p, n1, n2):
    s1, s2 = p.shape[1], p.shape[2]
    r1 = pltpu.roll(p, s2 - 1, 2)
    r2 = pltpu.roll(r1, s1 - 1, 1)
    return jnp.where(n2 == s2 - 1, jnp.where(n1 == s1 - 1, 0.0, r2), r1)


def _hy_conv_kernel(p0_ref, p1_ref, p2_ref, cw_ref, cb_ref, skip_ref, h_ref,
                    f1_ref, m3_ref, m4_ref, m6_ref, twr_ref, twi_ref,
                    out_ref, z_buf, t_buf, a_buf, b_buf, *, ct):
    r = FFT_R
    n1_len = SEQ // r
    nblk = ct * r // HY_RB
    shp = (ct, n1_len, r)
    n1 = lax.broadcasted_iota(jnp.int32, shp, 1)
    n2 = lax.broadcasted_iota(jnp.int32, shp, 2)

    def short_conv(p_ref, g):
        p = p_ref[...].astype(F32)
        return (_shift_prev(p, n1, n2) * cw_ref[3 * g] + p * cw_ref[3 * g + 1]
                + _shift_next(p, n1, n2) * cw_ref[3 * g + 2] + cb_ref[g])

    z_buf[...] = short_conv(p1_ref, 1) * short_conv(p2_ref, 2)
    _fwd_stage1(z_buf, None, ct, t_buf, a_buf, b_buf, f1_ref, twr_ref, twi_ref)

    def freq_domain(b):
        rows = _rows(b, HY_RB)
        x = jnp.dot(b_buf[rows, :], m3_ref[...], preferred_element_type=F32)
        h = h_ref[rows, :]
        xr, xi = x[:, :r], x[:, r:]
        hr, hi = h[:, :r], h[:, r:]
        y = jnp.concatenate([xr * hr - xi * hi, xr * hi + xi * hr], axis=1).astype(BF16)
        a_buf[rows, :] = jnp.dot(y, m4_ref[...], preferred_element_type=F32)

    _for_each(nblk, freq_domain)

    def inv_twiddle(c):
        rows = _rows(c, r)
        b = a_buf[rows, :]
        br, bi = b[:, :r], b[:, r:]
        twr, twi = twr_ref[...], twi_ref[...]
        b_buf[rows, :r] = (br * twr - bi * twi).T.astype(BF16)
        b_buf[rows, r:] = (br * twi + bi * twr).T.astype(BF16)

    _for_each(ct, inv_twiddle, unroll=4)

    def stage6(b):
        rows = _rows(b, HY_RB)
        a_buf[rows, :r] = jnp.dot(b_buf[rows, :], m6_ref[...], preferred_element_type=F32)

    _for_each(nblk, stage6)

    def finish(c):
        y_c = a_buf[_rows(c, r), :r].T[:n1_len, :]
        out_ref[c] = y_c + skip_ref[c] * z_buf[c]

    _for_each(ct, finish, unroll=4)
    out_ref[...] = short_conv(p0_ref, 0) * out_ref[...]


def _hyena(hp_t, l, conv_w, conv_b, w1, b1, w2, b2, freq, w3, decay, skip, feat_t, t_row):
    r = FFT_R
    n1 = SEQ // r
    ct = HY_CT
    k = _fft_constants()
    f1, m3, m4, m6 = k["f1"], k["m3"], k["m4"], k["m6"]
    twr, twi = k["tw"]

    def full(a):
        return pl.BlockSpec(a.shape, lambda *_: (0,) * a.ndim)

    h2 = pl.pallas_call(
        _hy_mlp_kernel,
        grid=(2,),
        in_specs=[pl.BlockSpec((feat_t.shape[0], SEQ), lambda i: (0, i)),
                  pl.BlockSpec((None, HY_HID, feat_t.shape[0]), lambda i: (l, 0, 0)),
                  pl.BlockSpec((None, HY_HID, 1), lambda i: (l, 0, 0)),
                  pl.BlockSpec((None, HY_HID, HY_HID), lambda i: (l, 0, 0)),
                  pl.BlockSpec((None, HY_HID, 1), lambda i: (l, 0, 0)),
                  pl.BlockSpec((None, 2, HY_HID, 1), lambda i: (l, 0, 0, 0))],
        out_specs=pl.BlockSpec((HY_HID, SEQ), lambda i: (0, i)),
        out_shape=jax.ShapeDtypeStruct((HY_HID, 2 * SEQ), F32),
        compiler_params=_cparams(("parallel",)),
        name="hyena_filter_mlp",
    )(feat_t, jnp.pad(jnp.swapaxes(w1, 1, 2), ((0, 0), (0, 0), (0, feat_t.shape[0] - w1.shape[1]))),
      b1[:, :, None], jnp.swapaxes(w2, 1, 2), b2[:, :, None], freq[:, :, :, None])

    rt = 128
    filt = pl.pallas_call(
        _hy_filter_kernel,
        grid=(2, BRANCH // rt),
        in_specs=[pl.BlockSpec((None, rt, HY_HID), lambda d, i: (l, d * (BRANCH // rt) + i, 0)),
                  pl.BlockSpec((HY_HID, SEQ), lambda d, i: (0, d)),
                  pl.BlockSpec((None, None, rt, 1), lambda d, i: (l, d, i, 0)),
                  pl.BlockSpec((1, SEQ), lambda d, i: (0, d))],
        out_specs=pl.BlockSpec((None, rt, SEQ), lambda d, i: (d, i, 0)),
        out_shape=jax.ShapeDtypeStruct((2, BRANCH, SEQ), F32),
        compiler_params=_cparams(("parallel", "parallel")),
        name="hyena_filter",
    )(jnp.swapaxes(w3, 1, 2), h2, decay[:, :, :, None], t_row)
    filt = filt.reshape(2, BRANCH, n1, r)

    scratch = [pltpu.VMEM((ct * r, r), BF16), pltpu.VMEM((ct * r, 2 * r), F32), pltpu.VMEM((ct * r, 2 * r), BF16)]
    spec = pl.pallas_call(
        functools.partial(_hy_spec_kernel, ct=ct),
        grid=(BRANCH // ct,),
        in_specs=[pl.BlockSpec((None, ct, n1, r), lambda i: (0, i, 0, 0)),
                  pl.BlockSpec((None, ct, n1, r), lambda i: (1, i, 0, 0)),
                  full(f1), full(m3), full(twr), full(twi)],
        out_specs=pl.BlockSpec((ct * r, 2 * r), lambda i: (i, 0)),
        out_shape=jax.ShapeDtypeStruct((BRANCH * r, 2 * r), F32),
        scratch_shapes=scratch,
        compiler_params=_cparams(("parallel",)),
        name="hyena_filter_spectrum",
    )(filt, filt, f1, m3, twr, twi)

    hp3 = hp_t.reshape(3 * BRANCH, n1, r)
    nb = BRANCH // ct
    cw = conv_w.reshape(DEPTH, 3, 3, BRANCH).transpose(0, 2, 1, 3).reshape(DEPTH, 9, BRANCH, 1, 1)
    cb = conv_b.reshape(DEPTH, 3, BRANCH, 1, 1)
    consts = (f1, m3, m4, m6, twr, twi)
    return pl.pallas_call(
        functools.partial(_hy_conv_kernel, ct=ct),
        grid=(nb,),
        in_specs=[pl.BlockSpec((ct, n1, r), lambda i: (i, 0, 0)),
                  pl.BlockSpec((ct, n1, r), lambda i: (nb + i, 0, 0)),
                  pl.BlockSpec((ct, n1, r), lambda i: (2 * nb + i, 0, 0)),
                  pl.BlockSpec((None, 9, ct, 1, 1), lambda i: (l, 0, i, 0, 0)),
                  pl.BlockSpec((None, 3, ct, 1, 1), lambda i: (l, 0, i, 0, 0)),
                  pl.BlockSpec((None, ct, 1, 1), lambda i: (l, i, 0, 0)),
                  pl.BlockSpec((ct * r, 2 * r), lambda i: (i, 0)),
                  *[full(a) for a in consts]],
        out_specs=pl.BlockSpec((ct, n1, r), lambda i: (i, 0, 0)),
        out_shape=jax.ShapeDtypeStruct((BRANCH, n1, r), F32),
        scratch_shapes=[pltpu.VMEM((ct, n1, r), F32)] + scratch,
        compiler_params=_cparams(("parallel",)),
        name="hyena_conv",
    )(hp3, hp3, hp3, cw, cb, skip.reshape(DEPTH, BRANCH, 1, 1), spec, *consts).reshape(BRANCH, SEQ)


def _s5_chunk_matrices(lam_re, lam_im, log_step, b_re, b_im, c_re, c_im):
    t = S5_T
    lam = lax.complex(lam_re, lam_im)
    step = jnp.exp(log_step)[..., None]
    lam_dt = lam * step
    b_bar = ((jnp.exp(lam_dt) - 1.0) / lam)[..., None] * lax.complex(b_re, b_im)
    c_mat = lax.complex(c_re, c_im)
    pw = jnp.exp(lam_dt[None] * jnp.arange(t + 1, dtype=F32)[:, None, None, None])
    kern = jnp.einsum("dgvp,mdgp,dgpn->mdgvn", c_mat, pw[:t], b_bar, precision="highest").real
    lag = np.arange(t)[:, None] - np.arange(t)[None, :]
    pick = np.stack([lag[:, :, None] == np.arange(t), -lag[:, :, None] == np.arange(t)], axis=-1)
    kfb = jnp.einsum("ajmd,mdgvn->ajgvn", jnp.asarray(pick, F32), kern, precision="highest")
    w_intra = jnp.transpose(kfb, (2, 1, 4, 0, 3)).reshape(S5_GROUPS, t * S5_GROUP, t * S5_GROUP)
    in_f = pw[:t, 0][::-1][..., None] * b_bar[0][None]
    in_b = pw[:t, 1][..., None] * b_bar[1][None]

    def in_mat(m):
        m = jnp.transpose(m, (1, 0, 3, 2)).reshape(S5_GROUPS, t * S5_GROUP, S5_STATE)
        return jnp.concatenate([m.real, m.imag], axis=-1)

    w_in = jnp.concatenate([in_mat(in_f), in_mat(in_b)], axis=-1)
    out_f = c_mat[0][None] * pw[1:, 0][:, :, None, :]
    out_b = c_mat[1][None] * pw[1:, 1][::-1][:, :, None, :]

    def out_mat(m):
        m = jnp.transpose(m, (1, 3, 0, 2)).reshape(S5_GROUPS, S5_STATE, t * S5_GROUP)
        return jnp.concatenate([m.real, -m.imag], axis=1)

    w_out = jnp.concatenate([out_mat(out_f), out_mat(out_b)], axis=1)
    a_pow = jnp.exp(lam_dt[None] * (t * 2.0 ** jnp.arange(S5_LOG_CHUNKS, dtype=F32))[:, None, None, None])
    a_pow = jnp.transpose(a_pow, (2, 1, 0, 3))
    a_re = jnp.concatenate([a_pow.real, a_pow.real], axis=-1)
    a_im = jnp.concatenate([-a_pow.imag, a_pow.imag], axis=-1)
    return w_intra, w_in, w_out, a_re, a_im


S5_TILE_GROUPS = 128 // S5_GROUP


def _s5_group(u, wi, win, wout, a_re, a_im, skip_row):
    nc = S5_CHUNKS
    half = S5_STATE
    ub = u.astype(BF16)
    y = jnp.dot(ub, wi.astype(BF16), preferred_element_type=F32) + skip_row * u
    s_in = jnp.dot(ub, win.astype(BF16), preferred_element_type=F32)
    row = lax.broadcasted_iota(jnp.int32, (nc, 2 * half), 0)

    def scan(s, d):
        for step in range(S5_LOG_CHUNKS):
            sh = 1 << step
            if d == 0:
                moved = jnp.where(row >= sh, pltpu.roll(s, sh, 0), 0.0)
            else:
                moved = jnp.where(row < nc - sh, pltpu.roll(s, nc - sh, 0), 0.0)
            swapped = pltpu.roll(moved, half, 1)
            s = s + a_re[d, step:step + 1, :] * moved + a_im[d, step:step + 1, :] * swapped
        if d == 0:
            return jnp.where(row >= 1, pltpu.roll(s, 1, 0), 0.0)
        return jnp.where(row < nc - 1, pltpu.roll(s, nc - 1, 0), 0.0)

    e = jnp.concatenate([scan(s_in[:, :2 * half], 0), scan(s_in[:, 2 * half:], 1)], axis=1)
    return y + jnp.dot(e.astype(BF16), wout.astype(BF16), preferred_element_type=F32)


def _s5_kernel(u_ref, wi_ref, win_ref, wout_ref, are_ref, aim_ref, skip_ref, y_ref, y_sc):
    t, gs, nc = S5_T, S5_GROUP, S5_CHUNKS
    for g in range(S5_TILE_GROUPS):
        u_g = jnp.concatenate([u_ref[pl.ds(tau, nc, stride=t), :][:, g * gs:(g + 1) * gs] for tau in range(t)], axis=1)
        y_sc[g] = _s5_group(u_g, wi_ref[g], win_ref[g], wout_ref[g], are_ref[g], aim_ref[g], skip_ref[g])
    for tau in range(t):
        y_ref[pl.ds(tau, nc, stride=t), :] = jnp.concatenate(
            [y_sc[g, :, tau * gs:(tau + 1) * gs] for g in range(S5_TILE_GROUPS)], axis=1)


def _s5(rest, mats, skip, l):
    w_intra, w_in, w_out, a_re, a_im = mats
    t = S5_T
    gw = t * S5_GROUP
    tg = S5_TILE_GROUPS
    lane = tg * S5_GROUP
    skip_t = jnp.tile(skip.reshape(DEPTH, S5_GROUPS, 1, S5_GROUP), (1, 1, t, 1)).reshape(DEPTH, S5_GROUPS, 1, gw)
    return pl.pallas_call(
        _s5_kernel,
        grid=(BRANCH // lane,),
        in_specs=[pl.BlockSpec((SEQ, lane), lambda j: (0, j)),
                  pl.BlockSpec((None, tg, gw, gw), lambda j: (l, j, 0, 0)),
                  pl.BlockSpec((None, tg, gw, 4 * S5_STATE), lambda j: (l, j, 0, 0)),
                  pl.BlockSpec((None, tg, 4 * S5_STATE, gw), lambda j: (l, j, 0, 0)),
                  pl.BlockSpec((None, tg, 2, S5_LOG_CHUNKS, 2 * S5_STATE), lambda j: (l, j, 0, 0, 0)),
                  pl.BlockSpec((None, tg, 2, S5_LOG_CHUNKS, 2 * S5_STATE), lambda j: (l, j, 0, 0, 0)),
                  pl.BlockSpec((None, tg, 1, gw), lambda j: (l, j, 0, 0))],
        out_specs=pl.BlockSpec((SEQ, lane), lambda j: (0, j)),
        out_shape=jax.ShapeDtypeStruct((SEQ, BRANCH), F32),
        scratch_shapes=[pltpu.VMEM((tg, S5_CHUNKS, gw), F32)],
        compiler_params=_cparams(("parallel",)),
        name="s5_scan",
    )(rest, w_intra, w_in, w_out, a_re, a_im, skip_t)


def _merge_kernel(hm_ref, hyt_ref, s5_ref, wa_ref, wb_ref, wga_ref, wgg_ref, g0_ref, g1_ref, g2_ref,
                  out_ref, hy_sc):
    @pl.when(pl.program_id(1) == 0)
    def _():
        hy_sc[...] = hyt_ref[...].T.astype(BF16)

    def proj(a, w_ref):
        return jnp.dot(a, w_ref[...].astype(BF16), preferred_element_type=F32)

    s5b = s5_ref[...].astype(BF16)
    out_a = proj(hm_ref[...], wa_ref)
    out_b = proj(hy_sc[...], wb_ref)
    out_c = proj(s5b, wga_ref) * jax.nn.sigmoid(proj(s5b, wgg_ref))
    merged = (jax.nn.sigmoid(g0_ref[...]) * out_a + jax.nn.sigmoid(g1_ref[...]) * out_b
              + jax.nn.sigmoid(g2_ref[...]) * out_c)
    out_ref[...] = merged.astype(out_ref.dtype)


def _merge(hm, hy_t, s5y, rest, w_a, w_b, w_glu, l, *, tm=512, tn=512):
    nj = D_MODEL // tn
    goff = BRANCH // tn
    return pl.pallas_call(
        _merge_kernel,
        grid=(SEQ // tm, nj),
        in_specs=[pl.BlockSpec((tm, BRANCH), lambda i, j: (i, 0)),
                  pl.BlockSpec((BRANCH, tm), lambda i, j: (0, i)),
                  pl.BlockSpec((tm, BRANCH), lambda i, j: (i, 0)),
                  pl.BlockSpec((None, BRANCH, tn), lambda i, j: (l, 0, j)),
                  pl.BlockSpec((None, BRANCH, tn), lambda i, j: (l, 0, j)),
                  pl.BlockSpec((None, BRANCH, tn), lambda i, j: (l, 0, j)),
                  pl.BlockSpec((None, BRANCH, tn), lambda i, j: (l, 0, nj + j)),
                  pl.BlockSpec((tm, tn), lambda i, j: (i, goff + j)),
                  pl.BlockSpec((tm, tn), lambda i, j: (i, goff + nj + j)),
                  pl.BlockSpec((tm, tn), lambda i, j: (i, goff + 2 * nj + j))],
        out_specs=pl.BlockSpec((tm, tn), lambda i, j: (i, j)),
        out_shape=jax.ShapeDtypeStruct((SEQ, D_MODEL), BF16),
        scratch_shapes=[pltpu.VMEM((tm, BRANCH), BF16)],
        compiler_params=_cparams(("parallel", "arbitrary")),
        name="branch_merge",
    )(hm, hy_t, s5y, w_a, w_b, w_glu, w_glu, rest, rest, rest)


def _layer_norm_rows(y, g, b):
    mu = jnp.mean(y, axis=1, keepdims=True)
    var = jnp.mean(jnp.square(y - mu), axis=1, keepdims=True)
    return (y - mu) * lax.rsqrt(var + LN_EPS) * g + b


def _res_ln_kernel(x_ref, f_ref, g_ref, b_ref, o_ref, ob_ref, *, transposed):
    f = f_ref[...]
    if transposed:
        f = f.T
    y = _layer_norm_rows(ALPHA * x_ref[...] + f, g_ref[...], b_ref[...])
    o_ref[...] = y
    ob_ref[...] = y.astype(BF16)


def _res_ln(x, f, g, b, l, *, transposed, tm=256):
    f_spec = (pl.BlockSpec((D_MODEL, tm), lambda i: (0, i)) if transposed
              else pl.BlockSpec((tm, D_MODEL), lambda i: (i, 0)))
    n_tok = x.shape[0]
    return pl.pallas_call(
        functools.partial(_res_ln_kernel, transposed=transposed),
        grid=(n_tok // tm,),
        in_specs=[pl.BlockSpec((tm, D_MODEL), lambda i: (i, 0)), f_spec,
                  pl.BlockSpec((None, 1, D_MODEL), lambda i: (l, 0, 0)),
                  pl.BlockSpec((None, 1, D_MODEL), lambda i: (l, 0, 0))],
        out_specs=[pl.BlockSpec((tm, D_MODEL), lambda i: (i, 0)),
                   pl.BlockSpec((tm, D_MODEL), lambda i: (i, 0))],
        out_shape=[jax.ShapeDtypeStruct((n_tok, D_MODEL), F32), jax.ShapeDtypeStruct((n_tok, D_MODEL), BF16)],
        compiler_params=_cparams(("parallel",)),
        name="residual_layernorm",
    )(x, f, g.reshape(DEPTH, 1, D_MODEL), b.reshape(DEPTH, 1, D_MODEL))


PEER_NO_RANK = 64.0


def _extract_sorted(s, n, out_ref, base, want_rank=False):
    rank = jnp.full_like(s, PEER_NO_RANK) if want_rank else None
    for r in range(n):
        m = jnp.max(s, axis=0, keepdims=True)
        out_ref[base + r:base + r + 1, :] = m
        hit = s == m
        if want_rank:
            rank = jnp.where(hit, float(r), rank)
        s = jnp.where(hit, -jnp.inf, s)
    return rank


def _peer_select_kernel(q_ref, keys_ref, cnt_ref, r2_ref, e1_ref, e2_ref, top_sc, kth_sc):
    k = PEER_TOPK
    qb = q_ref[...].astype(BF16)
    half = PEER_KEYS
    for h in range(PEER_HEADS):
        sc = []
        rank2 = None
        for c in range(2):
            qs = qb[:, (2 * h + c) * half:(2 * h + c + 1) * half]
            s = lax.dot_general(keys_ref[h, c].astype(BF16), qs, (((1,), (1,)), ((), ())),
                                preferred_element_type=F32)
            sc.append(s)
            rank2 = _extract_sorted(s, k, top_sc, c * k, want_rank=(c == 1))
        a = top_sc[0:k, :]
        b = top_sc[k:2 * k, :]
        sub = lax.broadcasted_iota(jnp.int32, (8, a.shape[1]), 0)
        groups = [a[0:1] + b[0:8], a[0:1] + b[8:16], a[1:2] + b[0:8]]
        for i in range(2, 8):
            groups.append(jnp.where(sub < k // (i + 1), a[i:i + 1] + b[0:8], -jnp.inf))
        groups.append(a[8:16] + b[0:1])
        cand = jnp.concatenate(groups, axis=0)
        _extract_sorted(cand, k, kth_sc, 0)
        tau = kth_sc[k - 1:k, :]
        top = a[0:1] + b[0:1]
        z = jnp.sum(jnp.where(cand >= tau, jnp.exp(cand - top), 0.0), axis=0, keepdims=True)
        cnt = jnp.zeros_like(sc[0])
        for j in range(k):
            bj = top_sc[k + j:k + j + 1, :]
            cnt = jnp.where(sc[0] + bj >= tau, j + 1.0, cnt)
        cnt_ref[h] = cnt
        r2_ref[h] = rank2.astype(BF16)
        e1_ref[h] = jnp.exp(sc[0] - a[0:1])
        e2_ref[h] = (jnp.exp(sc[1] - b[0:1]) / z).astype(BF16)


def _peer_select(q, keys, l, *, tt=128):
    n_tok = q.shape[0]
    big = jax.ShapeDtypeStruct((PEER_HEADS, PEER_KEYS, n_tok), F32)
    big_bf = jax.ShapeDtypeStruct((PEER_HEADS, PEER_KEYS, n_tok), BF16)
    big_spec = pl.BlockSpec((PEER_HEADS, PEER_KEYS, tt), lambda i: (0, 0, i))
    return pl.pallas_call(
        _peer_select_kernel,
        grid=(n_tok // tt,),
        in_specs=[pl.BlockSpec((tt, D_MODEL), lambda i: (i, 0)),
                  pl.BlockSpec((None, PEER_HEADS, 2, PEER_KEYS, PEER_KEYS), lambda i: (l, 0, 0, 0, 0))],
        out_specs=[big_spec, big_spec, big_spec, big_spec],
        out_shape=[big, big_bf, big, big_bf],
        scratch_shapes=[pltpu.VMEM((2 * PEER_TOPK, tt), F32), pltpu.VMEM((PEER_TOPK, tt), F32)],
        compiler_params=_cparams(("parallel",)),
        name="peer_select",
    )(q, keys)


def _peer_dense_kernel(x_ref, u_ref, vt_ref, cnt_ref, r2_ref, e1_ref, e2_ref, out_ref, act_sc, g_sc, *, rows):
    e = pl.program_id(1)

    @pl.when(e == 0)
    def _():
        out_ref[...] = jnp.zeros_like(out_ref)

    act_sc[...] = lax.dot_general(u_ref[...], x_ref[...], (((1,), (1,)), ((), ())),
                                  preferred_element_type=F32)
    kk = PEER_KEYS
    for tc in range(act_sc.shape[1] // kk):
        lanes = slice(tc * kk, (tc + 1) * kk)
        for i in range(rows):
            w = jnp.zeros((kk, kk), BF16)
            for h in range(PEER_HEADS):
                hit = r2_ref[h, :, lanes] < cnt_ref[h, i:i + 1, lanes].astype(BF16)
                val = e2_ref[h, :, lanes] * e1_ref[h, i:i + 1, lanes].astype(BF16)
                w = w + jnp.where(hit, val, jnp.zeros_like(val))
            g = jax.nn.gelu(act_sc[i * kk:(i + 1) * kk, lanes]) * w.astype(F32)
            g_sc[i * kk:(i + 1) * kk, lanes] = g.astype(BF16)
    out_ref[...] += jnp.dot(vt_ref[...], g_sc[...], preferred_element_type=F32)


def _transpose_tile_kernel(v_ref, o_ref):
    o_ref[...] = v_ref[...].T.astype(o_ref.dtype)


def _peer_value_tiles(v, l, te):
    return pl.pallas_call(
        _transpose_tile_kernel,
        grid=(N_EXPERTS // te,),
        in_specs=[pl.BlockSpec((None, te, D_MODEL), lambda e: (l, e, 0))],
        out_specs=pl.BlockSpec((None, D_MODEL, te), lambda e: (e, 0, 0)),
        out_shape=jax.ShapeDtypeStruct((N_EXPERTS // te, D_MODEL, te), BF16),
        compiler_params=_cparams(("parallel",)),
        name="peer_value_tiles",
    )(v)


def _peer_dense(x_bf, u_bf, v, l, sel, *, tt=512, rows=8):
    cnt, r2, e1, e2 = sel
    te = rows * PEER_KEYS
    n_tok = x_bf.shape[0]
    vt_bf = _peer_value_tiles(v, l, te)
    row_spec = pl.BlockSpec((PEER_HEADS, rows, tt), lambda i, e: (0, e, i))
    all_spec = pl.BlockSpec((PEER_HEADS, PEER_KEYS, tt), lambda i, e: (0, 0, i))
    return pl.pallas_call(
        functools.partial(_peer_dense_kernel, rows=rows),
        grid=(n_tok // tt, N_EXPERTS // te),
        in_specs=[pl.BlockSpec((tt, D_MODEL), lambda i, e: (i, 0)),
                  pl.BlockSpec((te, D_MODEL), lambda i, e: (e, 0)),
                  pl.BlockSpec((None, D_MODEL, te), lambda i, e: (e, 0, 0)),
                  row_spec, all_spec, row_spec, all_spec],
        out_specs=pl.BlockSpec((D_MODEL, tt), lambda i, e: (0, i)),
        out_shape=jax.ShapeDtypeStruct((D_MODEL, n_tok), F32),
        scratch_shapes=[pltpu.VMEM((te, tt), F32), pltpu.VMEM((te, tt), BF16)],
        compiler_params=_cparams(("parallel", "arbitrary")),
        name="peer_dense",
    )(x_bf, u_bf, vt_bf, cnt, r2, e1, e2)


def _hyena_features():
    pos = jnp.arange(SEQ, dtype=F32)
    t = pos / (SEQ - 1)
    bands = jnp.linspace(1e-4, 16 - 1, 16, dtype=F32)
    ang = 2.0 * math.pi * pos[:, None] * bands[None, :] / SEQ
    feat = jnp.concatenate([t[:, None], jnp.cos(ang), -jnp.sin(ang)], axis=-1)
    feat_t = jnp.pad(feat.T, ((0, HY_EMB_PAD - feat.shape[1]), (0, 0)))
    t_row = t[None, :]

    def reversed_time(a):
        return jnp.roll(a[:, ::-1], 1, axis=1)

    return (jnp.concatenate([feat_t, reversed_time(feat_t)], axis=1),
            jnp.concatenate([t_row, reversed_time(t_row)], axis=1))


def kernel(x, w_in, mlstm_gate_bias, mlstm_norm_gain, w_mlstm_out, hyena_conv_w, hyena_conv_b, hyena_w1,
           hyena_b1, hyena_w2, hyena_b2, hyena_freq, hyena_w3, hyena_decay, hyena_skip, w_hyena_out,
           s5_lambda_re, s5_lambda_im, s5_log_step, s5_b_re, s5_b_im, s5_c_re, s5_c_im, s5_skip, w_s5_glu,
           w_out, ln1_g, ln1_b, peer_w_q, peer_subkeys, peer_u, peer_v, ln2_g, ln2_b):
    xf = x.reshape(SEQ, D_MODEL)
    xb = xf.astype(BF16)
    feat_t, t_row = _hyena_features()
    w_gates = w_in[:, :, OFF_GATES:OFF_HYENA]
    w_tail = _w_in_tail(w_in)
    s5_mats = jax.vmap(_s5_chunk_matrices)(s5_lambda_re, s5_lambda_im, s5_log_step, s5_b_re, s5_b_im,
                                           s5_c_re, s5_c_im)
    for l in range(DEPTH):
        qkvo = _mm(xb, w_in, l, 0, 4 * BRANCH, tm=1024, tn=512, name="proj_qkvo")
        gates = _mm(xb, w_gates, l, 0, N_GATES, tm=1024, tn=N_GATES, name="proj_gates")
        hp_t = _mm_tn(w_tail, l, 3 * BRANCH, xb, tr=512, tm=1024, name="proj_hyena_t")
        rest = _mm(xb, w_tail, l, 3 * BRANCH, BRANCH + 3 * D_MODEL, tm=1024, tn=512, name="proj_s5_mixgates")

        hdir = _mlstm(qkvo, gates, mlstm_gate_bias, l)
        hm = _mlstm_post(hdir, qkvo, mlstm_norm_gain, l)
        hy_t = _hyena(hp_t, l, hyena_conv_w, hyena_conv_b, hyena_w1, hyena_b1, hyena_w2, hyena_b2,
                      hyena_freq, hyena_w3, hyena_decay, hyena_skip, feat_t, t_row)
        s5y = _s5(rest, s5_mats, s5_skip, l)
        merged = _merge(hm, hy_t, s5y, rest, w_mlstm_out, w_hyena_out, w_s5_glu, l)
        mix = _mm(merged, w_out, l, 0, D_MODEL, tm=1024, tn=512, name="proj_out")
        xf, xb = _res_ln(xf, mix, ln1_g, ln1_b, l, transposed=False)

        q = _mm(xb, peer_w_q, l, 0, PEER_HEADS * 2 * PEER_KEYS, tm=1024, tn=512, name="peer_query")
        sel = _peer_select(q, peer_subkeys, l)
        ffn_t = _peer_dense(xb, peer_u[l].astype(BF16), peer_v, l, sel)
        xf, xb = _res_ln(xf, ffn_t, ln2_g, ln2_b, l, transposed=True)
    return xf.reshape(1, SEQ, D_MODEL)
```

```python
import functools
import math

import numpy as np
import jax
import jax.numpy as jnp
from jax import lax
from jax.experimental import pallas as pl
from jax.experimental.pallas import tpu as pltpu

F32 = jnp.float32
BF16 = jnp.bfloat16

D_MODEL = 2048
SEQ = 8192
DEPTH = 4
BRANCH = 1024
HEADS = 4
HEAD_DIM = 256
CHUNK = 128
N_GATES = 16
HY_HID = 64
HY_EMB_PAD = 128
S5_GROUP = 16
S5_GROUPS = 64
S5_STATE = 64
S5_T = 16
S5_CHUNKS = SEQ // S5_T
S5_LOG_CHUNKS = 9
PEER_HEADS = 8
PEER_KEYS = 128
PEER_TOPK = 16
N_EXPERTS = PEER_KEYS * PEER_KEYS
ALPHA = (2 * DEPTH) ** 0.25
LN_EPS = 1e-5
FFT_N = 2 * SEQ
FFT_R = 128
OFF_GATES = 4 * BRANCH
OFF_HYENA = OFF_GATES + N_GATES
OFF_S5 = OFF_HYENA + 3 * BRANCH
OFF_MIXG = OFF_S5 + BRANCH

VMEM_LIMIT = 56 * 1024 * 1024


def _cparams(sem):
    return pltpu.CompilerParams(dimension_semantics=sem, vmem_limit_bytes=VMEM_LIMIT)


def _split_bf16(a):
    hi = a.astype(BF16)
    lo = (a - hi.astype(F32)).astype(BF16)
    return hi, lo


def _dot3(a, b_hi, b_lo):
    a_hi, a_lo = _split_bf16(a)
    acc = jnp.dot(a_hi, b_hi, preferred_element_type=F32)
    acc += jnp.dot(a_lo, b_hi, preferred_element_type=F32)
    acc += jnp.dot(a_hi, b_lo, preferred_element_type=F32)
    return acc


def _dot3_both(a, b):
    b_hi, b_lo = _split_bf16(b)
    return _dot3(a, b_hi, b_lo)


def _mm_kernel(a_ref, b_ref, o_ref):
    o_ref[...] = jnp.dot(a_ref[...].astype(BF16), b_ref[...].astype(BF16),
                         preferred_element_type=F32).astype(o_ref.dtype)


def _mm(a, w, l, col_off, n_cols, *, tm, tn, name):
    m, k = a.shape
    cb = col_off // tn
    assert col_off % tn == 0 and n_cols % tn == 0 and m % tm == 0
    return pl.pallas_call(
        _mm_kernel,
        grid=(m // tm, n_cols // tn),
        in_specs=[pl.BlockSpec((tm, k), lambda i, j: (i, 0)),
                  pl.BlockSpec((None, k, tn), lambda i, j: (l, 0, cb + j))],
        out_specs=pl.BlockSpec((tm, tn), lambda i, j: (i, j)),
        out_shape=jax.ShapeDtypeStruct((m, n_cols), F32),
        compiler_params=_cparams(("parallel", "parallel")),
        name=name,
    )(a, w)


def _mm_wt_kernel(a_ref, w_ref, o_ref):
    o_ref[...] = lax.dot_general(a_ref[...].astype(BF16), w_ref[...].astype(BF16), (((1,), (1,)), ((), ())),
                                 preferred_element_type=F32).astype(o_ref.dtype)


def _mm_wt(a, wt, l, row_off, n_rows, *, tm, tn, name):
    m, k = a.shape
    rb = row_off // tn
    assert row_off % tn == 0 and n_rows % tn == 0 and m % tm == 0
    return pl.pallas_call(
        _mm_wt_kernel,
        grid=(m // tm, n_rows // tn),
        in_specs=[pl.BlockSpec((tm, k), lambda i, j: (i, 0)),
                  pl.BlockSpec((None, tn, k), lambda i, j: (l, rb + j, 0))],
        out_specs=pl.BlockSpec((tm, tn), lambda i, j: (i, j)),
        out_shape=jax.ShapeDtypeStruct((m, n_rows), F32),
        compiler_params=_cparams(("parallel", "parallel")),
        name=name,
    )(a, wt)


def _mm_nt_kernel(w_ref, a_ref, o_ref):
    o_ref[...] = lax.dot_general(w_ref[...].astype(BF16), a_ref[...].astype(BF16), (((1,), (1,)), ((), ())),
                                 preferred_element_type=F32).astype(o_ref.dtype)


def _mm_nt(wt, l, row_off, n_rows, a, *, tr, tm, name, out_dtype):
    k = wt.shape[2]
    m = a.shape[0]
    rb = row_off // tr
    assert row_off % tr == 0 and n_rows % tr == 0
    return pl.pallas_call(
        _mm_nt_kernel,
        grid=(n_rows // tr, m // tm),
        in_specs=[pl.BlockSpec((None, tr, k), lambda i, j: (l, rb + i, 0)),
                  pl.BlockSpec((tm, k), lambda i, j: (j, 0))],
        out_specs=pl.BlockSpec((tr, tm), lambda i, j: (i, j)),
        out_shape=jax.ShapeDtypeStruct((n_rows, m), out_dtype),
        compiler_params=_cparams(("parallel", "parallel")),
        name=name,
    )(wt, a)


def _w_tail_kernel(cur_ref, nxt_ref, o_ref, *, shift):
    o_ref[...] = jnp.concatenate([cur_ref[shift:, :], nxt_ref[...]], axis=0).astype(o_ref.dtype)


def _w_in_tail(w_in_t, *, tn=512):
    depth, n_all, k = w_in_t.shape
    n_out = n_all - OFF_HYENA
    return pl.pallas_call(
        functools.partial(_w_tail_kernel, shift=N_GATES),
        grid=(depth, n_out // tn),
        in_specs=[pl.BlockSpec((None, tn, k), lambda l, j: (l, OFF_GATES // tn + j, 0)),
                  pl.BlockSpec((None, N_GATES, k), lambda l, j: (l, (OFF_GATES + (j + 1) * tn) // N_GATES, 0))],
        out_specs=pl.BlockSpec((None, tn, k), lambda l, j: (l, j, 0)),
        out_shape=jax.ShapeDtypeStruct((depth, n_out, k), BF16),
        compiler_params=_cparams(("parallel", "parallel")),
        name="w_in_tail",
    )(w_in_t, w_in_t)


def _log_sigmoid(x):
    return jnp.minimum(x, 0.0) - jnp.log1p(jnp.exp(-jnp.abs(x)))


def _mlstm_chunk(d, q, k, v, li_r, lf_r, li_c, lf_c, c_sc, n_sc, m_sc):
    n = CHUNK
    row = lax.broadcasted_iota(jnp.int32, (n, n), 0)
    col = lax.broadcasted_iota(jnp.int32, (n, n), 1)
    valid = (row >= col) if d == 0 else (row <= col)
    valid_t = (col >= row) if d == 0 else (col <= row)
    b_col = jnp.sum(jnp.where(valid, lf_r, 0.0), axis=1, keepdims=True)
    b_row = jnp.sum(jnp.where(valid_t, lf_c, 0.0), axis=0, keepdims=True)
    total = jnp.sum(lf_r, axis=1, keepdims=True)

    m_prev = m_sc[...]
    m_inter = b_col + m_prev
    log_d = jnp.where(valid, b_col - b_row + li_r, -jnp.inf)
    m_t = jnp.maximum(m_inter, jnp.max(log_d, axis=1, keepdims=True))
    inter = jnp.exp(m_inter - m_t)
    dmat = jnp.exp(log_d - m_t)

    k = k * (HEAD_DIM ** -0.5)
    qb = q.astype(BF16)
    kb = k.astype(BF16)
    vb = v.astype(BF16)
    s = lax.dot_general(qb, kb, (((1,), (1,)), ((), ())), preferred_element_type=F32) * dmat
    num = jnp.dot(s.astype(BF16), vb, preferred_element_type=F32)
    num += inter * jnp.dot(qb, c_sc[...].astype(BF16), preferred_element_type=F32)
    qn = jnp.sum(q * n_sc[...], axis=1, keepdims=True)
    den = jnp.sum(s, axis=1, keepdims=True) + inter * qn
    h_out = num / jnp.maximum(jnp.abs(den), jnp.exp(-m_t))

    log_w = total - b_col + li_c
    m_new = jnp.maximum(total + m_prev, jnp.max(log_w, axis=0, keepdims=True))
    w = jnp.exp(log_w - m_new)
    decay = jnp.exp(total + m_prev - m_new)
    kw = k * w
    c_sc[...] = decay * c_sc[...] + lax.dot_general(
        kw.astype(BF16), vb, (((0,), (0,)), ((), ())), preferred_element_type=F32)
    n_sc[...] = decay * n_sc[...] + jnp.sum(kw, axis=0, keepdims=True)
    m_sc[...] = m_new
    return h_out


def _mlstm_kernel(qkv_f_ref, qkv_b_ref, gr_f_ref, gr_b_ref, gc_f_ref, gc_b_ref, bias_ref,
                  o_f_ref, o_b_ref, c_sc, n_sc, m_sc):
    @pl.when(pl.program_id(0) == 0)
    def _():
        c_sc[...] = jnp.zeros_like(c_sc)
        n_sc[...] = jnp.zeros_like(n_sc)
        m_sc[...] = jnp.zeros_like(m_sc)

    for d, (qkv_ref, gr_ref, gc_ref, o_ref) in enumerate(((qkv_f_ref, gr_f_ref, gc_f_ref, o_f_ref),
                                                         (qkv_b_ref, gr_b_ref, gc_b_ref, o_b_ref))):
        for h in range(HEADS):
            cols = slice(h * HEAD_DIM, (h + 1) * HEAD_DIM)
            b_i = bias_ref[d * 2 * HEADS + h]
            b_f = bias_ref[d * 2 * HEADS + HEADS + h]
            chain = d * HEADS + h
            o_ref[:, cols] = _mlstm_chunk(
                d,
                qkv_ref[:, cols],
                qkv_ref[:, BRANCH + h * HEAD_DIM:BRANCH + (h + 1) * HEAD_DIM],
                qkv_ref[:, 2 * BRANCH + h * HEAD_DIM:2 * BRANCH + (h + 1) * HEAD_DIM],
                gr_ref[h] + b_i, _log_sigmoid(gr_ref[HEADS + h] + b_f),
                gc_ref[h] + b_i, _log_sigmoid(gc_ref[HEADS + h] + b_f),
                c_sc.at[chain], n_sc.at[chain], m_sc.at[chain])


def _mlstm(qkvo, gates_t, gate_bias, l):
    nc = SEQ // CHUNK
    g_rows = gates_t.reshape(N_GATES, 1, SEQ)
    g_cols = gates_t.reshape(N_GATES, SEQ, 1)
    bias = gate_bias.reshape(DEPTH, N_GATES, 1, 1)
    half = N_GATES // 2
    chains = 2 * HEADS
    out = jax.ShapeDtypeStruct((SEQ, BRANCH), F32)
    return pl.pallas_call(
        _mlstm_kernel,
        grid=(nc,),
        in_specs=[
            pl.BlockSpec((CHUNK, 4 * BRANCH), lambda c: (c, 0)),
            pl.BlockSpec((CHUNK, 4 * BRANCH), lambda c: (nc - 1 - c, 0)),
            pl.BlockSpec((half, 1, CHUNK), lambda c: (0, 0, c)),
            pl.BlockSpec((half, 1, CHUNK), lambda c: (1, 0, nc - 1 - c)),
            pl.BlockSpec((half, CHUNK, 1), lambda c: (0, c, 0)),
            pl.BlockSpec((half, CHUNK, 1), lambda c: (1, nc - 1 - c, 0)),
            pl.BlockSpec((None, N_GATES, 1, 1), lambda c: (l, 0, 0, 0)),
        ],
        out_specs=[pl.BlockSpec((CHUNK, BRANCH), lambda c: (c, 0)),
                   pl.BlockSpec((CHUNK, BRANCH), lambda c: (nc - 1 - c, 0))],
        out_shape=[out, out],
        scratch_shapes=[pltpu.VMEM((chains, HEAD_DIM, HEAD_DIM), F32),
                        pltpu.VMEM((chains, 1, HEAD_DIM), F32),
                        pltpu.VMEM((chains, 1, 1), F32)],
        compiler_params=_cparams(("arbitrary",)),
        name="mlstm_scan",
    )(qkvo, qkvo, g_rows, g_rows, g_cols, g_cols, bias)


def _mlstm_post_kernel(hf_ref, hb_ref, o_ref, gain_ref, out_ref):
    h = hf_ref[...] + hb_ref[...]
    h = jax.nn.sigmoid(o_ref[...]) * h
    for hd in range(HEADS):
        sl = slice(hd * HEAD_DIM, (hd + 1) * HEAD_DIM)
        hh = h[:, sl]
        mu = jnp.mean(hh, axis=1, keepdims=True)
        var = jnp.mean(jnp.square(hh - mu), axis=1, keepdims=True)
        out_ref[:, sl] = ((hh - mu) * lax.rsqrt(var + LN_EPS) * gain_ref[:, sl]).astype(out_ref.dtype)


def _mlstm_post(hdir, qkvo, gain, l, *, tm=512):
    return pl.pallas_call(
        _mlstm_post_kernel,
        grid=(SEQ // tm,),
        in_specs=[pl.BlockSpec((tm, BRANCH), lambda i: (i, 0)),
                  pl.BlockSpec((tm, BRANCH), lambda i: (i, 0)),
                  pl.BlockSpec((tm, BRANCH), lambda i: (i, 3)),
                  pl.BlockSpec((None, 1, BRANCH), lambda i: (l, 0, 0))],
        out_specs=pl.BlockSpec((tm, BRANCH), lambda i: (i, 0)),
        out_shape=jax.ShapeDtypeStruct((SEQ, BRANCH), BF16),
        compiler_params=_cparams(("parallel",)),
        name="mlstm_post",
    )(hdir[0], hdir[1], qkvo, gain.reshape(DEPTH, 1, BRANCH))


HY_CT = 32
HY_RB = 512


@functools.lru_cache(maxsize=None)
def _fft_constants():
    r = FFT_R
    idx = np.arange(r)
    th = 2.0 * np.pi * np.outer(idx, idx) / r
    cs, sn = np.cos(th), np.sin(th)
    ph = 2.0 * np.pi * np.outer(idx, idx) / FFT_N
    f1 = np.concatenate([cs, -sn], axis=1)
    m3 = np.block([[cs, -sn], [sn, cs]])
    m4 = np.block([[cs, sn], [-sn, cs]])
    m6 = np.concatenate([cs, -sn], axis=0) / FFT_N
    m6[:, r // 2:] = 0.0
    consts = {name: jnp.asarray(mat, F32).astype(BF16)
              for name, mat in (("f1", f1), ("m3", m3), ("m4", m4), ("m6", m6))}
    consts["tw"] = (jnp.asarray(np.cos(ph), F32), jnp.asarray(np.sin(ph), F32))
    return consts


def _rows(i, n):
    return pl.ds(pl.multiple_of(i * n, n), n)


def _for_each(n, body, unroll=1):
    def step(i, carry):
        body(i)
        return carry

    lax.fori_loop(0, n, step, 0, unroll=unroll)


def _fwd_stage1(sig_ref, hi_ref, ct, t_buf, a_buf, b_buf, f1_ref, twr_ref, twi_ref):
    r = FFT_R
    zeros = jnp.zeros((r // 2, r), F32)

    def tr_in(c):
        hi = zeros if hi_ref is None else hi_ref[c]
        t_buf[_rows(c, r), :] = jnp.concatenate([sig_ref[c], hi], axis=0).T.astype(BF16)

    _for_each(ct, tr_in, unroll=4)

    def stage1(b):
        rows = _rows(b, HY_RB)
        a_buf[rows, :] = jnp.dot(t_buf[rows, :], f1_ref[...], preferred_element_type=F32)

    _for_each(ct * r // HY_RB, stage1)

    def twiddle(c):
        rows = _rows(c, r)
        a = a_buf[rows, :]
        ar, ai = a[:, :r], a[:, r:]
        twr, twi = twr_ref[...], twi_ref[...]
        b_buf[rows, :r] = (ar * twr + ai * twi).T.astype(BF16)
        b_buf[rows, r:] = (ai * twr - ar * twi).T.astype(BF16)

    _for_each(ct, twiddle, unroll=4)


def _hy_mlp_kernel(feat_ref, w1_ref, b1_ref, w2_ref, b2_ref, fr_ref, out_ref):
    h = _dot3_both(w1_ref[...], feat_ref[...]) + b1_ref[...]
    h = jnp.sin(fr_ref[0] * h)
    h = _dot3_both(w2_ref[...], h) + b2_ref[...]
    out_ref[...] = jnp.sin(fr_ref[1] * h)


def _hy_filter_kernel(w3_ref, h_ref, dec_ref, t_ref, out_ref):
    d = pl.program_id(0)
    filt = _dot3_both(w3_ref[...], h_ref[...])
    filt = filt * jnp.exp(-t_ref[...] * jnp.abs(dec_ref[...]))
    filt = filt / (jnp.sum(jnp.abs(filt), axis=1, keepdims=True) + 1e-6)
    lane = lax.broadcasted_iota(jnp.int32, filt.shape, 1)
    out_ref[...] = jnp.where((lane == 0) & (d == 1), 0.0, filt)


def _hy_spec_kernel(fa_ref, fb_ref, f1_ref, m3_ref, twr_ref, twi_ref, out_ref, t_buf, a_buf, b_buf, *, ct):
    r = FFT_R
    nblk = ct * r // HY_RB
    _fwd_stage1(fa_ref, fb_ref, ct, t_buf, a_buf, b_buf, f1_ref, twr_ref, twi_ref)

    def put(b):
        rows = _rows(b, HY_RB)
        out_ref[rows, :] = jnp.dot(b_buf[rows, :], m3_ref[...], preferred_element_type=F32)

    _for_each(nblk, put)


def _shift_prev(p, n1, n2):
    r1 = pltpu.roll(p, 1, 2)
    r2 = pltpu.roll(r1, 1, 1)
    return jnp.where(n2 == 0, jnp.where(n1 == 0, 0.0, r2), r1)


def _shift_next(p, n1, n2):
    s1, s2 = p.shape[1], p.shape[2]
    r1 = pltpu.roll(p, s2 - 1, 2)
    r2 = pltpu.roll(r1, s1 - 1, 1)
    return jnp.where(n2 == s2 - 1, jnp.where(n1 == s1 - 1, 0.0, r2), r1)


def _hy_conv_kernel(p0_ref, p1_ref, p2_ref, cw_ref, cb_ref, skip_ref, h_ref,
                    f1_ref, m3_ref, m4_ref, m6_ref, twr_ref, twi_ref,
                    out_ref, z_buf, t_buf, a_buf, b_buf, *, ct):
    r = FFT_R
    n1_len = SEQ // r
    nblk = ct * r // HY_RB
    shp = (ct, n1_len, r)
    n1 = lax.broadcasted_iota(jnp.int32, shp, 1)
    n2 = lax.broadcasted_iota(jnp.int32, shp, 2)

    def short_conv(p_ref, g):
        p = p_ref[...].astype(F32)
        return (_shift_prev(p, n1, n2) * cw_ref[3 * g] + p * cw_ref[3 * g + 1]
                + _shift_next(p, n1, n2) * cw_ref[3 * g + 2] + cb_ref[g])

    z_buf[...] = short_conv(p1_ref, 1) * short_conv(p2_ref, 2)
    _fwd_stage1(z_buf, None, ct, t_buf, a_buf, b_buf, f1_ref, twr_ref, twi_ref)

    def freq_domain(b):
        rows = _rows(b, HY_RB)
        x = jnp.dot(b_buf[rows, :], m3_ref[...], preferred_element_type=F32)
        h = h_ref[rows, :]
        xr, xi = x[:, :r], x[:, r:]
        hr, hi = h[:, :r], h[:, r:]
        y = jnp.concatenate([xr * hr - xi * hi, xr * hi + xi * hr], axis=1).astype(BF16)
        a_buf[rows, :] = jnp.dot(y, m4_ref[...], preferred_element_type=F32)

    _for_each(nblk, freq_domain)

    def inv_twiddle(c):
        rows = _rows(c, r)
        b = a_buf[rows, :]
        br, bi = b[:, :r], b[:, r:]
        twr, twi = twr_ref[...], twi_ref[...]
        b_buf[rows, :r] = (br * twr - bi * twi).T.astype(BF16)
        b_buf[rows, r:] = (br * twi + bi * twr).T.astype(BF16)

    _for_each(ct, inv_twiddle, unroll=4)

    def stage6(b):
        rows = _rows(b, HY_RB)
        a_buf[rows, :r] = jnp.dot(b_buf[rows, :], m6_ref[...], preferred_element_type=F32)

    _for_each(nblk, stage6)

    def finish(c):
        y_c = a_buf[_rows(c, r), :r].T[:n1_len, :]
        out_ref[c] = y_c + skip_ref[c] * z_buf[c]

    _for_each(ct, finish, unroll=4)
    out_ref[...] = short_conv(p0_ref, 0) * out_ref[...]


def _hyena(hp_t, l, conv_w, conv_b, w1, b1, w2, b2, freq, w3, decay, skip, feat_t, t_row):
    r = FFT_R
    n1 = SEQ // r
    ct = HY_CT
    k = _fft_constants()
    f1, m3, m4, m6 = k["f1"], k["m3"], k["m4"], k["m6"]
    twr, twi = k["tw"]

    def full(a):
        return pl.BlockSpec(a.shape, lambda *_: (0,) * a.ndim)

    h2 = pl.pallas_call(
        _hy_mlp_kernel,
        grid=(2,),
        in_specs=[pl.BlockSpec((feat_t.shape[0], SEQ), lambda i: (0, i)),
                  pl.BlockSpec((None, HY_HID, feat_t.shape[0]), lambda i: (l, 0, 0)),
                  pl.BlockSpec((None, HY_HID, 1), lambda i: (l, 0, 0)),
                  pl.BlockSpec((None, HY_HID, HY_HID), lambda i: (l, 0, 0)),
                  pl.BlockSpec((None, HY_HID, 1), lambda i: (l, 0, 0)),
                  pl.BlockSpec((None, 2, HY_HID, 1), lambda i: (l, 0, 0, 0))],
        out_specs=pl.BlockSpec((HY_HID, SEQ), lambda i: (0, i)),
        out_shape=jax.ShapeDtypeStruct((HY_HID, 2 * SEQ), F32),
        compiler_params=_cparams(("parallel",)),
        name="hyena_filter_mlp",
    )(feat_t, jnp.pad(jnp.swapaxes(w1, 1, 2), ((0, 0), (0, 0), (0, feat_t.shape[0] - w1.shape[1]))),
      b1[:, :, None], jnp.swapaxes(w2, 1, 2), b2[:, :, None], freq[:, :, :, None])

    rt = 128
    filt = pl.pallas_call(
        _hy_filter_kernel,
        grid=(2, BRANCH // rt),
        in_specs=[pl.BlockSpec((None, rt, HY_HID), lambda d, i: (l, d * (BRANCH // rt) + i, 0)),
                  pl.BlockSpec((HY_HID, SEQ), lambda d, i: (0, d)),
                  pl.BlockSpec((None, None, rt, 1), lambda d, i: (l, d, i, 0)),
                  pl.BlockSpec((1, SEQ), lambda d, i: (0, d))],
        out_specs=pl.BlockSpec((None, rt, SEQ), lambda d, i: (d, i, 0)),
        out_shape=jax.ShapeDtypeStruct((2, BRANCH, SEQ), F32),
        compiler_params=_cparams(("parallel", "parallel")),
        name="hyena_filter",
    )(jnp.swapaxes(w3, 1, 2), h2, decay[:, :, :, None], t_row)
    filt = filt.reshape(2, BRANCH, n1, r)

    scratch = [pltpu.VMEM((ct * r, r), BF16), pltpu.VMEM((ct * r, 2 * r), F32), pltpu.VMEM((ct * r, 2 * r), BF16)]
    spec = pl.pallas_call(
        functools.partial(_hy_spec_kernel, ct=ct),
        grid=(BRANCH // ct,),
        in_specs=[pl.BlockSpec((None, ct, n1, r), lambda i: (0, i, 0, 0)),
                  pl.BlockSpec((None, ct, n1, r), lambda i: (1, i, 0, 0)),
                  full(f1), full(m3), full(twr), full(twi)],
        out_specs=pl.BlockSpec((ct * r, 2 * r), lambda i: (i, 0)),
        out_shape=jax.ShapeDtypeStruct((BRANCH * r, 2 * r), F32),
        scratch_shapes=scratch,
        compiler_params=_cparams(("parallel",)),
        name="hyena_filter_spectrum",
    )(filt, filt, f1, m3, twr, twi)

    hp3 = hp_t.reshape(3 * BRANCH, n1, r)
    nb = BRANCH // ct
    cw = conv_w.reshape(DEPTH, 3, 3, BRANCH).transpose(0, 2, 1, 3).reshape(DEPTH, 9, BRANCH, 1, 1)
    cb = conv_b.reshape(DEPTH, 3, BRANCH, 1, 1)
    consts = (f1, m3, m4, m6, twr, twi)
    return pl.pallas_call(
        functools.partial(_hy_conv_kernel, ct=ct),
        grid=(nb,),
        in_specs=[pl.BlockSpec((ct, n1, r), lambda i: (i, 0, 0)),
                  pl.BlockSpec((ct, n1, r), lambda i: (nb + i, 0, 0)),
                  pl.BlockSpec((ct, n1, r), lambda i: (2 * nb + i, 0, 0)),
                  pl.BlockSpec((None, 9, ct, 1, 1), lambda i: (l, 0, i, 0, 0)),
                  pl.BlockSpec((None, 3, ct, 1, 1), lambda i: (l, 0, i, 0, 0)),
                  pl.BlockSpec((None, ct, 1, 1), lambda i: (l, i, 0, 0)),
                  pl.BlockSpec((ct * r, 2 * r), lambda i: (i, 0)),
                  *[full(a) for a in consts]],
        out_specs=pl.BlockSpec((ct, n1, r), lambda i: (i, 0, 0)),
        out_shape=jax.ShapeDtypeStruct((BRANCH, n1, r), F32),
        scratch_shapes=[pltpu.VMEM((ct, n1, r), F32)] + scratch,
        compiler_params=_cparams(("parallel",)),
        name="hyena_conv",
    )(hp3, hp3, hp3, cw, cb, skip.reshape(DEPTH, BRANCH, 1, 1), spec, *consts).reshape(BRANCH, SEQ)


def _s5_chunk_matrices(lam_re, lam_im, log_step, b_re, b_im, c_re, c_im):
    t = S5_T
    lam = lax.complex(lam_re, lam_im)
    step = jnp.exp(log_step)[..., None]
    lam_dt = lam * step
    b_bar = ((jnp.exp(lam_dt) - 1.0) / lam)[..., None] * lax.complex(b_re, b_im)
    c_mat = lax.complex(c_re, c_im)
    pw = jnp.exp(lam_dt[None] * jnp.arange(t + 1, dtype=F32)[:, None, None, None])
    kern = jnp.einsum("dgvp,mdgp,dgpn->mdgvn", c_mat, pw[:t], b_bar, precision="highest").real
    lag = np.arange(t)[:, None] - np.arange(t)[None, :]
    pick = np.stack([lag[:, :, None] == np.arange(t), -lag[:, :, None] == np.arange(t)], axis=-1)
    kfb = jnp.einsum("ajmd,mdgvn->ajgvn", jnp.asarray(pick, F32), kern, precision="highest")
    w_intra = jnp.transpose(kfb, (2, 1, 4, 0, 3)).reshape(S5_GROUPS, t * S5_GROUP, t * S5_GROUP)
    in_f = pw[:t, 0][::-1][..., None] * b_bar[0][None]
    in_b = pw[:t, 1][..., None] * b_bar[1][None]

    def in_mat(m):
        m = jnp.transpose(m, (1, 0, 3, 2)).reshape(S5_GROUPS, t * S5_GROUP, S5_STATE)
        return jnp.concatenate([m.real, m.imag], axis=-1)

    w_in = jnp.concatenate([in_mat(in_f), in_mat(in_b)], axis=-1)
    out_f = c_mat[0][None] * pw[1:, 0][:, :, None, :]
    out_b = c_mat[1][None] * pw[1:, 1][::-1][:, :, None, :]

    def out_mat(m):
        m = jnp.transpose(m, (1, 3, 0, 2)).reshape(S5_GROUPS, S5_STATE, t * S5_GROUP)
        return jnp.concatenate([m.real, -m.imag], axis=1)

    w_out = jnp.concatenate([out_mat(out_f), out_mat(out_b)], axis=1)
    a_pow = jnp.exp(lam_dt[None] * (t * 2.0 ** jnp.arange(S5_LOG_CHUNKS, dtype=F32))[:, None, None, None])
    a_pow = jnp.transpose(a_pow, (2, 1, 0, 3))
    a_re = jnp.concatenate([a_pow.real, a_pow.real], axis=-1)
    a_im = jnp.concatenate([-a_pow.imag, a_pow.imag], axis=-1)
    return w_intra, w_in, w_out, a_re, a_im


S5_TILE_GROUPS = 128 // S5_GROUP


def _s5_group(u, wi, win, wout, a_re, a_im, skip_row):
    nc = S5_CHUNKS
    half = S5_STATE
    ub = u.astype(BF16)
    y = jnp.dot(ub, wi.astype(BF16), preferred_element_type=F32) + skip_row * u
    s_in = jnp.dot(ub, win.astype(BF16), preferred_element_type=F32)
    row = lax.broadcasted_iota(jnp.int32, (nc, 2 * half), 0)

    def scan(s, d):
        for step in range(S5_LOG_CHUNKS):
            sh = 1 << step
            if d == 0:
                moved = jnp.where(row >= sh, pltpu.roll(s, sh, 0), 0.0)
            else:
                moved = jnp.where(row < nc - sh, pltpu.roll(s, nc - sh, 0), 0.0)
            swapped = pltpu.roll(moved, half, 1)
            s = s + a_re[d, step:step + 1, :] * moved + a_im[d, step:step + 1, :] * swapped
        if d == 0:
            return jnp.where(row >= 1, pltpu.roll(s, 1, 0), 0.0)
        return jnp.where(row < nc - 1, pltpu.roll(s, nc - 1, 0), 0.0)

    e = jnp.concatenate([scan(s_in[:, :2 * half], 0), scan(s_in[:, 2 * half:], 1)], axis=1)
    return y + jnp.dot(e.astype(BF16), wout.astype(BF16), preferred_element_type=F32)


def _s5_kernel(u_ref, wi_ref, win_ref, wout_ref, are_ref, aim_ref, skip_ref, y_ref, y_sc):
    t, gs, nc = S5_T, S5_GROUP, S5_CHUNKS
    for g in range(S5_TILE_GROUPS):
        u_g = jnp.concatenate([u_ref[pl.ds(tau, nc, stride=t), :][:, g * gs:(g + 1) * gs] for tau in range(t)], axis=1)
        y_sc[g] = _s5_group(u_g, wi_ref[g], win_ref[g], wout_ref[g], are_ref[g], aim_ref[g], skip_ref[g])
    for tau in range(t):
        y_ref[pl.ds(tau, nc, stride=t), :] = jnp.concatenate(
            [y_sc[g, :, tau * gs:(tau + 1) * gs] for g in range(S5_TILE_GROUPS)], axis=1)


def _s5(rest, mats, skip, l):
    w_intra, w_in, w_out, a_re, a_im = mats
    t = S5_T
    gw = t * S5_GROUP
    tg = S5_TILE_GROUPS
    lane = tg * S5_GROUP
    skip_t = jnp.tile(skip.reshape(DEPTH, S5_GROUPS, 1, S5_GROUP), (1, 1, t, 1)).reshape(DEPTH, S5_GROUPS, 1, gw)
    return pl.pallas_call(
        _s5_kernel,
        grid=(BRANCH // lane,),
        in_specs=[pl.BlockSpec((SEQ, lane), lambda j: (0, j)),
                  pl.BlockSpec((None, tg, gw, gw), lambda j: (l, j, 0, 0)),
                  pl.BlockSpec((None, tg, gw, 4 * S5_STATE), lambda j: (l, j, 0, 0)),
                  pl.BlockSpec((None, tg, 4 * S5_STATE, gw), lambda j: (l, j, 0, 0)),
                  pl.BlockSpec((None, tg, 2, S5_LOG_CHUNKS, 2 * S5_STATE), lambda j: (l, j, 0, 0, 0)),
                  pl.BlockSpec((None, tg, 2, S5_LOG_CHUNKS, 2 * S5_STATE), lambda j: (l, j, 0, 0, 0)),
                  pl.BlockSpec((None, tg, 1, gw), lambda j: (l, j, 0, 0))],
        out_specs=pl.BlockSpec((SEQ, lane), lambda j: (0, j)),
        out_shape=jax.ShapeDtypeStruct((SEQ, BRANCH), F32),
        scratch_shapes=[pltpu.VMEM((tg, S5_CHUNKS, gw), F32)],
        compiler_params=_cparams(("parallel",)),
        name="s5_scan",
    )(rest, w_intra, w_in, w_out, a_re, a_im, skip_t)


def _merge_kernel(hm_ref, hyt_ref, s5_ref, wa_ref, wb_ref, wga_ref, wgg_ref, g0_ref, g1_ref, g2_ref,
                  out_ref, hy_sc):
    @pl.when(pl.program_id(1) == 0)
    def _():
        hy_sc[...] = hyt_ref[...].T.astype(BF16)

    def proj(a, w_ref):
        return jnp.dot(a, w_ref[...].astype(BF16), preferred_element_type=F32)

    s5b = s5_ref[...].astype(BF16)
    out_a = proj(hm_ref[...], wa_ref)
    out_b = proj(hy_sc[...], wb_ref)
    out_c = proj(s5b, wga_ref) * jax.nn.sigmoid(proj(s5b, wgg_ref))
    merged = (jax.nn.sigmoid(g0_ref[...]) * out_a + jax.nn.sigmoid(g1_ref[...]) * out_b
              + jax.nn.sigmoid(g2_ref[...]) * out_c)
    out_ref[...] = merged.astype(out_ref.dtype)


def _merge(hm, hy_t, s5y, rest, w_a, w_b, w_glu, l, *, tm=512, tn=512):
    nj = D_MODEL // tn
    goff = BRANCH // tn
    return pl.pallas_call(
        _merge_kernel,
        grid=(SEQ // tm, nj),
        in_specs=[pl.BlockSpec((tm, BRANCH), lambda i, j: (i, 0)),
                  pl.BlockSpec((BRANCH, tm), lambda i, j: (0, i)),
                  pl.BlockSpec((tm, BRANCH), lambda i, j: (i, 0)),
                  pl.BlockSpec((None, BRANCH, tn), lambda i, j: (l, 0, j)),
                  pl.BlockSpec((None, BRANCH, tn), lambda i, j: (l, 0, j)),
                  pl.BlockSpec((None, BRANCH, tn), lambda i, j: (l, 0, j)),
                  pl.BlockSpec((None, BRANCH, tn), lambda i, j: (l, 0, nj + j)),
                  pl.BlockSpec((tm, tn), lambda i, j: (i, goff + j)),
                  pl.BlockSpec((tm, tn), lambda i, j: (i, goff + nj + j)),
                  pl.BlockSpec((tm, tn), lambda i, j: (i, goff + 2 * nj + j))],
        out_specs=pl.BlockSpec((tm, tn), lambda i, j: (i, j)),
        out_shape=jax.ShapeDtypeStruct((SEQ, D_MODEL), BF16),
        scratch_shapes=[pltpu.VMEM((tm, BRANCH), BF16)],
        compiler_params=_cparams(("parallel", "arbitrary")),
        name="branch_merge",
    )(hm, hy_t, s5y, w_a, w_b, w_glu, w_glu, rest, rest, rest)


def _layer_norm_rows(y, g, b):
    mu = jnp.mean(y, axis=1, keepdims=True)
    var = jnp.mean(jnp.square(y - mu), axis=1, keepdims=True)
    return (y - mu) * lax.rsqrt(var + LN_EPS) * g + b


def _res_ln_kernel(x_ref, f_ref, g_ref, b_ref, o_ref, ob_ref, *, transposed):
    f = f_ref[...]
    if transposed:
        f = f.T
    y = _layer_norm_rows(ALPHA * x_ref[...] + f, g_ref[...], b_ref[...])
    o_ref[...] = y
    ob_ref[...] = y.astype(BF16)


def _res_ln(x, f, g, b, l, *, transposed, tm=256):
    f_spec = (pl.BlockSpec((D_MODEL, tm), lambda i: (0, i)) if transposed
              else pl.BlockSpec((tm, D_MODEL), lambda i: (i, 0)))
    n_tok = x.shape[0]
    return pl.pallas_call(
        functools.partial(_res_ln_kernel, transposed=transposed),
        grid=(n_tok // tm,),
        in_specs=[pl.BlockSpec((tm, D_MODEL), lambda i: (i, 0)), f_spec,
                  pl.BlockSpec((None, 1, D_MODEL), lambda i: (l, 0, 0)),
                  pl.BlockSpec((None, 1, D_MODEL), lambda i: (l, 0, 0))],
        out_specs=[pl.BlockSpec((tm, D_MODEL), lambda i: (i, 0)),
                   pl.BlockSpec((tm, D_MODEL), lambda i: (i, 0))],
        out_shape=[jax.ShapeDtypeStruct((n_tok, D_MODEL), F32), jax.ShapeDtypeStruct((n_tok, D_MODEL), BF16)],
        compiler_params=_cparams(("parallel",)),
        name="residual_layernorm",
    )(x, f, g.reshape(DEPTH, 1, D_MODEL), b.reshape(DEPTH, 1, D_MODEL))


PEER_NO_RANK = 64.0


def _extract_sorted(s, n, out_ref, base, want_rank=False):
    rank = jnp.full_like(s, PEER_NO_RANK) if want_rank else None
    for r in range(n):
        m = jnp.max(s, axis=0, keepdims=True)
        out_ref[base + r:base + r + 1, :] = m
        hit = s == m
        if want_rank:
            rank = jnp.where(hit, float(r), rank)
        s = jnp.where(hit, -jnp.inf, s)
    return rank


def _peer_select_kernel(q_ref, keys_ref, cnt_ref, r2_ref, e1_ref, e2_ref, top_sc, kth_sc):
    k = PEER_TOPK
    qb = q_ref[...].astype(BF16)
    half = PEER_KEYS
    for h in range(PEER_HEADS):
        sc = []
        rank2 = None
        for c in range(2):
            qs = qb[:, (2 * h + c) * half:(2 * h + c + 1) * half]
            s = lax.dot_general(keys_ref[h, c].astype(BF16), qs, (((1,), (1,)), ((), ())),
                                preferred_element_type=F32)
            sc.append(s)
            rank2 = _extract_sorted(s, k, top_sc, c * k, want_rank=(c == 1))
        a = top_sc[0:k, :]
        b = top_sc[k:2 * k, :]
        sub = lax.broadcasted_iota(jnp.int32, (8, a.shape[1]), 0)
        groups = [a[0:1] + b[0:8], a[0:1] + b[8:16], a[1:2] + b[0:8]]
        for i in range(2, 8):
            groups.append(jnp.where(sub < k // (i + 1), a[i:i + 1] + b[0:8], -jnp.inf))
        groups.append(a[8:16] + b[0:1])
        cand = jnp.concatenate(groups, axis=0)
        _extract_sorted(cand, k, kth_sc, 0)
        tau = kth_sc[k - 1:k, :]
        top = a[0:1] + b[0:1]
        z = jnp.sum(jnp.where(cand >= tau, jnp.exp(cand - top), 0.0), axis=0, keepdims=True)
        cnt = jnp.zeros_like(sc[0])
        for j in range(k):
            bj = top_sc[k + j:k + j + 1, :]
            cnt = jnp.where(sc[0] + bj >= tau, j + 1.0, cnt)
        cnt_ref[h] = cnt
        r2_ref[h] = rank2.astype(BF16)
        e1_ref[h] = jnp.exp(sc[0] - a[0:1])
        e2_ref[h] = (jnp.exp(sc[1] - b[0:1]) / z).astype(BF16)


def _peer_select(q, keys, l, *, tt=128):
    n_tok = q.shape[0]
    big = jax.ShapeDtypeStruct((PEER_HEADS, PEER_KEYS, n_tok), F32)
    big_bf = jax.ShapeDtypeStruct((PEER_HEADS, PEER_KEYS, n_tok), BF16)
    big_spec = pl.BlockSpec((PEER_HEADS, PEER_KEYS, tt), lambda i: (0, 0, i))
    return pl.pallas_call(
        _peer_select_kernel,
        grid=(n_tok // tt,),
        in_specs=[pl.BlockSpec((tt, D_MODEL), lambda i: (i, 0)),
                  pl.BlockSpec((None, PEER_HEADS, 2, PEER_KEYS, PEER_KEYS), lambda i: (l, 0, 0, 0, 0))],
        out_specs=[big_spec, big_spec, big_spec, big_spec],
        out_shape=[big, big_bf, big, big_bf],
        scratch_shapes=[pltpu.VMEM((2 * PEER_TOPK, tt), F32), pltpu.VMEM((PEER_TOPK, tt), F32)],
        compiler_params=_cparams(("parallel",)),
        name="peer_select",
    )(q, keys)


def _peer_dense_kernel(x_ref, u_ref, vt_ref, cnt_ref, r2_ref, e1_ref, e2_ref, out_ref, act_sc, g_sc, *, rows):
    e = pl.program_id(1)

    @pl.when(e == 0)
    def _():
        out_ref[...] = jnp.zeros_like(out_ref)

    act_sc[...] = lax.dot_general(u_ref[...], x_ref[...], (((1,), (1,)), ((), ())),
                                  preferred_element_type=F32)
    kk = PEER_KEYS
    for tc in range(act_sc.shape[1] // kk):
        lanes = slice(tc * kk, (tc + 1) * kk)
        for i in range(rows):
            w = jnp.zeros((kk, kk), BF16)
            for h in range(PEER_HEADS):
                hit = r2_ref[h, :, lanes] < cnt_ref[h, i:i + 1, lanes].astype(BF16)
                val = e2_ref[h, :, lanes] * e1_ref[h, i:i + 1, lanes].astype(BF16)
                w = w + jnp.where(hit, val, jnp.zeros_like(val))
            g_sc[i * kk:(i + 1) * kk, lanes] = jax.nn.gelu(act_sc[i * kk:(i + 1) * kk, lanes].astype(BF16)) * w
    out_ref[...] += jnp.dot(vt_ref[...], g_sc[...], preferred_element_type=F32)


def _transpose_tile_kernel(v_ref, o_ref):
    o_ref[...] = v_ref[...].T.astype(o_ref.dtype)


def _peer_value_tiles(v, l, te):
    return pl.pallas_call(
        _transpose_tile_kernel,
        grid=(N_EXPERTS // te,),
        in_specs=[pl.BlockSpec((None, te, D_MODEL), lambda e: (l, e, 0))],
        out_specs=pl.BlockSpec((None, D_MODEL, te), lambda e: (e, 0, 0)),
        out_shape=jax.ShapeDtypeStruct((N_EXPERTS // te, D_MODEL, te), BF16),
        compiler_params=_cparams(("parallel",)),
        name="peer_value_tiles",
    )(v)


def _peer_dense(x_bf, u_bf, v, l, sel, *, tt=512, rows=8):
    cnt, r2, e1, e2 = sel
    te = rows * PEER_KEYS
    n_tok = x_bf.shape[0]
    vt_bf = _peer_value_tiles(v, l, te)
    row_spec = pl.BlockSpec((PEER_HEADS, rows, tt), lambda i, e: (0, e, i))
    all_spec = pl.BlockSpec((PEER_HEADS, PEER_KEYS, tt), lambda i, e: (0, 0, i))
    return pl.pallas_call(
        functools.partial(_peer_dense_kernel, rows=rows),
        grid=(n_tok // tt, N_EXPERTS // te),
        in_specs=[pl.BlockSpec((tt, D_MODEL), lambda i, e: (i, 0)),
                  pl.BlockSpec((te, D_MODEL), lambda i, e: (e, 0)),
                  pl.BlockSpec((None, D_MODEL, te), lambda i, e: (e, 0, 0)),
                  row_spec, all_spec, row_spec, all_spec],
        out_specs=pl.BlockSpec((D_MODEL, tt), lambda i, e: (0, i)),
        out_shape=jax.ShapeDtypeStruct((D_MODEL, n_tok), F32),
        scratch_shapes=[pltpu.VMEM((te, tt), F32), pltpu.VMEM((te, tt), BF16)],
        compiler_params=_cparams(("parallel", "arbitrary")),
        name="peer_dense",
    )(x_bf, u_bf, vt_bf, cnt, r2, e1, e2)


def _hyena_features():
    pos = jnp.arange(SEQ, dtype=F32)
    t = pos / (SEQ - 1)
    bands = jnp.linspace(1e-4, 16 - 1, 16, dtype=F32)
    ang = 2.0 * math.pi * pos[:, None] * bands[None, :] / SEQ
    feat = jnp.concatenate([t[:, None], jnp.cos(ang), -jnp.sin(ang)], axis=-1)
    feat_t = jnp.pad(feat.T, ((0, HY_EMB_PAD - feat.shape[1]), (0, 0)))
    t_row = t[None, :]

    def reversed_time(a):
        return jnp.roll(a[:, ::-1], 1, axis=1)

    return (jnp.concatenate([feat_t, reversed_time(feat_t)], axis=1),
            jnp.concatenate([t_row, reversed_time(t_row)], axis=1))


def kernel(x, w_in, mlstm_gate_bias, mlstm_norm_gain, w_mlstm_out, hyena_conv_w, hyena_conv_b, hyena_w1,
           hyena_b1, hyena_w2, hyena_b2, hyena_freq, hyena_w3, hyena_decay, hyena_skip, w_hyena_out,
           s5_lambda_re, s5_lambda_im, s5_log_step, s5_b_re, s5_b_im, s5_c_re, s5_c_im, s5_skip, w_s5_glu,
           w_out, ln1_g, ln1_b, peer_w_q, peer_subkeys, peer_u, peer_v, ln2_g, ln2_b):
    xf = x.reshape(SEQ, D_MODEL)
    xb = xf.astype(BF16)
    feat_t, t_row = _hyena_features()
    w_in_t = jnp.swapaxes(w_in, 1, 2)
    w_tail = _w_in_tail(w_in_t)
    s5_mats = jax.vmap(_s5_chunk_matrices)(s5_lambda_re, s5_lambda_im, s5_log_step, s5_b_re, s5_b_im,
                                           s5_c_re, s5_c_im)
    for l in range(DEPTH):
        qkvo = _mm_wt(xb, w_in_t, l, 0, 4 * BRANCH, tm=1024, tn=512, name="proj_qkvo")
        gates_t = _mm_nt(w_in_t, l, OFF_GATES, N_GATES, xb, tr=N_GATES, tm=1024, name="proj_gates", out_dtype=F32)
        hp_t = _mm_nt(w_tail, l, 0, 3 * BRANCH, xb, tr=512, tm=1024, name="proj_hyena_t", out_dtype=BF16)
        rest = _mm_wt(xb, w_tail, l, 3 * BRANCH, BRANCH + 3 * D_MODEL, tm=1024, tn=512, name="proj_s5_mixgates")

        hdir = _mlstm(qkvo, gates_t, mlstm_gate_bias, l)
        hm = _mlstm_post(hdir, qkvo, mlstm_norm_gain, l)
        hy_t = _hyena(hp_t, l, hyena_conv_w, hyena_conv_b, hyena_w1, hyena_b1, hyena_w2, hyena_b2,
                      hyena_freq, hyena_w3, hyena_decay, hyena_skip, feat_t, t_row)
        s5y = _s5(rest, s5_mats, s5_skip, l)
        merged = _merge(hm, hy_t, s5y, rest, w_mlstm_out, w_hyena_out, w_s5_glu, l)
        mix = _mm(merged, w_out, l, 0, D_MODEL, tm=1024, tn=512, name="proj_out")
        xf, xb = _res_ln(xf, mix, ln1_g, ln1_b, l, transposed=False)

        q = _mm(xb, peer_w_q, l, 0, PEER_HEADS * 2 * PEER_KEYS, tm=1024, tn=512, name="peer_query")
        sel = _peer_select(q, peer_subkeys, l)
        ffn_t = _peer_dense(xb, peer_u[l].astype(BF16), peer_v, l, sel)
        xf, xb = _res_ln(xf, ffn_t, ln2_g, ln2_b, l, transposed=True)
    return xf.reshape(1, SEQ, D_MODEL)
```

```python
import functools
import math

import numpy as np
import jax
import jax.numpy as jnp
from jax import lax
from jax.experimental import pallas as pl
from jax.experimental.pallas import tpu as pltpu

F32 = jnp.float32
BF16 = jnp.bfloat16

D_MODEL = 2048
SEQ = 8192
DEPTH = 4
BRANCH = 1024
HEADS = 4
HEAD_DIM = 256
CHUNK = 256
N_GATES = 16
HY_HID = 64
HY_EMB_PAD = 128
S5_GROUP = 16
S5_GROUPS = 64
S5_STATE = 64
S5_T = 16
S5_CHUNKS = SEQ // S5_T
S5_LOG_CHUNKS = 9
PEER_HEADS = 8
PEER_KEYS = 128
PEER_TOPK = 16
N_EXPERTS = PEER_KEYS * PEER_KEYS
ALPHA = (2 * DEPTH) ** 0.25
LN_EPS = 1e-5
FFT_N = 2 * SEQ
FFT_R = 128
OFF_GATES = 4 * BRANCH
OFF_HYENA = OFF_GATES + N_GATES
OFF_S5 = OFF_HYENA + 3 * BRANCH
OFF_MIXG = OFF_S5 + BRANCH

VMEM_LIMIT = 56 * 1024 * 1024


def _cparams(sem):
    return pltpu.CompilerParams(dimension_semantics=sem, vmem_limit_bytes=VMEM_LIMIT)


def _split_bf16(a):
    hi = a.astype(BF16)
    lo = (a - hi.astype(F32)).astype(BF16)
    return hi, lo


def _dot3(a, b_hi, b_lo):
    a_hi, a_lo = _split_bf16(a)
    acc = jnp.dot(a_hi, b_hi, preferred_element_type=F32)
    acc += jnp.dot(a_lo, b_hi, preferred_element_type=F32)
    acc += jnp.dot(a_hi, b_lo, preferred_element_type=F32)
    return acc


def _dot3_both(a, b):
    b_hi, b_lo = _split_bf16(b)
    return _dot3(a, b_hi, b_lo)


def _mm_kernel(a_ref, b_ref, o_ref):
    o_ref[...] = jnp.dot(a_ref[...].astype(BF16), b_ref[...].astype(BF16),
                         preferred_element_type=F32).astype(o_ref.dtype)


def _mm(a, w, l, col_off, n_cols, *, tm, tn, name):
    m, k = a.shape
    cb = col_off // tn
    assert col_off % tn == 0 and n_cols % tn == 0 and m % tm == 0
    return pl.pallas_call(
        _mm_kernel,
        grid=(m // tm, n_cols // tn),
        in_specs=[pl.BlockSpec((tm, k), lambda i, j: (i, 0)),
                  pl.BlockSpec((None, k, tn), lambda i, j: (l, 0, cb + j))],
        out_specs=pl.BlockSpec((tm, tn), lambda i, j: (i, j)),
        out_shape=jax.ShapeDtypeStruct((m, n_cols), F32),
        compiler_params=_cparams(("parallel", "parallel")),
        name=name,
    )(a, w)


def _mm_wt_kernel(a_ref, w_ref, o_ref):
    o_ref[...] = lax.dot_general(a_ref[...].astype(BF16), w_ref[...].astype(BF16), (((1,), (1,)), ((), ())),
                                 preferred_element_type=F32).astype(o_ref.dtype)


def _mm_wt(a, wt, l, row_off, n_rows, *, tm, tn, name):
    m, k = a.shape
    rb = row_off // tn
    assert row_off % tn == 0 and n_rows % tn == 0 and m % tm == 0
    return pl.pallas_call(
        _mm_wt_kernel,
        grid=(m // tm, n_rows // tn),
        in_specs=[pl.BlockSpec((tm, k), lambda i, j: (i, 0)),
                  pl.BlockSpec((None, tn, k), lambda i, j: (l, rb + j, 0))],
        out_specs=pl.BlockSpec((tm, tn), lambda i, j: (i, j)),
        out_shape=jax.ShapeDtypeStruct((m, n_rows), F32),
        compiler_params=_cparams(("parallel", "parallel")),
        name=name,
    )(a, wt)


def _mm_nt_kernel(w_ref, a_ref, o_ref):
    o_ref[...] = lax.dot_general(w_ref[...].astype(BF16), a_ref[...].astype(BF16), (((1,), (1,)), ((), ())),
                                 preferred_element_type=F32).astype(o_ref.dtype)


def _mm_nt(wt, l, row_off, n_rows, a, *, tr, tm, name, out_dtype):
    k = wt.shape[2]
    m = a.shape[0]
    rb = row_off // tr
    assert row_off % tr == 0 and n_rows % tr == 0
    return pl.pallas_call(
        _mm_nt_kernel,
        grid=(n_rows // tr, m // tm),
        in_specs=[pl.BlockSpec((None, tr, k), lambda i, j: (l, rb + i, 0)),
                  pl.BlockSpec((tm, k), lambda i, j: (j, 0))],
        out_specs=pl.BlockSpec((tr, tm), lambda i, j: (i, j)),
        out_shape=jax.ShapeDtypeStruct((n_rows, m), out_dtype),
        compiler_params=_cparams(("parallel", "parallel")),
        name=name,
    )(wt, a)


def _w_tail_kernel(cur_ref, nxt_ref, o_ref, *, shift):
    o_ref[...] = jnp.concatenate([cur_ref[shift:, :], nxt_ref[...]], axis=0).astype(o_ref.dtype)


def _w_in_tail(w_in_t, *, tn=512):
    depth, n_all, k = w_in_t.shape
    n_out = n_all - OFF_HYENA
    return pl.pallas_call(
        functools.partial(_w_tail_kernel, shift=N_GATES),
        grid=(depth, n_out // tn),
        in_specs=[pl.BlockSpec((None, tn, k), lambda l, j: (l, OFF_GATES // tn + j, 0)),
                  pl.BlockSpec((None, N_GATES, k), lambda l, j: (l, (OFF_GATES + (j + 1) * tn) // N_GATES, 0))],
        out_specs=pl.BlockSpec((None, tn, k), lambda l, j: (l, j, 0)),
        out_shape=jax.ShapeDtypeStruct((depth, n_out, k), BF16),
        compiler_params=_cparams(("parallel", "parallel")),
        name="w_in_tail",
    )(w_in_t, w_in_t)


def _log_sigmoid(x):
    return jnp.minimum(x, 0.0) - jnp.log1p(jnp.exp(-jnp.abs(x)))


def _mlstm_chunk(d, q, k, v, li_r, lf_r, li_c, lf_c, c_sc, n_sc, m_sc):
    n = CHUNK
    row = lax.broadcasted_iota(jnp.int32, (n, n), 0)
    col = lax.broadcasted_iota(jnp.int32, (n, n), 1)
    valid = (row >= col) if d == 0 else (row <= col)
    valid_t = (col >= row) if d == 0 else (col <= row)
    b_col = jnp.sum(jnp.where(valid, lf_r, 0.0), axis=1, keepdims=True)
    b_row = jnp.sum(jnp.where(valid_t, lf_c, 0.0), axis=0, keepdims=True)
    total = jnp.sum(lf_r, axis=1, keepdims=True)

    m_prev = m_sc[...]
    m_inter = b_col + m_prev
    log_d = jnp.where(valid, b_col - b_row + li_r, -jnp.inf)
    m_t = jnp.maximum(m_inter, jnp.max(log_d, axis=1, keepdims=True))
    inter = jnp.exp(m_inter - m_t)
    dmat = jnp.exp(log_d - m_t)

    k = k * (HEAD_DIM ** -0.5)
    qb = q.astype(BF16)
    kb = k.astype(BF16)
    vb = v.astype(BF16)
    s = lax.dot_general(qb, kb, (((1,), (1,)), ((), ())), preferred_element_type=F32) * dmat
    num = jnp.dot(s.astype(BF16), vb, preferred_element_type=F32)
    num += inter * jnp.dot(qb, c_sc[...].astype(BF16), preferred_element_type=F32)
    qn = jnp.sum(q * n_sc[...], axis=1, keepdims=True)
    den = jnp.sum(s, axis=1, keepdims=True) + inter * qn
    h_out = num / jnp.maximum(jnp.abs(den), jnp.exp(-m_t))

    log_w = total - b_col + li_c
    m_new = jnp.maximum(total + m_prev, jnp.max(log_w, axis=0, keepdims=True))
    w = jnp.exp(log_w - m_new)
    decay = jnp.exp(total + m_prev - m_new)
    kw = k * w
    c_sc[...] = decay * c_sc[...] + lax.dot_general(
        kw.astype(BF16), vb, (((0,), (0,)), ((), ())), preferred_element_type=F32)
    n_sc[...] = decay * n_sc[...] + jnp.sum(kw, axis=0, keepdims=True)
    m_sc[...] = m_new
    return h_out


def _mlstm_kernel(qkv_f_ref, qkv_b_ref, gr_f_ref, gr_b_ref, gc_f_ref, gc_b_ref, bias_ref,
                  o_f_ref, o_b_ref, c_sc, n_sc, m_sc):
    @pl.when(pl.program_id(0) == 0)
    def _():
        c_sc[...] = jnp.zeros_like(c_sc)
        n_sc[...] = jnp.zeros_like(n_sc)
        m_sc[...] = jnp.zeros_like(m_sc)

    for d, (qkv_ref, gr_ref, gc_ref, o_ref) in enumerate(((qkv_f_ref, gr_f_ref, gc_f_ref, o_f_ref),
                                                         (qkv_b_ref, gr_b_ref, gc_b_ref, o_b_ref))):
        for h in range(HEADS):
            cols = slice(h * HEAD_DIM, (h + 1) * HEAD_DIM)
            b_i = bias_ref[d * 2 * HEADS + h]
            b_f = bias_ref[d * 2 * HEADS + HEADS + h]
            chain = d * HEADS + h
            o_ref[:, cols] = _mlstm_chunk(
                d,
                qkv_ref[:, cols],
                qkv_ref[:, BRANCH + h * HEAD_DIM:BRANCH + (h + 1) * HEAD_DIM],
                qkv_ref[:, 2 * BRANCH + h * HEAD_DIM:2 * BRANCH + (h + 1) * HEAD_DIM],
                gr_ref[h] + b_i, _log_sigmoid(gr_ref[HEADS + h] + b_f),
                gc_ref[h] + b_i, _log_sigmoid(gc_ref[HEADS + h] + b_f),
                c_sc.at[chain], n_sc.at[chain], m_sc.at[chain])


def _mlstm(qkvo, gates_t, gate_bias, l):
    nc = SEQ // CHUNK
    g_rows = gates_t.reshape(N_GATES, 1, SEQ)
    g_cols = gates_t.reshape(N_GATES, SEQ, 1)
    bias = gate_bias.reshape(DEPTH, N_GATES, 1, 1)
    half = N_GATES // 2
    chains = 2 * HEADS
    out = jax.ShapeDtypeStruct((SEQ, BRANCH), F32)
    return pl.pallas_call(
        _mlstm_kernel,
        grid=(nc,),
        in_specs=[
            pl.BlockSpec((CHUNK, 4 * BRANCH), lambda c: (c, 0)),
            pl.BlockSpec((CHUNK, 4 * BRANCH), lambda c: (nc - 1 - c, 0)),
            pl.BlockSpec((half, 1, CHUNK), lambda c: (0, 0, c)),
            pl.BlockSpec((half, 1, CHUNK), lambda c: (1, 0, nc - 1 - c)),
            pl.BlockSpec((half, CHUNK, 1), lambda c: (0, c, 0)),
            pl.BlockSpec((half, CHUNK, 1), lambda c: (1, nc - 1 - c, 0)),
            pl.BlockSpec((None, N_GATES, 1, 1), lambda c: (l, 0, 0, 0)),
        ],
        out_specs=[pl.BlockSpec((CHUNK, BRANCH), lambda c: (c, 0)),
                   pl.BlockSpec((CHUNK, BRANCH), lambda c: (nc - 1 - c, 0))],
        out_shape=[out, out],
        scratch_shapes=[pltpu.VMEM((chains, HEAD_DIM, HEAD_DIM), F32),
                        pltpu.VMEM((chains, 1, HEAD_DIM), F32),
                        pltpu.VMEM((chains, 1, 1), F32)],
        compiler_params=_cparams(("arbitrary",)),
        name="mlstm_scan",
    )(qkvo, qkvo, g_rows, g_rows, g_cols, g_cols, bias)


def _mlstm_post_kernel(hf_ref, hb_ref, o_ref, gain_ref, out_ref):
    h = hf_ref[...] + hb_ref[...]
    h = jax.nn.sigmoid(o_ref[...]) * h
    for hd in range(HEADS):
        sl = slice(hd * HEAD_DIM, (hd + 1) * HEAD_DIM)
        hh = h[:, sl]
        mu = jnp.mean(hh, axis=1, keepdims=True)
        var = jnp.mean(jnp.square(hh - mu), axis=1, keepdims=True)
        out_ref[:, sl] = ((hh - mu) * lax.rsqrt(var + LN_EPS) * gain_ref[:, sl]).astype(out_ref.dtype)


def _mlstm_post(hdir, qkvo, gain, l, *, tm=512):
    return pl.pallas_call(
        _mlstm_post_kernel,
        grid=(SEQ // tm,),
        in_specs=[pl.BlockSpec((tm, BRANCH), lambda i: (i, 0)),
                  pl.BlockSpec((tm, BRANCH), lambda i: (i, 0)),
                  pl.BlockSpec((tm, BRANCH), lambda i: (i, 3)),
                  pl.BlockSpec((None, 1, BRANCH), lambda i: (l, 0, 0))],
        out_specs=pl.BlockSpec((tm, BRANCH), lambda i: (i, 0)),
        out_shape=jax.ShapeDtypeStruct((SEQ, BRANCH), BF16),
        compiler_params=_cparams(("parallel",)),
        name="mlstm_post",
    )(hdir[0], hdir[1], qkvo, gain.reshape(DEPTH, 1, BRANCH))


HY_CT = 32
HY_RB = 512


@functools.lru_cache(maxsize=None)
def _fft_constants():
    r = FFT_R
    idx = np.arange(r)
    th = 2.0 * np.pi * np.outer(idx, idx) / r
    cs, sn = np.cos(th), np.sin(th)
    ph = 2.0 * np.pi * np.outer(idx, idx) / FFT_N
    f1 = np.concatenate([cs, -sn], axis=1)
    m3 = np.block([[cs, -sn], [sn, cs]])
    m4 = np.block([[cs, sn], [-sn, cs]])
    m6 = np.concatenate([cs, -sn], axis=0) / FFT_N
    m6[:, r // 2:] = 0.0
    consts = {name: jnp.asarray(mat, F32).astype(BF16)
              for name, mat in (("f1", f1), ("m3", m3), ("m4", m4), ("m6", m6))}
    consts["tw"] = (jnp.asarray(np.cos(ph), F32), jnp.asarray(np.sin(ph), F32))
    return consts


def _rows(i, n):
    return pl.ds(pl.multiple_of(i * n, n), n)


def _for_each(n, body, unroll=1):
    def step(i, carry):
        body(i)
        return carry

    lax.fori_loop(0, n, step, 0, unroll=unroll)


def _fwd_stage1(sig_ref, hi_ref, ct, t_buf, a_buf, b_buf, f1_ref, twr_ref, twi_ref):
    r = FFT_R
    zeros = jnp.zeros((r // 2, r), F32)

    def tr_in(c):
        hi = zeros if hi_ref is None else hi_ref[c]
        t_buf[_rows(c, r), :] = jnp.concatenate([sig_ref[c], hi], axis=0).T.astype(BF16)

    _for_each(ct, tr_in, unroll=4)

    def stage1(b):
        rows = _rows(b, HY_RB)
        a_buf[rows, :] = jnp.dot(t_buf[rows, :], f1_ref[...], preferred_element_type=F32)

    _for_each(ct * r // HY_RB, stage1)

    def twiddle(c):
        rows = _rows(c, r)
        a = a_buf[rows, :]
        ar, ai = a[:, :r], a[:, r:]
        twr, twi = twr_ref[...], twi_ref[...]
        b_buf[rows, :r] = (ar * twr + ai * twi).T.astype(BF16)
        b_buf[rows, r:] = (ai * twr - ar * twi).T.astype(BF16)

    _for_each(ct, twiddle, unroll=4)


def _hy_mlp_kernel(feat_ref, w1_ref, b1_ref, w2_ref, b2_ref, fr_ref, out_ref):
    h = _dot3_both(w1_ref[...], feat_ref[...]) + b1_ref[...]
    h = jnp.sin(fr_ref[0] * h)
    h = _dot3_both(w2_ref[...], h) + b2_ref[...]
    out_ref[...] = jnp.sin(fr_ref[1] * h)


def _hy_filter_kernel(w3_ref, h_ref, dec_ref, t_ref, out_ref):
    d = pl.program_id(0)
    filt = _dot3_both(w3_ref[...], h_ref[...])
    filt = filt * jnp.exp(-t_ref[...] * jnp.abs(dec_ref[...]))
    filt = filt / (jnp.sum(jnp.abs(filt), axis=1, keepdims=True) + 1e-6)
    lane = lax.broadcasted_iota(jnp.int32, filt.shape, 1)
    out_ref[...] = jnp.where((lane == 0) & (d == 1), 0.0, filt)


def _hy_spec_kernel(fa_ref, fb_ref, f1_ref, m3_ref, twr_ref, twi_ref, out_ref, t_buf, a_buf, b_buf, *, ct):
    r = FFT_R
    nblk = ct * r // HY_RB
    _fwd_stage1(fa_ref, fb_ref, ct, t_buf, a_buf, b_buf, f1_ref, twr_ref, twi_ref)

    def put(b):
        rows = _rows(b, HY_RB)
        out_ref[rows, :] = jnp.dot(b_buf[rows, :], m3_ref[...], preferred_element_type=F32)

    _for_each(nblk, put)


def _shift_prev(p, n1, n2):
    r1 = pltpu.roll(p, 1, 2)
    r2 = pltpu.roll(r1, 1, 1)
    return jnp.where(n2 == 0, jnp.where(n1 == 0, 0.0, r2), r1)


def _shift_next(p, n1, n2):
    s1, s2 = p.shape[1], p.shape[2]
    r1 = pltpu.roll(p, s2 - 1, 2)
    r2 = pltpu.roll(r1, s1 - 1, 1)
    return jnp.where(n2 == s2 - 1, jnp.where(n1 == s1 - 1, 0.0, r2), r1)


def _hy_conv_kernel(p0_ref, p1_ref, p2_ref, cw_ref, cb_ref, skip_ref, h_ref,
                    f1_ref, m3_ref, m4_ref, m6_ref, twr_ref, twi_ref,
                    out_ref, z_buf, t_buf, a_buf, b_buf, *, ct):
    r = FFT_R
    n1_len = SEQ // r
    nblk = ct * r // HY_RB
    shp = (ct, n1_len, r)
    n1 = lax.broadcasted_iota(jnp.int32, shp, 1)
    n2 = lax.broadcasted_iota(jnp.int32, shp, 2)

    def short_conv(p_ref, g):
        p = p_ref[...].astype(F32)
        return (_shift_prev(p, n1, n2) * cw_ref[3 * g] + p * cw_ref[3 * g + 1]
                + _shift_next(p, n1, n2) * cw_ref[3 * g + 2] + cb_ref[g])

    z_buf[...] = short_conv(p1_ref, 1) * short_conv(p2_ref, 2)
    _fwd_stage1(z_buf, None, ct, t_buf, a_buf, b_buf, f1_ref, twr_ref, twi_ref)

    def freq_domain(b):
        rows = _rows(b, HY_RB)
        x = jnp.dot(b_buf[rows, :], m3_ref[...], preferred_element_type=F32)
        h = h_ref[rows, :]
        xr, xi = x[:, :r], x[:, r:]
        hr, hi = h[:, :r], h[:, r:]
        y = jnp.concatenate([xr * hr - xi * hi, xr * hi + xi * hr], axis=1).astype(BF16)
        a_buf[rows, :] = jnp.dot(y, m4_ref[...], preferred_element_type=F32)

    _for_each(nblk, freq_domain)

    def inv_twiddle(c):
        rows = _rows(c, r)
        b = a_buf[rows, :]
        br, bi = b[:, :r], b[:, r:]
        twr, twi = twr_ref[...], twi_ref[...]
        b_buf[rows, :r] = (br * twr - bi * twi).T.astype(BF16)
        b_buf[rows, r:] = (br * twi + bi * twr).T.astype(BF16)

    _for_each(ct, inv_twiddle, unroll=4)

    def stage6(b):
        rows = _rows(b, HY_RB)
        a_buf[rows, :r] = jnp.dot(b_buf[rows, :], m6_ref[...], preferred_element_type=F32)

    _for_each(nblk, stage6)

    def finish(c):
        y_c = a_buf[_rows(c, r), :r].T[:n1_len, :]
        out_ref[c] = y_c + skip_ref[c] * z_buf[c]

    _for_each(ct, finish, unroll=4)
    out_ref[...] = short_conv(p0_ref, 0) * out_ref[...]


def _hyena(hp_t, l, conv_w, conv_b, w1, b1, w2, b2, freq, w3, decay, skip, feat_t, t_row):
    r = FFT_R
    n1 = SEQ // r
    ct = HY_CT
    k = _fft_constants()
    f1, m3, m4, m6 = k["f1"], k["m3"], k["m4"], k["m6"]
    twr, twi = k["tw"]

    def full(a):
        return pl.BlockSpec(a.shape, lambda *_: (0,) * a.ndim)

    h2 = pl.pallas_call(
        _hy_mlp_kernel,
        grid=(2,),
        in_specs=[pl.BlockSpec((feat_t.shape[0], SEQ), lambda i: (0, i)),
                  pl.BlockSpec((None, HY_HID, feat_t.shape[0]), lambda i: (l, 0, 0)),
                  pl.BlockSpec((None, HY_HID, 1), lambda i: (l, 0, 0)),
                  pl.BlockSpec((None, HY_HID, HY_HID), lambda i: (l, 0, 0)),
                  pl.BlockSpec((None, HY_HID, 1), lambda i: (l, 0, 0)),
                  pl.BlockSpec((None, 2, HY_HID, 1), lambda i: (l, 0, 0, 0))],
        out_specs=pl.BlockSpec((HY_HID, SEQ), lambda i: (0, i)),
        out_shape=jax.ShapeDtypeStruct((HY_HID, 2 * SEQ), F32),
        compiler_params=_cparams(("parallel",)),
        name="hyena_filter_mlp",
    )(feat_t, jnp.pad(jnp.swapaxes(w1, 1, 2), ((0, 0), (0, 0), (0, feat_t.shape[0] - w1.shape[1]))),
      b1[:, :, None], jnp.swapaxes(w2, 1, 2), b2[:, :, None], freq[:, :, :, None])

    rt = 128
    filt = pl.pallas_call(
        _hy_filter_kernel,
        grid=(2, BRANCH // rt),
        in_specs=[pl.BlockSpec((None, rt, HY_HID), lambda d, i: (l, d * (BRANCH // rt) + i, 0)),
                  pl.BlockSpec((HY_HID, SEQ), lambda d, i: (0, d)),
                  pl.BlockSpec((None, None, rt, 1), lambda d, i: (l, d, i, 0)),
                  pl.BlockSpec((1, SEQ), lambda d, i: (0, d))],
        out_specs=pl.BlockSpec((None, rt, SEQ), lambda d, i: (d, i, 0)),
        out_shape=jax.ShapeDtypeStruct((2, BRANCH, SEQ), F32),
        compiler_params=_cparams(("parallel", "parallel")),
        name="hyena_filter",
    )(jnp.swapaxes(w3, 1, 2), h2, decay[:, :, :, None], t_row)
    filt = filt.reshape(2, BRANCH, n1, r)

    scratch = [pltpu.VMEM((ct * r, r), BF16), pltpu.VMEM((ct * r, 2 * r), F32), pltpu.VMEM((ct * r, 2 * r), BF16)]
    spec = pl.pallas_call(
        functools.partial(_hy_spec_kernel, ct=ct),
        grid=(BRANCH // ct,),
        in_specs=[pl.BlockSpec((None, ct, n1, r), lambda i: (0, i, 0, 0)),
                  pl.BlockSpec((None, ct, n1, r), lambda i: (1, i, 0, 0)),
                  full(f1), full(m3), full(twr), full(twi)],
        out_specs=pl.BlockSpec((ct * r, 2 * r), lambda i: (i, 0)),
        out_shape=jax.ShapeDtypeStruct((BRANCH * r, 2 * r), F32),
        scratch_shapes=scratch,
        compiler_params=_cparams(("parallel",)),
        name="hyena_filter_spectrum",
    )(filt, filt, f1, m3, twr, twi)

    hp3 = hp_t.reshape(3 * BRANCH, n1, r)
    nb = BRANCH // ct
    cw = conv_w.reshape(DEPTH, 3, 3, BRANCH).transpose(0, 2, 1, 3).reshape(DEPTH, 9, BRANCH, 1, 1)
    cb = conv_b.reshape(DEPTH, 3, BRANCH, 1, 1)
    consts = (f1, m3, m4, m6, twr, twi)
    return pl.pallas_call(
        functools.partial(_hy_conv_kernel, ct=ct),
        grid=(nb,),
        in_specs=[pl.BlockSpec((ct, n1, r), lambda i: (i, 0, 0)),
                  pl.BlockSpec((ct, n1, r), lambda i: (nb + i, 0, 0)),
                  pl.BlockSpec((ct, n1, r), lambda i: (2 * nb + i, 0, 0)),
                  pl.BlockSpec((None, 9, ct, 1, 1), lambda i: (l, 0, i, 0, 0)),
                  pl.BlockSpec((None, 3, ct, 1, 1), lambda i: (l, 0, i, 0, 0)),
                  pl.BlockSpec((None, ct, 1, 1), lambda i: (l, i, 0, 0)),
                  pl.BlockSpec((ct * r, 2 * r), lambda i: (i, 0)),
                  *[full(a) for a in consts]],
        out_specs=pl.BlockSpec((ct, n1, r), lambda i: (i, 0, 0)),
        out_shape=jax.ShapeDtypeStruct((BRANCH, n1, r), F32),
        scratch_shapes=[pltpu.VMEM((ct, n1, r), F32)] + scratch,
        compiler_params=_cparams(("parallel",)),
        name="hyena_conv",
    )(hp3, hp3, hp3, cw, cb, skip.reshape(DEPTH, BRANCH, 1, 1), spec, *consts).reshape(BRANCH, SEQ)


def _s5_chunk_matrices(lam_re, lam_im, log_step, b_re, b_im, c_re, c_im):
    t = S5_T
    lam = lax.complex(lam_re, lam_im)
    step = jnp.exp(log_step)[..., None]
    lam_dt = lam * step
    b_bar = ((jnp.exp(lam_dt) - 1.0) / lam)[..., None] * lax.complex(b_re, b_im)
    c_mat = lax.complex(c_re, c_im)
    pw = jnp.exp(lam_dt[None] * jnp.arange(t + 1, dtype=F32)[:, None, None, None])
    kern = jnp.einsum("dgvp,mdgp,dgpn->mdgvn", c_mat, pw[:t], b_bar, precision="highest").real
    lag = np.arange(t)[:, None] - np.arange(t)[None, :]
    pick = np.stack([lag[:, :, None] == np.arange(t), -lag[:, :, None] == np.arange(t)], axis=-1)
    kfb = jnp.einsum("ajmd,mdgvn->ajgvn", jnp.asarray(pick, F32), kern, precision="highest")
    w_intra = jnp.transpose(kfb, (2, 1, 4, 0, 3)).reshape(S5_GROUPS, t * S5_GROUP, t * S5_GROUP)
    in_f = pw[:t, 0][::-1][..., None] * b_bar[0][None]
    in_b = pw[:t, 1][..., None] * b_bar[1][None]

    def in_mat(m):
        m = jnp.transpose(m, (1, 0, 3, 2)).reshape(S5_GROUPS, t * S5_GROUP, S5_STATE)
        return jnp.concatenate([m.real, m.imag], axis=-1)

    w_in = jnp.concatenate([in_mat(in_f), in_mat(in_b)], axis=-1)
    out_f = c_mat[0][None] * pw[1:, 0][:, :, None, :]
    out_b = c_mat[1][None] * pw[1:, 1][::-1][:, :, None, :]

    def out_mat(m):
        m = jnp.transpose(m, (1, 3, 0, 2)).reshape(S5_GROUPS, S5_STATE, t * S5_GROUP)
        return jnp.concatenate([m.real, -m.imag], axis=1)

    w_out = jnp.concatenate([out_mat(out_f), out_mat(out_b)], axis=1)
    a_pow = jnp.exp(lam_dt[None] * (t * 2.0 ** jnp.arange(S5_LOG_CHUNKS, dtype=F32))[:, None, None, None])
    a_pow = jnp.transpose(a_pow, (2, 1, 0, 3))
    a_re = jnp.concatenate([a_pow.real, a_pow.real], axis=-1)
    a_im = jnp.concatenate([-a_pow.imag, a_pow.imag], axis=-1)
    return w_intra, w_in, w_out, a_re, a_im


S5_TILE_GROUPS = 128 // S5_GROUP


def _s5_group(u, wi, win, wout, a_re, a_im, skip_row):
    nc = S5_CHUNKS
    half = S5_STATE
    ub = u.astype(BF16)
    y = jnp.dot(ub, wi.astype(BF16), preferred_element_type=F32) + skip_row * u
    s_in = jnp.dot(ub, win.astype(BF16), preferred_element_type=F32)
    row = lax.broadcasted_iota(jnp.int32, (nc, 2 * half), 0)

    def scan(s, d):
        for step in range(S5_LOG_CHUNKS):
            sh = 1 << step
            if d == 0:
                moved = jnp.where(row >= sh, pltpu.roll(s, sh, 0), 0.0)
            else:
                moved = jnp.where(row < nc - sh, pltpu.roll(s, nc - sh, 0), 0.0)
            swapped = pltpu.roll(moved, half, 1)
            s = s + a_re[d, step:step + 1, :] * moved + a_im[d, step:step + 1, :] * swapped
        if d == 0:
            return jnp.where(row >= 1, pltpu.roll(s, 1, 0), 0.0)
        return jnp.where(row < nc - 1, pltpu.roll(s, nc - 1, 0), 0.0)

    e = jnp.concatenate([scan(s_in[:, :2 * half], 0), scan(s_in[:, 2 * half:], 1)], axis=1)
    return y + jnp.dot(e.astype(BF16), wout.astype(BF16), preferred_element_type=F32)


def _s5_kernel(u_ref, wi_ref, win_ref, wout_ref, are_ref, aim_ref, skip_ref, y_ref, y_sc):
    t, gs, nc = S5_T, S5_GROUP, S5_CHUNKS
    for g in range(S5_TILE_GROUPS):
        u_g = jnp.concatenate([u_ref[pl.ds(tau, nc, stride=t), :][:, g * gs:(g + 1) * gs] for tau in range(t)], axis=1)
        y_sc[g] = _s5_group(u_g, wi_ref[g], win_ref[g], wout_ref[g], are_ref[g], aim_ref[g], skip_ref[g])
    for tau in range(t):
        y_ref[pl.ds(tau, nc, stride=t), :] = jnp.concatenate(
            [y_sc[g, :, tau * gs:(tau + 1) * gs] for g in range(S5_TILE_GROUPS)], axis=1)


def _s5(rest, mats, skip, l):
    w_intra, w_in, w_out, a_re, a_im = mats
    t = S5_T
    gw = t * S5_GROUP
    tg = S5_TILE_GROUPS
    lane = tg * S5_GROUP
    skip_t = jnp.tile(skip.reshape(DEPTH, S5_GROUPS, 1, S5_GROUP), (1, 1, t, 1)).reshape(DEPTH, S5_GROUPS, 1, gw)
    return pl.pallas_call(
        _s5_kernel,
        grid=(BRANCH // lane,),
        in_specs=[pl.BlockSpec((SEQ, lane), lambda j: (0, j)),
                  pl.BlockSpec((None, tg, gw, gw), lambda j: (l, j, 0, 0)),
                  pl.BlockSpec((None, tg, gw, 4 * S5_STATE), lambda j: (l, j, 0, 0)),
                  pl.BlockSpec((None, tg, 4 * S5_STATE, gw), lambda j: (l, j, 0, 0)),
                  pl.BlockSpec((None, tg, 2, S5_LOG_CHUNKS, 2 * S5_STATE), lambda j: (l, j, 0, 0, 0)),
                  pl.BlockSpec((None, tg, 2, S5_LOG_CHUNKS, 2 * S5_STATE), lambda j: (l, j, 0, 0, 0)),
                  pl.BlockSpec((None, tg, 1, gw), lambda j: (l, j, 0, 0))],
        out_specs=pl.BlockSpec((SEQ, lane), lambda j: (0, j)),
        out_shape=jax.ShapeDtypeStruct((SEQ, BRANCH), F32),
        scratch_shapes=[pltpu.VMEM((tg, S5_CHUNKS, gw), F32)],
        compiler_params=_cparams(("parallel",)),
        name="s5_scan",
    )(rest, w_intra, w_in, w_out, a_re, a_im, skip_t)


def _merge_kernel(hm_ref, hyt_ref, s5_ref, wa_ref, wb_ref, wga_ref, wgg_ref, g0_ref, g1_ref, g2_ref,
                  out_ref, hy_sc):
    @pl.when(pl.program_id(1) == 0)
    def _():
        hy_sc[...] = hyt_ref[...].T.astype(BF16)

    def proj(a, w_ref):
        return jnp.dot(a, w_ref[...].astype(BF16), preferred_element_type=F32)

    s5b = s5_ref[...].astype(BF16)
    out_a = proj(hm_ref[...], wa_ref)
    out_b = proj(hy_sc[...], wb_ref)
    out_c = proj(s5b, wga_ref) * jax.nn.sigmoid(proj(s5b, wgg_ref))
    merged = (jax.nn.sigmoid(g0_ref[...]) * out_a + jax.nn.sigmoid(g1_ref[...]) * out_b
              + jax.nn.sigmoid(g2_ref[...]) * out_c)
    out_ref[...] = merged.astype(out_ref.dtype)


def _merge(hm, hy_t, s5y, rest, w_a, w_b, w_glu, l, *, tm=512, tn=512):
    nj = D_MODEL // tn
    goff = BRANCH // tn
    return pl.pallas_call(
        _merge_kernel,
        grid=(SEQ // tm, nj),
        in_specs=[pl.BlockSpec((tm, BRANCH), lambda i, j: (i, 0)),
                  pl.BlockSpec((BRANCH, tm), lambda i, j: (0, i)),
                  pl.BlockSpec((tm, BRANCH), lambda i, j: (i, 0)),
                  pl.BlockSpec((None, BRANCH, tn), lambda i, j: (l, 0, j)),
                  pl.BlockSpec((None, BRANCH, tn), lambda i, j: (l, 0, j)),
                  pl.BlockSpec((None, BRANCH, tn), lambda i, j: (l, 0, j)),
                  pl.BlockSpec((None, BRANCH, tn), lambda i, j: (l, 0, nj + j)),
                  pl.BlockSpec((tm, tn), lambda i, j: (i, goff + j)),
                  pl.BlockSpec((tm, tn), lambda i, j: (i, goff + nj + j)),
                  pl.BlockSpec((tm, tn), lambda i, j: (i, goff + 2 * nj + j))],
        out_specs=pl.BlockSpec((tm, tn), lambda i, j: (i, j)),
        out_shape=jax.ShapeDtypeStruct((SEQ, D_MODEL), BF16),
        scratch_shapes=[pltpu.VMEM((tm, BRANCH), BF16)],
        compiler_params=_cparams(("parallel", "arbitrary")),
        name="branch_merge",
    )(hm, hy_t, s5y, w_a, w_b, w_glu, w_glu, rest, rest, rest)


def _layer_norm_rows(y, g, b):
    mu = jnp.mean(y, axis=1, keepdims=True)
    var = jnp.mean(jnp.square(y - mu), axis=1, keepdims=True)
    return (y - mu) * lax.rsqrt(var + LN_EPS) * g + b


def _res_ln_kernel(x_ref, f_ref, g_ref, b_ref, o_ref, ob_ref, *, transposed):
    f = f_ref[...]
    if transposed:
        f = f.T
    y = _layer_norm_rows(ALPHA * x_ref[...] + f, g_ref[...], b_ref[...])
    o_ref[...] = y
    ob_ref[...] = y.astype(BF16)


def _res_ln(x, f, g, b, l, *, transposed, tm=256):
    f_spec = (pl.BlockSpec((D_MODEL, tm), lambda i: (0, i)) if transposed
              else pl.BlockSpec((tm, D_MODEL), lambda i: (i, 0)))
    n_tok = x.shape[0]
    return pl.pallas_call(
        functools.partial(_res_ln_kernel, transposed=transposed),
        grid=(n_tok // tm,),
        in_specs=[pl.BlockSpec((tm, D_MODEL), lambda i: (i, 0)), f_spec,
                  pl.BlockSpec((None, 1, D_MODEL), lambda i: (l, 0, 0)),
                  pl.BlockSpec((None, 1, D_MODEL), lambda i: (l, 0, 0))],
        out_specs=[pl.BlockSpec((tm, D_MODEL), lambda i: (i, 0)),
                   pl.BlockSpec((tm, D_MODEL), lambda i: (i, 0))],
        out_shape=[jax.ShapeDtypeStruct((n_tok, D_MODEL), F32), jax.ShapeDtypeStruct((n_tok, D_MODEL), BF16)],
        compiler_params=_cparams(("parallel",)),
        name="residual_layernorm",
    )(x, f, g.reshape(DEPTH, 1, D_MODEL), b.reshape(DEPTH, 1, D_MODEL))


PEER_NO_RANK = 64.0


def _extract_sorted(s, n, out_ref, base, want_rank=False):
    rank = jnp.full_like(s, PEER_NO_RANK) if want_rank else None
    for r in range(n):
        m = jnp.max(s, axis=0, keepdims=True)
        out_ref[base + r:base + r + 1, :] = m
        hit = s == m
        if want_rank:
            rank = jnp.where(hit, float(r), rank)
        s = jnp.where(hit, -jnp.inf, s)
    return rank


def _peer_select_kernel(q_ref, keys_ref, cnt_ref, r2_ref, e1_ref, e2_ref, top_sc, kth_sc):
    k = PEER_TOPK
    qb = q_ref[...].astype(BF16)
    half = PEER_KEYS
    for h in range(PEER_HEADS):
        sc = []
        rank2 = None
        for c in range(2):
            qs = qb[:, (2 * h + c) * half:(2 * h + c + 1) * half]
            s = lax.dot_general(keys_ref[h, c].astype(BF16), qs, (((1,), (1,)), ((), ())),
                                preferred_element_type=F32)
            sc.append(s)
            rank2 = _extract_sorted(s, k, top_sc, c * k, want_rank=(c == 1))
        a = top_sc[0:k, :]
        b = top_sc[k:2 * k, :]
        sub = lax.broadcasted_iota(jnp.int32, (8, a.shape[1]), 0)
        groups = [a[0:1] + b[0:8], a[0:1] + b[8:16], a[1:2] + b[0:8]]
        for i in range(2, 8):
            groups.append(jnp.where(sub < k // (i + 1), a[i:i + 1] + b[0:8], -jnp.inf))
        groups.append(a[8:16] + b[0:1])
        cand = jnp.concatenate(groups, axis=0)
        _extract_sorted(cand, k, kth_sc, 0)
        tau = kth_sc[k - 1:k, :]
        top = a[0:1] + b[0:1]
        z = jnp.sum(jnp.where(cand >= tau, jnp.exp(cand - top), 0.0), axis=0, keepdims=True)
        cnt = jnp.zeros_like(sc[0])
        for j in range(k):
            bj = top_sc[k + j:k + j + 1, :]
            cnt = jnp.where(sc[0] + bj >= tau, j + 1.0, cnt)
        cnt_ref[h] = cnt
        r2_ref[h] = rank2.astype(BF16)
        e1_ref[h] = jnp.exp(sc[0] - a[0:1])
        e2_ref[h] = (jnp.exp(sc[1] - b[0:1]) / z).astype(BF16)


def _peer_select(q, keys, l, *, tt=128):
    n_tok = q.shape[0]
    big = jax.ShapeDtypeStruct((PEER_HEADS, PEER_KEYS, n_tok), F32)
    big_bf = jax.ShapeDtypeStruct((PEER_HEADS, PEER_KEYS, n_tok), BF16)
    big_spec = pl.BlockSpec((PEER_HEADS, PEER_KEYS, tt), lambda i: (0, 0, i))
    return pl.pallas_call(
        _peer_select_kernel,
        grid=(n_tok // tt,),
        in_specs=[pl.BlockSpec((tt, D_MODEL), lambda i: (i, 0)),
                  pl.BlockSpec((None, PEER_HEADS, 2, PEER_KEYS, PEER_KEYS), lambda i: (l, 0, 0, 0, 0))],
        out_specs=[big_spec, big_spec, big_spec, big_spec],
        out_shape=[big, big_bf, big, big_bf],
        scratch_shapes=[pltpu.VMEM((2 * PEER_TOPK, tt), F32), pltpu.VMEM((PEER_TOPK, tt), F32)],
        compiler_params=_cparams(("parallel",)),
        name="peer_select",
    )(q, keys)


def _peer_dense_kernel(x_ref, u_ref, vt_ref, cnt_ref, r2_ref, e1_ref, e2_ref, out_ref, act_sc, g_sc, *, rows):
    e = pl.program_id(1)

    @pl.when(e == 0)
    def _():
        out_ref[...] = jnp.zeros_like(out_ref)

    act_sc[...] = lax.dot_general(u_ref[...], x_ref[...], (((1,), (1,)), ((), ())),
                                  preferred_element_type=F32)
    kk = PEER_KEYS
    for tc in range(act_sc.shape[1] // kk):
        lanes = slice(tc * kk, (tc + 1) * kk)
        for i in range(rows):
            w = jnp.zeros((kk, kk), BF16)
            for h in range(PEER_HEADS):
                hit = r2_ref[h, :, lanes] < cnt_ref[h, i:i + 1, lanes].astype(BF16)
                val = e2_ref[h, :, lanes] * e1_ref[h, i:i + 1, lanes].astype(BF16)
                w = w + jnp.where(hit, val, jnp.zeros_like(val))
            g_sc[i * kk:(i + 1) * kk, lanes] = jax.nn.gelu(act_sc[i * kk:(i + 1) * kk, lanes].astype(BF16)) * w
    out_ref[...] += jnp.dot(vt_ref[...], g_sc[...], preferred_element_type=F32)


def _transpose_tile_kernel(v_ref, o_ref):
    o_ref[...] = v_ref[...].T.astype(o_ref.dtype)


def _peer_value_tiles(v, l, te):
    return pl.pallas_call(
        _transpose_tile_kernel,
        grid=(N_EXPERTS // te,),
        in_specs=[pl.BlockSpec((None, te, D_MODEL), lambda e: (l, e, 0))],
        out_specs=pl.BlockSpec((None, D_MODEL, te), lambda e: (e, 0, 0)),
        out_shape=jax.ShapeDtypeStruct((N_EXPERTS // te, D_MODEL, te), BF16),
        compiler_params=_cparams(("parallel",)),
        name="peer_value_tiles",
    )(v)


def _peer_dense(x_bf, u_bf, v, l, sel, *, tt=512, rows=8):
    cnt, r2, e1, e2 = sel
    te = rows * PEER_KEYS
    n_tok = x_bf.shape[0]
    vt_bf = _peer_value_tiles(v, l, te)
    row_spec = pl.BlockSpec((PEER_HEADS, rows, tt), lambda i, e: (0, e, i))
    all_spec = pl.BlockSpec((PEER_HEADS, PEER_KEYS, tt), lambda i, e: (0, 0, i))
    return pl.pallas_call(
        functools.partial(_peer_dense_kernel, rows=rows),
        grid=(n_tok // tt, N_EXPERTS // te),
        in_specs=[pl.BlockSpec((tt, D_MODEL), lambda i, e: (i, 0)),
                  pl.BlockSpec((None, te, D_MODEL), lambda i, e: (l, e, 0)),
                  pl.BlockSpec((None, D_MODEL, te), lambda i, e: (e, 0, 0)),
                  row_spec, all_spec, row_spec, all_spec],
        out_specs=pl.BlockSpec((D_MODEL, tt), lambda i, e: (0, i)),
        out_shape=jax.ShapeDtypeStruct((D_MODEL, n_tok), F32),
        scratch_shapes=[pltpu.VMEM((te, tt), F32), pltpu.VMEM((te, tt), BF16)],
        compiler_params=_cparams(("parallel", "arbitrary")),
        name="peer_dense",
    )(x_bf, u_bf, vt_bf, cnt, r2, e1, e2)


def _hyena_features():
    pos = jnp.arange(SEQ, dtype=F32)
    t = pos / (SEQ - 1)
    bands = jnp.linspace(1e-4, 16 - 1, 16, dtype=F32)
    ang = 2.0 * math.pi * pos[:, None] * bands[None, :] / SEQ
    feat = jnp.concatenate([t[:, None], jnp.cos(ang), -jnp.sin(ang)], axis=-1)
    feat_t = jnp.pad(feat.T, ((0, HY_EMB_PAD - feat.shape[1]), (0, 0)))
    t_row = t[None, :]

    def reversed_time(a):
        return jnp.roll(a[:, ::-1], 1, axis=1)

    return (jnp.concatenate([feat_t, reversed_time(feat_t)], axis=1),
            jnp.concatenate([t_row, reversed_time(t_row)], axis=1))


def kernel(x, w_in, mlstm_gate_bias, mlstm_norm_gain, w_mlstm_out, hyena_conv_w, hyena_conv_b, hyena_w1,
           hyena_b1, hyena_w2, hyena_b2, hyena_freq, hyena_w3, hyena_decay, hyena_skip, w_hyena_out,
           s5_lambda_re, s5_lambda_im, s5_log_step, s5_b_re, s5_b_im, s5_c_re, s5_c_im, s5_skip, w_s5_glu,
           w_out, ln1_g, ln1_b, peer_w_q, peer_subkeys, peer_u, peer_v, ln2_g, ln2_b):
    xf = x.reshape(SEQ, D_MODEL)
    xb = xf.astype(BF16)
    feat_t, t_row = _hyena_features()
    w_in_t = jnp.swapaxes(w_in, 1, 2)
    w_tail = _w_in_tail(w_in_t)
    u_bf = peer_u.astype(BF16)
    s5_mats = jax.vmap(_s5_chunk_matrices)(s5_lambda_re, s5_lambda_im, s5_log_step, s5_b_re, s5_b_im,
                                           s5_c_re, s5_c_im)
    for l in range(DEPTH):
        qkvo = _mm_wt(xb, w_in_t, l, 0, 4 * BRANCH, tm=1024, tn=512, name="proj_qkvo")
        gates_t = _mm_nt(w_in_t, l, OFF_GATES, N_GATES, xb, tr=N_GATES, tm=1024, name="proj_gates", out_dtype=F32)
        hp_t = _mm_nt(w_tail, l, 0, 3 * BRANCH, xb, tr=512, tm=1024, name="proj_hyena_t", out_dtype=BF16)
        rest = _mm_wt(xb, w_tail, l, 3 * BRANCH, BRANCH + 3 * D_MODEL, tm=1024, tn=512, name="proj_s5_mixgates")

        hdir = _mlstm(qkvo, gates_t, mlstm_gate_bias, l)
        hm = _mlstm_post(hdir, qkvo, mlstm_norm_gain, l)
        hy_t = _hyena(hp_t, l, hyena_conv_w, hyena_conv_b, hyena_w1, hyena_b1, hyena_w2, hyena_b2,
                      hyena_freq, hyena_w3, hyena_decay, hyena_skip, feat_t, t_row)
        s5y = _s5(rest, s5_mats, s5_skip, l)
        merged = _merge(hm, hy_t, s5y, rest, w_mlstm_out, w_hyena_out, w_s5_glu, l)
        mix = _mm(merged, w_out, l, 0, D_MODEL, tm=1024, tn=512, name="proj_out")
        xf, xb = _res_ln(xf, mix, ln1_g, ln1_b, l, transposed=False)

        q = _mm(xb, peer_w_q, l, 0, PEER_HEADS * 2 * PEER_KEYS, tm=1024, tn=512, name="peer_query")
        sel = _peer_select(q, peer_subkeys, l)
        ffn_t = _peer_dense(xb, u_bf, peer_v, l, sel)
        xf, xb = _res_ln(xf, ffn_t, ln2_g, ln2_b, l, transposed=True)
    return xf.reshape(1, SEQ, D_MODEL)
```

```python
import functools
import math

import numpy as np
import jax
import jax.numpy as jnp
from jax import lax
from jax.experimental import pallas as pl
from jax.experimental.pallas import tpu as pltpu

F32 = jnp.float32
BF16 = jnp.bfloat16

D_MODEL = 2048
SEQ = 8192
DEPTH = 4
BRANCH = 1024
HEADS = 4
HEAD_DIM = 256
CHUNK = 256
N_GATES = 16
HY_HID = 64
HY_EMB_PAD = 128
S5_GROUP = 16
S5_GROUPS = 64
S5_STATE = 64
S5_T = 16
S5_CHUNKS = SEQ // S5_T
S5_LOG_CHUNKS = 9
PEER_HEADS = 8
PEER_KEYS = 128
PEER_TOPK = 16
N_EXPERTS = PEER_KEYS * PEER_KEYS
ALPHA = (2 * DEPTH) ** 0.25
LN_EPS = 1e-5
FFT_N = 2 * SEQ
FFT_R = 128
OFF_GATES = 4 * BRANCH
OFF_HYENA = OFF_GATES + N_GATES
OFF_S5 = OFF_HYENA + 3 * BRANCH
OFF_MIXG = OFF_S5 + BRANCH

VMEM_LIMIT = 56 * 1024 * 1024


def _cparams(sem):
    return pltpu.CompilerParams(dimension_semantics=sem, vmem_limit_bytes=VMEM_LIMIT)


def _split_bf16(a):
    hi = a.astype(BF16)
    lo = (a - hi.astype(F32)).astype(BF16)
    return hi, lo


def _dot3(a, b_hi, b_lo):
    a_hi, a_lo = _split_bf16(a)
    acc = jnp.dot(a_hi, b_hi, preferred_element_type=F32)
    acc += jnp.dot(a_lo, b_hi, preferred_element_type=F32)
    acc += jnp.dot(a_hi, b_lo, preferred_element_type=F32)
    return acc


def _dot3_both(a, b):
    b_hi, b_lo = _split_bf16(b)
    return _dot3(a, b_hi, b_lo)


def _mm_kernel(a_ref, b_ref, o_ref):
    o_ref[...] = jnp.dot(a_ref[...].astype(BF16), b_ref[...].astype(BF16),
                         preferred_element_type=F32).astype(o_ref.dtype)


def _mm(a, w, l, col_off, n_cols, *, tm, tn, name):
    m, k = a.shape
    cb = col_off // tn
    assert col_off % tn == 0 and n_cols % tn == 0 and m % tm == 0
    return pl.pallas_call(
        _mm_kernel,
        grid=(m // tm, n_cols // tn),
        in_specs=[pl.BlockSpec((tm, k), lambda i, j: (i, 0)),
                  pl.BlockSpec((None, k, tn), lambda i, j: (l, 0, cb + j))],
        out_specs=pl.BlockSpec((tm, tn), lambda i, j: (i, j)),
        out_shape=jax.ShapeDtypeStruct((m, n_cols), F32),
        compiler_params=_cparams(("parallel", "parallel")),
        name=name,
    )(a, w)


def _mm_wt_kernel(a_ref, w_ref, o_ref):
    o_ref[...] = lax.dot_general(a_ref[...].astype(BF16), w_ref[...].astype(BF16), (((1,), (1,)), ((), ())),
                                 preferred_element_type=F32).astype(o_ref.dtype)


def _mm_wt(a, wt, l, row_off, n_rows, *, tm, tn, name):
    m, k = a.shape
    rb = row_off // tn
    assert row_off % tn == 0 and n_rows % tn == 0 and m % tm == 0
    return pl.pallas_call(
        _mm_wt_kernel,
        grid=(m // tm, n_rows // tn),
        in_specs=[pl.BlockSpec((tm, k), lambda i, j: (i, 0)),
                  pl.BlockSpec((None, tn, k), lambda i, j: (l, rb + j, 0))],
        out_specs=pl.BlockSpec((tm, tn), lambda i, j: (i, j)),
        out_shape=jax.ShapeDtypeStruct((m, n_rows), F32),
        compiler_params=_cparams(("parallel", "parallel")),
        name=name,
    )(a, wt)


def _mm_nt_kernel(w_ref, a_ref, o_ref):
    o_ref[...] = lax.dot_general(w_ref[...].astype(BF16), a_ref[...].astype(BF16), (((1,), (1,)), ((), ())),
                                 preferred_element_type=F32).astype(o_ref.dtype)


def _mm_nt(wt, l, row_off, n_rows, a, *, tr, tm, name, out_dtype):
    k = wt.shape[2]
    m = a.shape[0]
    rb = row_off // tr
    assert row_off % tr == 0 and n_rows % tr == 0
    return pl.pallas_call(
        _mm_nt_kernel,
        grid=(n_rows // tr, m // tm),
        in_specs=[pl.BlockSpec((None, tr, k), lambda i, j: (l, rb + i, 0)),
                  pl.BlockSpec((tm, k), lambda i, j: (j, 0))],
        out_specs=pl.BlockSpec((tr, tm), lambda i, j: (i, j)),
        out_shape=jax.ShapeDtypeStruct((n_rows, m), out_dtype),
        compiler_params=_cparams(("parallel", "parallel")),
        name=name,
    )(wt, a)


def _w_tail_kernel(cur_ref, nxt_ref, o_ref, *, shift):
    o_ref[...] = jnp.concatenate([cur_ref[shift:, :], nxt_ref[...]], axis=0).astype(o_ref.dtype)


def _w_in_tail(w_in_t, *, tn=512):
    depth, n_all, k = w_in_t.shape
    n_out = n_all - OFF_HYENA
    return pl.pallas_call(
        functools.partial(_w_tail_kernel, shift=N_GATES),
        grid=(depth, n_out // tn),
        in_specs=[pl.BlockSpec((None, tn, k), lambda l, j: (l, OFF_GATES // tn + j, 0)),
                  pl.BlockSpec((None, N_GATES, k), lambda l, j: (l, (OFF_GATES + (j + 1) * tn) // N_GATES, 0))],
        out_specs=pl.BlockSpec((None, tn, k), lambda l, j: (l, j, 0)),
        out_shape=jax.ShapeDtypeStruct((depth, n_out, k), BF16),
        compiler_params=_cparams(("parallel", "parallel")),
        name="w_in_tail",
    )(w_in_t, w_in_t)


def _log_sigmoid(x):
    return jnp.minimum(x, 0.0) - jnp.log1p(jnp.exp(-jnp.abs(x)))


def _mlstm_chunk(d, q, k, v, li_r, lf_r, li_c, lf_c, c_sc, n_sc, m_sc):
    n = CHUNK
    row = lax.broadcasted_iota(jnp.int32, (n, n), 0)
    col = lax.broadcasted_iota(jnp.int32, (n, n), 1)
    valid = (row >= col) if d == 0 else (row <= col)
    valid_t = (col >= row) if d == 0 else (col <= row)
    b_col = jnp.sum(jnp.where(valid, lf_r, 0.0), axis=1, keepdims=True)
    b_row = jnp.sum(jnp.where(valid_t, lf_c, 0.0), axis=0, keepdims=True)
    total = jnp.sum(lf_r, axis=1, keepdims=True)

    m_prev = m_sc[...]
    m_inter = b_col + m_prev
    log_d = jnp.where(valid, b_col - b_row + li_r, -jnp.inf)
    m_t = jnp.maximum(m_inter, jnp.max(log_d, axis=1, keepdims=True))
    inter = jnp.exp(m_inter - m_t)
    dmat = jnp.exp(log_d - m_t)

    k = k * (HEAD_DIM ** -0.5)
    qb = q.astype(BF16)
    kb = k.astype(BF16)
    vb = v.astype(BF16)
    s = lax.dot_general(qb, kb, (((1,), (1,)), ((), ())), preferred_element_type=F32) * dmat
    num = jnp.dot(s.astype(BF16), vb, preferred_element_type=F32)
    num += inter * jnp.dot(qb, c_sc[...].astype(BF16), preferred_element_type=F32)
    qn = jnp.sum(q * n_sc[...], axis=1, keepdims=True)
    den = jnp.sum(s, axis=1, keepdims=True) + inter * qn
    h_out = num / jnp.maximum(jnp.abs(den), jnp.exp(-m_t))

    log_w = total - b_col + li_c
    m_new = jnp.maximum(total + m_prev, jnp.max(log_w, axis=0, keepdims=True))
    w = jnp.exp(log_w - m_new)
    decay = jnp.exp(total + m_prev - m_new)
    kw = k * w
    c_sc[...] = decay * c_sc[...] + lax.dot_general(
        kw.astype(BF16), vb, (((0,), (0,)), ((), ())), preferred_element_type=F32)
    n_sc[...] = decay * n_sc[...] + jnp.sum(kw, axis=0, keepdims=True)
    m_sc[...] = m_new
    return h_out


def _mlstm_kernel(qkv_f_ref, qkv_b_ref, gr_f_ref, gr_b_ref, gc_f_ref, gc_b_ref, bias_ref,
                  o_f_ref, o_b_ref, c_sc, n_sc, m_sc):
    @pl.when(pl.program_id(0) == 0)
    def _():
        c_sc[...] = jnp.zeros_like(c_sc)
        n_sc[...] = jnp.zeros_like(n_sc)
        m_sc[...] = jnp.zeros_like(m_sc)

    for d, (qkv_ref, gr_ref, gc_ref, o_ref) in enumerate(((qkv_f_ref, gr_f_ref, gc_f_ref, o_f_ref),
                                                         (qkv_b_ref, gr_b_ref, gc_b_ref, o_b_ref))):
        for h in range(HEADS):
            cols = slice(h * HEAD_DIM, (h + 1) * HEAD_DIM)
            b_i = bias_ref[d * 2 * HEADS + h]
            b_f = bias_ref[d * 2 * HEADS + HEADS + h]
            chain = d * HEADS + h
            o_ref[:, cols] = _mlstm_chunk(
                d,
                qkv_ref[:, cols],
                qkv_ref[:, BRANCH + h * HEAD_DIM:BRANCH + (h + 1) * HEAD_DIM],
                qkv_ref[:, 2 * BRANCH + h * HEAD_DIM:2 * BRANCH + (h + 1) * HEAD_DIM],
                gr_ref[h] + b_i, _log_sigmoid(gr_ref[HEADS + h] + b_f),
                gc_ref[h] + b_i, _log_sigmoid(gc_ref[HEADS + h] + b_f),
                c_sc.at[chain], n_sc.at[chain], m_sc.at[chain])


def _mlstm(qkvo, gates_t, gate_bias, l):
    nc = SEQ // CHUNK
    g_rows = gates_t.reshape(N_GATES, 1, SEQ)
    g_cols = gates_t.reshape(N_GATES, SEQ, 1)
    bias = gate_bias.reshape(DEPTH, N_GATES, 1, 1)
    half = N_GATES // 2
    chains = 2 * HEADS
    out = jax.ShapeDtypeStruct((SEQ, BRANCH), F32)
    return pl.pallas_call(
        _mlstm_kernel,
        grid=(nc,),
        in_specs=[
            pl.BlockSpec((CHUNK, 4 * BRANCH), lambda c: (c, 0)),
            pl.BlockSpec((CHUNK, 4 * BRANCH), lambda c: (nc - 1 - c, 0)),
            pl.BlockSpec((half, 1, CHUNK), lambda c: (0, 0, c)),
            pl.BlockSpec((half, 1, CHUNK), lambda c: (1, 0, nc - 1 - c)),
            pl.BlockSpec((half, CHUNK, 1), lambda c: (0, c, 0)),
            pl.BlockSpec((half, CHUNK, 1), lambda c: (1, nc - 1 - c, 0)),
            pl.BlockSpec((None, N_GATES, 1, 1), lambda c: (l, 0, 0, 0)),
        ],
        out_specs=[pl.BlockSpec((CHUNK, BRANCH), lambda c: (c, 0)),
                   pl.BlockSpec((CHUNK, BRANCH), lambda c: (nc - 1 - c, 0))],
        out_shape=[out, out],
        scratch_shapes=[pltpu.VMEM((chains, HEAD_DIM, HEAD_DIM), F32),
                        pltpu.VMEM((chains, 1, HEAD_DIM), F32),
                        pltpu.VMEM((chains, 1, 1), F32)],
        compiler_params=_cparams(("arbitrary",)),
        name="mlstm_scan",
    )(qkvo, qkvo, g_rows, g_rows, g_cols, g_cols, bias)


def _mlstm_post_kernel(hf_ref, hb_ref, o_ref, gain_ref, out_ref):
    h = hf_ref[...] + hb_ref[...]
    h = jax.nn.sigmoid(o_ref[...]) * h
    for hd in range(HEADS):
        sl = slice(hd * HEAD_DIM, (hd + 1) * HEAD_DIM)
        hh = h[:, sl]
        mu = jnp.mean(hh, axis=1, keepdims=True)
        var = jnp.mean(jnp.square(hh - mu), axis=1, keepdims=True)
        out_ref[:, sl] = ((hh - mu) * lax.rsqrt(var + LN_EPS) * gain_ref[:, sl]).astype(out_ref.dtype)


def _mlstm_post(hdir, qkvo, gain, l, *, tm=512):
    return pl.pallas_call(
        _mlstm_post_kernel,
        grid=(SEQ // tm,),
        in_specs=[pl.BlockSpec((tm, BRANCH), lambda i: (i, 0)),
                  pl.BlockSpec((tm, BRANCH), lambda i: (i, 0)),
                  pl.BlockSpec((tm, BRANCH), lambda i: (i, 3)),
                  pl.BlockSpec((None, 1, BRANCH), lambda i: (l, 0, 0))],
        out_specs=pl.BlockSpec((tm, BRANCH), lambda i: (i, 0)),
        out_shape=jax.ShapeDtypeStruct((SEQ, BRANCH), BF16),
        compiler_params=_cparams(("parallel",)),
        name="mlstm_post",
    )(hdir[0], hdir[1], qkvo, gain.reshape(DEPTH, 1, BRANCH))


HY_CT = 32
HY_RB = 512


@functools.lru_cache(maxsize=None)
def _fft_constants():
    r = FFT_R
    idx = np.arange(r)
    th = 2.0 * np.pi * np.outer(idx, idx) / r
    cs, sn = np.cos(th), np.sin(th)
    ph = 2.0 * np.pi * np.outer(idx, idx) / FFT_N
    f1 = np.concatenate([cs, -sn], axis=1)
    m3 = np.block([[cs, -sn], [sn, cs]])
    m4 = np.block([[cs, sn], [-sn, cs]])
    m6 = np.concatenate([cs, -sn], axis=0) / FFT_N
    m6[:, r // 2:] = 0.0
    consts = {name: jnp.asarray(mat, F32).astype(BF16)
              for name, mat in (("f1", f1), ("m3", m3), ("m4", m4), ("m6", m6))}
    consts["tw"] = (jnp.asarray(np.cos(ph), F32), jnp.asarray(np.sin(ph), F32))
    return consts


def _rows(i, n):
    return pl.ds(pl.multiple_of(i * n, n), n)


def _for_each(n, body, unroll=1):
    def step(i, carry):
        body(i)
        return carry

    lax.fori_loop(0, n, step, 0, unroll=unroll)


def _fwd_stage1(sig_ref, hi_ref, ct, t_buf, a_buf, b_buf, f1_ref, twr_ref, twi_ref):
    r = FFT_R
    zeros = jnp.zeros((r // 2, r), F32)

    def tr_in(c):
        hi = zeros if hi_ref is None else hi_ref[c]
        t_buf[_rows(c, r), :] = jnp.concatenate([sig_ref[c], hi], axis=0).T.astype(BF16)

    _for_each(ct, tr_in, unroll=4)

    def stage1(b):
        rows = _rows(b, HY_RB)
        a_buf[rows, :] = jnp.dot(t_buf[rows, :], f1_ref[...], preferred_element_type=F32)

    _for_each(ct * r // HY_RB, stage1)

    def twiddle(c):
        rows = _rows(c, r)
        a = a_buf[rows, :]
        ar, ai = a[:, :r], a[:, r:]
        twr, twi = twr_ref[...], twi_ref[...]
        b_buf[rows, :r] = (ar * twr + ai * twi).T.astype(BF16)
        b_buf[rows, r:] = (ai * twr - ar * twi).T.astype(BF16)

    _for_each(ct, twiddle, unroll=4)


def _hy_mlp_kernel(feat_ref, w1_ref, b1_ref, w2_ref, b2_ref, fr_ref, out_ref):
    h = _dot3_both(w1_ref[...], feat_ref[...]) + b1_ref[...]
    h = jnp.sin(fr_ref[0] * h)
    h = _dot3_both(w2_ref[...], h) + b2_ref[...]
    out_ref[...] = jnp.sin(fr_ref[1] * h)


def _hy_filter_kernel(w3_ref, h_ref, dec_ref, t_ref, out_ref):
    d = pl.program_id(0)
    filt = _dot3_both(w3_ref[...], h_ref[...])
    filt = filt * jnp.exp(-t_ref[...] * jnp.abs(dec_ref[...]))
    filt = filt / (jnp.sum(jnp.abs(filt), axis=1, keepdims=True) + 1e-6)
    lane = lax.broadcasted_iota(jnp.int32, filt.shape, 1)
    out_ref[...] = jnp.where((lane == 0) & (d == 1), 0.0, filt)


def _hy_spec_kernel(fa_ref, fb_ref, f1_ref, m3_ref, twr_ref, twi_ref, out_ref, t_buf, a_buf, b_buf, *, ct):
    r = FFT_R
    nblk = ct * r // HY_RB
    _fwd_stage1(fa_ref, fb_ref, ct, t_buf, a_buf, b_buf, f1_ref, twr_ref, twi_ref)

    def put(b):
        rows = _rows(b, HY_RB)
        out_ref[rows, :] = jnp.dot(b_buf[rows, :], m3_ref[...], preferred_element_type=F32)

    _for_each(nblk, put)


def _shift_prev(p, n1, n2):
    r1 = pltpu.roll(p, 1, 2)
    r2 = pltpu.roll(r1, 1, 1)
    return jnp.where(n2 == 0, jnp.where(n1 == 0, 0.0, r2), r1)


def _shift_next(p, n1, n2):
    s1, s2 = p.shape[1], p.shape[2]
    r1 = pltpu.roll(p, s2 - 1, 2)
    r2 = pltpu.roll(r1, s1 - 1, 1)
    return jnp.where(n2 == s2 - 1, jnp.where(n1 == s1 - 1, 0.0, r2), r1)


def _hy_conv_kernel(p0_ref, p1_ref, p2_ref, cw_ref, cb_ref, skip_ref, h_ref,
                    f1_ref, m3_ref, m4_ref, m6_ref, twr_ref, twi_ref,
                    out_ref, z_buf, t_buf, a_buf, b_buf, *, ct):
    r = FFT_R
    n1_len = SEQ // r
    nblk = ct * r // HY_RB
    shp = (ct, n1_len, r)
    n1 = lax.broadcasted_iota(jnp.int32, shp, 1)
    n2 = lax.broadcasted_iota(jnp.int32, shp, 2)

    def short_conv(p_ref, g):
        p = p_ref[...].astype(F32)
        return (_shift_prev(p, n1, n2) * cw_ref[3 * g] + p * cw_ref[3 * g + 1]
                + _shift_next(p, n1, n2) * cw_ref[3 * g + 2] + cb_ref[g])

    z_buf[...] = short_conv(p1_ref, 1) * short_conv(p2_ref, 2)
    _fwd_stage1(z_buf, None, ct, t_buf, a_buf, b_buf, f1_ref, twr_ref, twi_ref)

    def freq_domain(b):
        rows = _rows(b, HY_RB)
        x = jnp.dot(b_buf[rows, :], m3_ref[...], preferred_element_type=F32)
        h = h_ref[rows, :]
        xr, xi = x[:, :r], x[:, r:]
        hr, hi = h[:, :r], h[:, r:]
        y = jnp.concatenate([xr * hr - xi * hi, xr * hi + xi * hr], axis=1).astype(BF16)
        a_buf[rows, :] = jnp.dot(y, m4_ref[...], preferred_element_type=F32)

    _for_each(nblk, freq_domain)

    def inv_twiddle(c):
        rows = _rows(c, r)
        b = a_buf[rows, :]
        br, bi = b[:, :r], b[:, r:]
        twr, twi = twr_ref[...], twi_ref[...]
        b_buf[rows, :r] = (br * twr - bi * twi).T.astype(BF16)
        b_buf[rows, r:] = (br * twi + bi * twr).T.astype(BF16)

    _for_each(ct, inv_twiddle, unroll=4)

    def stage6(b):
        rows = _rows(b, HY_RB)
        a_buf[rows, :r] = jnp.dot(b_buf[rows, :], m6_ref[...], preferred_element_type=F32)

    _for_each(nblk, stage6)

    def finish(c):
        y_c = a_buf[_rows(c, r), :r].T[:n1_len, :]
        out_ref[c] = y_c + skip_ref[c] * z_buf[c]

    _for_each(ct, finish, unroll=4)
    out_ref[...] = short_conv(p0_ref, 0) * out_ref[...]


def _hyena(hp_t, l, conv_w, conv_b, w1, b1, w2, b2, freq, w3, decay, skip, feat_t, t_row):
    r = FFT_R
    n1 = SEQ // r
    ct = HY_CT
    k = _fft_constants()
    f1, m3, m4, m6 = k["f1"], k["m3"], k["m4"], k["m6"]
    twr, twi = k["tw"]

    def full(a):
        return pl.BlockSpec(a.shape, lambda *_: (0,) * a.ndim)

    h2 = pl.pallas_call(
        _hy_mlp_kernel,
        grid=(2,),
        in_specs=[pl.BlockSpec((feat_t.shape[0], SEQ), lambda i: (0, i)),
                  pl.BlockSpec((None, HY_HID, feat_t.shape[0]), lambda i: (l, 0, 0)),
                  pl.BlockSpec((None, HY_HID, 1), lambda i: (l, 0, 0)),
                  pl.BlockSpec((None, HY_HID, HY_HID), lambda i: (l, 0, 0)),
                  pl.BlockSpec((None, HY_HID, 1), lambda i: (l, 0, 0)),
                  pl.BlockSpec((None, 2, HY_HID, 1), lambda i: (l, 0, 0, 0))],
        out_specs=pl.BlockSpec((HY_HID, SEQ), lambda i: (0, i)),
        out_shape=jax.ShapeDtypeStruct((HY_HID, 2 * SEQ), F32),
        compiler_params=_cparams(("parallel",)),
        name="hyena_filter_mlp",
    )(feat_t, jnp.pad(jnp.swapaxes(w1, 1, 2), ((0, 0), (0, 0), (0, feat_t.shape[0] - w1.shape[1]))),
      b1[:, :, None], jnp.swapaxes(w2, 1, 2), b2[:, :, None], freq[:, :, :, None])

    rt = 128
    filt = pl.pallas_call(
        _hy_filter_kernel,
        grid=(2, BRANCH // rt),
        in_specs=[pl.BlockSpec((None, rt, HY_HID), lambda d, i: (l, d * (BRANCH // rt) + i, 0)),
                  pl.BlockSpec((HY_HID, SEQ), lambda d, i: (0, d)),
                  pl.BlockSpec((None, None, rt, 1), lambda d, i: (l, d, i, 0)),
                  pl.BlockSpec((1, SEQ), lambda d, i: (0, d))],
        out_specs=pl.BlockSpec((None, rt, SEQ), lambda d, i: (d, i, 0)),
        out_shape=jax.ShapeDtypeStruct((2, BRANCH, SEQ), F32),
        compiler_params=_cparams(("parallel", "parallel")),
        name="hyena_filter",
    )(jnp.swapaxes(w3, 1, 2), h2, decay[:, :, :, None], t_row)
    filt = filt.reshape(2, BRANCH, n1, r)

    scratch = [pltpu.VMEM((ct * r, r), BF16), pltpu.VMEM((ct * r, 2 * r), F32), pltpu.VMEM((ct * r, 2 * r), BF16)]
    spec = pl.pallas_call(
        functools.partial(_hy_spec_kernel, ct=ct),
        grid=(BRANCH // ct,),
        in_specs=[pl.BlockSpec((None, ct, n1, r), lambda i: (0, i, 0, 0)),
                  pl.BlockSpec((None, ct, n1, r), lambda i: (1, i, 0, 0)),
                  full(f1), full(m3), full(twr), full(twi)],
        out_specs=pl.BlockSpec((ct * r, 2 * r), lambda i: (i, 0)),
        out_shape=jax.ShapeDtypeStruct((BRANCH * r, 2 * r), F32),
        scratch_shapes=scratch,
        compiler_params=_cparams(("parallel",)),
        name="hyena_filter_spectrum",
    )(filt, filt, f1, m3, twr, twi)

    hp3 = hp_t.reshape(3 * BRANCH, n1, r)
    nb = BRANCH // ct
    cw = conv_w.reshape(DEPTH, 3, 3, BRANCH).transpose(0, 2, 1, 3).reshape(DEPTH, 9, BRANCH, 1, 1)
    cb = conv_b.reshape(DEPTH, 3, BRANCH, 1, 1)
    consts = (f1, m3, m4, m6, twr, twi)
    return pl.pallas_call(
        functools.partial(_hy_conv_kernel, ct=ct),
        grid=(nb,),
        in_specs=[pl.BlockSpec((ct, n1, r), lambda i: (i, 0, 0)),
                  pl.BlockSpec((ct, n1, r), lambda i: (nb + i, 0, 0)),
                  pl.BlockSpec((ct, n1, r), lambda i: (2 * nb + i, 0, 0)),
                  pl.BlockSpec((None, 9, ct, 1, 1), lambda i: (l, 0, i, 0, 0)),
                  pl.BlockSpec((None, 3, ct, 1, 1), lambda i: (l, 0, i, 0, 0)),
                  pl.BlockSpec((None, ct, 1, 1), lambda i: (l, i, 0, 0)),
                  pl.BlockSpec((ct * r, 2 * r), lambda i: (i, 0)),
                  *[full(a) for a in consts]],
        out_specs=pl.BlockSpec((ct, n1, r), lambda i: (i, 0, 0)),
        out_shape=jax.ShapeDtypeStruct((BRANCH, n1, r), F32),
        scratch_shapes=[pltpu.VMEM((ct, n1, r), F32)] + scratch,
        compiler_params=_cparams(("parallel",)),
        name="hyena_conv",
    )(hp3, hp3, hp3, cw, cb, skip.reshape(DEPTH, BRANCH, 1, 1), spec, *consts).reshape(BRANCH, SEQ)


def _s5_chunk_matrices(lam_re, lam_im, log_step, b_re, b_im, c_re, c_im):
    t = S5_T
    lam = lax.complex(lam_re, lam_im)
    step = jnp.exp(log_step)[..., None]
    lam_dt = lam * step
    b_bar = ((jnp.exp(lam_dt) - 1.0) / lam)[..., None] * lax.complex(b_re, b_im)
    c_mat = lax.complex(c_re, c_im)
    pw = jnp.exp(lam_dt[None] * jnp.arange(t + 1, dtype=F32)[:, None, None, None])
    kern = jnp.einsum("dgvp,mdgp,dgpn->mdgvn", c_mat, pw[:t], b_bar, precision="highest").real
    lag = np.arange(t)[:, None] - np.arange(t)[None, :]
    pick = np.stack([lag[:, :, None] == np.arange(t), -lag[:, :, None] == np.arange(t)], axis=-1)
    kfb = jnp.einsum("ajmd,mdgvn->ajgvn", jnp.asarray(pick, F32), kern, precision="highest")
    w_intra = jnp.transpose(kfb, (2, 1, 4, 0, 3)).reshape(S5_GROUPS, t * S5_GROUP, t * S5_GROUP)
    in_f = pw[:t, 0][::-1][..., None] * b_bar[0][None]
    in_b = pw[:t, 1][..., None] * b_bar[1][None]

    def in_mat(m):
        m = jnp.transpose(m, (1, 0, 3, 2)).reshape(S5_GROUPS, t * S5_GROUP, S5_STATE)
        return jnp.concatenate([m.real, m.imag], axis=-1)

    w_in = jnp.concatenate([in_mat(in_f), in_mat(in_b)], axis=-1)
    out_f = c_mat[0][None] * pw[1:, 0][:, :, None, :]
    out_b = c_mat[1][None] * pw[1:, 1][::-1][:, :, None, :]

    def out_mat(m):
        m = jnp.transpose(m, (1, 3, 0, 2)).reshape(S5_GROUPS, S5_STATE, t * S5_GROUP)
        return jnp.concatenate([m.real, -m.imag], axis=1)

    w_out = jnp.concatenate([out_mat(out_f), out_mat(out_b)], axis=1)
    a_pow = jnp.exp(lam_dt[None] * (t * 2.0 ** jnp.arange(S5_LOG_CHUNKS, dtype=F32))[:, None, None, None])
    a_pow = jnp.transpose(a_pow, (2, 1, 0, 3))
    a_re = jnp.concatenate([a_pow.real, a_pow.real], axis=-1)
    a_im = jnp.concatenate([-a_pow.imag, a_pow.imag], axis=-1)
    return w_intra, w_in, w_out, a_re, a_im


S5_TILE_GROUPS = 128 // S5_GROUP


def _s5_group(u, wi, win, wout, a_re, a_im, skip_row):
    nc = S5_CHUNKS
    half = S5_STATE
    ub = u.astype(BF16)
    y = jnp.dot(ub, wi.astype(BF16), preferred_element_type=F32) + skip_row * u
    s_in = jnp.dot(ub, win.astype(BF16), preferred_element_type=F32)
    row = lax.broadcasted_iota(jnp.int32, (nc, 2 * half), 0)

    def scan(s, d):
        for step in range(S5_LOG_CHUNKS):
            sh = 1 << step
            if d == 0:
                moved = jnp.where(row >= sh, pltpu.roll(s, sh, 0), 0.0)
            else:
                moved = jnp.where(row < nc - sh, pltpu.roll(s, nc - sh, 0), 0.0)
            swapped = pltpu.roll(moved, half, 1)
            s = s + a_re[d, step:step + 1, :] * moved + a_im[d, step:step + 1, :] * swapped
        if d == 0:
            return jnp.where(row >= 1, pltpu.roll(s, 1, 0), 0.0)
        return jnp.where(row < nc - 1, pltpu.roll(s, nc - 1, 0), 0.0)

    e = jnp.concatenate([scan(s_in[:, :2 * half], 0), scan(s_in[:, 2 * half:], 1)], axis=1)
    return y + jnp.dot(e.astype(BF16), wout.astype(BF16), preferred_element_type=F32)


def _s5_kernel(u_ref, wi_ref, win_ref, wout_ref, are_ref, aim_ref, skip_ref, y_ref, y_sc):
    t, gs, nc = S5_T, S5_GROUP, S5_CHUNKS
    for g in range(S5_TILE_GROUPS):
        u_g = jnp.concatenate([u_ref[pl.ds(tau, nc, stride=t), :][:, g * gs:(g + 1) * gs] for tau in range(t)], axis=1)
        y_sc[g] = _s5_group(u_g, wi_ref[g], win_ref[g], wout_ref[g], are_ref[g], aim_ref[g], skip_ref[g])
    for tau in range(t):
        y_ref[pl.ds(tau, nc, stride=t), :] = jnp.concatenate(
            [y_sc[g, :, tau * gs:(tau + 1) * gs] for g in range(S5_TILE_GROUPS)], axis=1)


def _s5(u, mats, skip, l):
    w_intra, w_in, w_out, a_re, a_im = mats
    t = S5_T
    gw = t * S5_GROUP
    tg = S5_TILE_GROUPS
    lane = tg * S5_GROUP
    skip_t = jnp.tile(skip.reshape(DEPTH, S5_GROUPS, 1, S5_GROUP), (1, 1, t, 1)).reshape(DEPTH, S5_GROUPS, 1, gw)
    return pl.pallas_call(
        _s5_kernel,
        grid=(BRANCH // lane,),
        in_specs=[pl.BlockSpec((SEQ, lane), lambda j: (0, j)),
                  pl.BlockSpec((None, tg, gw, gw), lambda j: (l, j, 0, 0)),
                  pl.BlockSpec((None, tg, gw, 4 * S5_STATE), lambda j: (l, j, 0, 0)),
                  pl.BlockSpec((None, tg, 4 * S5_STATE, gw), lambda j: (l, j, 0, 0)),
                  pl.BlockSpec((None, tg, 2, S5_LOG_CHUNKS, 2 * S5_STATE), lambda j: (l, j, 0, 0, 0)),
                  pl.BlockSpec((None, tg, 2, S5_LOG_CHUNKS, 2 * S5_STATE), lambda j: (l, j, 0, 0, 0)),
                  pl.BlockSpec((None, tg, 1, gw), lambda j: (l, j, 0, 0))],
        out_specs=pl.BlockSpec((SEQ, lane), lambda j: (0, j)),
        out_shape=jax.ShapeDtypeStruct((SEQ, BRANCH), F32),
        scratch_shapes=[pltpu.VMEM((tg, S5_CHUNKS, gw), F32)],
        compiler_params=_cparams(("parallel",)),
        name="s5_scan",
    )(u, w_intra, w_in, w_out, a_re, a_im, skip_t)


def _merge_kernel(hm_ref, hyt_ref, s5_ref, x_ref, wa_ref, wb_ref, wga_ref, wgg_ref, wg0_ref, wg1_ref, wg2_ref,
                  out_ref, hy_sc):
    @pl.when(pl.program_id(1) == 0)
    def _():
        hy_sc[...] = hyt_ref[...].T.astype(BF16)

    def proj(a, w_ref):
        return jnp.dot(a, w_ref[...].astype(BF16), preferred_element_type=F32)

    def gate(w_ref):
        pre = lax.dot_general(x_ref[...], w_ref[...], (((1,), (1,)), ((), ())), preferred_element_type=F32)
        return jax.nn.sigmoid(pre)

    s5b = s5_ref[...].astype(BF16)
    out_a = proj(hm_ref[...], wa_ref)
    out_b = proj(hy_sc[...], wb_ref)
    out_c = proj(s5b, wga_ref) * jax.nn.sigmoid(proj(s5b, wgg_ref))
    merged = gate(wg0_ref) * out_a + gate(wg1_ref) * out_b + gate(wg2_ref) * out_c
    out_ref[...] = merged.astype(out_ref.dtype)


def _merge(hm, hy_t, s5y, x_bf, w_tail, w_a, w_b, w_glu, l, *, tm=512, tn=512):
    nj = D_MODEL // tn
    goff = (OFF_MIXG - OFF_HYENA) // tn

    def gate_spec(g):
        return pl.BlockSpec((None, tn, D_MODEL), lambda i, j: (l, goff + g * nj + j, 0))

    return pl.pallas_call(
        _merge_kernel,
        grid=(SEQ // tm, nj),
        in_specs=[pl.BlockSpec((tm, BRANCH), lambda i, j: (i, 0)),
                  pl.BlockSpec((BRANCH, tm), lambda i, j: (0, i)),
                  pl.BlockSpec((tm, BRANCH), lambda i, j: (i, 0)),
                  pl.BlockSpec((tm, D_MODEL), lambda i, j: (i, 0)),
                  pl.BlockSpec((None, BRANCH, tn), lambda i, j: (l, 0, j)),
                  pl.BlockSpec((None, BRANCH, tn), lambda i, j: (l, 0, j)),
                  pl.BlockSpec((None, BRANCH, tn), lambda i, j: (l, 0, j)),
                  pl.BlockSpec((None, BRANCH, tn), lambda i, j: (l, 0, nj + j)),
                  gate_spec(0), gate_spec(1), gate_spec(2)],
        out_specs=pl.BlockSpec((tm, tn), lambda i, j: (i, j)),
        out_shape=jax.ShapeDtypeStruct((SEQ, D_MODEL), BF16),
        scratch_shapes=[pltpu.VMEM((tm, BRANCH), BF16)],
        compiler_params=_cparams(("parallel", "arbitrary")),
        name="branch_merge",
    )(hm, hy_t, s5y, x_bf, w_a, w_b, w_glu, w_glu, w_tail, w_tail, w_tail)


def _layer_norm_rows(y, g, b):
    mu = jnp.mean(y, axis=1, keepdims=True)
    var = jnp.mean(jnp.square(y - mu), axis=1, keepdims=True)
    return (y - mu) * lax.rsqrt(var + LN_EPS) * g + b


def _res_ln_kernel(x_ref, f_ref, g_ref, b_ref, o_ref, ob_ref, *, transposed):
    f = f_ref[...]
    if transposed:
        f = f.T
    y = _layer_norm_rows(ALPHA * x_ref[...] + f, g_ref[...], b_ref[...])
    o_ref[...] = y
    ob_ref[...] = y.astype(BF16)


def _res_ln(x, f, g, b, l, *, transposed, tm=256):
    f_spec = (pl.BlockSpec((D_MODEL, tm), lambda i: (0, i)) if transposed
              else pl.BlockSpec((tm, D_MODEL), lambda i: (i, 0)))
    n_tok = x.shape[0]
    return pl.pallas_call(
        functools.partial(_res_ln_kernel, transposed=transposed),
        grid=(n_tok // tm,),
        in_specs=[pl.BlockSpec((tm, D_MODEL), lambda i: (i, 0)), f_spec,
                  pl.BlockSpec((None, 1, D_MODEL), lambda i: (l, 0, 0)),
                  pl.BlockSpec((None, 1, D_MODEL), lambda i: (l, 0, 0))],
        out_specs=[pl.BlockSpec((tm, D_MODEL), lambda i: (i, 0)),
                   pl.BlockSpec((tm, D_MODEL), lambda i: (i, 0))],
        out_shape=[jax.ShapeDtypeStruct((n_tok, D_MODEL), F32), jax.ShapeDtypeStruct((n_tok, D_MODEL), BF16)],
        compiler_params=_cparams(("parallel",)),
        name="residual_layernorm",
    )(x, f, g.reshape(DEPTH, 1, D_MODEL), b.reshape(DEPTH, 1, D_MODEL))


PEER_NO_RANK = 64.0


def _extract_sorted(s, n, out_ref, base, want_rank=False):
    rank = jnp.full_like(s, PEER_NO_RANK) if want_rank else None
    for r in range(n):
        m = jnp.max(s, axis=0, keepdims=True)
        out_ref[base + r:base + r + 1, :] = m
        hit = s == m
        if want_rank:
            rank = jnp.where(hit, float(r), rank)
        s = jnp.where(hit, -jnp.inf, s)
    return rank


def _peer_select_kernel(q_ref, keys_ref, cnt_ref, r2_ref, e1_ref, e2_ref, top_sc, kth_sc):
    k = PEER_TOPK
    qb = q_ref[...].astype(BF16)
    half = PEER_KEYS
    for h in range(PEER_HEADS):
        sc = []
        rank2 = None
        for c in range(2):
            qs = qb[:, (2 * h + c) * half:(2 * h + c + 1) * half]
            s = lax.dot_general(keys_ref[h, c].astype(BF16), qs, (((1,), (1,)), ((), ())),
                                preferred_element_type=F32)
            sc.append(s)
            rank2 = _extract_sorted(s, k, top_sc, c * k, want_rank=(c == 1))
        a = top_sc[0:k, :]
        b = top_sc[k:2 * k, :]
        sub = lax.broadcasted_iota(jnp.int32, (8, a.shape[1]), 0)
        groups = [a[0:1] + b[0:8], a[0:1] + b[8:16], a[1:2] + b[0:8]]
        for i in range(2, 8):
            groups.append(jnp.where(sub < k // (i + 1), a[i:i + 1] + b[0:8], -jnp.inf))
        groups.append(a[8:16] + b[0:1])
        cand = jnp.concatenate(groups, axis=0)
        _extract_sorted(cand, k, kth_sc, 0)
        tau = kth_sc[k - 1:k, :]
        top = a[0:1] + b[0:1]
        z = jnp.sum(jnp.where(cand >= tau, jnp.exp(cand - top), 0.0), axis=0, keepdims=True)
        cnt = jnp.zeros_like(sc[0])
        for j in range(k):
            bj = top_sc[k + j:k + j + 1, :]
            cnt = jnp.where(sc[0] + bj >= tau, j + 1.0, cnt)
        cnt_ref[h] = cnt
        r2_ref[h] = rank2.astype(BF16)
        e1_ref[h] = jnp.exp(sc[0] - a[0:1])
        e2_ref[h] = (jnp.exp(sc[1] - b[0:1]) / z).astype(BF16)


def _peer_select(q, keys, l, *, tt=128):
    n_tok = q.shape[0]
    big = jax.ShapeDtypeStruct((PEER_HEADS, PEER_KEYS, n_tok), F32)
    big_bf = jax.ShapeDtypeStruct((PEER_HEADS, PEER_KEYS, n_tok), BF16)
    big_spec = pl.BlockSpec((PEER_HEADS, PEER_KEYS, tt), lambda i: (0, 0, i))
    return pl.pallas_call(
        _peer_select_kernel,
        grid=(n_tok // tt,),
        in_specs=[pl.BlockSpec((tt, D_MODEL), lambda i: (i, 0)),
                  pl.BlockSpec((None, PEER_HEADS, 2, PEER_KEYS, PEER_KEYS), lambda i: (l, 0, 0, 0, 0))],
        out_specs=[big_spec, big_spec, big_spec, big_spec],
        out_shape=[big, big_bf, big, big_bf],
        scratch_shapes=[pltpu.VMEM((2 * PEER_TOPK, tt), F32), pltpu.VMEM((PEER_TOPK, tt), F32)],
        compiler_params=_cparams(("parallel",)),
        name="peer_select",
    )(q, keys)


def _peer_dense_kernel(x_ref, u_ref, vt_ref, cnt_ref, r2_ref, e1_ref, e2_ref, out_ref, act_sc, g_sc, *, rows):
    e = pl.program_id(1)

    @pl.when(e == 0)
    def _():
        out_ref[...] = jnp.zeros_like(out_ref)

    act_sc[...] = lax.dot_general(u_ref[...], x_ref[...], (((1,), (1,)), ((), ())),
                                  preferred_element_type=F32)
    kk = PEER_KEYS
    for tc in range(act_sc.shape[1] // kk):
        lanes = slice(tc * kk, (tc + 1) * kk)
        for i in range(rows):
            w = jnp.zeros((kk, kk), BF16)
            for h in range(PEER_HEADS):
                hit = r2_ref[h, :, lanes] < cnt_ref[h, i:i + 1, lanes].astype(BF16)
                val = e2_ref[h, :, lanes] * e1_ref[h, i:i + 1, lanes].astype(BF16)
                w = w + jnp.where(hit, val, jnp.zeros_like(val))
            g_sc[i * kk:(i + 1) * kk, lanes] = jax.nn.gelu(act_sc[i * kk:(i + 1) * kk, lanes].astype(BF16)) * w
    out_ref[...] += jnp.dot(vt_ref[...], g_sc[...], preferred_element_type=F32)


def _transpose_tile_kernel(v_ref, o_ref):
    o_ref[...] = v_ref[...].T.astype(o_ref.dtype)


def _peer_value_tiles(v, l, te):
    return pl.pallas_call(
        _transpose_tile_kernel,
        grid=(N_EXPERTS // te,),
        in_specs=[pl.BlockSpec((None, te, D_MODEL), lambda e: (l, e, 0))],
        out_specs=pl.BlockSpec((None, D_MODEL, te), lambda e: (e, 0, 0)),
        out_shape=jax.ShapeDtypeStruct((N_EXPERTS // te, D_MODEL, te), BF16),
        compiler_params=_cparams(("parallel",)),
        name="peer_value_tiles",
    )(v)


def _peer_dense(x_bf, u_bf, v, l, sel, *, tt=512, rows=8):
    cnt, r2, e1, e2 = sel
    te = rows * PEER_KEYS
    n_tok = x_bf.shape[0]
    vt_bf = _peer_value_tiles(v, l, te)
    row_spec = pl.BlockSpec((PEER_HEADS, rows, tt), lambda i, e: (0, e, i))
    all_spec = pl.BlockSpec((PEER_HEADS, PEER_KEYS, tt), lambda i, e: (0, 0, i))
    return pl.pallas_call(
        functools.partial(_peer_dense_kernel, rows=rows),
        grid=(n_tok // tt, N_EXPERTS // te),
        in_specs=[pl.BlockSpec((tt, D_MODEL), lambda i, e: (i, 0)),
                  pl.BlockSpec((None, te, D_MODEL), lambda i, e: (l, e, 0)),
                  pl.BlockSpec((None, D_MODEL, te), lambda i, e: (e, 0, 0)),
                  row_spec, all_spec, row_spec, all_spec],
        out_specs=pl.BlockSpec((D_MODEL, tt), lambda i, e: (0, i)),
        out_shape=jax.ShapeDtypeStruct((D_MODEL, n_tok), F32),
        scratch_shapes=[pltpu.VMEM((te, tt), F32), pltpu.VMEM((te, tt), BF16)],
        compiler_params=_cparams(("parallel", "arbitrary")),
        name="peer_dense",
    )(x_bf, u_bf, vt_bf, cnt, r2, e1, e2)


def _hyena_features():
    pos = jnp.arange(SEQ, dtype=F32)
    t = pos / (SEQ - 1)
    bands = jnp.linspace(1e-4, 16 - 1, 16, dtype=F32)
    ang = 2.0 * math.pi * pos[:, None] * bands[None, :] / SEQ
    feat = jnp.concatenate([t[:, None], jnp.cos(ang), -jnp.sin(ang)], axis=-1)
    feat_t = jnp.pad(feat.T, ((0, HY_EMB_PAD - feat.shape[1]), (0, 0)))
    t_row = t[None, :]

    def reversed_time(a):
        return jnp.roll(a[:, ::-1], 1, axis=1)

    return (jnp.concatenate([feat_t, reversed_time(feat_t)], axis=1),
            jnp.concatenate([t_row, reversed_time(t_row)], axis=1))


def kernel(x, w_in, mlstm_gate_bias, mlstm_norm_gain, w_mlstm_out, hyena_conv_w, hyena_conv_b, hyena_w1,
           hyena_b1, hyena_w2, hyena_b2, hyena_freq, hyena_w3, hyena_decay, hyena_skip, w_hyena_out,
           s5_lambda_re, s5_lambda_im, s5_log_step, s5_b_re, s5_b_im, s5_c_re, s5_c_im, s5_skip, w_s5_glu,
           w_out, ln1_g, ln1_b, peer_w_q, peer_subkeys, peer_u, peer_v, ln2_g, ln2_b):
    xf = x.reshape(SEQ, D_MODEL)
    xb = xf.astype(BF16)
    feat_t, t_row = _hyena_features()
    w_in_t = jnp.swapaxes(w_in, 1, 2)
    w_tail = _w_in_tail(w_in_t)
    u_bf = peer_u.astype(BF16)
    s5_mats = jax.vmap(_s5_chunk_matrices)(s5_lambda_re, s5_lambda_im, s5_log_step, s5_b_re, s5_b_im,
                                           s5_c_re, s5_c_im)
    for l in range(DEPTH):
        qkvo = _mm_wt(xb, w_in_t, l, 0, 4 * BRANCH, tm=1024, tn=512, name="proj_qkvo")
        gates_t = _mm_nt(w_in_t, l, OFF_GATES, N_GATES, xb, tr=N_GATES, tm=1024, name="proj_gates", out_dtype=F32)
        hp_t = _mm_nt(w_tail, l, 0, 3 * BRANCH, xb, tr=512, tm=1024, name="proj_hyena_t", out_dtype=BF16)
        s5_in = _mm_wt(xb, w_tail, l, 3 * BRANCH, BRANCH, tm=1024, tn=512, name="proj_s5")

        hdir = _mlstm(qkvo, gates_t, mlstm_gate_bias, l)
        hm = _mlstm_post(hdir, qkvo, mlstm_norm_gain, l)
        hy_t = _hyena(hp_t, l, hyena_conv_w, hyena_conv_b, hyena_w1, hyena_b1, hyena_w2, hyena_b2,
                      hyena_freq, hyena_w3, hyena_decay, hyena_skip, feat_t, t_row)
        s5y = _s5(s5_in, s5_mats, s5_skip, l)
        merged = _merge(hm, hy_t, s5y, xb, w_tail, w_mlstm_out, w_hyena_out, w_s5_glu, l)
        mix = _mm(merged, w_out, l, 0, D_MODEL, tm=1024, tn=512, name="proj_out")
        xf, xb = _res_ln(xf, mix, ln1_g, ln1_b, l, transposed=False)

        q = _mm(xb, peer_w_q, l, 0, PEER_HEADS * 2 * PEER_KEYS, tm=1024, tn=512, name="peer_query")
        sel = _peer_select(q, peer_subkeys, l)
        ffn_t = _peer_dense(xb, u_bf, peer_v, l, sel)
        xf, xb = _res_ln(xf, ffn_t, ln2_g, ln2_b, l, transposed=True)
    return xf.reshape(1, SEQ, D_MODEL)
```

```python
import functools
import math

import numpy as np
import jax
import jax.numpy as jnp
from jax import lax
from jax.experimental import pallas as pl
from jax.experimental.pallas import tpu as pltpu

F32 = jnp.float32
BF16 = jnp.bfloat16

D_MODEL = 2048
SEQ = 8192
DEPTH = 4
BRANCH = 1024
HEADS = 4
HEAD_DIM = 256
CHUNK = 256
N_GATES = 16
HY_HID = 64
HY_EMB_PAD = 128
S5_GROUP = 16
S5_GROUPS = 64
S5_STATE = 64
S5_T = 16
S5_CHUNKS = SEQ // S5_T
S5_LOG_CHUNKS = 9
PEER_HEADS = 8
PEER_KEYS = 128
PEER_TOPK = 16
N_EXPERTS = PEER_KEYS * PEER_KEYS
ALPHA = (2 * DEPTH) ** 0.25
LN_EPS = 1e-5
FFT_N = 2 * SEQ
FFT_R = 128
OFF_GATES = 4 * BRANCH
OFF_HYENA = OFF_GATES + N_GATES
OFF_S5 = OFF_HYENA + 3 * BRANCH
OFF_MIXG = OFF_S5 + BRANCH

VMEM_LIMIT = 56 * 1024 * 1024


def _cparams(sem):
    return pltpu.CompilerParams(dimension_semantics=sem, vmem_limit_bytes=VMEM_LIMIT)


def _split_bf16(a):
    hi = a.astype(BF16)
    lo = (a - hi.astype(F32)).astype(BF16)
    return hi, lo


def _dot3(a, b_hi, b_lo):
    a_hi, a_lo = _split_bf16(a)
    acc = jnp.dot(a_hi, b_hi, preferred_element_type=F32)
    acc += jnp.dot(a_lo, b_hi, preferred_element_type=F32)
    acc += jnp.dot(a_hi, b_lo, preferred_element_type=F32)
    return acc


def _dot3_both(a, b):
    b_hi, b_lo = _split_bf16(b)
    return _dot3(a, b_hi, b_lo)


def _mm_kernel(a_ref, b_ref, o_ref):
    o_ref[...] = jnp.dot(a_ref[...].astype(BF16), b_ref[...].astype(BF16),
                         preferred_element_type=F32).astype(o_ref.dtype)


def _mm(a, w, l, col_off, n_cols, *, tm, tn, name):
    m, k = a.shape
    cb = col_off // tn
    assert col_off % tn == 0 and n_cols % tn == 0 and m % tm == 0
    return pl.pallas_call(
        _mm_kernel,
        grid=(m // tm, n_cols // tn),
        in_specs=[pl.BlockSpec((tm, k), lambda i, j: (i, 0)),
                  pl.BlockSpec((None, k, tn), lambda i, j: (l, 0, cb + j))],
        out_specs=pl.BlockSpec((tm, tn), lambda i, j: (i, j)),
        out_shape=jax.ShapeDtypeStruct((m, n_cols), F32),
        compiler_params=_cparams(("parallel", "parallel")),
        name=name,
    )(a, w)


def _mm_wt_kernel(a_ref, w_ref, o_ref):
    o_ref[...] = lax.dot_general(a_ref[...].astype(BF16), w_ref[...].astype(BF16), (((1,), (1,)), ((), ())),
                                 preferred_element_type=F32).astype(o_ref.dtype)


def _mm_wt(a, wt, l, row_off, n_rows, *, tm, tn, name):
    m, k = a.shape
    rb = row_off // tn
    assert row_off % tn == 0 and n_rows % tn == 0 and m % tm == 0
    return pl.pallas_call(
        _mm_wt_kernel,
        grid=(m // tm, n_rows // tn),
        in_specs=[pl.BlockSpec((tm, k), lambda i, j: (i, 0)),
                  pl.BlockSpec((None, tn, k), lambda i, j: (l, rb + j, 0))],
        out_specs=pl.BlockSpec((tm, tn), lambda i, j: (i, j)),
        out_shape=jax.ShapeDtypeStruct((m, n_rows), F32),
        compiler_params=_cparams(("parallel", "parallel")),
        name=name,
    )(a, wt)


def _mm_nt_kernel(w_ref, a_ref, o_ref):
    o_ref[...] = lax.dot_general(w_ref[...].astype(BF16), a_ref[...].astype(BF16), (((1,), (1,)), ((), ())),
                                 preferred_element_type=F32).astype(o_ref.dtype)


def _mm_nt(wt, l, row_off, n_rows, a, *, tr, tm, name, out_dtype):
    k = wt.shape[2]
    m = a.shape[0]
    rb = row_off // tr
    assert row_off % tr == 0 and n_rows % tr == 0
    return pl.pallas_call(
        _mm_nt_kernel,
        grid=(n_rows // tr, m // tm),
        in_specs=[pl.BlockSpec((None, tr, k), lambda i, j: (l, rb + i, 0)),
                  pl.BlockSpec((tm, k), lambda i, j: (j, 0))],
        out_specs=pl.BlockSpec((tr, tm), lambda i, j: (i, j)),
        out_shape=jax.ShapeDtypeStruct((n_rows, m), out_dtype),
        compiler_params=_cparams(("parallel", "parallel")),
        name=name,
    )(wt, a)


def _w_tail_kernel(cur_ref, nxt_ref, o_ref, *, shift):
    o_ref[...] = jnp.concatenate([cur_ref[shift:, :], nxt_ref[...]], axis=0).astype(o_ref.dtype)


def _w_in_tail(w_in_t, *, tn=512):
    depth, n_all, k = w_in_t.shape
    n_out = n_all - OFF_HYENA
    return pl.pallas_call(
        functools.partial(_w_tail_kernel, shift=N_GATES),
        grid=(depth, n_out // tn),
        in_specs=[pl.BlockSpec((None, tn, k), lambda l, j: (l, OFF_GATES // tn + j, 0)),
                  pl.BlockSpec((None, N_GATES, k), lambda l, j: (l, (OFF_GATES + (j + 1) * tn) // N_GATES, 0))],
        out_specs=pl.BlockSpec((None, tn, k), lambda l, j: (l, j, 0)),
        out_shape=jax.ShapeDtypeStruct((depth, n_out, k), BF16),
        compiler_params=_cparams(("parallel", "parallel")),
        name="w_in_tail",
    )(w_in_t, w_in_t)


def _log_sigmoid(x):
    return jnp.minimum(x, 0.0) - jnp.log1p(jnp.exp(-jnp.abs(x)))


def _mlstm_chunk(d, q, k, v, li_r, lf_r, li_c, lf_c, c_sc, n_sc, m_sc):
    n = CHUNK
    row = lax.broadcasted_iota(jnp.int32, (n, n), 0)
    col = lax.broadcasted_iota(jnp.int32, (n, n), 1)
    valid = (row >= col) if d == 0 else (row <= col)
    valid_t = (col >= row) if d == 0 else (col <= row)
    b_col = jnp.sum(jnp.where(valid, lf_r, 0.0), axis=1, keepdims=True)
    b_row = jnp.sum(jnp.where(valid_t, lf_c, 0.0), axis=0, keepdims=True)
    total = jnp.sum(lf_r, axis=1, keepdims=True)

    m_prev = m_sc[...]
    m_inter = b_col + m_prev
    log_d = jnp.where(valid, b_col - b_row + li_r, -jnp.inf)
    m_t = jnp.maximum(m_inter, jnp.max(log_d, axis=1, keepdims=True))
    inter = jnp.exp(m_inter - m_t)
    dmat = jnp.exp(log_d - m_t)

    k = k * (HEAD_DIM ** -0.5)
    qb = q.astype(BF16)
    kb = k.astype(BF16)
    vb = v.astype(BF16)
    s = lax.dot_general(qb, kb, (((1,), (1,)), ((), ())), preferred_element_type=F32) * dmat
    num = jnp.dot(s.astype(BF16), vb, preferred_element_type=F32)
    num += inter * jnp.dot(qb, c_sc[...].astype(BF16), preferred_element_type=F32)
    qn = jnp.sum(q * n_sc[...], axis=1, keepdims=True)
    den = jnp.sum(s, axis=1, keepdims=True) + inter * qn
    h_out = num / jnp.maximum(jnp.abs(den), jnp.exp(-m_t))

    log_w = total - b_col + li_c
    m_new = jnp.maximum(total + m_prev, jnp.max(log_w, axis=0, keepdims=True))
    w = jnp.exp(log_w - m_new)
    decay = jnp.exp(total + m_prev - m_new)
    kw = k * w
    c_sc[...] = decay * c_sc[...] + lax.dot_general(
        kw.astype(BF16), vb, (((0,), (0,)), ((), ())), preferred_element_type=F32)
    n_sc[...] = decay * n_sc[...] + jnp.sum(kw, axis=0, keepdims=True)
    m_sc[...] = m_new
    return h_out


def _mlstm_kernel(qkv_f_ref, qkv_b_ref, gr_f_ref, gr_b_ref, gc_f_ref, gc_b_ref, bias_ref,
                  o_f_ref, o_b_ref, c_sc, n_sc, m_sc):
    @pl.when(pl.program_id(0) == 0)
    def _():
        c_sc[...] = jnp.zeros_like(c_sc)
        n_sc[...] = jnp.zeros_like(n_sc)
        m_sc[...] = jnp.zeros_like(m_sc)

    for d, (qkv_ref, gr_ref, gc_ref, o_ref) in enumerate(((qkv_f_ref, gr_f_ref, gc_f_ref, o_f_ref),
                                                         (qkv_b_ref, gr_b_ref, gc_b_ref, o_b_ref))):
        for h in range(HEADS):
            cols = slice(h * HEAD_DIM, (h + 1) * HEAD_DIM)
            b_i = bias_ref[d * 2 * HEADS + h]
            b_f = bias_ref[d * 2 * HEADS + HEADS + h]
            chain = d * HEADS + h
            o_ref[:, cols] = _mlstm_chunk(
                d,
                qkv_ref[:, cols],
                qkv_ref[:, BRANCH + h * HEAD_DIM:BRANCH + (h + 1) * HEAD_DIM],
                qkv_ref[:, 2 * BRANCH + h * HEAD_DIM:2 * BRANCH + (h + 1) * HEAD_DIM],
                gr_ref[h] + b_i, _log_sigmoid(gr_ref[HEADS + h] + b_f),
                gc_ref[h] + b_i, _log_sigmoid(gc_ref[HEADS + h] + b_f),
                c_sc.at[chain], n_sc.at[chain], m_sc.at[chain])


def _mlstm(qkvo, gates_t, gate_bias, l):
    nc = SEQ // CHUNK
    g_rows = gates_t.reshape(N_GATES, 1, SEQ)
    g_cols = gates_t.reshape(N_GATES, SEQ, 1)
    bias = gate_bias.reshape(DEPTH, N_GATES, 1, 1)
    half = N_GATES // 2
    chains = 2 * HEADS
    out = jax.ShapeDtypeStruct((SEQ, BRANCH), F32)
    return pl.pallas_call(
        _mlstm_kernel,
        grid=(nc,),
        in_specs=[
            pl.BlockSpec((CHUNK, 4 * BRANCH), lambda c: (c, 0)),
            pl.BlockSpec((CHUNK, 4 * BRANCH), lambda c: (nc - 1 - c, 0)),
            pl.BlockSpec((half, 1, CHUNK), lambda c: (0, 0, c)),
            pl.BlockSpec((half, 1, CHUNK), lambda c: (1, 0, nc - 1 - c)),
            pl.BlockSpec((half, CHUNK, 1), lambda c: (0, c, 0)),
            pl.BlockSpec((half, CHUNK, 1), lambda c: (1, nc - 1 - c, 0)),
            pl.BlockSpec((None, N_GATES, 1, 1), lambda c: (l, 0, 0, 0)),
        ],
        out_specs=[pl.BlockSpec((CHUNK, BRANCH), lambda c: (c, 0)),
                   pl.BlockSpec((CHUNK, BRANCH), lambda c: (nc - 1 - c, 0))],
        out_shape=[out, out],
        scratch_shapes=[pltpu.VMEM((chains, HEAD_DIM, HEAD_DIM), F32),
                        pltpu.VMEM((chains, 1, HEAD_DIM), F32),
                        pltpu.VMEM((chains, 1, 1), F32)],
        compiler_params=_cparams(("arbitrary",)),
        name="mlstm_scan",
    )(qkvo, qkvo, g_rows, g_rows, g_cols, g_cols, bias)


def _mlstm_post_kernel(hf_ref, hb_ref, o_ref, gain_ref, out_ref):
    h = hf_ref[...] + hb_ref[...]
    h = jax.nn.sigmoid(o_ref[...]) * h
    for hd in range(HEADS):
        sl = slice(hd * HEAD_DIM, (hd + 1) * HEAD_DIM)
        hh = h[:, sl]
        mu = jnp.mean(hh, axis=1, keepdims=True)
        var = jnp.mean(jnp.square(hh - mu), axis=1, keepdims=True)
        out_ref[:, sl] = ((hh - mu) * lax.rsqrt(var + LN_EPS) * gain_ref[:, sl]).astype(out_ref.dtype)


def _mlstm_post(hdir, qkvo, gain, l, *, tm=512):
    return pl.pallas_call(
        _mlstm_post_kernel,
        grid=(SEQ // tm,),
        in_specs=[pl.BlockSpec((tm, BRANCH), lambda i: (i, 0)),
                  pl.BlockSpec((tm, BRANCH), lambda i: (i, 0)),
                  pl.BlockSpec((tm, BRANCH), lambda i: (i, 3)),
                  pl.BlockSpec((None, 1, BRANCH), lambda i: (l, 0, 0))],
        out_specs=pl.BlockSpec((tm, BRANCH), lambda i: (i, 0)),
        out_shape=jax.ShapeDtypeStruct((SEQ, BRANCH), BF16),
        compiler_params=_cparams(("parallel",)),
        name="mlstm_post",
    )(hdir[0], hdir[1], qkvo, gain.reshape(DEPTH, 1, BRANCH))


HY_CT = 32
HY_RB = 512


@functools.lru_cache(maxsize=None)
def _fft_constants():
    r = FFT_R
    idx = np.arange(r)
    th = 2.0 * np.pi * np.outer(idx, idx) / r
    cs, sn = np.cos(th), np.sin(th)
    ph = 2.0 * np.pi * np.outer(idx, idx) / FFT_N
    f1 = np.concatenate([cs, -sn], axis=1)
    m3 = np.block([[cs, -sn], [sn, cs]])
    m4 = np.block([[cs, sn], [-sn, cs]])
    m6 = np.concatenate([cs, -sn], axis=0) / FFT_N
    m6[:, r // 2:] = 0.0
    consts = {name: jnp.asarray(mat, F32).astype(BF16)
              for name, mat in (("f1", f1), ("m3", m3), ("m4", m4), ("m6", m6))}
    consts["tw"] = (jnp.asarray(np.cos(ph), F32), jnp.asarray(np.sin(ph), F32))
    return consts


def _rows(i, n):
    return pl.ds(pl.multiple_of(i * n, n), n)


def _for_each(n, body, unroll=1):
    def step(i, carry):
        body(i)
        return carry

    lax.fori_loop(0, n, step, 0, unroll=unroll)


def _fwd_stage1(sig_ref, hi_ref, ct, t_buf, a_buf, b_buf, f1_ref, twr_ref, twi_ref):
    r = FFT_R
    zeros = jnp.zeros((r // 2, r), F32)

    def tr_in(c):
        hi = zeros if hi_ref is None else hi_ref[c]
        t_buf[_rows(c, r), :] = jnp.concatenate([sig_ref[c], hi], axis=0).T.astype(BF16)

    _for_each(ct, tr_in, unroll=4)

    def stage1(b):
        rows = _rows(b, HY_RB)
        a_buf[rows, :] = jnp.dot(t_buf[rows, :], f1_ref[...], preferred_element_type=F32)

    _for_each(ct * r // HY_RB, stage1)

    def twiddle(c):
        rows = _rows(c, r)
        a = a_buf[rows, :]
        ar, ai = a[:, :r], a[:, r:]
        twr, twi = twr_ref[...], twi_ref[...]
        b_buf[rows, :r] = (ar * twr + ai * twi).T.astype(BF16)
        b_buf[rows, r:] = (ai * twr - ar * twi).T.astype(BF16)

    _for_each(ct, twiddle, unroll=4)


def _hy_mlp_kernel(feat_ref, w1_ref, b1_ref, w2_ref, b2_ref, fr_ref, out_ref):
    h = _dot3_both(w1_ref[...], feat_ref[...]) + b1_ref[...]
    h = jnp.sin(fr_ref[0] * h)
    h = _dot3_both(w2_ref[...], h) + b2_ref[...]
    out_ref[...] = jnp.sin(fr_ref[1] * h)


def _hy_filter_kernel(w3_ref, h_ref, dec_ref, t_ref, out_ref):
    d = pl.program_id(0)
    filt = _dot3_both(w3_ref[...], h_ref[...])
    filt = filt * jnp.exp(-t_ref[...] * jnp.abs(dec_ref[...]))
    filt = filt / (jnp.sum(jnp.abs(filt), axis=1, keepdims=True) + 1e-6)
    lane = lax.broadcasted_iota(jnp.int32, filt.shape, 1)
    out_ref[...] = jnp.where((lane == 0) & (d == 1), 0.0, filt)


def _hy_spec_kernel(fa_ref, fb_ref, f1_ref, m3_ref, twr_ref, twi_ref, out_ref, t_buf, a_buf, b_buf, *, ct):
    r = FFT_R
    nblk = ct * r // HY_RB
    _fwd_stage1(fa_ref, fb_ref, ct, t_buf, a_buf, b_buf, f1_ref, twr_ref, twi_ref)

    def put(b):
        rows = _rows(b, HY_RB)
        out_ref[rows, :] = jnp.dot(b_buf[rows, :], m3_ref[...], preferred_element_type=F32)

    _for_each(nblk, put)


def _shift_prev(p, n1, n2):
    r1 = pltpu.roll(p, 1, 2)
    r2 = pltpu.roll(r1, 1, 1)
    return jnp.where(n2 == 0, jnp.where(n1 == 0, 0.0, r2), r1)


def _shift_next(p, n1, n2):
    s1, s2 = p.shape[1], p.shape[2]
    r1 = pltpu.roll(p, s2 - 1, 2)
    r2 = pltpu.roll(r1, s1 - 1, 1)
    return jnp.where(n2 == s2 - 1, jnp.where(n1 == s1 - 1, 0.0, r2), r1)


def _hy_conv_kernel(p0_ref, p1_ref, p2_ref, cw_ref, cb_ref, skip_ref, h_ref,
                    f1_ref, m3_ref, m4_ref, m6_ref, twr_ref, twi_ref,
                    out_ref, z_buf, t_buf, a_buf, b_buf, *, ct):
    r = FFT_R
    n1_len = SEQ // r
    nblk = ct * r // HY_RB
    shp = (ct, n1_len, r)
    n1 = lax.broadcasted_iota(jnp.int32, shp, 1)
    n2 = lax.broadcasted_iota(jnp.int32, shp, 2)

    def short_conv(p_ref, g):
        p = p_ref[...].astype(F32)
        return (_shift_prev(p, n1, n2) * cw_ref[3 * g] + p * cw_ref[3 * g + 1]
                + _shift_next(p, n1, n2) * cw_ref[3 * g + 2] + cb_ref[g])

    z_buf[...] = short_conv(p1_ref, 1) * short_conv(p2_ref, 2)
    _fwd_stage1(z_buf, None, ct, t_buf, a_buf, b_buf, f1_ref, twr_ref, twi_ref)

    def freq_domain(b):
        rows = _rows(b, HY_RB)
        x = jnp.dot(b_buf[rows, :], m3_ref[...], preferred_element_type=F32)
        h = h_ref[rows, :]
        xr, xi = x[:, :r], x[:, r:]
        hr, hi = h[:, :r], h[:, r:]
        y = jnp.concatenate([xr * hr - xi * hi, xr * hi + xi * hr], axis=1).astype(BF16)
        a_buf[rows, :] = jnp.dot(y, m4_ref[...], preferred_element_type=F32)

    _for_each(nblk, freq_domain)

    def inv_twiddle(c):
        rows = _rows(c, r)
        b = a_buf[rows, :]
        br, bi = b[:, :r], b[:, r:]
        twr, twi = twr_ref[...], twi_ref[...]
        b_buf[rows, :r] = (br * twr - bi * twi).T.astype(BF16)
        b_buf[rows, r:] = (br * twi + bi * twr).T.astype(BF16)

    _for_each(ct, inv_twiddle, unroll=4)

    def stage6(b):
        rows = _rows(b, HY_RB)
        a_buf[rows, :r] = jnp.dot(b_buf[rows, :], m6_ref[...], preferred_element_type=F32)

    _for_each(nblk, stage6)

    def finish(c):
        y_c = a_buf[_rows(c, r), :r].T[:n1_len, :]
        out_ref[c] = y_c + skip_ref[c] * z_buf[c]

    _for_each(ct, finish, unroll=4)
    out_ref[...] = short_conv(p0_ref, 0) * out_ref[...]


def _hyena(hp_t, l, conv_w, conv_b, w1, b1, w2, b2, freq, w3, decay, skip, feat_t, t_row):
    r = FFT_R
    n1 = SEQ // r
    ct = HY_CT
    k = _fft_constants()
    f1, m3, m4, m6 = k["f1"], k["m3"], k["m4"], k["m6"]
    twr, twi = k["tw"]

    def full(a):
        return pl.BlockSpec(a.shape, lambda *_: (0,) * a.ndim)

    h2 = pl.pallas_call(
        _hy_mlp_kernel,
        grid=(2,),
        in_specs=[pl.BlockSpec((feat_t.shape[0], SEQ), lambda i: (0, i)),
                  pl.BlockSpec((None, HY_HID, feat_t.shape[0]), lambda i: (l, 0, 0)),
                  pl.BlockSpec((None, HY_HID, 1), lambda i: (l, 0, 0)),
                  pl.BlockSpec((None, HY_HID, HY_HID), lambda i: (l, 0, 0)),
                  pl.BlockSpec((None, HY_HID, 1), lambda i: (l, 0, 0)),
                  pl.BlockSpec((None, 2, HY_HID, 1), lambda i: (l, 0, 0, 0))],
        out_specs=pl.BlockSpec((HY_HID, SEQ), lambda i: (0, i)),
        out_shape=jax.ShapeDtypeStruct((HY_HID, 2 * SEQ), F32),
        compiler_params=_cparams(("parallel",)),
        name="hyena_filter_mlp",
    )(feat_t, jnp.pad(jnp.swapaxes(w1, 1, 2), ((0, 0), (0, 0), (0, feat_t.shape[0] - w1.shape[1]))),
      b1[:, :, None], jnp.swapaxes(w2, 1, 2), b2[:, :, None], freq[:, :, :, None])

    rt = 128
    filt = pl.pallas_call(
        _hy_filter_kernel,
        grid=(2, BRANCH // rt),
        in_specs=[pl.BlockSpec((None, rt, HY_HID), lambda d, i: (l, d * (BRANCH // rt) + i, 0)),
                  pl.BlockSpec((HY_HID, SEQ), lambda d, i: (0, d)),
                  pl.BlockSpec((None, None, rt, 1), lambda d, i: (l, d, i, 0)),
                  pl.BlockSpec((1, SEQ), lambda d, i: (0, d))],
        out_specs=pl.BlockSpec((None, rt, SEQ), lambda d, i: (d, i, 0)),
        out_shape=jax.ShapeDtypeStruct((2, BRANCH, SEQ), F32),
        compiler_params=_cparams(("parallel", "parallel")),
        name="hyena_filter",
    )(jnp.swapaxes(w3, 1, 2), h2, decay[:, :, :, None], t_row)
    filt = filt.reshape(2, BRANCH, n1, r)

    scratch = [pltpu.VMEM((ct * r, r), BF16), pltpu.VMEM((ct * r, 2 * r), F32), pltpu.VMEM((ct * r, 2 * r), BF16)]
    spec = pl.pallas_call(
        functools.partial(_hy_spec_kernel, ct=ct),
        grid=(BRANCH // ct,),
        in_specs=[pl.BlockSpec((None, ct, n1, r), lambda i: (0, i, 0, 0)),
                  pl.BlockSpec((None, ct, n1, r), lambda i: (1, i, 0, 0)),
                  full(f1), full(m3), full(twr), full(twi)],
        out_specs=pl.BlockSpec((ct * r, 2 * r), lambda i: (i, 0)),
        out_shape=jax.ShapeDtypeStruct((BRANCH * r, 2 * r), F32),
        scratch_shapes=scratch,
        compiler_params=_cparams(("parallel",)),
        name="hyena_filter_spectrum",
    )(filt, filt, f1, m3, twr, twi)

    hp3 = hp_t.reshape(3 * BRANCH, n1, r)
    nb = BRANCH // ct
    cw = conv_w.reshape(DEPTH, 3, 3, BRANCH).transpose(0, 2, 1, 3).reshape(DEPTH, 9, BRANCH, 1, 1)
    cb = conv_b.reshape(DEPTH, 3, BRANCH, 1, 1)
    consts = (f1, m3, m4, m6, twr, twi)
    return pl.pallas_call(
        functools.partial(_hy_conv_kernel, ct=ct),
        grid=(nb,),
        in_specs=[pl.BlockSpec((ct, n1, r), lambda i: (i, 0, 0)),
                  pl.BlockSpec((ct, n1, r), lambda i: (nb + i, 0, 0)),
                  pl.BlockSpec((ct, n1, r), lambda i: (2 * nb + i, 0, 0)),
                  pl.BlockSpec((None, 9, ct, 1, 1), lambda i: (l, 0, i, 0, 0)),
                  pl.BlockSpec((None, 3, ct, 1, 1), lambda i: (l, 0, i, 0, 0)),
                  pl.BlockSpec((None, ct, 1, 1), lambda i: (l, i, 0, 0)),
                  pl.BlockSpec((ct * r, 2 * r), lambda i: (i, 0)),
                  *[full(a) for a in consts]],
        out_specs=pl.BlockSpec((ct, n1, r), lambda i: (i, 0, 0)),
        out_shape=jax.ShapeDtypeStruct((BRANCH, n1, r), F32),
        scratch_shapes=[pltpu.VMEM((ct, n1, r), F32)] + scratch,
        compiler_params=_cparams(("parallel",)),
        name="hyena_conv",
    )(hp3, hp3, hp3, cw, cb, skip.reshape(DEPTH, BRANCH, 1, 1), spec, *consts).reshape(BRANCH, SEQ)


def _s5_chunk_matrices(lam_re, lam_im, log_step, b_re, b_im, c_re, c_im):
    t = S5_T
    lam = lax.complex(lam_re, lam_im)
    step = jnp.exp(log_step)[..., None]
    lam_dt = lam * step
    b_bar = ((jnp.exp(lam_dt) - 1.0) / lam)[..., None] * lax.complex(b_re, b_im)
    c_mat = lax.complex(c_re, c_im)
    pw = jnp.exp(lam_dt[None] * jnp.arange(t + 1, dtype=F32)[:, None, None, None])
    kern = jnp.einsum("dgvp,mdgp,dgpn->mdgvn", c_mat, pw[:t], b_bar, precision="highest").real
    lag = np.arange(t)[:, None] - np.arange(t)[None, :]
    pick = np.stack([lag[:, :, None] == np.arange(t), -lag[:, :, None] == np.arange(t)], axis=-1)
    kfb = jnp.einsum("ajmd,mdgvn->ajgvn", jnp.asarray(pick, F32), kern, precision="highest")
    w_intra = jnp.transpose(kfb, (2, 1, 4, 0, 3)).reshape(S5_GROUPS, t * S5_GROUP, t * S5_GROUP)
    in_f = pw[:t, 0][::-1][..., None] * b_bar[0][None]
    in_b = pw[:t, 1][..., None] * b_bar[1][None]

    def in_mat(m):
        m = jnp.transpose(m, (1, 0, 3, 2)).reshape(S5_GROUPS, t * S5_GROUP, S5_STATE)
        return jnp.concatenate([m.real, m.imag], axis=-1)

    w_in = jnp.concatenate([in_mat(in_f), in_mat(in_b)], axis=-1)
    out_f = c_mat[0][None] * pw[1:, 0][:, :, None, :]
    out_b = c_mat[1][None] * pw[1:, 1][::-1][:, :, None, :]

    def out_mat(m):
        m = jnp.transpose(m, (1, 3, 0, 2)).reshape(S5_GROUPS, S5_STATE, t * S5_GROUP)
        return jnp.concatenate([m.real, -m.imag], axis=1)

    w_out = jnp.concatenate([out_mat(out_f), out_mat(out_b)], axis=1)
    a_pow = jnp.exp(lam_dt[None] * (t * 2.0 ** jnp.arange(S5_LOG_CHUNKS, dtype=F32))[:, None, None, None])
    a_pow = jnp.transpose(a_pow, (2, 1, 0, 3))
    a_re = jnp.concatenate([a_pow.real, a_pow.real], axis=-1)
    a_im = jnp.concatenate([-a_pow.imag, a_pow.imag], axis=-1)
    return w_intra, w_in, w_out, a_re, a_im


S5_TILE_GROUPS = 128 // S5_GROUP


def _s5_group(ub, wi, win, wout, a_re, a_im):
    nc = S5_CHUNKS
    half = S5_STATE
    y = jnp.dot(ub, wi.astype(BF16), preferred_element_type=F32)
    s_in = jnp.dot(ub, win.astype(BF16), preferred_element_type=F32)
    row = lax.broadcasted_iota(jnp.int32, (nc, 2 * half), 0)

    def scan(s, d):
        for step in range(S5_LOG_CHUNKS):
            sh = 1 << step
            if d == 0:
                moved = jnp.where(row >= sh, pltpu.roll(s, sh, 0), 0.0)
            else:
                moved = jnp.where(row < nc - sh, pltpu.roll(s, nc - sh, 0), 0.0)
            swapped = pltpu.roll(moved, half, 1)
            s = s + a_re[d, step:step + 1, :] * moved + a_im[d, step:step + 1, :] * swapped
        if d == 0:
            return jnp.where(row >= 1, pltpu.roll(s, 1, 0), 0.0)
        return jnp.where(row < nc - 1, pltpu.roll(s, nc - 1, 0), 0.0)

    e = jnp.concatenate([scan(s_in[:, :2 * half], 0), scan(s_in[:, 2 * half:], 1)], axis=1)
    return y + jnp.dot(e.astype(BF16), wout.astype(BF16), preferred_element_type=F32)


def _s5_kernel(u_ref, wi_ref, win_ref, wout_ref, are_ref, aim_ref, skip_ref, y_ref, x_sc, y_sc):
    t, gs, nc = S5_T, S5_GROUP, S5_CHUNKS
    for tau in range(t):
        x_sc[tau] = u_ref[pl.ds(tau, nc, stride=t), :].astype(BF16)
    for g in range(S5_TILE_GROUPS):
        u_g = jnp.concatenate([x_sc[tau, :, g * gs:(g + 1) * gs] for tau in range(t)], axis=1)
        y_sc[g] = _s5_group(u_g, wi_ref[g], win_ref[g], wout_ref[g], are_ref[g], aim_ref[g]).astype(BF16)
    for tau in range(t):
        rows = pl.ds(tau, nc, stride=t)
        y_mat = jnp.concatenate([y_sc[g, :, tau * gs:(tau + 1) * gs] for g in range(S5_TILE_GROUPS)], axis=1)
        y_ref[rows, :] = y_mat.astype(F32) + skip_ref[...] * u_ref[rows, :]


def _s5(u, mats, skip, l):
    w_intra, w_in, w_out, a_re, a_im = mats
    t = S5_T
    gw = t * S5_GROUP
    tg = S5_TILE_GROUPS
    lane = tg * S5_GROUP
    return pl.pallas_call(
        _s5_kernel,
        grid=(BRANCH // lane,),
        in_specs=[pl.BlockSpec((SEQ, lane), lambda j: (0, j)),
                  pl.BlockSpec((None, tg, gw, gw), lambda j: (l, j, 0, 0)),
                  pl.BlockSpec((None, tg, gw, 4 * S5_STATE), lambda j: (l, j, 0, 0)),
                  pl.BlockSpec((None, tg, 4 * S5_STATE, gw), lambda j: (l, j, 0, 0)),
                  pl.BlockSpec((None, tg, 2, S5_LOG_CHUNKS, 2 * S5_STATE), lambda j: (l, j, 0, 0, 0)),
                  pl.BlockSpec((None, tg, 2, S5_LOG_CHUNKS, 2 * S5_STATE), lambda j: (l, j, 0, 0, 0)),
                  pl.BlockSpec((None, 1, lane), lambda j: (l, 0, j))],
        out_specs=pl.BlockSpec((SEQ, lane), lambda j: (0, j)),
        out_shape=jax.ShapeDtypeStruct((SEQ, BRANCH), F32),
        scratch_shapes=[pltpu.VMEM((t, S5_CHUNKS, lane), BF16), pltpu.VMEM((tg, S5_CHUNKS, gw), BF16)],
        compiler_params=_cparams(("parallel",)),
        name="s5_scan",
    )(u, w_intra, w_in, w_out, a_re, a_im, skip.reshape(DEPTH, 1, BRANCH))


def _merge_kernel(hm_ref, hyt_ref, s5_ref, x_ref, wa_ref, wb_ref, wga_ref, wgg_ref, wg0_ref, wg1_ref, wg2_ref,
                  out_ref, hy_sc):
    @pl.when(pl.program_id(1) == 0)
    def _():
        hy_sc[...] = hyt_ref[...].T.astype(BF16)

    def proj(a, w_ref):
        return jnp.dot(a, w_ref[...].astype(BF16), preferred_element_type=F32)

    def gate(w_ref):
        pre = lax.dot_general(x_ref[...], w_ref[...], (((1,), (1,)), ((), ())), preferred_element_type=F32)
        return jax.nn.sigmoid(pre)

    s5b = s5_ref[...].astype(BF16)
    out_a = proj(hm_ref[...], wa_ref)
    out_b = proj(hy_sc[...], wb_ref)
    out_c = proj(s5b, wga_ref) * jax.nn.sigmoid(proj(s5b, wgg_ref))
    merged = gate(wg0_ref) * out_a + gate(wg1_ref) * out_b + gate(wg2_ref) * out_c
    out_ref[...] = merged.astype(out_ref.dtype)


def _merge(hm, hy_t, s5y, x_bf, w_tail, w_a, w_b, w_glu, l, *, tm=512, tn=512):
    nj = D_MODEL // tn
    goff = (OFF_MIXG - OFF_HYENA) // tn

    def gate_spec(g):
        return pl.BlockSpec((None, tn, D_MODEL), lambda i, j: (l, goff + g * nj + j, 0))

    return pl.pallas_call(
        _merge_kernel,
        grid=(SEQ // tm, nj),
        in_specs=[pl.BlockSpec((tm, BRANCH), lambda i, j: (i, 0)),
                  pl.BlockSpec((BRANCH, tm), lambda i, j: (0, i)),
                  pl.BlockSpec((tm, BRANCH), lambda i, j: (i, 0)),
                  pl.BlockSpec((tm, D_MODEL), lambda i, j: (i, 0)),
                  pl.BlockSpec((None, BRANCH, tn), lambda i, j: (l, 0, j)),
                  pl.BlockSpec((None, BRANCH, tn), lambda i, j: (l, 0, j)),
                  pl.BlockSpec((None, BRANCH, tn), lambda i, j: (l, 0, j)),
                  pl.BlockSpec((None, BRANCH, tn), lambda i, j: (l, 0, nj + j)),
                  gate_spec(0), gate_spec(1), gate_spec(2)],
        out_specs=pl.BlockSpec((tm, tn), lambda i, j: (i, j)),
        out_shape=jax.ShapeDtypeStruct((SEQ, D_MODEL), BF16),
        scratch_shapes=[pltpu.VMEM((tm, BRANCH), BF16)],
        compiler_params=_cparams(("parallel", "arbitrary")),
        name="branch_merge",
    )(hm, hy_t, s5y, x_bf, w_a, w_b, w_glu, w_glu, w_tail, w_tail, w_tail)


def _layer_norm_rows(y, g, b):
    mu = jnp.mean(y, axis=1, keepdims=True)
    var = jnp.mean(jnp.square(y - mu), axis=1, keepdims=True)
    return (y - mu) * lax.rsqrt(var + LN_EPS) * g + b


def _res_ln_kernel(x_ref, f_ref, g_ref, b_ref, o_ref, ob_ref, *, transposed):
    f = f_ref[...]
    if transposed:
        f = f.T
    y = _layer_norm_rows(ALPHA * x_ref[...] + f, g_ref[...], b_ref[...])
    o_ref[...] = y
    ob_ref[...] = y.astype(BF16)


def _res_ln(x, f, g, b, l, *, transposed, tm=256):
    f_spec = (pl.BlockSpec((D_MODEL, tm), lambda i: (0, i)) if transposed
              else pl.BlockSpec((tm, D_MODEL), lambda i: (i, 0)))
    n_tok = x.shape[0]
    return pl.pallas_call(
        functools.partial(_res_ln_kernel, transposed=transposed),
        grid=(n_tok // tm,),
        in_specs=[pl.BlockSpec((tm, D_MODEL), lambda i: (i, 0)), f_spec,
                  pl.BlockSpec((None, 1, D_MODEL), lambda i: (l, 0, 0)),
                  pl.BlockSpec((None, 1, D_MODEL), lambda i: (l, 0, 0))],
        out_specs=[pl.BlockSpec((tm, D_MODEL), lambda i: (i, 0)),
                   pl.BlockSpec((tm, D_MODEL), lambda i: (i, 0))],
        out_shape=[jax.ShapeDtypeStruct((n_tok, D_MODEL), F32), jax.ShapeDtypeStruct((n_tok, D_MODEL), BF16)],
        compiler_params=_cparams(("parallel",)),
        name="residual_layernorm",
    )(x, f, g.reshape(DEPTH, 1, D_MODEL), b.reshape(DEPTH, 1, D_MODEL))


PEER_NO_RANK = 64.0


def _extract_sorted(s, n, out_ref, base, want_rank=False):
    rank = jnp.full_like(s, PEER_NO_RANK) if want_rank else None
    for r in range(n):
        m = jnp.max(s, axis=0, keepdims=True)
        out_ref[base + r:base + r + 1, :] = m
        hit = s == m
        if want_rank:
            rank = jnp.where(hit, float(r), rank)
        s = jnp.where(hit, -jnp.inf, s)
    return rank


def _peer_select_kernel(q_ref, keys_ref, cnt_ref, r2_ref, e1_ref, e2_ref, top_sc, kth_sc):
    k = PEER_TOPK
    qb = q_ref[...].astype(BF16)
    half = PEER_KEYS
    for h in range(PEER_HEADS):
        sc = []
        rank2 = None
        for c in range(2):
            qs = qb[:, (2 * h + c) * half:(2 * h + c + 1) * half]
            s = lax.dot_general(keys_ref[h, c].astype(BF16), qs, (((1,), (1,)), ((), ())),
                                preferred_element_type=F32)
            sc.append(s)
            rank2 = _extract_sorted(s, k, top_sc, c * k, want_rank=(c == 1))
        a = top_sc[0:k, :]
        b = top_sc[k:2 * k, :]
        sub = lax.broadcasted_iota(jnp.int32, (8, a.shape[1]), 0)
        groups = [a[0:1] + b[0:8], a[0:1] + b[8:16], a[1:2] + b[0:8]]
        for i in range(2, 8):
            groups.append(jnp.where(sub < k // (i + 1), a[i:i + 1] + b[0:8], -jnp.inf))
        groups.append(a[8:16] + b[0:1])
        cand = jnp.concatenate(groups, axis=0)
        _extract_sorted(cand, k, kth_sc, 0)
        tau = kth_sc[k - 1:k, :]
        top = a[0:1] + b[0:1]
        z = jnp.sum(jnp.where(cand >= tau, jnp.exp(cand - top), 0.0), axis=0, keepdims=True)
        cnt = jnp.zeros_like(sc[0])
        for j in range(k):
            bj = top_sc[k + j:k + j + 1, :]
            cnt = jnp.where(sc[0] + bj >= tau, j + 1.0, cnt)
        cnt_ref[h] = cnt
        r2_ref[h] = rank2.astype(BF16)
        e1_ref[h] = jnp.exp(sc[0] - a[0:1])
        e2_ref[h] = (jnp.exp(sc[1] - b[0:1]) / z).astype(BF16)


def _peer_select(q, keys, l, *, tt=128):
    n_tok = q.shape[0]
    big = jax.ShapeDtypeStruct((PEER_HEADS, PEER_KEYS, n_tok), F32)
    big_bf = jax.ShapeDtypeStruct((PEER_HEADS, PEER_KEYS, n_tok), BF16)
    big_spec = pl.BlockSpec((PEER_HEADS, PEER_KEYS, tt), lambda i: (0, 0, i))
    return pl.pallas_call(
        _peer_select_kernel,
        grid=(n_tok // tt,),
        in_specs=[pl.BlockSpec((tt, D_MODEL), lambda i: (i, 0)),
                  pl.BlockSpec((None, PEER_HEADS, 2, PEER_KEYS, PEER_KEYS), lambda i: (l, 0, 0, 0, 0))],
        out_specs=[big_spec, big_spec, big_spec, big_spec],
        out_shape=[big, big_bf, big, big_bf],
        scratch_shapes=[pltpu.VMEM((2 * PEER_TOPK, tt), F32), pltpu.VMEM((PEER_TOPK, tt), F32)],
        compiler_params=_cparams(("parallel",)),
        name="peer_select",
    )(q, keys)


def _peer_dense_kernel(x_ref, u_ref, vt_ref, cnt_ref, r2_ref, e1_ref, e2_ref, out_ref, act_sc, g_sc, *, rows):
    e = pl.program_id(1)

    @pl.when(e == 0)
    def _():
        out_ref[...] = jnp.zeros_like(out_ref)

    act_sc[...] = lax.dot_general(u_ref[...], x_ref[...], (((1,), (1,)), ((), ())),
                                  preferred_element_type=F32)
    kk = PEER_KEYS
    for tc in range(act_sc.shape[1] // kk):
        lanes = slice(tc * kk, (tc + 1) * kk)
        for i in range(rows):
            w = jnp.zeros((kk, kk), BF16)
            for h in range(PEER_HEADS):
                hit = r2_ref[h, :, lanes] < cnt_ref[h, i:i + 1, lanes].astype(BF16)
                val = e2_ref[h, :, lanes] * e1_ref[h, i:i + 1, lanes].astype(BF16)
                w = w + jnp.where(hit, val, jnp.zeros_like(val))
            g_sc[i * kk:(i + 1) * kk, lanes] = jax.nn.gelu(act_sc[i * kk:(i + 1) * kk, lanes].astype(BF16)) * w
    out_ref[...] += jnp.dot(vt_ref[...], g_sc[...], preferred_element_type=F32)


def _transpose_tile_kernel(v_ref, o_ref):
    o_ref[...] = v_ref[...].T.astype(o_ref.dtype)


def _peer_value_tiles(v, l, te):
    return pl.pallas_call(
        _transpose_tile_kernel,
        grid=(N_EXPERTS // te,),
        in_specs=[pl.BlockSpec((None, te, D_MODEL), lambda e: (l, e, 0))],
        out_specs=pl.BlockSpec((None, D_MODEL, te), lambda e: (e, 0, 0)),
        out_shape=jax.ShapeDtypeStruct((N_EXPERTS // te, D_MODEL, te), BF16),
        compiler_params=_cparams(("parallel",)),
        name="peer_value_tiles",
    )(v)


def _peer_dense(x_bf, u_bf, v, l, sel, *, tt=512, rows=8):
    cnt, r2, e1, e2 = sel
    te = rows * PEER_KEYS
    n_tok = x_bf.shape[0]
    vt_bf = _peer_value_tiles(v, l, te)
    row_spec = pl.BlockSpec((PEER_HEADS, rows, tt), lambda i, e: (0, e, i))
    all_spec = pl.BlockSpec((PEER_HEADS, PEER_KEYS, tt), lambda i, e: (0, 0, i))
    return pl.pallas_call(
        functools.partial(_peer_dense_kernel, rows=rows),
        grid=(n_tok // tt, N_EXPERTS // te),
        in_specs=[pl.BlockSpec((tt, D_MODEL), lambda i, e: (i, 0)),
                  pl.BlockSpec((None, te, D_MODEL), lambda i, e: (l, e, 0)),
                  pl.BlockSpec((None, D_MODEL, te), lambda i, e: (e, 0, 0)),
                  row_spec, all_spec, row_spec, all_spec],
        out_specs=pl.BlockSpec((D_MODEL, tt), lambda i, e: (0, i)),
        out_shape=jax.ShapeDtypeStruct((D_MODEL, n_tok), F32),
        scratch_shapes=[pltpu.VMEM((te, tt), F32), pltpu.VMEM((te, tt), BF16)],
        compiler_params=_cparams(("parallel", "arbitrary")),
        name="peer_dense",
    )(x_bf, u_bf, vt_bf, cnt, r2, e1, e2)


def _hyena_features():
    pos = jnp.arange(SEQ, dtype=F32)
    t = pos / (SEQ - 1)
    bands = jnp.linspace(1e-4, 16 - 1, 16, dtype=F32)
    ang = 2.0 * math.pi * pos[:, None] * bands[None, :] / SEQ
    feat = jnp.concatenate([t[:, None], jnp.cos(ang), -jnp.sin(ang)], axis=-1)
    feat_t = jnp.pad(feat.T, ((0, HY_EMB_PAD - feat.shape[1]), (0, 0)))
    t_row = t[None, :]

    def reversed_time(a):
        return jnp.roll(a[:, ::-1], 1, axis=1)

    return (jnp.concatenate([feat_t, reversed_time(feat_t)], axis=1),
            jnp.concatenate([t_row, reversed_time(t_row)], axis=1))


def kernel(x, w_in, mlstm_gate_bias, mlstm_norm_gain, w_mlstm_out, hyena_conv_w, hyena_conv_b, hyena_w1,
           hyena_b1, hyena_w2, hyena_b2, hyena_freq, hyena_w3, hyena_decay, hyena_skip, w_hyena_out,
           s5_lambda_re, s5_lambda_im, s5_log_step, s5_b_re, s5_b_im, s5_c_re, s5_c_im, s5_skip, w_s5_glu,
           w_out, ln1_g, ln1_b, peer_w_q, peer_subkeys, peer_u, peer_v, ln2_g, ln2_b):
    xf = x.reshape(SEQ, D_MODEL)
    xb = xf.astype(BF16)
    feat_t, t_row = _hyena_features()
    w_in_t = jnp.swapaxes(w_in, 1, 2)
    w_tail = _w_in_tail(w_in_t)
    u_bf = peer_u.astype(BF16)
    s5_mats = jax.vmap(_s5_chunk_matrices)(s5_lambda_re, s5_lambda_im, s5_log_step, s5_b_re, s5_b_im,
                                           s5_c_re, s5_c_im)
    for l in range(DEPTH):
        qkvo = _mm_wt(xb, w_in_t, l, 0, 4 * BRANCH, tm=1024, tn=512, name="proj_qkvo")
        gates_t = _mm_nt(w_in_t, l, OFF_GATES, N_GATES, xb, tr=N_GATES, tm=1024, name="proj_gates", out_dtype=F32)
        hp_t = _mm_nt(w_tail, l, 0, 3 * BRANCH, xb, tr=512, tm=1024, name="proj_hyena_t", out_dtype=BF16)
        s5_in = _mm_wt(xb, w_tail, l, 3 * BRANCH, BRANCH, tm=1024, tn=512, name="proj_s5")

        hdir = _mlstm(qkvo, gates_t, mlstm_gate_bias, l)
        hm = _mlstm_post(hdir, qkvo, mlstm_norm_gain, l)
        hy_t = _hyena(hp_t, l, hyena_conv_w, hyena_conv_b, hyena_w1, hyena_b1, hyena_w2, hyena_b2,
                      hyena_freq, hyena_w3, hyena_decay, hyena_skip, feat_t, t_row)
        s5y = _s5(s5_in, s5_mats, s5_skip, l)
        merged = _merge(hm, hy_t, s5y, xb, w_tail, w_mlstm_out, w_hyena_out, w_s5_glu, l)
        mix = _mm(merged, w_out, l, 0, D_MODEL, tm=1024, tn=512, name="proj_out")
        xf, xb = _res_ln(xf, mix, ln1_g, ln1_b, l, transposed=False)

        q = _mm(xb, peer_w_q, l, 0, PEER_HEADS * 2 * PEER_KEYS, tm=1024, tn=512, name="peer_query")
        sel = _peer_select(q, peer_subkeys, l)
        ffn_t = _peer_dense(xb, u_bf, peer_v, l, sel)
        xf, xb = _res_ln(xf, ffn_t, ln2_g, ln2_b, l, transposed=True)
    return xf.reshape(1, SEQ, D_MODEL)
```

```python
import functools
import math

import numpy as np
import jax
import jax.numpy as jnp
from jax import lax
from jax.experimental import pallas as pl
from jax.experimental.pallas import tpu as pltpu

F32 = jnp.float32
BF16 = jnp.bfloat16

D_MODEL = 2048
SEQ = 8192
DEPTH = 4
BRANCH = 1024
HEADS = 4
HEAD_DIM = 256
CHUNK = 256
N_GATES = 16
HY_HID = 64
HY_EMB_PAD = 128
S5_GROUP = 16
S5_GROUPS = 64
S5_STATE = 64
S5_T = 16
S5_CHUNKS = SEQ // S5_T
S5_LOG_CHUNKS = 9
PEER_HEADS = 8
PEER_KEYS = 128
PEER_TOPK = 16
N_EXPERTS = PEER_KEYS * PEER_KEYS
ALPHA = (2 * DEPTH) ** 0.25
LN_EPS = 1e-5
FFT_N = 2 * SEQ
FFT_R = 128
OFF_GATES = 4 * BRANCH
OFF_HYENA = OFF_GATES + N_GATES
OFF_S5 = OFF_HYENA + 3 * BRANCH
OFF_MIXG = OFF_S5 + BRANCH

VMEM_LIMIT = 56 * 1024 * 1024


def _cparams(sem):
    return pltpu.CompilerParams(dimension_semantics=sem, vmem_limit_bytes=VMEM_LIMIT)


def _split_bf16(a):
    hi = a.astype(BF16)
    lo = (a - hi.astype(F32)).astype(BF16)
    return hi, lo


def _dot3(a, b_hi, b_lo):
    a_hi, a_lo = _split_bf16(a)
    acc = jnp.dot(a_hi, b_hi, preferred_element_type=F32)
    acc += jnp.dot(a_lo, b_hi, preferred_element_type=F32)
    acc += jnp.dot(a_hi, b_lo, preferred_element_type=F32)
    return acc


def _dot3_both(a, b):
    b_hi, b_lo = _split_bf16(b)
    return _dot3(a, b_hi, b_lo)


def _mm_kernel(a_ref, b_ref, o_ref):
    o_ref[...] = jnp.dot(a_ref[...].astype(BF16), b_ref[...].astype(BF16),
                         preferred_element_type=F32).astype(o_ref.dtype)


def _mm(a, w, l, col_off, n_cols, *, tm, tn, name):
    m, k = a.shape
    cb = col_off // tn
    assert col_off % tn == 0 and n_cols % tn == 0 and m % tm == 0
    return pl.pallas_call(
        _mm_kernel,
        grid=(m // tm, n_cols // tn),
        in_specs=[pl.BlockSpec((tm, k), lambda i, j: (i, 0)),
                  pl.BlockSpec((None, k, tn), lambda i, j: (l, 0, cb + j))],
        out_specs=pl.BlockSpec((tm, tn), lambda i, j: (i, j)),
        out_shape=jax.ShapeDtypeStruct((m, n_cols), F32),
        compiler_params=_cparams(("parallel", "parallel")),
        name=name,
    )(a, w)


def _mm_wt_kernel(a_ref, w_ref, o_ref):
    o_ref[...] = lax.dot_general(a_ref[...].astype(BF16), w_ref[...].astype(BF16), (((1,), (1,)), ((), ())),
                                 preferred_element_type=F32).astype(o_ref.dtype)


def _mm_wt(a, wt, l, row_off, n_rows, *, tm, tn, name):
    m, k = a.shape
    rb = row_off // tn
    assert row_off % tn == 0 and n_rows % tn == 0 and m % tm == 0
    return pl.pallas_call(
        _mm_wt_kernel,
        grid=(m // tm, n_rows // tn),
        in_specs=[pl.BlockSpec((tm, k), lambda i, j: (i, 0)),
                  pl.BlockSpec((None, tn, k), lambda i, j: (l, rb + j, 0))],
        out_specs=pl.BlockSpec((tm, tn), lambda i, j: (i, j)),
        out_shape=jax.ShapeDtypeStruct((m, n_rows), F32),
        compiler_params=_cparams(("parallel", "parallel")),
        name=name,
    )(a, wt)


def _mm_nt_kernel(w_ref, a_ref, o_ref):
    o_ref[...] = lax.dot_general(w_ref[...].astype(BF16), a_ref[...].astype(BF16), (((1,), (1,)), ((), ())),
                                 preferred_element_type=F32).astype(o_ref.dtype)


def _mm_nt(wt, l, row_off, n_rows, a, *, tr, tm, name, out_dtype):
    k = wt.shape[2]
    m = a.shape[0]
    rb = row_off // tr
    assert row_off % tr == 0 and n_rows % tr == 0
    return pl.pallas_call(
        _mm_nt_kernel,
        grid=(n_rows // tr, m // tm),
        in_specs=[pl.BlockSpec((None, tr, k), lambda i, j: (l, rb + i, 0)),
                  pl.BlockSpec((tm, k), lambda i, j: (j, 0))],
        out_specs=pl.BlockSpec((tr, tm), lambda i, j: (i, j)),
        out_shape=jax.ShapeDtypeStruct((n_rows, m), out_dtype),
        compiler_params=_cparams(("parallel", "parallel")),
        name=name,
    )(wt, a)


def _w_tail_kernel(cur_ref, nxt_ref, o_ref, *, shift):
    o_ref[...] = jnp.concatenate([cur_ref[shift:, :], nxt_ref[...]], axis=0).astype(o_ref.dtype)


def _w_in_tail(w_in_t, *, tn=512):
    depth, n_all, k = w_in_t.shape
    n_out = n_all - OFF_HYENA
    return pl.pallas_call(
        functools.partial(_w_tail_kernel, shift=N_GATES),
        grid=(depth, n_out // tn),
        in_specs=[pl.BlockSpec((None, tn, k), lambda l, j: (l, OFF_GATES // tn + j, 0)),
                  pl.BlockSpec((None, N_GATES, k), lambda l, j: (l, (OFF_GATES + (j + 1) * tn) // N_GATES, 0))],
        out_specs=pl.BlockSpec((None, tn, k), lambda l, j: (l, j, 0)),
        out_shape=jax.ShapeDtypeStruct((depth, n_out, k), BF16),
        compiler_params=_cparams(("parallel", "parallel")),
        name="w_in_tail",
    )(w_in_t, w_in_t)


def _log_sigmoid(x):
    return jnp.minimum(x, 0.0) - jnp.log1p(jnp.exp(-jnp.abs(x)))


def _mlstm_chunk(d, q, k, v, li_r, lf_r, li_c, lf_c, c_sc, n_sc, m_sc):
    n = CHUNK
    row = lax.broadcasted_iota(jnp.int32, (n, n), 0)
    col = lax.broadcasted_iota(jnp.int32, (n, n), 1)
    valid = (row >= col) if d == 0 else (row <= col)
    valid_t = (col >= row) if d == 0 else (col <= row)
    b_col = jnp.sum(jnp.where(valid, lf_r, 0.0), axis=1, keepdims=True)
    b_row = jnp.sum(jnp.where(valid_t, lf_c, 0.0), axis=0, keepdims=True)
    total = jnp.sum(lf_r, axis=1, keepdims=True)

    m_prev = m_sc[...]
    m_inter = b_col + m_prev
    log_d = jnp.where(valid, b_col - b_row + li_r, -jnp.inf)
    m_t = jnp.maximum(m_inter, jnp.max(log_d, axis=1, keepdims=True))
    inter = jnp.exp(m_inter - m_t)
    dmat = jnp.exp(log_d - m_t)

    k = k * (HEAD_DIM ** -0.5)
    qb = q.astype(BF16)
    kb = k.astype(BF16)
    vb = v.astype(BF16)
    s = lax.dot_general(qb, kb, (((1,), (1,)), ((), ())), preferred_element_type=F32) * dmat
    num = jnp.dot(s.astype(BF16), vb, preferred_element_type=F32)
    num += inter * jnp.dot(qb, c_sc[...].astype(BF16), preferred_element_type=F32)
    qn = jnp.sum(q * n_sc[...], axis=1, keepdims=True)
    den = jnp.sum(s, axis=1, keepdims=True) + inter * qn
    h_out = num / jnp.maximum(jnp.abs(den), jnp.exp(-m_t))

    log_w = total - b_col + li_c
    m_new = jnp.maximum(total + m_prev, jnp.max(log_w, axis=0, keepdims=True))
    w = jnp.exp(log_w - m_new)
    decay = jnp.exp(total + m_prev - m_new)
    kw = k * w
    c_sc[...] = decay * c_sc[...] + lax.dot_general(
        kw.astype(BF16), vb, (((0,), (0,)), ((), ())), preferred_element_type=F32)
    n_sc[...] = decay * n_sc[...] + jnp.sum(kw, axis=0, keepdims=True)
    m_sc[...] = m_new
    return h_out


def _mlstm_kernel(qkv_f_ref, qkv_b_ref, gr_f_ref, gr_b_ref, gc_f_ref, gc_b_ref, bias_ref,
                  o_f_ref, o_b_ref, c_sc, n_sc, m_sc):
    @pl.when(pl.program_id(0) == 0)
    def _():
        c_sc[...] = jnp.zeros_like(c_sc)
        n_sc[...] = jnp.zeros_like(n_sc)
        m_sc[...] = jnp.zeros_like(m_sc)

    for d, (qkv_ref, gr_ref, gc_ref, o_ref) in enumerate(((qkv_f_ref, gr_f_ref, gc_f_ref, o_f_ref),
                                                         (qkv_b_ref, gr_b_ref, gc_b_ref, o_b_ref))):
        for h in range(HEADS):
            cols = slice(h * HEAD_DIM, (h + 1) * HEAD_DIM)
            b_i = bias_ref[d * 2 * HEADS + h]
            b_f = bias_ref[d * 2 * HEADS + HEADS + h]
            chain = d * HEADS + h
            o_ref[:, cols] = _mlstm_chunk(
                d,
                qkv_ref[:, cols],
                qkv_ref[:, BRANCH + h * HEAD_DIM:BRANCH + (h + 1) * HEAD_DIM],
                qkv_ref[:, 2 * BRANCH + h * HEAD_DIM:2 * BRANCH + (h + 1) * HEAD_DIM],
                gr_ref[h] + b_i, _log_sigmoid(gr_ref[HEADS + h] + b_f),
                gc_ref[h] + b_i, _log_sigmoid(gc_ref[HEADS + h] + b_f),
                c_sc.at[chain], n_sc.at[chain], m_sc.at[chain])


def _mlstm(qkvo, gates_t, gate_bias, l):
    nc = SEQ // CHUNK
    g_rows = gates_t.reshape(N_GATES, 1, SEQ)
    g_cols = gates_t.reshape(N_GATES, SEQ, 1)
    bias = gate_bias.reshape(DEPTH, N_GATES, 1, 1)
    half = N_GATES // 2
    chains = 2 * HEADS
    out = jax.ShapeDtypeStruct((SEQ, BRANCH), F32)
    return pl.pallas_call(
        _mlstm_kernel,
        grid=(nc,),
        in_specs=[
            pl.BlockSpec((CHUNK, 4 * BRANCH), lambda c: (c, 0)),
            pl.BlockSpec((CHUNK, 4 * BRANCH), lambda c: (nc - 1 - c, 0)),
            pl.BlockSpec((half, 1, CHUNK), lambda c: (0, 0, c)),
            pl.BlockSpec((half, 1, CHUNK), lambda c: (1, 0, nc - 1 - c)),
            pl.BlockSpec((half, CHUNK, 1), lambda c: (0, c, 0)),
            pl.BlockSpec((half, CHUNK, 1), lambda c: (1, nc - 1 - c, 0)),
            pl.BlockSpec((None, N_GATES, 1, 1), lambda c: (l, 0, 0, 0)),
        ],
        out_specs=[pl.BlockSpec((CHUNK, BRANCH), lambda c: (c, 0)),
                   pl.BlockSpec((CHUNK, BRANCH), lambda c: (nc - 1 - c, 0))],
        out_shape=[out, out],
        scratch_shapes=[pltpu.VMEM((chains, HEAD_DIM, HEAD_DIM), F32),
                        pltpu.VMEM((chains, 1, HEAD_DIM), F32),
                        pltpu.VMEM((chains, 1, 1), F32)],
        compiler_params=_cparams(("arbitrary",)),
        name="mlstm_scan",
    )(qkvo, qkvo, g_rows, g_rows, g_cols, g_cols, bias)


def _mlstm_post_kernel(hf_ref, hb_ref, o_ref, gain_ref, out_ref):
    h = hf_ref[...] + hb_ref[...]
    h = jax.nn.sigmoid(o_ref[...]) * h
    for hd in range(HEADS):
        sl = slice(hd * HEAD_DIM, (hd + 1) * HEAD_DIM)
        hh = h[:, sl]
        mu = jnp.mean(hh, axis=1, keepdims=True)
        var = jnp.mean(jnp.square(hh - mu), axis=1, keepdims=True)
        out_ref[:, sl] = ((hh - mu) * lax.rsqrt(var + LN_EPS) * gain_ref[:, sl]).astype(out_ref.dtype)


def _mlstm_post(hdir, qkvo, gain, l, *, tm=512):
    return pl.pallas_call(
        _mlstm_post_kernel,
        grid=(SEQ // tm,),
        in_specs=[pl.BlockSpec((tm, BRANCH), lambda i: (i, 0)),
                  pl.BlockSpec((tm, BRANCH), lambda i: (i, 0)),
                  pl.BlockSpec((tm, BRANCH), lambda i: (i, 3)),
                  pl.BlockSpec((None, 1, BRANCH), lambda i: (l, 0, 0))],
        out_specs=pl.BlockSpec((tm, BRANCH), lambda i: (i, 0)),
        out_shape=jax.ShapeDtypeStruct((SEQ, BRANCH), BF16),
        compiler_params=_cparams(("parallel",)),
        name="mlstm_post",
    )(hdir[0], hdir[1], qkvo, gain.reshape(DEPTH, 1, BRANCH))


HY_CT = 32
HY_RB = 512
HY_UNROLL = 8


@functools.lru_cache(maxsize=None)
def _fft_constants():
    r = FFT_R
    idx = np.arange(r)
    th = 2.0 * np.pi * np.outer(idx, idx) / r
    cs, sn = np.cos(th), np.sin(th)
    ph = 2.0 * np.pi * np.outer(idx, idx) / FFT_N
    f1 = np.concatenate([cs, -sn], axis=1)
    m3 = np.block([[cs, -sn], [sn, cs]])
    m4 = np.block([[cs, sn], [-sn, cs]])
    m6 = np.concatenate([cs, -sn], axis=0) / FFT_N
    m6[:, r // 2:] = 0.0
    consts = {name: jnp.asarray(mat, F32).astype(BF16)
              for name, mat in (("f1", f1), ("m3", m3), ("m4", m4), ("m6", m6))}
    consts["tw"] = (jnp.asarray(np.cos(ph), F32), jnp.asarray(np.sin(ph), F32))
    return consts


def _rows(i, n):
    return pl.ds(pl.multiple_of(i * n, n), n)


def _for_each(n, body, unroll=1):
    def step(i, carry):
        body(i)
        return carry

    lax.fori_loop(0, n, step, 0, unroll=unroll)


def _fwd_stage1(sig_ref, hi_ref, ct, t_buf, a_buf, b_buf, f1_ref, twr_ref, twi_ref):
    r = FFT_R
    zeros = jnp.zeros((r // 2, r), F32)

    def tr_in(c):
        hi = zeros if hi_ref is None else hi_ref[c]
        t_buf[_rows(c, r), :] = jnp.concatenate([sig_ref[c], hi], axis=0).T.astype(BF16)

    _for_each(ct, tr_in, unroll=HY_UNROLL)

    def stage1(b):
        rows = _rows(b, HY_RB)
        a_buf[rows, :] = jnp.dot(t_buf[rows, :], f1_ref[...], preferred_element_type=F32)

    _for_each(ct * r // HY_RB, stage1, unroll=HY_UNROLL)

    def twiddle(c):
        rows = _rows(c, r)
        a = a_buf[rows, :]
        ar, ai = a[:, :r], a[:, r:]
        twr, twi = twr_ref[...], twi_ref[...]
        b_buf[rows, :r] = (ar * twr + ai * twi).T.astype(BF16)
        b_buf[rows, r:] = (ai * twr - ar * twi).T.astype(BF16)

    _for_each(ct, twiddle, unroll=HY_UNROLL)


def _hy_mlp_kernel(feat_ref, w1_ref, b1_ref, w2_ref, b2_ref, fr_ref, out_ref):
    h = _dot3_both(w1_ref[...], feat_ref[...]) + b1_ref[...]
    h = jnp.sin(fr_ref[0] * h)
    h = _dot3_both(w2_ref[...], h) + b2_ref[...]
    out_ref[...] = jnp.sin(fr_ref[1] * h)


def _hy_filter_kernel(w3_ref, h_ref, dec_ref, t_ref, out_ref):
    d = pl.program_id(0)
    filt = _dot3_both(w3_ref[...], h_ref[...])
    filt = filt * jnp.exp(-t_ref[...] * jnp.abs(dec_ref[...]))
    filt = filt / (jnp.sum(jnp.abs(filt), axis=1, keepdims=True) + 1e-6)
    lane = lax.broadcasted_iota(jnp.int32, filt.shape, 1)
    out_ref[...] = jnp.where((lane == 0) & (d == 1), 0.0, filt)


def _hy_spec_kernel(fa_ref, fb_ref, f1_ref, m3_ref, twr_ref, twi_ref, out_ref, t_buf, a_buf, b_buf, *, ct):
    r = FFT_R
    nblk = ct * r // HY_RB
    _fwd_stage1(fa_ref, fb_ref, ct, t_buf, a_buf, b_buf, f1_ref, twr_ref, twi_ref)

    def put(b):
        rows = _rows(b, HY_RB)
        out_ref[rows, :] = jnp.dot(b_buf[rows, :], m3_ref[...], preferred_element_type=F32)

    _for_each(nblk, put, unroll=HY_UNROLL)


def _shift_prev(p, n1, n2):
    r1 = pltpu.roll(p, 1, 2)
    r2 = pltpu.roll(r1, 1, 1)
    return jnp.where(n2 == 0, jnp.where(n1 == 0, 0.0, r2), r1)


def _shift_next(p, n1, n2):
    s1, s2 = p.shape[1], p.shape[2]
    r1 = pltpu.roll(p, s2 - 1, 2)
    r2 = pltpu.roll(r1, s1 - 1, 1)
    return jnp.where(n2 == s2 - 1, jnp.where(n1 == s1 - 1, 0.0, r2), r1)


def _hy_conv_kernel(p0_ref, p1_ref, p2_ref, cw_ref, cb_ref, skip_ref, h_ref,
                    f1_ref, m3_ref, m4_ref, m6_ref, twr_ref, twi_ref,
                    out_ref, z_buf, t_buf, a_buf, b_buf, *, ct):
    r = FFT_R
    n1_len = SEQ // r
    nblk = ct * r // HY_RB
    shp = (ct, n1_len, r)
    n1 = lax.broadcasted_iota(jnp.int32, shp, 1)
    n2 = lax.broadcasted_iota(jnp.int32, shp, 2)

    def short_conv(p_ref, g):
        p = p_ref[...].astype(F32)
        return (_shift_prev(p, n1, n2) * cw_ref[3 * g] + p * cw_ref[3 * g + 1]
                + _shift_next(p, n1, n2) * cw_ref[3 * g + 2] + cb_ref[g])

    z_buf[...] = short_conv(p1_ref, 1) * short_conv(p2_ref, 2)
    _fwd_stage1(z_buf, None, ct, t_buf, a_buf, b_buf, f1_ref, twr_ref, twi_ref)

    def freq_domain(b):
        rows = _rows(b, HY_RB)
        x = jnp.dot(b_buf[rows, :], m3_ref[...], preferred_element_type=F32)
        h = h_ref[rows, :]
        xr, xi = x[:, :r], x[:, r:]
        hr, hi = h[:, :r], h[:, r:]
        y = jnp.concatenate([xr * hr - xi * hi, xr * hi + xi * hr], axis=1).astype(BF16)
        a_buf[rows, :] = jnp.dot(y, m4_ref[...], preferred_element_type=F32)

    _for_each(nblk, freq_domain, unroll=HY_UNROLL)

    def inv_twiddle(c):
        rows = _rows(c, r)
        b = a_buf[rows, :]
        br, bi = b[:, :r], b[:, r:]
        twr, twi = twr_ref[...], twi_ref[...]
        b_buf[rows, :r] = (br * twr - bi * twi).T.astype(BF16)
        b_buf[rows, r:] = (br * twi + bi * twr).T.astype(BF16)

    _for_each(ct, inv_twiddle, unroll=HY_UNROLL)

    def stage6(b):
        rows = _rows(b, HY_RB)
        a_buf[rows, :r] = jnp.dot(b_buf[rows, :], m6_ref[...], preferred_element_type=F32)

    _for_each(nblk, stage6, unroll=HY_UNROLL)

    def finish(c):
        y_c = a_buf[_rows(c, r), :r].T[:n1_len, :]
        out_ref[c] = y_c + skip_ref[c] * z_buf[c]

    _for_each(ct, finish, unroll=HY_UNROLL)
    out_ref[...] = short_conv(p0_ref, 0) * out_ref[...]


def _hyena(hp_t, l, conv_w, conv_b, w1, b1, w2, b2, freq, w3, decay, skip, feat_t, t_row):
    r = FFT_R
    n1 = SEQ // r
    ct = HY_CT
    k = _fft_constants()
    f1, m3, m4, m6 = k["f1"], k["m3"], k["m4"], k["m6"]
    twr, twi = k["tw"]

    def full(a):
        return pl.BlockSpec(a.shape, lambda *_: (0,) * a.ndim)

    h2 = pl.pallas_call(
        _hy_mlp_kernel,
        grid=(2,),
        in_specs=[pl.BlockSpec((feat_t.shape[0], SEQ), lambda i: (0, i)),
                  pl.BlockSpec((None, HY_HID, feat_t.shape[0]), lambda i: (l, 0, 0)),
                  pl.BlockSpec((None, HY_HID, 1), lambda i: (l, 0, 0)),
                  pl.BlockSpec((None, HY_HID, HY_HID), lambda i: (l, 0, 0)),
                  pl.BlockSpec((None, HY_HID, 1), lambda i: (l, 0, 0)),
                  pl.BlockSpec((None, 2, HY_HID, 1), lambda i: (l, 0, 0, 0))],
        out_specs=pl.BlockSpec((HY_HID, SEQ), lambda i: (0, i)),
        out_shape=jax.ShapeDtypeStruct((HY_HID, 2 * SEQ), F32),
        compiler_params=_cparams(("parallel",)),
        name="hyena_filter_mlp",
    )(feat_t, jnp.pad(jnp.swapaxes(w1, 1, 2), ((0, 0), (0, 0), (0, feat_t.shape[0] - w1.shape[1]))),
      b1[:, :, None], jnp.swapaxes(w2, 1, 2), b2[:, :, None], freq[:, :, :, None])

    rt = 128
    filt = pl.pallas_call(
        _hy_filter_kernel,
        grid=(2, BRANCH // rt),
        in_specs=[pl.BlockSpec((None, rt, HY_HID), lambda d, i: (l, d * (BRANCH // rt) + i, 0)),
                  pl.BlockSpec((HY_HID, SEQ), lambda d, i: (0, d)),
                  pl.BlockSpec((None, None, rt, 1), lambda d, i: (l, d, i, 0)),
                  pl.BlockSpec((1, SEQ), lambda d, i: (0, d))],
        out_specs=pl.BlockSpec((None, rt, SEQ), lambda d, i: (d, i, 0)),
        out_shape=jax.ShapeDtypeStruct((2, BRANCH, SEQ), F32),
        compiler_params=_cparams(("parallel", "parallel")),
        name="hyena_filter",
    )(jnp.swapaxes(w3, 1, 2), h2, decay[:, :, :, None], t_row)
    filt = filt.reshape(2, BRANCH, n1, r)

    scratch = [pltpu.VMEM((ct * r, r), BF16), pltpu.VMEM((ct * r, 2 * r), F32), pltpu.VMEM((ct * r, 2 * r), BF16)]
    spec = pl.pallas_call(
        functools.partial(_hy_spec_kernel, ct=ct),
        grid=(BRANCH // ct,),
        in_specs=[pl.BlockSpec((None, ct, n1, r), lambda i: (0, i, 0, 0)),
                  pl.BlockSpec((None, ct, n1, r), lambda i: (1, i, 0, 0)),
                  full(f1), full(m3), full(twr), full(twi)],
        out_specs=pl.BlockSpec((ct * r, 2 * r), lambda i: (i, 0)),
        out_shape=jax.ShapeDtypeStruct((BRANCH * r, 2 * r), F32),
        scratch_shapes=scratch,
        compiler_params=_cparams(("parallel",)),
        name="hyena_filter_spectrum",
    )(filt, filt, f1, m3, twr, twi)

    hp3 = hp_t.reshape(3 * BRANCH, n1, r)
    nb = BRANCH // ct
    cw = conv_w.reshape(DEPTH, 3, 3, BRANCH).transpose(0, 2, 1, 3).reshape(DEPTH, 9, BRANCH, 1, 1)
    cb = conv_b.reshape(DEPTH, 3, BRANCH, 1, 1)
    consts = (f1, m3, m4, m6, twr, twi)
    return pl.pallas_call(
        functools.partial(_hy_conv_kernel, ct=ct),
        grid=(nb,),
        in_specs=[pl.BlockSpec((ct, n1, r), lambda i: (i, 0, 0)),
                  pl.BlockSpec((ct, n1, r), lambda i: (nb + i, 0, 0)),
                  pl.BlockSpec((ct, n1, r), lambda i: (2 * nb + i, 0, 0)),
                  pl.BlockSpec((None, 9, ct, 1, 1), lambda i: (l, 0, i, 0, 0)),
                  pl.BlockSpec((None, 3, ct, 1, 1), lambda i: (l, 0, i, 0, 0)),
                  pl.BlockSpec((None, ct, 1, 1), lambda i: (l, i, 0, 0)),
                  pl.BlockSpec((ct * r, 2 * r), lambda i: (i, 0)),
                  *[full(a) for a in consts]],
        out_specs=pl.BlockSpec((ct, n1, r), lambda i: (i, 0, 0)),
        out_shape=jax.ShapeDtypeStruct((BRANCH, n1, r), F32),
        scratch_shapes=[pltpu.VMEM((ct, n1, r), F32)] + scratch,
        compiler_params=_cparams(("parallel",)),
        name="hyena_conv",
    )(hp3, hp3, hp3, cw, cb, skip.reshape(DEPTH, BRANCH, 1, 1), spec, *consts).reshape(BRANCH, SEQ)


def _s5_chunk_matrices(lam_re, lam_im, log_step, b_re, b_im, c_re, c_im):
    t = S5_T
    lam = lax.complex(lam_re, lam_im)
    step = jnp.exp(log_step)[..., None]
    lam_dt = lam * step
    b_bar = ((jnp.exp(lam_dt) - 1.0) / lam)[..., None] * lax.complex(b_re, b_im)
    c_mat = lax.complex(c_re, c_im)
    pw = jnp.exp(lam_dt[None] * jnp.arange(t + 1, dtype=F32)[:, None, None, None])
    kern = jnp.einsum("dgvp,mdgp,dgpn->mdgvn", c_mat, pw[:t], b_bar, precision="highest").real
    lag = np.arange(t)[:, None] - np.arange(t)[None, :]
    pick = np.stack([lag[:, :, None] == np.arange(t), -lag[:, :, None] == np.arange(t)], axis=-1)
    kfb = jnp.einsum("ajmd,mdgvn->ajgvn", jnp.asarray(pick, F32), kern, precision="highest")
    w_intra = jnp.transpose(kfb, (2, 1, 4, 0, 3)).reshape(S5_GROUPS, t * S5_GROUP, t * S5_GROUP)
    in_f = pw[:t, 0][::-1][..., None] * b_bar[0][None]
    in_b = pw[:t, 1][..., None] * b_bar[1][None]

    def in_mat(m):
        m = jnp.transpose(m, (1, 0, 3, 2)).reshape(S5_GROUPS, t * S5_GROUP, S5_STATE)
        return jnp.concatenate([m.real, m.imag], axis=-1)

    w_in = jnp.concatenate([in_mat(in_f), in_mat(in_b)], axis=-1)
    out_f = c_mat[0][None] * pw[1:, 0][:, :, None, :]
    out_b = c_mat[1][None] * pw[1:, 1][::-1][:, :, None, :]

    def out_mat(m):
        m = jnp.transpose(m, (1, 3, 0, 2)).reshape(S5_GROUPS, S5_STATE, t * S5_GROUP)
        return jnp.concatenate([m.real, -m.imag], axis=1)

    w_out = jnp.concatenate([out_mat(out_f), out_mat(out_b)], axis=1)
    a_pow = jnp.exp(lam_dt[None] * (t * 2.0 ** jnp.arange(S5_LOG_CHUNKS, dtype=F32))[:, None, None, None])
    a_pow = jnp.transpose(a_pow, (2, 1, 0, 3))
    a_re = jnp.concatenate([a_pow.real, a_pow.real], axis=-1)
    a_im = jnp.concatenate([-a_pow.imag, a_pow.imag], axis=-1)
    return w_intra, w_in, w_out, a_re, a_im


S5_TILE_GROUPS = 128 // S5_GROUP


def _s5_group(ub, wi, win, wout, a_re, a_im):
    nc = S5_CHUNKS
    half = S5_STATE
    y = jnp.dot(ub, wi.astype(BF16), preferred_element_type=F32)
    s_in = jnp.dot(ub, win.astype(BF16), preferred_element_type=F32)
    row = lax.broadcasted_iota(jnp.int32, (nc, 2 * half), 0)

    def scan(s, d):
        for step in range(S5_LOG_CHUNKS):
            sh = 1 << step
            if d == 0:
                moved = jnp.where(row >= sh, pltpu.roll(s, sh, 0), 0.0)
            else:
                moved = jnp.where(row < nc - sh, pltpu.roll(s, nc - sh, 0), 0.0)
            swapped = pltpu.roll(moved, half, 1)
            s = s + a_re[d, step:step + 1, :] * moved + a_im[d, step:step + 1, :] * swapped
        if d == 0:
            return jnp.where(row >= 1, pltpu.roll(s, 1, 0), 0.0)
        return jnp.where(row < nc - 1, pltpu.roll(s, nc - 1, 0), 0.0)

    e = jnp.concatenate([scan(s_in[:, :2 * half], 0), scan(s_in[:, 2 * half:], 1)], axis=1)
    return y + jnp.dot(e.astype(BF16), wout.astype(BF16), preferred_element_type=F32)


def _s5_kernel(u_ref, wi_ref, win_ref, wout_ref, are_ref, aim_ref, skip_ref, y_ref, x_sc, y_sc):
    t, gs, nc = S5_T, S5_GROUP, S5_CHUNKS
    for tau in range(t):
        x_sc[tau] = u_ref[pl.ds(tau, nc, stride=t), :].astype(BF16)
    for g in range(S5_TILE_GROUPS):
        u_g = jnp.concatenate([x_sc[tau, :, g * gs:(g + 1) * gs] for tau in range(t)], axis=1)
        y_sc[g] = _s5_group(u_g, wi_ref[g], win_ref[g], wout_ref[g], are_ref[g], aim_ref[g]).astype(BF16)
    for tau in range(t):
        rows = pl.ds(tau, nc, stride=t)
        y_mat = jnp.concatenate([y_sc[g, :, tau * gs:(tau + 1) * gs] for g in range(S5_TILE_GROUPS)], axis=1)
        y_ref[rows, :] = y_mat.astype(F32) + skip_ref[...] * u_ref[rows, :]


def _s5(u, mats, skip, l):
    w_intra, w_in, w_out, a_re, a_im = mats
    t = S5_T
    gw = t * S5_GROUP
    tg = S5_TILE_GROUPS
    lane = tg * S5_GROUP
    return pl.pallas_call(
        _s5_kernel,
        grid=(BRANCH // lane,),
        in_specs=[pl.BlockSpec((SEQ, lane), lambda j: (0, j)),
                  pl.BlockSpec((None, tg, gw, gw), lambda j: (l, j, 0, 0)),
                  pl.BlockSpec((None, tg, gw, 4 * S5_STATE), lambda j: (l, j, 0, 0)),
                  pl.BlockSpec((None, tg, 4 * S5_STATE, gw), lambda j: (l, j, 0, 0)),
                  pl.BlockSpec((None, tg, 2, S5_LOG_CHUNKS, 2 * S5_STATE), lambda j: (l, j, 0, 0, 0)),
                  pl.BlockSpec((None, tg, 2, S5_LOG_CHUNKS, 2 * S5_STATE), lambda j: (l, j, 0, 0, 0)),
                  pl.BlockSpec((None, 1, lane), lambda j: (l, 0, j))],
        out_specs=pl.BlockSpec((SEQ, lane), lambda j: (0, j)),
        out_shape=jax.ShapeDtypeStruct((SEQ, BRANCH), F32),
        scratch_shapes=[pltpu.VMEM((t, S5_CHUNKS, lane), BF16), pltpu.VMEM((tg, S5_CHUNKS, gw), BF16)],
        compiler_params=_cparams(("parallel",)),
        name="s5_scan",
    )(u, w_intra, w_in, w_out, a_re, a_im, skip.reshape(DEPTH, 1, BRANCH))


def _merge_kernel(hm_ref, hyt_ref, s5_ref, x_ref, wa_ref, wb_ref, wga_ref, wgg_ref, wg0_ref, wg1_ref, wg2_ref,
                  out_ref, hy_sc):
    @pl.when(pl.program_id(1) == 0)
    def _():
        hy_sc[...] = hyt_ref[...].T.astype(BF16)

    def proj(a, w_ref):
        return jnp.dot(a, w_ref[...].astype(BF16), preferred_element_type=F32)

    def gate(w_ref):
        pre = lax.dot_general(x_ref[...], w_ref[...], (((1,), (1,)), ((), ())), preferred_element_type=F32)
        return jax.nn.sigmoid(pre)

    s5b = s5_ref[...].astype(BF16)
    out_a = proj(hm_ref[...], wa_ref)
    out_b = proj(hy_sc[...], wb_ref)
    out_c = proj(s5b, wga_ref) * jax.nn.sigmoid(proj(s5b, wgg_ref))
    merged = gate(wg0_ref) * out_a + gate(wg1_ref) * out_b + gate(wg2_ref) * out_c
    out_ref[...] = merged.astype(out_ref.dtype)


def _merge(hm, hy_t, s5y, x_bf, w_tail, w_a, w_b, w_glu, l, *, tm=512, tn=512):
    nj = D_MODEL // tn
    goff = (OFF_MIXG - OFF_HYENA) // tn

    def gate_spec(g):
        return pl.BlockSpec((None, tn, D_MODEL), lambda i, j: (l, goff + g * nj + j, 0))

    return pl.pallas_call(
        _merge_kernel,
        grid=(SEQ // tm, nj),
        in_specs=[pl.BlockSpec((tm, BRANCH), lambda i, j: (i, 0)),
                  pl.BlockSpec((BRANCH, tm), lambda i, j: (0, i)),
                  pl.BlockSpec((tm, BRANCH), lambda i, j: (i, 0)),
                  pl.BlockSpec((tm, D_MODEL), lambda i, j: (i, 0)),
                  pl.BlockSpec((None, BRANCH, tn), lambda i, j: (l, 0, j)),
                  pl.BlockSpec((None, BRANCH, tn), lambda i, j: (l, 0, j)),
                  pl.BlockSpec((None, BRANCH, tn), lambda i, j: (l, 0, j)),
                  pl.BlockSpec((None, BRANCH, tn), lambda i, j: (l, 0, nj + j)),
                  gate_spec(0), gate_spec(1), gate_spec(2)],
        out_specs=pl.BlockSpec((tm, tn), lambda i, j: (i, j)),
        out_shape=jax.ShapeDtypeStruct((SEQ, D_MODEL), BF16),
        scratch_shapes=[pltpu.VMEM((tm, BRANCH), BF16)],
        compiler_params=_cparams(("parallel", "arbitrary")),
        name="branch_merge",
    )(hm, hy_t, s5y, x_bf, w_a, w_b, w_glu, w_glu, w_tail, w_tail, w_tail)


def _layer_norm_rows(y, g, b):
    mu = jnp.mean(y, axis=1, keepdims=True)
    var = jnp.mean(jnp.square(y - mu), axis=1, keepdims=True)
    return (y - mu) * lax.rsqrt(var + LN_EPS) * g + b


def _res_ln_kernel(x_ref, f_ref, g_ref, b_ref, o_ref, ob_ref, *, transposed):
    f = f_ref[...]
    if transposed:
        f = f.T
    y = _layer_norm_rows(ALPHA * x_ref[...] + f, g_ref[...], b_ref[...])
    o_ref[...] = y
    ob_ref[...] = y.astype(BF16)


def _res_ln(x, f, g, b, l, *, transposed, tm=256):
    f_spec = (pl.BlockSpec((D_MODEL, tm), lambda i: (0, i)) if transposed
              else pl.BlockSpec((tm, D_MODEL), lambda i: (i, 0)))
    n_tok = x.shape[0]
    return pl.pallas_call(
        functools.partial(_res_ln_kernel, transposed=transposed),
        grid=(n_tok // tm,),
        in_specs=[pl.BlockSpec((tm, D_MODEL), lambda i: (i, 0)), f_spec,
                  pl.BlockSpec((None, 1, D_MODEL), lambda i: (l, 0, 0)),
                  pl.BlockSpec((None, 1, D_MODEL), lambda i: (l, 0, 0))],
        out_specs=[pl.BlockSpec((tm, D_MODEL), lambda i: (i, 0)),
                   pl.BlockSpec((tm, D_MODEL), lambda i: (i, 0))],
        out_shape=[jax.ShapeDtypeStruct((n_tok, D_MODEL), F32), jax.ShapeDtypeStruct((n_tok, D_MODEL), BF16)],
        compiler_params=_cparams(("parallel",)),
        name="residual_layernorm",
    )(x, f, g.reshape(DEPTH, 1, D_MODEL), b.reshape(DEPTH, 1, D_MODEL))


PEER_NO_RANK = 64.0


def _extract_sorted(s, n, out_ref, base, want_rank=False):
    rank = jnp.full_like(s, PEER_NO_RANK) if want_rank else None
    for r in range(n):
        m = jnp.max(s, axis=0, keepdims=True)
        out_ref[base + r:base + r + 1, :] = m
        hit = s == m
        if want_rank:
            rank = jnp.where(hit, float(r), rank)
        s = jnp.where(hit, -jnp.inf, s)
    return rank


def _peer_select_kernel(q_ref, keys_ref, cnt_ref, r2_ref, e1_ref, e2_ref, top_sc, kth_sc):
    k = PEER_TOPK
    qb = q_ref[...].astype(BF16)
    half = PEER_KEYS
    for h in range(PEER_HEADS):
        sc = []
        rank2 = None
        for c in range(2):
            qs = qb[:, (2 * h + c) * half:(2 * h + c + 1) * half]
            s = lax.dot_general(keys_ref[h, c].astype(BF16), qs, (((1,), (1,)), ((), ())),
                                preferred_element_type=F32)
            sc.append(s)
            rank2 = _extract_sorted(s, k, top_sc, c * k, want_rank=(c == 1))
        a = top_sc[0:k, :]
        b = top_sc[k:2 * k, :]
        sub = lax.broadcasted_iota(jnp.int32, (8, a.shape[1]), 0)
        groups = [a[0:1] + b[0:8], a[0:1] + b[8:16], a[1:2] + b[0:8]]
        for i in range(2, 8):
            groups.append(jnp.where(sub < k // (i + 1), a[i:i + 1] + b[0:8], -jnp.inf))
        groups.append(a[8:16] + b[0:1])
        cand = jnp.concatenate(groups, axis=0)
        _extract_sorted(cand, k, kth_sc, 0)
        tau = kth_sc[k - 1:k, :]
        top = a[0:1] + b[0:1]
        z = jnp.sum(jnp.where(cand >= tau, jnp.exp(cand - top), 0.0), axis=0, keepdims=True)
        cnt = jnp.zeros_like(sc[0])
        for j in range(k):
            bj = top_sc[k + j:k + j + 1, :]
            cnt = jnp.where(sc[0] + bj >= tau, j + 1.0, cnt)
        cnt_ref[h] = cnt
        r2_ref[h] = rank2.astype(BF16)
        e1_ref[h] = jnp.exp(sc[0] - a[0:1])
        e2_ref[h] = (jnp.exp(sc[1] - b[0:1]) / z).astype(BF16)


def _peer_select(q, keys, l, *, tt=128):
    n_tok = q.shape[0]
    big = jax.ShapeDtypeStruct((PEER_HEADS, PEER_KEYS, n_tok), F32)
    big_bf = jax.ShapeDtypeStruct((PEER_HEADS, PEER_KEYS, n_tok), BF16)
    big_spec = pl.BlockSpec((PEER_HEADS, PEER_KEYS, tt), lambda i: (0, 0, i))
    return pl.pallas_call(
        _peer_select_kernel,
        grid=(n_tok // tt,),
        in_specs=[pl.BlockSpec((tt, D_MODEL), lambda i: (i, 0)),
                  pl.BlockSpec((None, PEER_HEADS, 2, PEER_KEYS, PEER_KEYS), lambda i: (l, 0, 0, 0, 0))],
        out_specs=[big_spec, big_spec, big_spec, big_spec],
        out_shape=[big, big_bf, big, big_bf],
        scratch_shapes=[pltpu.VMEM((2 * PEER_TOPK, tt), F32), pltpu.VMEM((PEER_TOPK, tt), F32)],
        compiler_params=_cparams(("parallel",)),
        name="peer_select",
    )(q, keys)


def _peer_dense_kernel(x_ref, u_ref, vt_ref, cnt_ref, r2_ref, e1_ref, e2_ref, out_ref, act_sc, g_sc, *, rows):
    e = pl.program_id(1)

    @pl.when(e == 0)
    def _():
        out_ref[...] = jnp.zeros_like(out_ref)

    act_sc[...] = lax.dot_general(u_ref[...], x_ref[...], (((1,), (1,)), ((), ())),
                                  preferred_element_type=F32)
    kk = PEER_KEYS
    for tc in range(act_sc.shape[1] // kk):
        lanes = slice(tc * kk, (tc + 1) * kk)
        for i in range(rows):
            w = jnp.zeros((kk, kk), BF16)
            for h in range(PEER_HEADS):
                hit = r2_ref[h, :, lanes] < cnt_ref[h, i:i + 1, lanes].astype(BF16)
                val = e2_ref[h, :, lanes] * e1_ref[h, i:i + 1, lanes].astype(BF16)
                w = w + jnp.where(hit, val, jnp.zeros_like(val))
            g_sc[i * kk:(i + 1) * kk, lanes] = jax.nn.gelu(act_sc[i * kk:(i + 1) * kk, lanes].astype(BF16)) * w
    out_ref[...] += jnp.dot(vt_ref[...], g_sc[...], preferred_element_type=F32)


def _transpose_tile_kernel(v_ref, o_ref):
    o_ref[...] = v_ref[...].T.astype(o_ref.dtype)


def _peer_value_tiles(v, l, te):
    return pl.pallas_call(
        _transpose_tile_kernel,
        grid=(N_EXPERTS // te,),
        in_specs=[pl.BlockSpec((None, te, D_MODEL), lambda e: (l, e, 0))],
        out_specs=pl.BlockSpec((None, D_MODEL, te), lambda e: (e, 0, 0)),
        out_shape=jax.ShapeDtypeStruct((N_EXPERTS // te, D_MODEL, te), BF16),
        compiler_params=_cparams(("parallel",)),
        name="peer_value_tiles",
    )(v)


def _peer_dense(x_bf, u_bf, v, l, sel, *, tt=512, rows=8):
    cnt, r2, e1, e2 = sel
    te = rows * PEER_KEYS
    n_tok = x_bf.shape[0]
    vt_bf = _peer_value_tiles(v, l, te)
    row_spec = pl.BlockSpec((PEER_HEADS, rows, tt), lambda i, e: (0, e, i))
    all_spec = pl.BlockSpec((PEER_HEADS, PEER_KEYS, tt), lambda i, e: (0, 0, i))
    return pl.pallas_call(
        functools.partial(_peer_dense_kernel, rows=rows),
        grid=(n_tok // tt, N_EXPERTS // te),
        in_specs=[pl.BlockSpec((tt, D_MODEL), lambda i, e: (i, 0)),
                  pl.BlockSpec((None, te, D_MODEL), lambda i, e: (l, e, 0)),
                  pl.BlockSpec((None, D_MODEL, te), lambda i, e: (e, 0, 0)),
                  row_spec, all_spec, row_spec, all_spec],
        out_specs=pl.BlockSpec((D_MODEL, tt), lambda i, e: (0, i)),
        out_shape=jax.ShapeDtypeStruct((D_MODEL, n_tok), F32),
        scratch_shapes=[pltpu.VMEM((te, tt), F32), pltpu.VMEM((te, tt), BF16)],
        compiler_params=_cparams(("parallel", "arbitrary")),
        name="peer_dense",
    )(x_bf, u_bf, vt_bf, cnt, r2, e1, e2)


def _hyena_features():
    pos = jnp.arange(SEQ, dtype=F32)
    t = pos / (SEQ - 1)
    bands = jnp.linspace(1e-4, 16 - 1, 16, dtype=F32)
    ang = 2.0 * math.pi * pos[:, None] * bands[None, :] / SEQ
    feat = jnp.concatenate([t[:, None], jnp.cos(ang), -jnp.sin(ang)], axis=-1)
    feat_t = jnp.pad(feat.T, ((0, HY_EMB_PAD - feat.shape[1]), (0, 0)))
    t_row = t[None, :]

    def reversed_time(a):
        return jnp.roll(a[:, ::-1], 1, axis=1)

    return (jnp.concatenate([feat_t, reversed_time(feat_t)], axis=1),
            jnp.concatenate([t_row, reversed_time(t_row)], axis=1))


def kernel(x, w_in, mlstm_gate_bias, mlstm_norm_gain, w_mlstm_out, hyena_conv_w, hyena_conv_b, hyena_w1,
           hyena_b1, hyena_w2, hyena_b2, hyena_freq, hyena_w3, hyena_decay, hyena_skip, w_hyena_out,
           s5_lambda_re, s5_lambda_im, s5_log_step, s5_b_re, s5_b_im, s5_c_re, s5_c_im, s5_skip, w_s5_glu,
           w_out, ln1_g, ln1_b, peer_w_q, peer_subkeys, peer_u, peer_v, ln2_g, ln2_b):
    xf = x.reshape(SEQ, D_MODEL)
    xb = xf.astype(BF16)
    feat_t, t_row = _hyena_features()
    w_in_t = jnp.swapaxes(w_in, 1, 2)
    w_tail = _w_in_tail(w_in_t)
    u_bf = peer_u.astype(BF16)
    s5_mats = jax.vmap(_s5_chunk_matrices)(s5_lambda_re, s5_lambda_im, s5_log_step, s5_b_re, s5_b_im,
                                           s5_c_re, s5_c_im)
    for l in range(DEPTH):
        qkvo = _mm_wt(xb, w_in_t, l, 0, 4 * BRANCH, tm=1024, tn=512, name="proj_qkvo")
        gates_t = _mm_nt(w_in_t, l, OFF_GATES, N_GATES, xb, tr=N_GATES, tm=1024, name="proj_gates", out_dtype=F32)
        hp_t = _mm_nt(w_tail, l, 0, 3 * BRANCH, xb, tr=512, tm=1024, name="proj_hyena_t", out_dtype=BF16)
        s5_in = _mm_wt(xb, w_tail, l, 3 * BRANCH, BRANCH, tm=1024, tn=512, name="proj_s5")

        hdir = _mlstm(qkvo, gates_t, mlstm_gate_bias, l)
        hm = _mlstm_post(hdir, qkvo, mlstm_norm_gain, l)
        hy_t = _hyena(hp_t, l, hyena_conv_w, hyena_conv_b, hyena_w1, hyena_b1, hyena_w2, hyena_b2,
                      hyena_freq, hyena_w3, hyena_decay, hyena_skip, feat_t, t_row)
        s5y = _s5(s5_in, s5_mats, s5_skip, l)
        merged = _merge(hm, hy_t, s5y, xb, w_tail, w_mlstm_out, w_hyena_out, w_s5_glu, l)
        mix = _mm(merged, w_out, l, 0, D_MODEL, tm=1024, tn=512, name="proj_out")
        xf, xb = _res_ln(xf, mix, ln1_g, ln1_b, l, transposed=False)

        q = _mm(xb, peer_w_q, l, 0, PEER_HEADS * 2 * PEER_KEYS, tm=1024, tn=512, name="peer_query")
        sel = _peer_select(q, peer_subkeys, l)
        ffn_t = _peer_dense(xb, u_bf, peer_v, l, sel)
        xf, xb = _res_ln(xf, ffn_t, ln2_g, ln2_b, l, transposed=True)
    return xf.reshape(1, SEQ, D_MODEL)
```

```python
import functools
import math

import numpy as np
import jax
import jax.numpy as jnp
from jax import lax
from jax.experimental import pallas as pl
from jax.experimental.pallas import tpu as pltpu

F32 = jnp.float32
BF16 = jnp.bfloat16

D_MODEL = 2048
SEQ = 8192
DEPTH = 4
BRANCH = 1024
HEADS = 4
HEAD_DIM = 256
CHUNK = 256
N_GATES = 16
HY_HID = 64
HY_EMB_PAD = 128
S5_GROUP = 16
S5_GROUPS = 64
S5_STATE = 64
S5_T = 16
S5_CHUNKS = SEQ // S5_T
S5_LOG_CHUNKS = 9
PEER_HEADS = 8
PEER_KEYS = 128
PEER_TOPK = 16
N_EXPERTS = PEER_KEYS * PEER_KEYS
ALPHA = (2 * DEPTH) ** 0.25
LN_EPS = 1e-5
FFT_N = 2 * SEQ
FFT_R = 128
OFF_GATES = 4 * BRANCH
OFF_HYENA = OFF_GATES + N_GATES
OFF_S5 = OFF_HYENA + 3 * BRANCH
OFF_MIXG = OFF_S5 + BRANCH

VMEM_LIMIT = 56 * 1024 * 1024


def _cparams(sem):
    return pltpu.CompilerParams(dimension_semantics=sem, vmem_limit_bytes=VMEM_LIMIT)


def _split_bf16(a):
    hi = a.astype(BF16)
    lo = (a - hi.astype(F32)).astype(BF16)
    return hi, lo


def _dot3(a, b_hi, b_lo):
    a_hi, a_lo = _split_bf16(a)
    acc = jnp.dot(a_hi, b_hi, preferred_element_type=F32)
    acc += jnp.dot(a_lo, b_hi, preferred_element_type=F32)
    acc += jnp.dot(a_hi, b_lo, preferred_element_type=F32)
    return acc


def _dot3_both(a, b):
    b_hi, b_lo = _split_bf16(b)
    return _dot3(a, b_hi, b_lo)


def _mm_kernel(a_ref, b_ref, o_ref):
    o_ref[...] = jnp.dot(a_ref[...].astype(BF16), b_ref[...].astype(BF16),
                         preferred_element_type=F32).astype(o_ref.dtype)


def _mm(a, w, l, col_off, n_cols, *, tm, tn, name):
    m, k = a.shape
    cb = col_off // tn
    assert col_off % tn == 0 and n_cols % tn == 0 and m % tm == 0
    return pl.pallas_call(
        _mm_kernel,
        grid=(m // tm, n_cols // tn),
        in_specs=[pl.BlockSpec((tm, k), lambda i, j: (i, 0)),
                  pl.BlockSpec((None, k, tn), lambda i, j: (l, 0, cb + j))],
        out_specs=pl.BlockSpec((tm, tn), lambda i, j: (i, j)),
        out_shape=jax.ShapeDtypeStruct((m, n_cols), F32),
        compiler_params=_cparams(("parallel", "parallel")),
        name=name,
    )(a, w)


def _mm_wt_kernel(a_ref, w_ref, o_ref):
    o_ref[...] = lax.dot_general(a_ref[...].astype(BF16), w_ref[...].astype(BF16), (((1,), (1,)), ((), ())),
                                 preferred_element_type=F32).astype(o_ref.dtype)


def _mm_wt(a, wt, l, row_off, n_rows, *, tm, tn, name):
    m, k = a.shape
    rb = row_off // tn
    assert row_off % tn == 0 and n_rows % tn == 0 and m % tm == 0
    return pl.pallas_call(
        _mm_wt_kernel,
        grid=(m // tm, n_rows // tn),
        in_specs=[pl.BlockSpec((tm, k), lambda i, j: (i, 0)),
                  pl.BlockSpec((None, tn, k), lambda i, j: (l, rb + j, 0))],
        out_specs=pl.BlockSpec((tm, tn), lambda i, j: (i, j)),
        out_shape=jax.ShapeDtypeStruct((m, n_rows), F32),
        compiler_params=_cparams(("parallel", "parallel")),
        name=name,
    )(a, wt)


def _mm_nt_kernel(w_ref, a_ref, o_ref):
    o_ref[...] = lax.dot_general(w_ref[...].astype(BF16), a_ref[...].astype(BF16), (((1,), (1,)), ((), ())),
                                 preferred_element_type=F32).astype(o_ref.dtype)


def _mm_nt(wt, l, row_off, n_rows, a, *, tr, tm, name, out_dtype):
    k = wt.shape[2]
    m = a.shape[0]
    rb = row_off // tr
    assert row_off % tr == 0 and n_rows % tr == 0
    return pl.pallas_call(
        _mm_nt_kernel,
        grid=(n_rows // tr, m // tm),
        in_specs=[pl.BlockSpec((None, tr, k), lambda i, j: (l, rb + i, 0)),
                  pl.BlockSpec((tm, k), lambda i, j: (j, 0))],
        out_specs=pl.BlockSpec((tr, tm), lambda i, j: (i, j)),
        out_shape=jax.ShapeDtypeStruct((n_rows, m), out_dtype),
        compiler_params=_cparams(("parallel", "parallel")),
        name=name,
    )(wt, a)


def _w_tail_kernel(cur_ref, nxt_ref, o_ref, *, shift):
    o_ref[...] = jnp.concatenate([cur_ref[shift:, :], nxt_ref[...]], axis=0).astype(o_ref.dtype)


def _w_in_tail(w_in_t, *, tn=512):
    depth, n_all, k = w_in_t.shape
    n_out = n_all - OFF_HYENA
    return pl.pallas_call(
        functools.partial(_w_tail_kernel, shift=N_GATES),
        grid=(depth, n_out // tn),
        in_specs=[pl.BlockSpec((None, tn, k), lambda l, j: (l, OFF_GATES // tn + j, 0)),
                  pl.BlockSpec((None, N_GATES, k), lambda l, j: (l, (OFF_GATES + (j + 1) * tn) // N_GATES, 0))],
        out_specs=pl.BlockSpec((None, tn, k), lambda l, j: (l, j, 0)),
        out_shape=jax.ShapeDtypeStruct((depth, n_out, k), BF16),
        compiler_params=_cparams(("parallel", "parallel")),
        name="w_in_tail",
    )(w_in_t, w_in_t)


def _log_sigmoid(x):
    return jnp.minimum(x, 0.0) - jnp.log1p(jnp.exp(-jnp.abs(x)))


def _mlstm_chunk(d, q, k, v, li_r, lf_r, li_c, lf_c, c_sc, n_sc, m_sc):
    n = CHUNK
    row = lax.broadcasted_iota(jnp.int32, (n, n), 0)
    col = lax.broadcasted_iota(jnp.int32, (n, n), 1)
    valid = (row >= col) if d == 0 else (row <= col)
    valid_t = (col >= row) if d == 0 else (col <= row)
    b_col = jnp.sum(jnp.where(valid, lf_r, 0.0), axis=1, keepdims=True)
    b_row = jnp.sum(jnp.where(valid_t, lf_c, 0.0), axis=0, keepdims=True)
    total = jnp.sum(lf_r, axis=1, keepdims=True)

    m_prev = m_sc[...]
    m_inter = b_col + m_prev
    log_d = jnp.where(valid, b_col - b_row + li_r, -jnp.inf)
    m_t = jnp.maximum(m_inter, jnp.max(log_d, axis=1, keepdims=True))
    inter = jnp.exp(m_inter - m_t)
    dmat = jnp.exp(log_d - m_t)

    k = k * (HEAD_DIM ** -0.5)
    qb = q.astype(BF16)
    kb = k.astype(BF16)
    vb = v.astype(BF16)
    s = lax.dot_general(qb, kb, (((1,), (1,)), ((), ())), preferred_element_type=F32) * dmat
    num = jnp.dot(s.astype(BF16), vb, preferred_element_type=F32)
    num += inter * jnp.dot(qb, c_sc[...].astype(BF16), preferred_element_type=F32)
    qn = jnp.sum(q * n_sc[...], axis=1, keepdims=True)
    den = jnp.sum(s, axis=1, keepdims=True) + inter * qn
    h_out = num / jnp.maximum(jnp.abs(den), jnp.exp(-m_t))

    log_w = total - b_col + li_c
    m_new = jnp.maximum(total + m_prev, jnp.max(log_w, axis=0, keepdims=True))
    w = jnp.exp(log_w - m_new)
    decay = jnp.exp(total + m_prev - m_new)
    kw = k * w
    c_sc[...] = decay * c_sc[...] + lax.dot_general(
        kw.astype(BF16), vb, (((0,), (0,)), ((), ())), preferred_element_type=F32)
    n_sc[...] = decay * n_sc[...] + jnp.sum(kw, axis=0, keepdims=True)
    m_sc[...] = m_new
    return h_out


def _mlstm_kernel(qkv_f_ref, qkv_b_ref, gr_f_ref, gr_b_ref, gc_f_ref, gc_b_ref, bias_ref,
                  o_f_ref, o_b_ref, c_sc, n_sc, m_sc):
    @pl.when(pl.program_id(0) == 0)
    def _():
        c_sc[...] = jnp.zeros_like(c_sc)
        n_sc[...] = jnp.zeros_like(n_sc)
        m_sc[...] = jnp.zeros_like(m_sc)

    for d, (qkv_ref, gr_ref, gc_ref, o_ref) in enumerate(((qkv_f_ref, gr_f_ref, gc_f_ref, o_f_ref),
                                                         (qkv_b_ref, gr_b_ref, gc_b_ref, o_b_ref))):
        for h in range(HEADS):
            cols = slice(h * HEAD_DIM, (h + 1) * HEAD_DIM)
            b_i = bias_ref[d * 2 * HEADS + h]
            b_f = bias_ref[d * 2 * HEADS + HEADS + h]
            chain = d * HEADS + h
            o_ref[:, cols] = _mlstm_chunk(
                d,
                qkv_ref[:, cols],
                qkv_ref[:, BRANCH + h * HEAD_DIM:BRANCH + (h + 1) * HEAD_DIM],
                qkv_ref[:, 2 * BRANCH + h * HEAD_DIM:2 * BRANCH + (h + 1) * HEAD_DIM],
                gr_ref[h] + b_i, _log_sigmoid(gr_ref[HEADS + h] + b_f),
                gc_ref[h] + b_i, _log_sigmoid(gc_ref[HEADS + h] + b_f),
                c_sc.at[chain], n_sc.at[chain], m_sc.at[chain])


def _mlstm(qkvo, gates_t, gate_bias, l):
    nc = SEQ // CHUNK
    g_rows = gates_t.reshape(N_GATES, 1, SEQ)
    g_cols = gates_t.reshape(N_GATES, SEQ, 1)
    bias = gate_bias.reshape(DEPTH, N_GATES, 1, 1)
    half = N_GATES // 2
    chains = 2 * HEADS
    out = jax.ShapeDtypeStruct((SEQ, BRANCH), F32)
    return pl.pallas_call(
        _mlstm_kernel,
        grid=(nc,),
        in_specs=[
            pl.BlockSpec((CHUNK, 4 * BRANCH), lambda c: (c, 0)),
            pl.BlockSpec((CHUNK, 4 * BRANCH), lambda c: (nc - 1 - c, 0)),
            pl.BlockSpec((half, 1, CHUNK), lambda c: (0, 0, c)),
            pl.BlockSpec((half, 1, CHUNK), lambda c: (1, 0, nc - 1 - c)),
            pl.BlockSpec((half, CHUNK, 1), lambda c: (0, c, 0)),
            pl.BlockSpec((half, CHUNK, 1), lambda c: (1, nc - 1 - c, 0)),
            pl.BlockSpec((None, N_GATES, 1, 1), lambda c: (l, 0, 0, 0)),
        ],
        out_specs=[pl.BlockSpec((CHUNK, BRANCH), lambda c: (c, 0)),
                   pl.BlockSpec((CHUNK, BRANCH), lambda c: (nc - 1 - c, 0))],
        out_shape=[out, out],
        scratch_shapes=[pltpu.VMEM((chains, HEAD_DIM, HEAD_DIM), F32),
                        pltpu.VMEM((chains, 1, HEAD_DIM), F32),
                        pltpu.VMEM((chains, 1, 1), F32)],
        compiler_params=_cparams(("arbitrary",)),
        name="mlstm_scan",
    )(qkvo, qkvo, g_rows, g_rows, g_cols, g_cols, bias)


def _mlstm_post_kernel(hf_ref, hb_ref, o_ref, gain_ref, out_ref):
    h = hf_ref[...] + hb_ref[...]
    h = jax.nn.sigmoid(o_ref[...]) * h
    for hd in range(HEADS):
        sl = slice(hd * HEAD_DIM, (hd + 1) * HEAD_DIM)
        hh = h[:, sl]
        mu = jnp.mean(hh, axis=1, keepdims=True)
        var = jnp.mean(jnp.square(hh - mu), axis=1, keepdims=True)
        out_ref[:, sl] = ((hh - mu) * lax.rsqrt(var + LN_EPS) * gain_ref[:, sl]).astype(out_ref.dtype)


def _mlstm_post(hdir, qkvo, gain, l, *, tm=512):
    return pl.pallas_call(
        _mlstm_post_kernel,
        grid=(SEQ // tm,),
        in_specs=[pl.BlockSpec((tm, BRANCH), lambda i: (i, 0)),
                  pl.BlockSpec((tm, BRANCH), lambda i: (i, 0)),
                  pl.BlockSpec((tm, BRANCH), lambda i: (i, 3)),
                  pl.BlockSpec((None, 1, BRANCH), lambda i: (l, 0, 0))],
        out_specs=pl.BlockSpec((tm, BRANCH), lambda i: (i, 0)),
        out_shape=jax.ShapeDtypeStruct((SEQ, BRANCH), BF16),
        compiler_params=_cparams(("parallel",)),
        name="mlstm_post",
    )(hdir[0], hdir[1], qkvo, gain.reshape(DEPTH, 1, BRANCH))


HY_CT = 32
HY_RB = 512
HY_UNROLL = 16


@functools.lru_cache(maxsize=None)
def _fft_constants():
    r = FFT_R
    idx = np.arange(r)
    th = 2.0 * np.pi * np.outer(idx, idx) / r
    cs, sn = np.cos(th), np.sin(th)
    ph = 2.0 * np.pi * np.outer(idx, idx) / FFT_N
    f1 = np.concatenate([cs, -sn], axis=1)
    m3 = np.block([[cs, -sn], [sn, cs]])
    m4 = np.block([[cs, sn], [-sn, cs]])
    m6 = np.concatenate([cs, -sn], axis=0) / FFT_N
    m6[:, r // 2:] = 0.0
    consts = {name: jnp.asarray(mat, F32).astype(BF16)
              for name, mat in (("f1", f1), ("m3", m3), ("m4", m4), ("m6", m6))}
    consts["tw"] = (jnp.asarray(np.cos(ph), F32), jnp.asarray(np.sin(ph), F32))
    return consts


def _rows(i, n):
    return pl.ds(pl.multiple_of(i * n, n), n)


def _for_each(n, body, unroll=1):
    def step(i, carry):
        body(i)
        return carry

    lax.fori_loop(0, n, step, 0, unroll=unroll)


def _fwd_stage1(sig_ref, hi_ref, ct, t_buf, a_buf, b_buf, f1_ref, twr_ref, twi_ref):
    r = FFT_R
    zeros = jnp.zeros((r // 2, r), F32)

    def tr_in(c):
        hi = zeros if hi_ref is None else hi_ref[c]
        t_buf[_rows(c, r), :] = jnp.concatenate([sig_ref[c], hi], axis=0).T.astype(BF16)

    _for_each(ct, tr_in, unroll=HY_UNROLL)

    def stage1(b):
        rows = _rows(b, HY_RB)
        a_buf[rows, :] = jnp.dot(t_buf[rows, :], f1_ref[...], preferred_element_type=F32)

    _for_each(ct * r // HY_RB, stage1, unroll=HY_UNROLL)

    def twiddle(c):
        rows = _rows(c, r)
        a = a_buf[rows, :]
        ar, ai = a[:, :r], a[:, r:]
        twr, twi = twr_ref[...], twi_ref[...]
        b_buf[rows, :r] = (ar * twr + ai * twi).T.astype(BF16)
        b_buf[rows, r:] = (ai * twr - ar * twi).T.astype(BF16)

    _for_each(ct, twiddle, unroll=HY_UNROLL)


def _hy_mlp_kernel(feat_ref, w1_ref, b1_ref, w2_ref, b2_ref, fr_ref, out_ref):
    h = _dot3_both(w1_ref[...], feat_ref[...]) + b1_ref[...]
    h = jnp.sin(fr_ref[0] * h)
    h = _dot3_both(w2_ref[...], h) + b2_ref[...]
    out_ref[...] = jnp.sin(fr_ref[1] * h)


def _hy_filter_kernel(w3_ref, h_ref, dec_ref, t_ref, out_ref):
    d = pl.program_id(0)
    filt = _dot3_both(w3_ref[...], h_ref[...])
    filt = filt * jnp.exp(-t_ref[...] * jnp.abs(dec_ref[...]))
    filt = filt / (jnp.sum(jnp.abs(filt), axis=1, keepdims=True) + 1e-6)
    lane = lax.broadcasted_iota(jnp.int32, filt.shape, 1)
    out_ref[...] = jnp.where((lane == 0) & (d == 1), 0.0, filt)


def _hy_spec_kernel(fa_ref, fb_ref, f1_ref, m3_ref, twr_ref, twi_ref, out_ref, t_buf, a_buf, b_buf, *, ct):
    r = FFT_R
    nblk = ct * r // HY_RB
    _fwd_stage1(fa_ref, fb_ref, ct, t_buf, a_buf, b_buf, f1_ref, twr_ref, twi_ref)

    def put(b):
        rows = _rows(b, HY_RB)
        out_ref[rows, :] = jnp.dot(b_buf[rows, :], m3_ref[...], preferred_element_type=F32)

    _for_each(nblk, put, unroll=HY_UNROLL)


def _shift_prev(p, n1, n2):
    r1 = pltpu.roll(p, 1, 2)
    r2 = pltpu.roll(r1, 1, 1)
    return jnp.where(n2 == 0, jnp.where(n1 == 0, 0.0, r2), r1)


def _shift_next(p, n1, n2):
    s1, s2 = p.shape[1], p.shape[2]
    r1 = pltpu.roll(p, s2 - 1, 2)
    r2 = pltpu.roll(r1, s1 - 1, 1)
    return jnp.where(n2 == s2 - 1, jnp.where(n1 == s1 - 1, 0.0, r2), r1)


def _hy_conv_kernel(p0_ref, p1_ref, p2_ref, cw_ref, cb_ref, skip_ref, h_ref,
                    f1_ref, m3_ref, m4_ref, m6_ref, twr_ref, twi_ref,
                    out_ref, z_buf, t_buf, a_buf, b_buf, *, ct):
    r = FFT_R
    n1_len = SEQ // r
    nblk = ct * r // HY_RB
    shp = (ct, n1_len, r)
    n1 = lax.broadcasted_iota(jnp.int32, shp, 1)
    n2 = lax.broadcasted_iota(jnp.int32, shp, 2)

    def short_conv(p_ref, g):
        p = p_ref[...].astype(F32)
        return (_shift_prev(p, n1, n2) * cw_ref[3 * g] + p * cw_ref[3 * g + 1]
                + _shift_next(p, n1, n2) * cw_ref[3 * g + 2] + cb_ref[g])

    z_buf[...] = short_conv(p1_ref, 1) * short_conv(p2_ref, 2)
    _fwd_stage1(z_buf, None, ct, t_buf, a_buf, b_buf, f1_ref, twr_ref, twi_ref)

    def freq_domain(b):
        rows = _rows(b, HY_RB)
        x = jnp.dot(b_buf[rows, :], m3_ref[...], preferred_element_type=F32)
        h = h_ref[rows, :]
        xr, xi = x[:, :r], x[:, r:]
        hr, hi = h[:, :r], h[:, r:]
        y = jnp.concatenate([xr * hr - xi * hi, xr * hi + xi * hr], axis=1).astype(BF16)
        a_buf[rows, :] = jnp.dot(y, m4_ref[...], preferred_element_type=F32)

    _for_each(nblk, freq_domain, unroll=HY_UNROLL)

    def inv_twiddle(c):
        rows = _rows(c, r)
        b = a_buf[rows, :]
        br, bi = b[:, :r], b[:, r:]
        twr, twi = twr_ref[...], twi_ref[...]
        b_buf[rows, :r] = (br * twr - bi * twi).T.astype(BF16)
        b_buf[rows, r:] = (br * twi + bi * twr).T.astype(BF16)

    _for_each(ct, inv_twiddle, unroll=HY_UNROLL)

    def stage6(b):
        rows = _rows(b, HY_RB)
        a_buf[rows, :r] = jnp.dot(b_buf[rows, :], m6_ref[...], preferred_element_type=F32)

    _for_each(nblk, stage6, unroll=HY_UNROLL)

    def finish(c):
        y_c = a_buf[_rows(c, r), :r].T[:n1_len, :]
        out_ref[c] = y_c + skip_ref[c] * z_buf[c]

    _for_each(ct, finish, unroll=HY_UNROLL)
    out_ref[...] = short_conv(p0_ref, 0) * out_ref[...]


def _hyena(hp_t, l, conv_w, conv_b, w1, b1, w2, b2, freq, w3, decay, skip, feat_t, t_row):
    r = FFT_R
    n1 = SEQ // r
    ct = HY_CT
    k = _fft_constants()
    f1, m3, m4, m6 = k["f1"], k["m3"], k["m4"], k["m6"]
    twr, twi = k["tw"]

    def full(a):
        return pl.BlockSpec(a.shape, lambda *_: (0,) * a.ndim)

    h2 = pl.pallas_call(
        _hy_mlp_kernel,
        grid=(2,),
        in_specs=[pl.BlockSpec((feat_t.shape[0], SEQ), lambda i: (0, i)),
                  pl.BlockSpec((None, HY_HID, feat_t.shape[0]), lambda i: (l, 0, 0)),
                  pl.BlockSpec((None, HY_HID, 1), lambda i: (l, 0, 0)),
                  pl.BlockSpec((None, HY_HID, HY_HID), lambda i: (l, 0, 0)),
                  pl.BlockSpec((None, HY_HID, 1), lambda i: (l, 0, 0)),
                  pl.BlockSpec((None, 2, HY_HID, 1), lambda i: (l, 0, 0, 0))],
        out_specs=pl.BlockSpec((HY_HID, SEQ), lambda i: (0, i)),
        out_shape=jax.ShapeDtypeStruct((HY_HID, 2 * SEQ), F32),
        compiler_params=_cparams(("parallel",)),
        name="hyena_filter_mlp",
    )(feat_t, jnp.pad(jnp.swapaxes(w1, 1, 2), ((0, 0), (0, 0), (0, feat_t.shape[0] - w1.shape[1]))),
      b1[:, :, None], jnp.swapaxes(w2, 1, 2), b2[:, :, None], freq[:, :, :, None])

    rt = 128
    filt = pl.pallas_call(
        _hy_filter_kernel,
        grid=(2, BRANCH // rt),
        in_specs=[pl.BlockSpec((None, rt, HY_HID), lambda d, i: (l, d * (BRANCH // rt) + i, 0)),
                  pl.BlockSpec((HY_HID, SEQ), lambda d, i: (0, d)),
                  pl.BlockSpec((None, None, rt, 1), lambda d, i: (l, d, i, 0)),
                  pl.BlockSpec((1, SEQ), lambda d, i: (0, d))],
        out_specs=pl.BlockSpec((None, rt, SEQ), lambda d, i: (d, i, 0)),
        out_shape=jax.ShapeDtypeStruct((2, BRANCH, SEQ), F32),
        compiler_params=_cparams(("parallel", "parallel")),
        name="hyena_filter",
    )(jnp.swapaxes(w3, 1, 2), h2, decay[:, :, :, None], t_row)
    filt = filt.reshape(2, BRANCH, n1, r)

    scratch = [pltpu.VMEM((ct * r, r), BF16), pltpu.VMEM((ct * r, 2 * r), F32), pltpu.VMEM((ct * r, 2 * r), BF16)]
    spec = pl.pallas_call(
        functools.partial(_hy_spec_kernel, ct=ct),
        grid=(BRANCH // ct,),
        in_specs=[pl.BlockSpec((None, ct, n1, r), lambda i: (0, i, 0, 0)),
                  pl.BlockSpec((None, ct, n1, r), lambda i: (1, i, 0, 0)),
                  full(f1), full(m3), full(twr), full(twi)],
        out_specs=pl.BlockSpec((ct * r, 2 * r), lambda i: (i, 0)),
        out_shape=jax.ShapeDtypeStruct((BRANCH * r, 2 * r), F32),
        scratch_shapes=scratch,
        compiler_params=_cparams(("parallel",)),
        name="hyena_filter_spectrum",
    )(filt, filt, f1, m3, twr, twi)

    hp3 = hp_t.reshape(3 * BRANCH, n1, r)
    nb = BRANCH // ct
    cw = conv_w.reshape(DEPTH, 3, 3, BRANCH).transpose(0, 2, 1, 3).reshape(DEPTH, 9, BRANCH, 1, 1)
    cb = conv_b.reshape(DEPTH, 3, BRANCH, 1, 1)
    consts = (f1, m3, m4, m6, twr, twi)
    return pl.pallas_call(
        functools.partial(_hy_conv_kernel, ct=ct),
        grid=(nb,),
        in_specs=[pl.BlockSpec((ct, n1, r), lambda i: (i, 0, 0)),
                  pl.BlockSpec((ct, n1, r), lambda i: (nb + i, 0, 0)),
                  pl.BlockSpec((ct, n1, r), lambda i: (2 * nb + i, 0, 0)),
                  pl.BlockSpec((None, 9, ct, 1, 1), lambda i: (l, 0, i, 0, 0)),
                  pl.BlockSpec((None, 3, ct, 1, 1), lambda i: (l, 0, i, 0, 0)),
                  pl.BlockSpec((None, ct, 1, 1), lambda i: (l, i, 0, 0)),
                  pl.BlockSpec((ct * r, 2 * r), lambda i: (i, 0)),
                  *[full(a) for a in consts]],
        out_specs=pl.BlockSpec((ct, n1, r), lambda i: (i, 0, 0)),
        out_shape=jax.ShapeDtypeStruct((BRANCH, n1, r), F32),
        scratch_shapes=[pltpu.VMEM((ct, n1, r), F32)] + scratch,
        compiler_params=_cparams(("parallel",)),
        name="hyena_conv",
    )(hp3, hp3, hp3, cw, cb, skip.reshape(DEPTH, BRANCH, 1, 1), spec, *consts).reshape(BRANCH, SEQ)


def _s5_chunk_matrices(lam_re, lam_im, log_step, b_re, b_im, c_re, c_im):
    t = S5_T
    lam = lax.complex(lam_re, lam_im)
    step = jnp.exp(log_step)[..., None]
    lam_dt = lam * step
    b_bar = ((jnp.exp(lam_dt) - 1.0) / lam)[..., None] * lax.complex(b_re, b_im)
    c_mat = lax.complex(c_re, c_im)
    pw = jnp.exp(lam_dt[None] * jnp.arange(t + 1, dtype=F32)[:, None, None, None])
    kern = jnp.einsum("dgvp,mdgp,dgpn->mdgvn", c_mat, pw[:t], b_bar, precision="highest").real
    lag = np.arange(t)[:, None] - np.arange(t)[None, :]
    pick = np.stack([lag[:, :, None] == np.arange(t), -lag[:, :, None] == np.arange(t)], axis=-1)
    kfb = jnp.einsum("ajmd,mdgvn->ajgvn", jnp.asarray(pick, F32), kern, precision="highest")
    w_intra = jnp.transpose(kfb, (2, 1, 4, 0, 3)).reshape(S5_GROUPS, t * S5_GROUP, t * S5_GROUP)
    in_f = pw[:t, 0][::-1][..., None] * b_bar[0][None]
    in_b = pw[:t, 1][..., None] * b_bar[1][None]

    def in_mat(m):
        m = jnp.transpose(m, (1, 0, 3, 2)).reshape(S5_GROUPS, t * S5_GROUP, S5_STATE)
        return jnp.concatenate([m.real, m.imag], axis=-1)

    w_in = jnp.concatenate([in_mat(in_f), in_mat(in_b)], axis=-1)
    out_f = c_mat[0][None] * pw[1:, 0][:, :, None, :]
    out_b = c_mat[1][None] * pw[1:, 1][::-1][:, :, None, :]

    def out_mat(m):
        m = jnp.transpose(m, (1, 3, 0, 2)).reshape(S5_GROUPS, S5_STATE, t * S5_GROUP)
        return jnp.concatenate([m.real, -m.imag], axis=1)

    w_out = jnp.concatenate([out_mat(out_f), out_mat(out_b)], axis=1)
    a_pow = jnp.exp(lam_dt[None] * (t * 2.0 ** jnp.arange(S5_LOG_CHUNKS, dtype=F32))[:, None, None, None])
    a_pow = jnp.transpose(a_pow, (2, 1, 0, 3))
    a_re = jnp.concatenate([a_pow.real, a_pow.real], axis=-1)
    a_im = jnp.concatenate([-a_pow.imag, a_pow.imag], axis=-1)
    return w_intra, w_in, w_out, a_re, a_im


S5_TILE_GROUPS = 128 // S5_GROUP


def _s5_group(ub, wi, win, wout, a_re, a_im):
    nc = S5_CHUNKS
    half = S5_STATE
    y = jnp.dot(ub, wi.astype(BF16), preferred_element_type=F32)
    s_in = jnp.dot(ub, win.astype(BF16), preferred_element_type=F32)
    row = lax.broadcasted_iota(jnp.int32, (nc, 2 * half), 0)

    def scan(s, d):
        for step in range(S5_LOG_CHUNKS):
            sh = 1 << step
            if d == 0:
                moved = jnp.where(row >= sh, pltpu.roll(s, sh, 0), 0.0)
            else:
                moved = jnp.where(row < nc - sh, pltpu.roll(s, nc - sh, 0), 0.0)
            swapped = pltpu.roll(moved, half, 1)
            s = s + a_re[d, step:step + 1, :] * moved + a_im[d, step:step + 1, :] * swapped
        if d == 0:
            return jnp.where(row >= 1, pltpu.roll(s, 1, 0), 0.0)
        return jnp.where(row < nc - 1, pltpu.roll(s, nc - 1, 0), 0.0)

    e = jnp.concatenate([scan(s_in[:, :2 * half], 0), scan(s_in[:, 2 * half:], 1)], axis=1)
    return y + jnp.dot(e.astype(BF16), wout.astype(BF16), preferred_element_type=F32)


@functools.lru_cache(maxsize=None)
def _s5_lane_permutation():
    t, tg, gs = S5_T, S5_TILE_GROUPS, S5_GROUP
    src = np.arange(t * tg * gs).reshape(t, tg, gs)
    dst = src.transpose(1, 0, 2).reshape(-1)
    perm = np.zeros((t * tg * gs, t * tg * gs), np.float32)
    perm[dst, np.arange(t * tg * gs)] = 1.0
    return jnp.asarray(perm, BF16)


def _s5_kernel(u_ref, perm_ref, wi_ref, win_ref, wout_ref, are_ref, aim_ref, skip_ref, y_ref, x_sc, y_sc):
    t, nc = S5_T, S5_CHUNKS
    lane = S5_TILE_GROUPS * S5_GROUP
    gw = t * S5_GROUP
    for tau in range(t):
        x_sc[:, tau * lane:(tau + 1) * lane] = u_ref[pl.ds(tau, nc, stride=t), :].astype(BF16)
    u_all = jnp.dot(x_sc[...], perm_ref[...], preferred_element_type=F32).astype(BF16)
    for g in range(S5_TILE_GROUPS):
        y_g = _s5_group(u_all[:, g * gw:(g + 1) * gw], wi_ref[g], win_ref[g], wout_ref[g], are_ref[g], aim_ref[g])
        y_sc[:, g * gw:(g + 1) * gw] = y_g.astype(BF16)
    y_nat = lax.dot_general(y_sc[...], perm_ref[...], (((1,), (1,)), ((), ())), preferred_element_type=F32)
    for tau in range(t):
        rows = pl.ds(tau, nc, stride=t)
        y_ref[rows, :] = y_nat[:, tau * lane:(tau + 1) * lane] + skip_ref[...] * u_ref[rows, :]


def _s5(u, mats, skip, l):
    w_intra, w_in, w_out, a_re, a_im = mats
    t = S5_T
    gw = t * S5_GROUP
    tg = S5_TILE_GROUPS
    lane = tg * S5_GROUP
    perm = _s5_lane_permutation()
    return pl.pallas_call(
        _s5_kernel,
        grid=(BRANCH // lane,),
        in_specs=[pl.BlockSpec((SEQ, lane), lambda j: (0, j)),
                  pl.BlockSpec(perm.shape, lambda j: (0, 0)),
                  pl.BlockSpec((None, tg, gw, gw), lambda j: (l, j, 0, 0)),
                  pl.BlockSpec((None, tg, gw, 4 * S5_STATE), lambda j: (l, j, 0, 0)),
                  pl.BlockSpec((None, tg, 4 * S5_STATE, gw), lambda j: (l, j, 0, 0)),
                  pl.BlockSpec((None, tg, 2, S5_LOG_CHUNKS, 2 * S5_STATE), lambda j: (l, j, 0, 0, 0)),
                  pl.BlockSpec((None, tg, 2, S5_LOG_CHUNKS, 2 * S5_STATE), lambda j: (l, j, 0, 0, 0)),
                  pl.BlockSpec((None, 1, lane), lambda j: (l, 0, j))],
        out_specs=pl.BlockSpec((SEQ, lane), lambda j: (0, j)),
        out_shape=jax.ShapeDtypeStruct((SEQ, BRANCH), F32),
        scratch_shapes=[pltpu.VMEM((S5_CHUNKS, t * lane), BF16), pltpu.VMEM((S5_CHUNKS, t * lane), BF16)],
        compiler_params=_cparams(("parallel",)),
        name="s5_scan",
    )(u, perm, w_intra, w_in, w_out, a_re, a_im, skip.reshape(DEPTH, 1, BRANCH))


def _merge_kernel(hm_ref, hyt_ref, s5_ref, x_ref, wa_ref, wb_ref, wga_ref, wgg_ref, wg0_ref, wg1_ref, wg2_ref,
                  out_ref, hy_sc):
    @pl.when(pl.program_id(1) == 0)
    def _():
        hy_sc[...] = hyt_ref[...].T.astype(BF16)

    def proj(a, w_ref):
        return jnp.dot(a, w_ref[...].astype(BF16), preferred_element_type=F32)

    def gate(w_ref):
        pre = lax.dot_general(x_ref[...], w_ref[...], (((1,), (1,)), ((), ())), preferred_element_type=F32)
        return jax.nn.sigmoid(pre)

    s5b = s5_ref[...].astype(BF16)
    out_a = proj(hm_ref[...], wa_ref)
    out_b = proj(hy_sc[...], wb_ref)
    out_c = proj(s5b, wga_ref) * jax.nn.sigmoid(proj(s5b, wgg_ref))
    merged = gate(wg0_ref) * out_a + gate(wg1_ref) * out_b + gate(wg2_ref) * out_c
    out_ref[...] = merged.astype(out_ref.dtype)


def _merge(hm, hy_t, s5y, x_bf, w_tail, w_a, w_b, w_glu, l, *, tm=512, tn=512):
    nj = D_MODEL // tn
    goff = (OFF_MIXG - OFF_HYENA) // tn

    def gate_spec(g):
        return pl.BlockSpec((None, tn, D_MODEL), lambda i, j: (l, goff + g * nj + j, 0))

    return pl.pallas_call(
        _merge_kernel,
        grid=(SEQ // tm, nj),
        in_specs=[pl.BlockSpec((tm, BRANCH), lambda i, j: (i, 0)),
                  pl.BlockSpec((BRANCH, tm), lambda i, j: (0, i)),
                  pl.BlockSpec((tm, BRANCH), lambda i, j: (i, 0)),
                  pl.BlockSpec((tm, D_MODEL), lambda i, j: (i, 0)),
                  pl.BlockSpec((None, BRANCH, tn), lambda i, j: (l, 0, j)),
                  pl.BlockSpec((None, BRANCH, tn), lambda i, j: (l, 0, j)),
                  pl.BlockSpec((None, BRANCH, tn), lambda i, j: (l, 0, j)),
                  pl.BlockSpec((None, BRANCH, tn), lambda i, j: (l, 0, nj + j)),
                  gate_spec(0), gate_spec(1), gate_spec(2)],
        out_specs=pl.BlockSpec((tm, tn), lambda i, j: (i, j)),
        out_shape=jax.ShapeDtypeStruct((SEQ, D_MODEL), BF16),
        scratch_shapes=[pltpu.VMEM((tm, BRANCH), BF16)],
        compiler_params=_cparams(("parallel", "arbitrary")),
        name="branch_merge",
    )(hm, hy_t, s5y, x_bf, w_a, w_b, w_glu, w_glu, w_tail, w_tail, w_tail)


def _layer_norm_rows(y, g, b):
    mu = jnp.mean(y, axis=1, keepdims=True)
    var = jnp.mean(jnp.square(y - mu), axis=1, keepdims=True)
    return (y - mu) * lax.rsqrt(var + LN_EPS) * g + b


def _res_ln_kernel(x_ref, f_ref, g_ref, b_ref, o_ref, ob_ref, *, transposed):
    f = f_ref[...]
    if transposed:
        f = f.T
    y = _layer_norm_rows(ALPHA * x_ref[...] + f, g_ref[...], b_ref[...])
    o_ref[...] = y
    ob_ref[...] = y.astype(BF16)


def _res_ln(x, f, g, b, l, *, transposed, tm=256):
    f_spec = (pl.BlockSpec((D_MODEL, tm), lambda i: (0, i)) if transposed
              else pl.BlockSpec((tm, D_MODEL), lambda i: (i, 0)))
    n_tok = x.shape[0]
    return pl.pallas_call(
        functools.partial(_res_ln_kernel, transposed=transposed),
        grid=(n_tok // tm,),
        in_specs=[pl.BlockSpec((tm, D_MODEL), lambda i: (i, 0)), f_spec,
                  pl.BlockSpec((None, 1, D_MODEL), lambda i: (l, 0, 0)),
                  pl.BlockSpec((None, 1, D_MODEL), lambda i: (l, 0, 0))],
        out_specs=[pl.BlockSpec((tm, D_MODEL), lambda i: (i, 0)),
                   pl.BlockSpec((tm, D_MODEL), lambda i: (i, 0))],
        out_shape=[jax.ShapeDtypeStruct((n_tok, D_MODEL), F32), jax.ShapeDtypeStruct((n_tok, D_MODEL), BF16)],
        compiler_params=_cparams(("parallel",)),
        name="residual_layernorm",
    )(x, f, g.reshape(DEPTH, 1, D_MODEL), b.reshape(DEPTH, 1, D_MODEL))


PEER_NO_RANK = 64.0


def _extract_sorted(s, n, out_ref, base, want_rank=False):
    rank = jnp.full_like(s, PEER_NO_RANK) if want_rank else None
    for r in range(n):
        m = jnp.max(s, axis=0, keepdims=True)
        out_ref[base + r:base + r + 1, :] = m
        hit = s == m
        if want_rank:
            rank = jnp.where(hit, float(r), rank)
        s = jnp.where(hit, -jnp.inf, s)
    return rank


def _peer_select_kernel(q_ref, keys_ref, cnt_ref, r2_ref, e1_ref, e2_ref, top_sc, kth_sc):
    k = PEER_TOPK
    qb = q_ref[...].astype(BF16)
    half = PEER_KEYS
    for h in range(PEER_HEADS):
        sc = []
        rank2 = None
        for c in range(2):
            qs = qb[:, (2 * h + c) * half:(2 * h + c + 1) * half]
            s = lax.dot_general(keys_ref[h, c].astype(BF16), qs, (((1,), (1,)), ((), ())),
                                preferred_element_type=F32)
            sc.append(s)
            rank2 = _extract_sorted(s, k, top_sc, c * k, want_rank=(c == 1))
        a = top_sc[0:k, :]
        b = top_sc[k:2 * k, :]
        sub = lax.broadcasted_iota(jnp.int32, (8, a.shape[1]), 0)
        groups = [a[0:1] + b[0:8], a[0:1] + b[8:16], a[1:2] + b[0:8]]
        for i in range(2, 8):
            groups.append(jnp.where(sub < k // (i + 1), a[i:i + 1] + b[0:8], -jnp.inf))
        groups.append(a[8:16] + b[0:1])
        cand = jnp.concatenate(groups, axis=0)
        _extract_sorted(cand, k, kth_sc, 0)
        tau = kth_sc[k - 1:k, :]
        top = a[0:1] + b[0:1]
        z = jnp.sum(jnp.where(cand >= tau, jnp.exp(cand - top), 0.0), axis=0, keepdims=True)
        cnt = jnp.zeros_like(sc[0])
        for j in range(k):
            bj = top_sc[k + j:k + j + 1, :]
            cnt = jnp.where(sc[0] + bj >= tau, j + 1.0, cnt)
        cnt_ref[h] = cnt
        r2_ref[h] = rank2.astype(BF16)
        e1_ref[h] = jnp.exp(sc[0] - a[0:1])
        e2_ref[h] = (jnp.exp(sc[1] - b[0:1]) / z).astype(BF16)


def _peer_select(q, keys, l, *, tt=128):
    n_tok = q.shape[0]
    big = jax.ShapeDtypeStruct((PEER_HEADS, PEER_KEYS, n_tok), F32)
    big_bf = jax.ShapeDtypeStruct((PEER_HEADS, PEER_KEYS, n_tok), BF16)
    big_spec = pl.BlockSpec((PEER_HEADS, PEER_KEYS, tt), lambda i: (0, 0, i))
    return pl.pallas_call(
        _peer_select_kernel,
        grid=(n_tok // tt,),
        in_specs=[pl.BlockSpec((tt, D_MODEL), lambda i: (i, 0)),
                  pl.BlockSpec((None, PEER_HEADS, 2, PEER_KEYS, PEER_KEYS), lambda i: (l, 0, 0, 0, 0))],
        out_specs=[big_spec, big_spec, big_spec, big_spec],
        out_shape=[big, big_bf, big, big_bf],
        scratch_shapes=[pltpu.VMEM((2 * PEER_TOPK, tt), F32), pltpu.VMEM((PEER_TOPK, tt), F32)],
        compiler_params=_cparams(("parallel",)),
        name="peer_select",
    )(q, keys)


def _peer_dense_kernel(x_ref, u_ref, vt_ref, cnt_ref, r2_ref, e1_ref, e2_ref, out_ref, act_sc, g_sc, *, rows):
    e = pl.program_id(1)

    @pl.when(e == 0)
    def _():
        out_ref[...] = jnp.zeros_like(out_ref)

    act_sc[...] = lax.dot_general(u_ref[...], x_ref[...], (((1,), (1,)), ((), ())),
                                  preferred_element_type=F32)
    kk = PEER_KEYS
    for tc in range(act_sc.shape[1] // kk):
        lanes = slice(tc * kk, (tc + 1) * kk)
        for i in range(rows):
            w = jnp.zeros((kk, kk), BF16)
            for h in range(PEER_HEADS):
                hit = r2_ref[h, :, lanes] < cnt_ref[h, i:i + 1, lanes].astype(BF16)
                val = e2_ref[h, :, lanes] * e1_ref[h, i:i + 1, lanes].astype(BF16)
                w = w + jnp.where(hit, val, jnp.zeros_like(val))
            g_sc[i * kk:(i + 1) * kk, lanes] = jax.nn.gelu(act_sc[i * kk:(i + 1) * kk, lanes].astype(BF16)) * w
    out_ref[...] += jnp.dot(vt_ref[...], g_sc[...], preferred_element_type=F32)


def _transpose_tile_kernel(v_ref, o_ref):
    o_ref[...] = v_ref[...].T.astype(o_ref.dtype)


def _peer_value_tiles(v, l, te):
    return pl.pallas_call(
        _transpose_tile_kernel,
        grid=(N_EXPERTS // te,),
        in_specs=[pl.BlockSpec((None, te, D_MODEL), lambda e: (l, e, 0))],
        out_specs=pl.BlockSpec((None, D_MODEL, te), lambda e: (e, 0, 0)),
        out_shape=jax.ShapeDtypeStruct((N_EXPERTS // te, D_MODEL, te), BF16),
        compiler_params=_cparams(("parallel",)),
        name="peer_value_tiles",
    )(v)


def _peer_dense(x_bf, u_bf, v, l, sel, *, tt=512, rows=8):
    cnt, r2, e1, e2 = sel
    te = rows * PEER_KEYS
    n_tok = x_bf.shape[0]
    vt_bf = _peer_value_tiles(v, l, te)
    row_spec = pl.BlockSpec((PEER_HEADS, rows, tt), lambda i, e: (0, e, i))
    all_spec = pl.BlockSpec((PEER_HEADS, PEER_KEYS, tt), lambda i, e: (0, 0, i))
    return pl.pallas_call(
        functools.partial(_peer_dense_kernel, rows=rows),
        grid=(n_tok // tt, N_EXPERTS // te),
        in_specs=[pl.BlockSpec((tt, D_MODEL), lambda i, e: (i, 0)),
                  pl.BlockSpec((None, te, D_MODEL), lambda i, e: (l, e, 0)),
                  pl.BlockSpec((None, D_MODEL, te), lambda i, e: (e, 0, 0)),
                  row_spec, all_spec, row_spec, all_spec],
        out_specs=pl.BlockSpec((D_MODEL, tt), lambda i, e: (0, i)),
        out_shape=jax.ShapeDtypeStruct((D_MODEL, n_tok), F32),
        scratch_shapes=[pltpu.VMEM((te, tt), F32), pltpu.VMEM((te, tt), BF16)],
        compiler_params=_cparams(("parallel", "arbitrary")),
        name="peer_dense",
    )(x_bf, u_bf, vt_bf, cnt, r2, e1, e2)


def _hyena_features():
    pos = jnp.arange(SEQ, dtype=F32)
    t = pos / (SEQ - 1)
    bands = jnp.linspace(1e-4, 16 - 1, 16, dtype=F32)
    ang = 2.0 * math.pi * pos[:, None] * bands[None, :] / SEQ
    feat = jnp.concatenate([t[:, None], jnp.cos(ang), -jnp.sin(ang)], axis=-1)
    feat_t = jnp.pad(feat.T, ((0, HY_EMB_PAD - feat.shape[1]), (0, 0)))
    t_row = t[None, :]

    def reversed_time(a):
        return jnp.roll(a[:, ::-1], 1, axis=1)

    return (jnp.concatenate([feat_t, reversed_time(feat_t)], axis=1),
            jnp.concatenate([t_row, reversed_time(t_row)], axis=1))


def kernel(x, w_in, mlstm_gate_bias, mlstm_norm_gain, w_mlstm_out, hyena_conv_w, hyena_conv_b, hyena_w1,
           hyena_b1, hyena_w2, hyena_b2, hyena_freq, hyena_w3, hyena_decay, hyena_skip, w_hyena_out,
           s5_lambda_re, s5_lambda_im, s5_log_step, s5_b_re, s5_b_im, s5_c_re, s5_c_im, s5_skip, w_s5_glu,
           w_out, ln1_g, ln1_b, peer_w_q, peer_subkeys, peer_u, peer_v, ln2_g, ln2_b):
    xf = x.reshape(SEQ, D_MODEL)
    xb = xf.astype(BF16)
    feat_t, t_row = _hyena_features()
    w_in_t = jnp.swapaxes(w_in, 1, 2)
    w_tail = _w_in_tail(w_in_t)
    u_bf = peer_u.astype(BF16)
    s5_mats = jax.vmap(_s5_chunk_matrices)(s5_lambda_re, s5_lambda_im, s5_log_step, s5_b_re, s5_b_im,
                                           s5_c_re, s5_c_im)
    for l in range(DEPTH):
        qkvo = _mm_wt(xb, w_in_t, l, 0, 4 * BRANCH, tm=1024, tn=512, name="proj_qkvo")
        gates_t = _mm_nt(w_in_t, l, OFF_GATES, N_GATES, xb, tr=N_GATES, tm=1024, name="proj_gates", out_dtype=F32)
        hp_t = _mm_nt(w_tail, l, 0, 3 * BRANCH, xb, tr=512, tm=1024, name="proj_hyena_t", out_dtype=BF16)
        s5_in = _mm_wt(xb, w_tail, l, 3 * BRANCH, BRANCH, tm=1024, tn=512, name="proj_s5")

        hdir = _mlstm(qkvo, gates_t, mlstm_gate_bias, l)
        hm = _mlstm_post(hdir, qkvo, mlstm_norm_gain, l)
        hy_t = _hyena(hp_t, l, hyena_conv_w, hyena_conv_b, hyena_w1, hyena_b1, hyena_w2, hyena_b2,
                      hyena_freq, hyena_w3, hyena_decay, hyena_skip, feat_t, t_row)
        s5y = _s5(s5_in, s5_mats, s5_skip, l)
        merged = _merge(hm, hy_t, s5y, xb, w_tail, w_mlstm_out, w_hyena_out, w_s5_glu, l)
        mix = _mm(merged, w_out, l, 0, D_MODEL, tm=1024, tn=512, name="proj_out")
        xf, xb = _res_ln(xf, mix, ln1_g, ln1_b, l, transposed=False)

        q = _mm(xb, peer_w_q, l, 0, PEER_HEADS * 2 * PEER_KEYS, tm=1024, tn=512, name="peer_query")
        sel = _peer_select(q, peer_subkeys, l)
        ffn_t = _peer_dense(xb, u_bf, peer_v, l, sel)
        xf, xb = _res_ln(xf, ffn_t, ln2_g, ln2_b, l, transposed=True)
    return xf.reshape(1, SEQ, D_MODEL)
```

```python
import functools
import math

import numpy as np
import jax
import jax.numpy as jnp
from jax import lax
from jax.experimental import pallas as pl
from jax.experimental.pallas import tpu as pltpu

F32 = jnp.float32
BF16 = jnp.bfloat16

D_MODEL = 2048
SEQ = 8192
DEPTH = 4
BRANCH = 1024
HEADS = 4
HEAD_DIM = 256
CHUNK = 256
N_GATES = 16
HY_HID = 64
HY_EMB_PAD = 128
S5_GROUP = 16
S5_GROUPS = 64
S5_STATE = 64
S5_T = 16
S5_CHUNKS = SEQ // S5_T
S5_LOG_CHUNKS = 9
PEER_HEADS = 8
PEER_KEYS = 128
PEER_TOPK = 16
N_EXPERTS = PEER_KEYS * PEER_KEYS
ALPHA = (2 * DEPTH) ** 0.25
LN_EPS = 1e-5
FFT_N = 2 * SEQ
FFT_R = 128
OFF_GATES = 4 * BRANCH
OFF_HYENA = OFF_GATES + N_GATES
OFF_S5 = OFF_HYENA + 3 * BRANCH
OFF_MIXG = OFF_S5 + BRANCH

VMEM_LIMIT = 56 * 1024 * 1024


def _cparams(sem):
    return pltpu.CompilerParams(dimension_semantics=sem, vmem_limit_bytes=VMEM_LIMIT)


def _split_bf16(a):
    hi = a.astype(BF16)
    lo = (a - hi.astype(F32)).astype(BF16)
    return hi, lo


def _dot3(a, b_hi, b_lo):
    a_hi, a_lo = _split_bf16(a)
    acc = jnp.dot(a_hi, b_hi, preferred_element_type=F32)
    acc += jnp.dot(a_lo, b_hi, preferred_element_type=F32)
    acc += jnp.dot(a_hi, b_lo, preferred_element_type=F32)
    return acc


def _dot3_both(a, b):
    b_hi, b_lo = _split_bf16(b)
    return _dot3(a, b_hi, b_lo)


def _mm_kernel(a_ref, b_ref, o_ref):
    o_ref[...] = jnp.dot(a_ref[...].astype(BF16), b_ref[...].astype(BF16),
                         preferred_element_type=F32).astype(o_ref.dtype)


def _mm(a, w, l, col_off, n_cols, *, tm, tn, name):
    m, k = a.shape
    cb = col_off // tn
    assert col_off % tn == 0 and n_cols % tn == 0 and m % tm == 0
    return pl.pallas_call(
        _mm_kernel,
        grid=(m // tm, n_cols // tn),
        in_specs=[pl.BlockSpec((tm, k), lambda i, j: (i, 0)),
                  pl.BlockSpec((None, k, tn), lambda i, j: (l, 0, cb + j))],
        out_specs=pl.BlockSpec((tm, tn), lambda i, j: (i, j)),
        out_shape=jax.ShapeDtypeStruct((m, n_cols), F32),
        compiler_params=_cparams(("parallel", "parallel")),
        name=name,
    )(a, w)


def _mm_wt_kernel(a_ref, w_ref, o_ref):
    o_ref[...] = lax.dot_general(a_ref[...].astype(BF16), w_ref[...].astype(BF16), (((1,), (1,)), ((), ())),
                                 preferred_element_type=F32).astype(o_ref.dtype)


def _mm_wt(a, wt, l, row_off, n_rows, *, tm, tn, name):
    m, k = a.shape
    rb = row_off // tn
    assert row_off % tn == 0 and n_rows % tn == 0 and m % tm == 0
    return pl.pallas_call(
        _mm_wt_kernel,
        grid=(m // tm, n_rows // tn),
        in_specs=[pl.BlockSpec((tm, k), lambda i, j: (i, 0)),
                  pl.BlockSpec((None, tn, k), lambda i, j: (l, rb + j, 0))],
        out_specs=pl.BlockSpec((tm, tn), lambda i, j: (i, j)),
        out_shape=jax.ShapeDtypeStruct((m, n_rows), F32),
        compiler_params=_cparams(("parallel", "parallel")),
        name=name,
    )(a, wt)


def _mm_nt_kernel(w_ref, a_ref, o_ref):
    o_ref[...] = lax.dot_general(w_ref[...].astype(BF16), a_ref[...].astype(BF16), (((1,), (1,)), ((), ())),
                                 preferred_element_type=F32).astype(o_ref.dtype)


def _mm_nt(wt, l, row_off, n_rows, a, *, tr, tm, name, out_dtype):
    k = wt.shape[2]
    m = a.shape[0]
    rb = row_off // tr
    assert row_off % tr == 0 and n_rows % tr == 0
    return pl.pallas_call(
        _mm_nt_kernel,
        grid=(n_rows // tr, m // tm),
        in_specs=[pl.BlockSpec((None, tr, k), lambda i, j: (l, rb + i, 0)),
                  pl.BlockSpec((tm, k), lambda i, j: (j, 0))],
        out_specs=pl.BlockSpec((tr, tm), lambda i, j: (i, j)),
        out_shape=jax.ShapeDtypeStruct((n_rows, m), out_dtype),
        compiler_params=_cparams(("parallel", "parallel")),
        name=name,
    )(wt, a)


def _w_tail_kernel(cur_ref, nxt_ref, o_ref, *, shift):
    o_ref[...] = jnp.concatenate([cur_ref[shift:, :], nxt_ref[...]], axis=0).astype(o_ref.dtype)


def _w_in_tail(w_in_t, *, tn=512):
    depth, n_all, k = w_in_t.shape
    n_out = n_all - OFF_HYENA
    return pl.pallas_call(
        functools.partial(_w_tail_kernel, shift=N_GATES),
        grid=(depth, n_out // tn),
        in_specs=[pl.BlockSpec((None, tn, k), lambda l, j: (l, OFF_GATES // tn + j, 0)),
                  pl.BlockSpec((None, N_GATES, k), lambda l, j: (l, (OFF_GATES + (j + 1) * tn) // N_GATES, 0))],
        out_specs=pl.BlockSpec((None, tn, k), lambda l, j: (l, j, 0)),
        out_shape=jax.ShapeDtypeStruct((depth, n_out, k), BF16),
        compiler_params=_cparams(("parallel", "parallel")),
        name="w_in_tail",
    )(w_in_t, w_in_t)


def _log_sigmoid(x):
    return jnp.minimum(x, 0.0) - jnp.log1p(jnp.exp(-jnp.abs(x)))


def _mlstm_chunk(d, q, k, v, li_r, lf_r, li_c, lf_c, c_sc, n_sc, m_sc):
    n = CHUNK
    row = lax.broadcasted_iota(jnp.int32, (n, n), 0)
    col = lax.broadcasted_iota(jnp.int32, (n, n), 1)
    valid = (row >= col) if d == 0 else (row <= col)
    valid_t = (col >= row) if d == 0 else (col <= row)
    b_col = jnp.sum(jnp.where(valid, lf_r, 0.0), axis=1, keepdims=True)
    b_row = jnp.sum(jnp.where(valid_t, lf_c, 0.0), axis=0, keepdims=True)
    total = jnp.sum(lf_r, axis=1, keepdims=True)

    m_prev = m_sc[...]
    m_inter = b_col + m_prev
    log_d = jnp.where(valid, b_col - b_row + li_r, -jnp.inf)
    m_t = jnp.maximum(m_inter, jnp.max(log_d, axis=1, keepdims=True))
    inter = jnp.exp(m_inter - m_t)
    dmat = jnp.exp(log_d - m_t)

    k = k * (HEAD_DIM ** -0.5)
    qb = q.astype(BF16)
    kb = k.astype(BF16)
    vb = v.astype(BF16)
    s = lax.dot_general(qb, kb, (((1,), (1,)), ((), ())), preferred_element_type=F32) * dmat
    num = jnp.dot(s.astype(BF16), vb, preferred_element_type=F32)
    num += inter * jnp.dot(qb, c_sc[...].astype(BF16), preferred_element_type=F32)
    qn = jnp.sum(q * n_sc[...], axis=1, keepdims=True)
    den = jnp.sum(s, axis=1, keepdims=True) + inter * qn
    h_out = num / jnp.maximum(jnp.abs(den), jnp.exp(-m_t))

    log_w = total - b_col + li_c
    m_new = jnp.maximum(total + m_prev, jnp.max(log_w, axis=0, keepdims=True))
    w = jnp.exp(log_w - m_new)
    decay = jnp.exp(total + m_prev - m_new)
    kw = k * w
    c_sc[...] = decay * c_sc[...] + lax.dot_general(
        kw.astype(BF16), vb, (((0,), (0,)), ((), ())), preferred_element_type=F32)
    n_sc[...] = decay * n_sc[...] + jnp.sum(kw, axis=0, keepdims=True)
    m_sc[...] = m_new
    return h_out


def _mlstm_kernel(qkv_f_ref, qkv_b_ref, gr_f_ref, gr_b_ref, gc_f_ref, gc_b_ref, bias_ref,
                  o_f_ref, o_b_ref, c_sc, n_sc, m_sc):
    @pl.when(pl.program_id(0) == 0)
    def _():
        c_sc[...] = jnp.zeros_like(c_sc)
        n_sc[...] = jnp.zeros_like(n_sc)
        m_sc[...] = jnp.zeros_like(m_sc)

    for d, (qkv_ref, gr_ref, gc_ref, o_ref) in enumerate(((qkv_f_ref, gr_f_ref, gc_f_ref, o_f_ref),
                                                         (qkv_b_ref, gr_b_ref, gc_b_ref, o_b_ref))):
        for h in range(HEADS):
            cols = slice(h * HEAD_DIM, (h + 1) * HEAD_DIM)
            b_i = bias_ref[d * 2 * HEADS + h]
            b_f = bias_ref[d * 2 * HEADS + HEADS + h]
            chain = d * HEADS + h
            o_ref[:, cols] = _mlstm_chunk(
                d,
                qkv_ref[:, cols],
                qkv_ref[:, BRANCH + h * HEAD_DIM:BRANCH + (h + 1) * HEAD_DIM],
                qkv_ref[:, 2 * BRANCH + h * HEAD_DIM:2 * BRANCH + (h + 1) * HEAD_DIM],
                gr_ref[h] + b_i, _log_sigmoid(gr_ref[HEADS + h] + b_f),
                gc_ref[h] + b_i, _log_sigmoid(gc_ref[HEADS + h] + b_f),
                c_sc.at[chain], n_sc.at[chain], m_sc.at[chain])


def _mlstm(qkvo, gates_t, gate_bias, l):
    nc = SEQ // CHUNK
    g_rows = gates_t.reshape(N_GATES, 1, SEQ)
    g_cols = gates_t.reshape(N_GATES, SEQ, 1)
    bias = gate_bias.reshape(DEPTH, N_GATES, 1, 1)
    half = N_GATES // 2
    chains = 2 * HEADS
    out = jax.ShapeDtypeStruct((SEQ, BRANCH), F32)
    return pl.pallas_call(
        _mlstm_kernel,
        grid=(nc,),
        in_specs=[
            pl.BlockSpec((CHUNK, 4 * BRANCH), lambda c: (c, 0)),
            pl.BlockSpec((CHUNK, 4 * BRANCH), lambda c: (nc - 1 - c, 0)),
            pl.BlockSpec((half, 1, CHUNK), lambda c: (0, 0, c)),
            pl.BlockSpec((half, 1, CHUNK), lambda c: (1, 0, nc - 1 - c)),
            pl.BlockSpec((half, CHUNK, 1), lambda c: (0, c, 0)),
            pl.BlockSpec((half, CHUNK, 1), lambda c: (1, nc - 1 - c, 0)),
            pl.BlockSpec((None, N_GATES, 1, 1), lambda c: (l, 0, 0, 0)),
        ],
        out_specs=[pl.BlockSpec((CHUNK, BRANCH), lambda c: (c, 0)),
                   pl.BlockSpec((CHUNK, BRANCH), lambda c: (nc - 1 - c, 0))],
        out_shape=[out, out],
        scratch_shapes=[pltpu.VMEM((chains, HEAD_DIM, HEAD_DIM), F32),
                        pltpu.VMEM((chains, 1, HEAD_DIM), F32),
                        pltpu.VMEM((chains, 1, 1), F32)],
        compiler_params=_cparams(("arbitrary",)),
        name="mlstm_scan",
    )(qkvo, qkvo, g_rows, g_rows, g_cols, g_cols, bias)


def _mlstm_post_kernel(hf_ref, hb_ref, o_ref, gain_ref, out_ref):
    h = hf_ref[...] + hb_ref[...]
    h = jax.nn.sigmoid(o_ref[...]) * h
    for hd in range(HEADS):
        sl = slice(hd * HEAD_DIM, (hd + 1) * HEAD_DIM)
        hh = h[:, sl]
        mu = jnp.mean(hh, axis=1, keepdims=True)
        var = jnp.mean(jnp.square(hh - mu), axis=1, keepdims=True)
        out_ref[:, sl] = ((hh - mu) * lax.rsqrt(var + LN_EPS) * gain_ref[:, sl]).astype(out_ref.dtype)


def _mlstm_post(hdir, qkvo, gain, l, *, tm=512):
    return pl.pallas_call(
        _mlstm_post_kernel,
        grid=(SEQ // tm,),
        in_specs=[pl.BlockSpec((tm, BRANCH), lambda i: (i, 0)),
                  pl.BlockSpec((tm, BRANCH), lambda i: (i, 0)),
                  pl.BlockSpec((tm, BRANCH), lambda i: (i, 3)),
                  pl.BlockSpec((None, 1, BRANCH), lambda i: (l, 0, 0))],
        out_specs=pl.BlockSpec((tm, BRANCH), lambda i: (i, 0)),
        out_shape=jax.ShapeDtypeStruct((SEQ, BRANCH), BF16),
        compiler_params=_cparams(("parallel",)),
        name="mlstm_post",
    )(hdir[0], hdir[1], qkvo, gain.reshape(DEPTH, 1, BRANCH))


HY_CT = 32
HY_RB = 512
HY_UNROLL = 16


@functools.lru_cache(maxsize=None)
def _fft_constants():
    r = FFT_R
    idx = np.arange(r)
    th = 2.0 * np.pi * np.outer(idx, idx) / r
    cs, sn = np.cos(th), np.sin(th)
    ph = 2.0 * np.pi * np.outer(idx, idx) / FFT_N
    f1 = np.concatenate([cs, -sn], axis=1)
    m3 = np.block([[cs, -sn], [sn, cs]])
    m4 = np.block([[cs, sn], [-sn, cs]])
    m6 = np.concatenate([cs, -sn], axis=0) / FFT_N
    m6[:, r // 2:] = 0.0
    consts = {name: jnp.asarray(mat, F32).astype(BF16)
              for name, mat in (("f1", f1), ("m3", m3), ("m4", m4), ("m6", m6))}
    consts["tw"] = (jnp.asarray(np.cos(ph), F32), jnp.asarray(np.sin(ph), F32))
    return consts


def _rows(i, n):
    return pl.ds(pl.multiple_of(i * n, n), n)


def _for_each(n, body, unroll=1):
    def step(i, carry):
        body(i)
        return carry

    lax.fori_loop(0, n, step, 0, unroll=unroll)


def _fwd_stage1(sig_ref, hi_ref, ct, t_buf, a_buf, b_buf, f1_ref, twr_ref, twi_ref):
    r = FFT_R
    zeros = jnp.zeros((r // 2, r), F32)

    def tr_in(c):
        hi = zeros if hi_ref is None else hi_ref[c]
        t_buf[_rows(c, r), :] = jnp.concatenate([sig_ref[c], hi], axis=0).T.astype(BF16)

    _for_each(ct, tr_in, unroll=HY_UNROLL)

    def stage1(b):
        rows = _rows(b, HY_RB)
        a_buf[rows, :] = jnp.dot(t_buf[rows, :], f1_ref[...], preferred_element_type=F32)

    _for_each(ct * r // HY_RB, stage1, unroll=HY_UNROLL)

    def twiddle(c):
        rows = _rows(c, r)
        a = a_buf[rows, :]
        ar, ai = a[:, :r], a[:, r:]
        twr, twi = twr_ref[...], twi_ref[...]
        b_buf[rows, :r] = (ar * twr + ai * twi).T.astype(BF16)
        b_buf[rows, r:] = (ai * twr - ar * twi).T.astype(BF16)

    _for_each(ct, twiddle, unroll=HY_UNROLL)


def _hy_mlp_kernel(feat_ref, w1_ref, b1_ref, w2_ref, b2_ref, fr_ref, out_ref):
    h = _dot3_both(w1_ref[...], feat_ref[...]) + b1_ref[...]
    h = jnp.sin(fr_ref[0] * h)
    h = _dot3_both(w2_ref[...], h) + b2_ref[...]
    out_ref[...] = jnp.sin(fr_ref[1] * h)


def _hy_filter_kernel(w3_ref, h_ref, dec_ref, t_ref, out_ref):
    d = pl.program_id(0)
    filt = _dot3_both(w3_ref[...], h_ref[...])
    filt = filt * jnp.exp(-t_ref[...] * jnp.abs(dec_ref[...]))
    filt = filt / (jnp.sum(jnp.abs(filt), axis=1, keepdims=True) + 1e-6)
    lane = lax.broadcasted_iota(jnp.int32, filt.shape, 1)
    out_ref[...] = jnp.where((lane == 0) & (d == 1), 0.0, filt)


def _hy_spec_kernel(fa_ref, fb_ref, f1_ref, m3_ref, twr_ref, twi_ref, out_ref, t_buf, a_buf, b_buf, *, ct):
    r = FFT_R
    nblk = ct * r // HY_RB
    _fwd_stage1(fa_ref, fb_ref, ct, t_buf, a_buf, b_buf, f1_ref, twr_ref, twi_ref)

    def put(b):
        rows = _rows(b, HY_RB)
        out_ref[rows, :] = jnp.dot(b_buf[rows, :], m3_ref[...], preferred_element_type=F32)

    _for_each(nblk, put, unroll=HY_UNROLL)


def _shift_prev(p, n1, n2):
    r1 = pltpu.roll(p, 1, 2)
    r2 = pltpu.roll(r1, 1, 1)
    return jnp.where(n2 == 0, jnp.where(n1 == 0, 0.0, r2), r1)


def _shift_next(p, n1, n2):
    s1, s2 = p.shape[1], p.shape[2]
    r1 = pltpu.roll(p, s2 - 1, 2)
    r2 = pltpu.roll(r1, s1 - 1, 1)
    return jnp.where(n2 == s2 - 1, jnp.where(n1 == s1 - 1, 0.0, r2), r1)


def _hy_conv_kernel(p0_ref, p1_ref, p2_ref, cw_ref, cb_ref, skip_ref, h_ref,
                    f1_ref, m3_ref, m4_ref, m6_ref, twr_ref, twi_ref,
                    out_ref, z_buf, t_buf, a_buf, b_buf, *, ct):
    r = FFT_R
    n1_len = SEQ // r
    nblk = ct * r // HY_RB
    shp = (ct, n1_len, r)
    n1 = lax.broadcasted_iota(jnp.int32, shp, 1)
    n2 = lax.broadcasted_iota(jnp.int32, shp, 2)

    def short_conv(p_ref, g):
        p = p_ref[...].astype(F32)
        return (_shift_prev(p, n1, n2) * cw_ref[3 * g] + p * cw_ref[3 * g + 1]
                + _shift_next(p, n1, n2) * cw_ref[3 * g + 2] + cb_ref[g])

    z_buf[...] = short_conv(p1_ref, 1) * short_conv(p2_ref, 2)
    _fwd_stage1(z_buf, None, ct, t_buf, a_buf, b_buf, f1_ref, twr_ref, twi_ref)

    def freq_domain(b):
        rows = _rows(b, HY_RB)
        x = jnp.dot(b_buf[rows, :], m3_ref[...], preferred_element_type=F32)
        h = h_ref[rows, :]
        xr, xi = x[:, :r], x[:, r:]
        hr, hi = h[:, :r], h[:, r:]
        y = jnp.concatenate([xr * hr - xi * hi, xr * hi + xi * hr], axis=1).astype(BF16)
        a_buf[rows, :] = jnp.dot(y, m4_ref[...], preferred_element_type=F32)

    _for_each(nblk, freq_domain, unroll=HY_UNROLL)

    def inv_twiddle(c):
        rows = _rows(c, r)
        b = a_buf[rows, :]
        br, bi = b[:, :r], b[:, r:]
        twr, twi = twr_ref[...], twi_ref[...]
        b_buf[rows, :r] = (br * twr - bi * twi).T.astype(BF16)
        b_buf[rows, r:] = (br * twi + bi * twr).T.astype(BF16)

    _for_each(ct, inv_twiddle, unroll=HY_UNROLL)

    def stage6(b):
        rows = _rows(b, HY_RB)
        a_buf[rows, :r] = jnp.dot(b_buf[rows, :], m6_ref[...], preferred_element_type=F32)

    _for_each(nblk, stage6, unroll=HY_UNROLL)

    def finish(c):
        y_c = a_buf[_rows(c, r), :r].T[:n1_len, :]
        out_ref[c] = y_c + skip_ref[c] * z_buf[c]

    _for_each(ct, finish, unroll=HY_UNROLL)
    out_ref[...] = short_conv(p0_ref, 0) * out_ref[...]


def _hyena(hp_t, l, conv_w, conv_b, w1, b1, w2, b2, freq, w3, decay, skip, feat_t, t_row):
    r = FFT_R
    n1 = SEQ // r
    ct = HY_CT
    k = _fft_constants()
    f1, m3, m4, m6 = k["f1"], k["m3"], k["m4"], k["m6"]
    twr, twi = k["tw"]

    def full(a):
        return pl.BlockSpec(a.shape, lambda *_: (0,) * a.ndim)

    h2 = pl.pallas_call(
        _hy_mlp_kernel,
        grid=(2,),
        in_specs=[pl.BlockSpec((feat_t.shape[0], SEQ), lambda i: (0, i)),
                  pl.BlockSpec((None, HY_HID, feat_t.shape[0]), lambda i: (l, 0, 0)),
                  pl.BlockSpec((None, HY_HID, 1), lambda i: (l, 0, 0)),
                  pl.BlockSpec((None, HY_HID, HY_HID), lambda i: (l, 0, 0)),
                  pl.BlockSpec((None, HY_HID, 1), lambda i: (l, 0, 0)),
                  pl.BlockSpec((None, 2, HY_HID, 1), lambda i: (l, 0, 0, 0))],
        out_specs=pl.BlockSpec((HY_HID, SEQ), lambda i: (0, i)),
        out_shape=jax.ShapeDtypeStruct((HY_HID, 2 * SEQ), F32),
        compiler_params=_cparams(("parallel",)),
        name="hyena_filter_mlp",
    )(feat_t, jnp.pad(jnp.swapaxes(w1, 1, 2), ((0, 0), (0, 0), (0, feat_t.shape[0] - w1.shape[1]))),
      b1[:, :, None], jnp.swapaxes(w2, 1, 2), b2[:, :, None], freq[:, :, :, None])

    rt = 128
    filt = pl.pallas_call(
        _hy_filter_kernel,
        grid=(2, BRANCH // rt),
        in_specs=[pl.BlockSpec((None, rt, HY_HID), lambda d, i: (l, d * (BRANCH // rt) + i, 0)),
                  pl.BlockSpec((HY_HID, SEQ), lambda d, i: (0, d)),
                  pl.BlockSpec((None, None, rt, 1), lambda d, i: (l, d, i, 0)),
                  pl.BlockSpec((1, SEQ), lambda d, i: (0, d))],
        out_specs=pl.BlockSpec((None, rt, SEQ), lambda d, i: (d, i, 0)),
        out_shape=jax.ShapeDtypeStruct((2, BRANCH, SEQ), F32),
        compiler_params=_cparams(("parallel", "parallel")),
        name="hyena_filter",
    )(jnp.swapaxes(w3, 1, 2), h2, decay[:, :, :, None], t_row)
    filt = filt.reshape(2, BRANCH, n1, r)

    scratch = [pltpu.VMEM((ct * r, r), BF16), pltpu.VMEM((ct * r, 2 * r), F32), pltpu.VMEM((ct * r, 2 * r), BF16)]
    spec = pl.pallas_call(
        functools.partial(_hy_spec_kernel, ct=ct),
        grid=(BRANCH // ct,),
        in_specs=[pl.BlockSpec((None, ct, n1, r), lambda i: (0, i, 0, 0)),
                  pl.BlockSpec((None, ct, n1, r), lambda i: (1, i, 0, 0)),
                  full(f1), full(m3), full(twr), full(twi)],
        out_specs=pl.BlockSpec((ct * r, 2 * r), lambda i: (i, 0)),
        out_shape=jax.ShapeDtypeStruct((BRANCH * r, 2 * r), F32),
        scratch_shapes=scratch,
        compiler_params=_cparams(("parallel",)),
        name="hyena_filter_spectrum",
    )(filt, filt, f1, m3, twr, twi)

    hp3 = hp_t.reshape(3 * BRANCH, n1, r)
    nb = BRANCH // ct
    cw = conv_w.reshape(DEPTH, 3, 3, BRANCH).transpose(0, 2, 1, 3).reshape(DEPTH, 9, BRANCH, 1, 1)
    cb = conv_b.reshape(DEPTH, 3, BRANCH, 1, 1)
    consts = (f1, m3, m4, m6, twr, twi)
    return pl.pallas_call(
        functools.partial(_hy_conv_kernel, ct=ct),
        grid=(nb,),
        in_specs=[pl.BlockSpec((ct, n1, r), lambda i: (i, 0, 0)),
                  pl.BlockSpec((ct, n1, r), lambda i: (nb + i, 0, 0)),
                  pl.BlockSpec((ct, n1, r), lambda i: (2 * nb + i, 0, 0)),
                  pl.BlockSpec((None, 9, ct, 1, 1), lambda i: (l, 0, i, 0, 0)),
                  pl.BlockSpec((None, 3, ct, 1, 1), lambda i: (l, 0, i, 0, 0)),
                  pl.BlockSpec((None, ct, 1, 1), lambda i: (l, i, 0, 0)),
                  pl.BlockSpec((ct * r, 2 * r), lambda i: (i, 0)),
                  *[full(a) for a in consts]],
        out_specs=pl.BlockSpec((ct, n1, r), lambda i: (i, 0, 0)),
        out_shape=jax.ShapeDtypeStruct((BRANCH, n1, r), F32),
        scratch_shapes=[pltpu.VMEM((ct, n1, r), F32)] + scratch,
        compiler_params=_cparams(("parallel",)),
        name="hyena_conv",
    )(hp3, hp3, hp3, cw, cb, skip.reshape(DEPTH, BRANCH, 1, 1), spec, *consts).reshape(BRANCH, SEQ)


def _s5_chunk_matrices(lam_re, lam_im, log_step, b_re, b_im, c_re, c_im):
    t = S5_T
    lam = lax.complex(lam_re, lam_im)
    step = jnp.exp(log_step)[..., None]
    lam_dt = lam * step
    b_bar = ((jnp.exp(lam_dt) - 1.0) / lam)[..., None] * lax.complex(b_re, b_im)
    c_mat = lax.complex(c_re, c_im)
    pw = jnp.exp(lam_dt[None] * jnp.arange(t + 1, dtype=F32)[:, None, None, None])
    kern = jnp.einsum("dgvp,mdgp,dgpn->mdgvn", c_mat, pw[:t], b_bar, precision="highest").real
    lag = np.arange(t)[:, None] - np.arange(t)[None, :]
    pick = np.stack([lag[:, :, None] == np.arange(t), -lag[:, :, None] == np.arange(t)], axis=-1)
    kfb = jnp.einsum("ajmd,mdgvn->ajgvn", jnp.asarray(pick, F32), kern, precision="highest")
    w_intra = jnp.transpose(kfb, (2, 1, 4, 0, 3)).reshape(S5_GROUPS, t * S5_GROUP, t * S5_GROUP)
    in_f = pw[:t, 0][::-1][..., None] * b_bar[0][None]
    in_b = pw[:t, 1][..., None] * b_bar[1][None]

    def in_mat(m):
        m = jnp.transpose(m, (1, 0, 3, 2)).reshape(S5_GROUPS, t * S5_GROUP, S5_STATE)
        return jnp.concatenate([m.real, m.imag], axis=-1)

    w_in = jnp.concatenate([in_mat(in_f), in_mat(in_b)], axis=-1)
    out_f = c_mat[0][None] * pw[1:, 0][:, :, None, :]
    out_b = c_mat[1][None] * pw[1:, 1][::-1][:, :, None, :]

    def out_mat(m):
        m = jnp.transpose(m, (1, 3, 0, 2)).reshape(S5_GROUPS, S5_STATE, t * S5_GROUP)
        return jnp.concatenate([m.real, -m.imag], axis=1)

    w_out = jnp.concatenate([out_mat(out_f), out_mat(out_b)], axis=1)
    a_pow = jnp.exp(lam_dt[None] * (t * 2.0 ** jnp.arange(S5_LOG_CHUNKS, dtype=F32))[:, None, None, None])
    a_pow = jnp.transpose(a_pow, (2, 1, 0, 3))
    a_re = jnp.concatenate([a_pow.real, a_pow.real], axis=-1)
    a_im = jnp.concatenate([-a_pow.imag, a_pow.imag], axis=-1)
    return w_intra, w_in, w_out, a_re, a_im


S5_TILE_GROUPS = 128 // S5_GROUP


def _s5_group(ub, wi, win, wout, a_re, a_im):
    nc = S5_CHUNKS
    half = S5_STATE
    y = jnp.dot(ub, wi.astype(BF16), preferred_element_type=F32)
    s_in = jnp.dot(ub, win.astype(BF16), preferred_element_type=F32)
    row = lax.broadcasted_iota(jnp.int32, (nc, 2 * half), 0)

    def scan(s, d):
        for step in range(S5_LOG_CHUNKS):
            sh = 1 << step
            if d == 0:
                moved = jnp.where(row >= sh, pltpu.roll(s, sh, 0), 0.0)
            else:
                moved = jnp.where(row < nc - sh, pltpu.roll(s, nc - sh, 0), 0.0)
            swapped = pltpu.roll(moved, half, 1)
            s = s + a_re[d, step:step + 1, :] * moved + a_im[d, step:step + 1, :] * swapped
        if d == 0:
            return jnp.where(row >= 1, pltpu.roll(s, 1, 0), 0.0)
        return jnp.where(row < nc - 1, pltpu.roll(s, nc - 1, 0), 0.0)

    e = jnp.concatenate([scan(s_in[:, :2 * half], 0), scan(s_in[:, 2 * half:], 1)], axis=1)
    return y + jnp.dot(e.astype(BF16), wout.astype(BF16), preferred_element_type=F32)


@functools.lru_cache(maxsize=None)
def _s5_lane_permutation():
    t, tg, gs = S5_T, S5_TILE_GROUPS, S5_GROUP
    src = np.arange(t * tg * gs).reshape(t, tg, gs)
    dst = src.transpose(1, 0, 2).reshape(-1)
    perm = np.zeros((t * tg * gs, t * tg * gs), np.float32)
    perm[dst, np.arange(t * tg * gs)] = 1.0
    return jnp.asarray(perm, BF16)


def _s5_kernel(u_ref, perm_ref, wi_ref, win_ref, wout_ref, are_ref, aim_ref, skip_ref, y_ref, x_sc, y_sc):
    t, nc = S5_T, S5_CHUNKS
    lane = S5_TILE_GROUPS * S5_GROUP
    gw = t * S5_GROUP
    for tau in range(t):
        x_sc[:, tau * lane:(tau + 1) * lane] = u_ref[pl.ds(tau, nc, stride=t), :].astype(BF16)
    u_all = jnp.dot(x_sc[...], perm_ref[...], preferred_element_type=F32).astype(BF16)
    for g in range(S5_TILE_GROUPS):
        y_g = _s5_group(u_all[:, g * gw:(g + 1) * gw], wi_ref[g], win_ref[g], wout_ref[g], are_ref[g], aim_ref[g])
        y_sc[:, g * gw:(g + 1) * gw] = y_g.astype(BF16)
    y_nat = lax.dot_general(y_sc[...], perm_ref[...], (((1,), (1,)), ((), ())), preferred_element_type=F32)
    for tau in range(t):
        rows = pl.ds(tau, nc, stride=t)
        y_ref[rows, :] = y_nat[:, tau * lane:(tau + 1) * lane] + skip_ref[...] * u_ref[rows, :]


def _s5(u, mats, skip, l):
    w_intra, w_in, w_out, a_re, a_im = mats
    t = S5_T
    gw = t * S5_GROUP
    tg = S5_TILE_GROUPS
    lane = tg * S5_GROUP
    perm = _s5_lane_permutation()
    return pl.pallas_call(
        _s5_kernel,
        grid=(BRANCH // lane,),
        in_specs=[pl.BlockSpec((SEQ, lane), lambda j: (0, j)),
                  pl.BlockSpec(perm.shape, lambda j: (0, 0)),
                  pl.BlockSpec((None, tg, gw, gw), lambda j: (l, j, 0, 0)),
                  pl.BlockSpec((None, tg, gw, 4 * S5_STATE), lambda j: (l, j, 0, 0)),
                  pl.BlockSpec((None, tg, 4 * S5_STATE, gw), lambda j: (l, j, 0, 0)),
                  pl.BlockSpec((None, tg, 2, S5_LOG_CHUNKS, 2 * S5_STATE), lambda j: (l, j, 0, 0, 0)),
                  pl.BlockSpec((None, tg, 2, S5_LOG_CHUNKS, 2 * S5_STATE), lambda j: (l, j, 0, 0, 0)),
                  pl.BlockSpec((None, 1, lane), lambda j: (l, 0, j))],
        out_specs=pl.BlockSpec((SEQ, lane), lambda j: (0, j)),
        out_shape=jax.ShapeDtypeStruct((SEQ, BRANCH), F32),
        scratch_shapes=[pltpu.VMEM((S5_CHUNKS, t * lane), BF16), pltpu.VMEM((S5_CHUNKS, t * lane), BF16)],
        compiler_params=_cparams(("parallel",)),
        name="s5_scan",
    )(u, perm, w_intra, w_in, w_out, a_re, a_im, skip.reshape(DEPTH, 1, BRANCH))


def _merge_kernel(hm_ref, hyt_ref, s5_ref, x_ref, wa_ref, wb_ref, wga_ref, wgg_ref, wg0_ref, wg1_ref, wg2_ref,
                  out_ref, hy_sc):
    @pl.when(pl.program_id(1) == 0)
    def _():
        hy_sc[...] = hyt_ref[...].T.astype(BF16)

    def proj(a, w_ref):
        return jnp.dot(a, w_ref[...].astype(BF16), preferred_element_type=F32)

    def gate(w_ref):
        pre = lax.dot_general(x_ref[...], w_ref[...], (((1,), (1,)), ((), ())), preferred_element_type=F32)
        return jax.nn.sigmoid(pre)

    s5b = s5_ref[...].astype(BF16)
    out_a = proj(hm_ref[...], wa_ref)
    out_b = proj(hy_sc[...], wb_ref)
    out_c = proj(s5b, wga_ref) * jax.nn.sigmoid(proj(s5b, wgg_ref))
    merged = gate(wg0_ref) * out_a + gate(wg1_ref) * out_b + gate(wg2_ref) * out_c
    out_ref[...] = merged.astype(out_ref.dtype)


def _merge(hm, hy_t, s5y, x_bf, w_tail, w_a, w_b, w_glu, l, *, tm=512, tn=512):
    nj = D_MODEL // tn
    goff = (OFF_MIXG - OFF_HYENA) // tn

    def gate_spec(g):
        return pl.BlockSpec((None, tn, D_MODEL), lambda i, j: (l, goff + g * nj + j, 0))

    return pl.pallas_call(
        _merge_kernel,
        grid=(SEQ // tm, nj),
        in_specs=[pl.BlockSpec((tm, BRANCH), lambda i, j: (i, 0)),
                  pl.BlockSpec((BRANCH, tm), lambda i, j: (0, i)),
                  pl.BlockSpec((tm, BRANCH), lambda i, j: (i, 0)),
                  pl.BlockSpec((tm, D_MODEL), lambda i, j: (i, 0)),
                  pl.BlockSpec((None, BRANCH, tn), lambda i, j: (l, 0, j)),
                  pl.BlockSpec((None, BRANCH, tn), lambda i, j: (l, 0, j)),
                  pl.BlockSpec((None, BRANCH, tn), lambda i, j: (l, 0, j)),
                  pl.BlockSpec((None, BRANCH, tn), lambda i, j: (l, 0, nj + j)),
                  gate_spec(0), gate_spec(1), gate_spec(2)],
        out_specs=pl.BlockSpec((tm, tn), lambda i, j: (i, j)),
        out_shape=jax.ShapeDtypeStruct((SEQ, D_MODEL), BF16),
        scratch_shapes=[pltpu.VMEM((tm, BRANCH), BF16)],
        compiler_params=_cparams(("parallel", "arbitrary")),
        name="branch_merge",
    )(hm, hy_t, s5y, x_bf, w_a, w_b, w_glu, w_glu, w_tail, w_tail, w_tail)


def _layer_norm_rows(y, g, b):
    mu = jnp.mean(y, axis=1, keepdims=True)
    var = jnp.mean(jnp.square(y - mu), axis=1, keepdims=True)
    return (y - mu) * lax.rsqrt(var + LN_EPS) * g + b


def _res_ln_kernel(x_ref, f_ref, g_ref, b_ref, o_ref, ob_ref, *, transposed):
    f = f_ref[...]
    if transposed:
        f = f.T
    y = _layer_norm_rows(ALPHA * x_ref[...] + f, g_ref[...], b_ref[...])
    o_ref[...] = y
    ob_ref[...] = y.astype(BF16)


def _res_ln(x, f, g, b, l, *, transposed, tm=256):
    f_spec = (pl.BlockSpec((D_MODEL, tm), lambda i: (0, i)) if transposed
              else pl.BlockSpec((tm, D_MODEL), lambda i: (i, 0)))
    n_tok = x.shape[0]
    return pl.pallas_call(
        functools.partial(_res_ln_kernel, transposed=transposed),
        grid=(n_tok // tm,),
        in_specs=[pl.BlockSpec((tm, D_MODEL), lambda i: (i, 0)), f_spec,
                  pl.BlockSpec((None, 1, D_MODEL), lambda i: (l, 0, 0)),
                  pl.BlockSpec((None, 1, D_MODEL), lambda i: (l, 0, 0))],
        out_specs=[pl.BlockSpec((tm, D_MODEL), lambda i: (i, 0)),
                   pl.BlockSpec((tm, D_MODEL), lambda i: (i, 0))],
        out_shape=[jax.ShapeDtypeStruct((n_tok, D_MODEL), F32), jax.ShapeDtypeStruct((n_tok, D_MODEL), BF16)],
        compiler_params=_cparams(("parallel",)),
        name="residual_layernorm",
    )(x, f, g.reshape(DEPTH, 1, D_MODEL), b.reshape(DEPTH, 1, D_MODEL))


PEER_NO_RANK = 64.0


def _extract_sorted(s, n, out_ref, base, want_rank=False):
    rank = jnp.full_like(s, PEER_NO_RANK) if want_rank else None
    for r in range(n):
        m = jnp.max(s, axis=0, keepdims=True)
        out_ref[base + r:base + r + 1, :] = m
        hit = s == m
        if want_rank:
            rank = jnp.where(hit, float(r), rank)
        s = jnp.where(hit, -jnp.inf, s)
    return rank


def _peer_select_kernel(q_ref, keys_ref, cnt_ref, r2_ref, e1_ref, e2_ref, top_sc, kth_sc):
    k = PEER_TOPK
    qb = q_ref[...].astype(BF16)
    half = PEER_KEYS
    for h in range(PEER_HEADS):
        sc = []
        rank2 = None
        for c in range(2):
            qs = qb[:, (2 * h + c) * half:(2 * h + c + 1) * half]
            s = lax.dot_general(keys_ref[h, c].astype(BF16), qs, (((1,), (1,)), ((), ())),
                                preferred_element_type=F32)
            sc.append(s)
            rank2 = _extract_sorted(s, k, top_sc, c * k, want_rank=(c == 1))
        a = top_sc[0:k, :]
        b = top_sc[k:2 * k, :]
        sub = lax.broadcasted_iota(jnp.int32, (8, a.shape[1]), 0)
        groups = [a[0:1] + b[0:8], a[0:1] + b[8:16], a[1:2] + b[0:8]]
        for i in range(2, 8):
            groups.append(jnp.where(sub < k // (i + 1), a[i:i + 1] + b[0:8], -jnp.inf))
        groups.append(a[8:16] + b[0:1])
        cand = jnp.concatenate(groups, axis=0)
        _extract_sorted(cand, k, kth_sc, 0)
        tau = kth_sc[k - 1:k, :]
        top = a[0:1] + b[0:1]
        z = jnp.sum(jnp.where(cand >= tau, jnp.exp(cand - top), 0.0), axis=0, keepdims=True)
        cnt = jnp.zeros_like(sc[0])
        for j in range(k):
            bj = top_sc[k + j:k + j + 1, :]
            cnt = jnp.where(sc[0] + bj >= tau, j + 1.0, cnt)
        cnt_ref[h] = cnt
        r2_ref[h] = rank2.astype(BF16)
        e1_ref[h] = jnp.exp(sc[0] - a[0:1])
        e2_ref[h] = (jnp.exp(sc[1] - b[0:1]) / z).astype(BF16)


def _peer_select(q, keys, l, *, tt=128):
    n_tok = q.shape[0]
    big = jax.ShapeDtypeStruct((PEER_HEADS, PEER_KEYS, n_tok), F32)
    big_bf = jax.ShapeDtypeStruct((PEER_HEADS, PEER_KEYS, n_tok), BF16)
    big_spec = pl.BlockSpec((PEER_HEADS, PEER_KEYS, tt), lambda i: (0, 0, i))
    return pl.pallas_call(
        _peer_select_kernel,
        grid=(n_tok // tt,),
        in_specs=[pl.BlockSpec((tt, D_MODEL), lambda i: (i, 0)),
                  pl.BlockSpec((None, PEER_HEADS, 2, PEER_KEYS, PEER_KEYS), lambda i: (l, 0, 0, 0, 0))],
        out_specs=[big_spec, big_spec, big_spec, big_spec],
        out_shape=[big, big_bf, big, big_bf],
        scratch_shapes=[pltpu.VMEM((2 * PEER_TOPK, tt), F32), pltpu.VMEM((PEER_TOPK, tt), F32)],
        compiler_params=_cparams(("parallel",)),
        name="peer_select",
    )(q, keys)


def _peer_dense_kernel(x_ref, u_ref, vt_ref, cnt_ref, r2_ref, e1_ref, e2_ref, out_ref, act_sc, g_sc, *, rows):
    e = pl.program_id(1)

    @pl.when(e == 0)
    def _():
        out_ref[...] = jnp.zeros_like(out_ref)

    act_sc[...] = lax.dot_general(u_ref[...], x_ref[...], (((1,), (1,)), ((), ())),
                                  preferred_element_type=F32)
    kk = PEER_KEYS
    for tc in range(act_sc.shape[1] // kk):
        lanes = slice(tc * kk, (tc + 1) * kk)
        for i in range(rows):
            w = jnp.zeros((kk, kk), BF16)
            for h in range(PEER_HEADS):
                hit = r2_ref[h, :, lanes] < cnt_ref[h, i:i + 1, lanes].astype(BF16)
                val = e2_ref[h, :, lanes] * e1_ref[h, i:i + 1, lanes].astype(BF16)
                w = w + jnp.where(hit, val, jnp.zeros_like(val))
            g_sc[i * kk:(i + 1) * kk, lanes] = jax.nn.gelu(act_sc[i * kk:(i + 1) * kk, lanes].astype(BF16)) * w
    out_ref[...] += jnp.dot(vt_ref[...], g_sc[...], preferred_element_type=F32)


def _transpose_tile_kernel(v_ref, o_ref):
    o_ref[...] = v_ref[...].T.astype(o_ref.dtype)


def _peer_value_tiles(v, l, te):
    return pl.pallas_call(
        _transpose_tile_kernel,
        grid=(N_EXPERTS // te,),
        in_specs=[pl.BlockSpec((None, te, D_MODEL), lambda e: (l, e, 0))],
        out_specs=pl.BlockSpec((None, D_MODEL, te), lambda e: (e, 0, 0)),
        out_shape=jax.ShapeDtypeStruct((N_EXPERTS // te, D_MODEL, te), BF16),
        compiler_params=_cparams(("parallel",)),
        name="peer_value_tiles",
    )(v)


def _peer_dense(x_bf, u_bf, v, l, sel, *, tt=512, rows=8):
    cnt, r2, e1, e2 = sel
    te = rows * PEER_KEYS
    n_tok = x_bf.shape[0]
    vt_bf = _peer_value_tiles(v, l, te)
    row_spec = pl.BlockSpec((PEER_HEADS, rows, tt), lambda i, e: (0, e, i))
    all_spec = pl.BlockSpec((PEER_HEADS, PEER_KEYS, tt), lambda i, e: (0, 0, i))
    return pl.pallas_call(
        functools.partial(_peer_dense_kernel, rows=rows),
        grid=(n_tok // tt, N_EXPERTS // te),
        in_specs=[pl.BlockSpec((tt, D_MODEL), lambda i, e: (i, 0)),
                  pl.BlockSpec((None, te, D_MODEL), lambda i, e: (l, e, 0)),
                  pl.BlockSpec((None, D_MODEL, te), lambda i, e: (e, 0, 0)),
                  row_spec, all_spec, row_spec, all_spec],
        out_specs=pl.BlockSpec((D_MODEL, tt), lambda i, e: (0, i)),
        out_shape=jax.ShapeDtypeStruct((D_MODEL, n_tok), F32),
        scratch_shapes=[pltpu.VMEM((te, tt), F32), pltpu.VMEM((te, tt), BF16)],
        compiler_params=_cparams(("parallel", "arbitrary")),
        name="peer_dense",
    )(x_bf, u_bf, vt_bf, cnt, r2, e1, e2)


def _hyena_features():
    pos = jnp.arange(SEQ, dtype=F32)
    t = pos / (SEQ - 1)
    bands = jnp.linspace(1e-4, 16 - 1, 16, dtype=F32)
    ang = 2.0 * math.pi * pos[:, None] * bands[None, :] / SEQ
    feat = jnp.concatenate([t[:, None], jnp.cos(ang), -jnp.sin(ang)], axis=-1)
    feat_t = jnp.pad(feat.T, ((0, HY_EMB_PAD - feat.shape[1]), (0, 0)))
    t_row = t[None, :]

    def reversed_time(a):
        return jnp.roll(a[:, ::-1], 1, axis=1)

    return (jnp.concatenate([feat_t, reversed_time(feat_t)], axis=1),
            jnp.concatenate([t_row, reversed_time(t_row)], axis=1))


def kernel(x, w_in, mlstm_gate_bias, mlstm_norm_gain, w_mlstm_out, hyena_conv_w, hyena_conv_b, hyena_w1,
           hyena_b1, hyena_w2, hyena_b2, hyena_freq, hyena_w3, hyena_decay, hyena_skip, w_hyena_out,
           s5_lambda_re, s5_lambda_im, s5_log_step, s5_b_re, s5_b_im, s5_c_re, s5_c_im, s5_skip, w_s5_glu,
           w_out, ln1_g, ln1_b, peer_w_q, peer_subkeys, peer_u, peer_v, ln2_g, ln2_b):
    xf = x.reshape(SEQ, D_MODEL)
    xb = xf.astype(BF16)
    feat_t, t_row = _hyena_features()
    w_in_t = jnp.swapaxes(w_in, 1, 2)
    w_tail = _w_in_tail(w_in_t)
    u_bf = peer_u.astype(BF16)
    s5_mats = jax.vmap(_s5_chunk_matrices)(s5_lambda_re, s5_lambda_im, s5_log_step, s5_b_re, s5_b_im,
                                           s5_c_re, s5_c_im)
    for l in range(DEPTH):
        qkvo = _mm_wt(xb, w_in_t, l, 0, 4 * BRANCH, tm=2048, tn=512, name="proj_qkvo")
        gates_t = _mm_nt(w_in_t, l, OFF_GATES, N_GATES, xb, tr=N_GATES, tm=1024, name="proj_gates", out_dtype=F32)
        hp_t = _mm_nt(w_tail, l, 0, 3 * BRANCH, xb, tr=512, tm=1024, name="proj_hyena_t", out_dtype=BF16)
        s5_in = _mm_wt(xb, w_tail, l, 3 * BRANCH, BRANCH, tm=1024, tn=512, name="proj_s5")

        hdir = _mlstm(qkvo, gates_t, mlstm_gate_bias, l)
        hm = _mlstm_post(hdir, qkvo, mlstm_norm_gain, l)
        hy_t = _hyena(hp_t, l, hyena_conv_w, hyena_conv_b, hyena_w1, hyena_b1, hyena_w2, hyena_b2,
                      hyena_freq, hyena_w3, hyena_decay, hyena_skip, feat_t, t_row)
        s5y = _s5(s5_in, s5_mats, s5_skip, l)
        merged = _merge(hm, hy_t, s5y, xb, w_tail, w_mlstm_out, w_hyena_out, w_s5_glu, l)
        mix = _mm(merged, w_out, l, 0, D_MODEL, tm=2048, tn=512, name="proj_out")
        xf, xb = _res_ln(xf, mix, ln1_g, ln1_b, l, transposed=False)

        q = _mm(xb, peer_w_q, l, 0, PEER_HEADS * 2 * PEER_KEYS, tm=2048, tn=512, name="peer_query")
        sel = _peer_select(q, peer_subkeys, l)
        ffn_t = _peer_dense(xb, u_bf, peer_v, l, sel)
        xf, xb = _res_ln(xf, ffn_t, ln2_g, ln2_b, l, transposed=True)
    return xf.reshape(1, SEQ, D_MODEL)
```

```python
import functools
import math

import numpy as np
import jax
import jax.numpy as jnp
from jax import lax
from jax.experimental import pallas as pl
from jax.experimental.pallas import tpu as pltpu

F32 = jnp.float32
BF16 = jnp.bfloat16

D_MODEL = 2048
SEQ = 8192
DEPTH = 4
BRANCH = 1024
HEADS = 4
HEAD_DIM = 256
CHUNK = 256
N_GATES = 16
HY_HID = 64
HY_EMB_PAD = 128
S5_GROUP = 16
S5_GROUPS = 64
S5_STATE = 64
S5_T = 16
S5_CHUNKS = SEQ // S5_T
S5_LOG_CHUNKS = 9
PEER_HEADS = 8
PEER_KEYS = 128
PEER_TOPK = 16
N_EXPERTS = PEER_KEYS * PEER_KEYS
ALPHA = (2 * DEPTH) ** 0.25
LN_EPS = 1e-5
FFT_N = 2 * SEQ
FFT_R = 128
OFF_GATES = 4 * BRANCH
OFF_HYENA = OFF_GATES + N_GATES
OFF_S5 = OFF_HYENA + 3 * BRANCH
OFF_MIXG = OFF_S5 + BRANCH

VMEM_LIMIT = 56 * 1024 * 1024


def _cparams(sem):
    return pltpu.CompilerParams(dimension_semantics=sem, vmem_limit_bytes=VMEM_LIMIT)


def _split_bf16(a):
    hi = a.astype(BF16)
    lo = (a - hi.astype(F32)).astype(BF16)
    return hi, lo


def _dot3(a, b_hi, b_lo):
    a_hi, a_lo = _split_bf16(a)
    acc = jnp.dot(a_hi, b_hi, preferred_element_type=F32)
    acc += jnp.dot(a_lo, b_hi, preferred_element_type=F32)
    acc += jnp.dot(a_hi, b_lo, preferred_element_type=F32)
    return acc


def _dot3_both(a, b):
    b_hi, b_lo = _split_bf16(b)
    return _dot3(a, b_hi, b_lo)


def _mm_kernel(a_ref, b_ref, o_ref):
    o_ref[...] = jnp.dot(a_ref[...].astype(BF16), b_ref[...].astype(BF16),
                         preferred_element_type=F32).astype(o_ref.dtype)


def _mm(a, w, l, col_off, n_cols, *, tm, tn, name):
    m, k = a.shape
    cb = col_off // tn
    assert col_off % tn == 0 and n_cols % tn == 0 and m % tm == 0
    return pl.pallas_call(
        _mm_kernel,
        grid=(m // tm, n_cols // tn),
        in_specs=[pl.BlockSpec((tm, k), lambda i, j: (i, 0)),
                  pl.BlockSpec((None, k, tn), lambda i, j: (l, 0, cb + j))],
        out_specs=pl.BlockSpec((tm, tn), lambda i, j: (i, j)),
        out_shape=jax.ShapeDtypeStruct((m, n_cols), F32),
        compiler_params=_cparams(("parallel", "parallel")),
        name=name,
    )(a, w)


def _mm_wt_kernel(a_ref, w_ref, o_ref):
    o_ref[...] = lax.dot_general(a_ref[...].astype(BF16), w_ref[...].astype(BF16), (((1,), (1,)), ((), ())),
                                 preferred_element_type=F32).astype(o_ref.dtype)


def _mm_wt(a, wt, l, row_off, n_rows, *, tm, tn, name):
    m, k = a.shape
    rb = row_off // tn
    assert row_off % tn == 0 and n_rows % tn == 0 and m % tm == 0
    return pl.pallas_call(
        _mm_wt_kernel,
        grid=(m // tm, n_rows // tn),
        in_specs=[pl.BlockSpec((tm, k), lambda i, j: (i, 0)),
                  pl.BlockSpec((None, tn, k), lambda i, j: (l, rb + j, 0))],
        out_specs=pl.BlockSpec((tm, tn), lambda i, j: (i, j)),
        out_shape=jax.ShapeDtypeStruct((m, n_rows), F32),
        compiler_params=_cparams(("parallel", "parallel")),
        name=name,
    )(a, wt)


def _mm_nt_kernel(w_ref, a_ref, o_ref):
    o_ref[...] = lax.dot_general(w_ref[...].astype(BF16), a_ref[...].astype(BF16), (((1,), (1,)), ((), ())),
                                 preferred_element_type=F32).astype(o_ref.dtype)


def _mm_nt(wt, l, row_off, n_rows, a, *, tr, tm, name, out_dtype):
    k = wt.shape[2]
    m = a.shape[0]
    rb = row_off // tr
    assert row_off % tr == 0 and n_rows % tr == 0
    return pl.pallas_call(
        _mm_nt_kernel,
        grid=(n_rows // tr, m // tm),
        in_specs=[pl.BlockSpec((None, tr, k), lambda i, j: (l, rb + i, 0)),
                  pl.BlockSpec((tm, k), lambda i, j: (j, 0))],
        out_specs=pl.BlockSpec((tr, tm), lambda i, j: (i, j)),
        out_shape=jax.ShapeDtypeStruct((n_rows, m), out_dtype),
        compiler_params=_cparams(("parallel", "parallel")),
        name=name,
    )(wt, a)


def _w_tail_kernel(cur_ref, nxt_ref, o_ref, *, shift):
    o_ref[...] = jnp.concatenate([cur_ref[shift:, :], nxt_ref[...]], axis=0).astype(o_ref.dtype)


def _w_in_tail(w_in_t, *, tn=512):
    depth, n_all, k = w_in_t.shape
    n_out = n_all - OFF_HYENA
    return pl.pallas_call(
        functools.partial(_w_tail_kernel, shift=N_GATES),
        grid=(depth, n_out // tn),
        in_specs=[pl.BlockSpec((None, tn, k), lambda l, j: (l, OFF_GATES // tn + j, 0)),
                  pl.BlockSpec((None, N_GATES, k), lambda l, j: (l, (OFF_GATES + (j + 1) * tn) // N_GATES, 0))],
        out_specs=pl.BlockSpec((None, tn, k), lambda l, j: (l, j, 0)),
        out_shape=jax.ShapeDtypeStruct((depth, n_out, k), BF16),
        compiler_params=_cparams(("parallel", "parallel")),
        name="w_in_tail",
    )(w_in_t, w_in_t)


def _log_sigmoid(x):
    return jnp.minimum(x, 0.0) - jnp.log1p(jnp.exp(-jnp.abs(x)))


def _mlstm_chunk(d, q, k, v, li_r, lf_r, li_c, lf_c, c_sc, n_sc, m_sc):
    n = CHUNK
    row = lax.broadcasted_iota(jnp.int32, (n, n), 0)
    col = lax.broadcasted_iota(jnp.int32, (n, n), 1)
    valid = (row >= col) if d == 0 else (row <= col)
    valid_t = (col >= row) if d == 0 else (col <= row)
    b_col = jnp.sum(jnp.where(valid, lf_r, 0.0), axis=1, keepdims=True)
    b_row = jnp.sum(jnp.where(valid_t, lf_c, 0.0), axis=0, keepdims=True)
    total = jnp.sum(lf_r, axis=1, keepdims=True)

    m_prev = m_sc[...]
    m_inter = b_col + m_prev
    log_d = jnp.where(valid, b_col - b_row + li_r, -jnp.inf)
    m_t = jnp.maximum(m_inter, jnp.max(log_d, axis=1, keepdims=True))
    inter = jnp.exp(m_inter - m_t)
    dmat = jnp.exp(log_d - m_t)

    k = k * (HEAD_DIM ** -0.5)
    qb = q.astype(BF16)
    kb = k.astype(BF16)
    vb = v.astype(BF16)
    s = lax.dot_general(qb, kb, (((1,), (1,)), ((), ())), preferred_element_type=F32) * dmat
    num = jnp.dot(s.astype(BF16), vb, preferred_element_type=F32)
    num += inter * jnp.dot(qb, c_sc[...].astype(BF16), preferred_element_type=F32)
    qn = jnp.sum(q * n_sc[...], axis=1, keepdims=True)
    den = jnp.sum(s, axis=1, keepdims=True) + inter * qn
    h_out = num / jnp.maximum(jnp.abs(den), jnp.exp(-m_t))

    log_w = total - b_col + li_c
    m_new = jnp.maximum(total + m_prev, jnp.max(log_w, axis=0, keepdims=True))
    w = jnp.exp(log_w - m_new)
    decay = jnp.exp(total + m_prev - m_new)
    kw = k * w
    c_sc[...] = decay * c_sc[...] + lax.dot_general(
        kw.astype(BF16), vb, (((0,), (0,)), ((), ())), preferred_element_type=F32)
    n_sc[...] = decay * n_sc[...] + jnp.sum(kw, axis=0, keepdims=True)
    m_sc[...] = m_new
    return h_out


def _mlstm_kernel(qkv_f_ref, qkv_b_ref, gr_f_ref, gr_b_ref, gc_f_ref, gc_b_ref, bias_ref,
                  o_f_ref, o_b_ref, c_sc, n_sc, m_sc):
    @pl.when(pl.program_id(0) == 0)
    def _():
        c_sc[...] = jnp.zeros_like(c_sc)
        n_sc[...] = jnp.zeros_like(n_sc)
        m_sc[...] = jnp.zeros_like(m_sc)

    for d, (qkv_ref, gr_ref, gc_ref, o_ref) in enumerate(((qkv_f_ref, gr_f_ref, gc_f_ref, o_f_ref),
                                                         (qkv_b_ref, gr_b_ref, gc_b_ref, o_b_ref))):
        for h in range(HEADS):
            cols = slice(h * HEAD_DIM, (h + 1) * HEAD_DIM)
            b_i = bias_ref[d * 2 * HEADS + h]
            b_f = bias_ref[d * 2 * HEADS + HEADS + h]
            chain = d * HEADS + h
            o_ref[:, cols] = _mlstm_chunk(
                d,
                qkv_ref[:, cols],
                qkv_ref[:, BRANCH + h * HEAD_DIM:BRANCH + (h + 1) * HEAD_DIM],
                qkv_ref[:, 2 * BRANCH + h * HEAD_DIM:2 * BRANCH + (h + 1) * HEAD_DIM],
                gr_ref[h] + b_i, _log_sigmoid(gr_ref[HEADS + h] + b_f),
                gc_ref[h] + b_i, _log_sigmoid(gc_ref[HEADS + h] + b_f),
                c_sc.at[chain], n_sc.at[chain], m_sc.at[chain])


def _mlstm(qkvo, gates_t, gate_bias, l):
    nc = SEQ // CHUNK
    g_rows = gates_t.reshape(N_GATES, 1, SEQ)
    g_cols = gates_t.reshape(N_GATES, SEQ, 1)
    bias = gate_bias.reshape(DEPTH, N_GATES, 1, 1)
    half = N_GATES // 2
    chains = 2 * HEADS
    out = jax.ShapeDtypeStruct((SEQ, BRANCH), F32)
    return pl.pallas_call(
        _mlstm_kernel,
        grid=(nc,),
        in_specs=[
            pl.BlockSpec((CHUNK, 4 * BRANCH), lambda c: (c, 0)),
            pl.BlockSpec((CHUNK, 4 * BRANCH), lambda c: (nc - 1 - c, 0)),
            pl.BlockSpec((half, 1, CHUNK), lambda c: (0, 0, c)),
            pl.BlockSpec((half, 1, CHUNK), lambda c: (1, 0, nc - 1 - c)),
            pl.BlockSpec((half, CHUNK, 1), lambda c: (0, c, 0)),
            pl.BlockSpec((half, CHUNK, 1), lambda c: (1, nc - 1 - c, 0)),
            pl.BlockSpec((None, N_GATES, 1, 1), lambda c: (l, 0, 0, 0)),
        ],
        out_specs=[pl.BlockSpec((CHUNK, BRANCH), lambda c: (c, 0)),
                   pl.BlockSpec((CHUNK, BRANCH), lambda c: (nc - 1 - c, 0))],
        out_shape=[out, out],
        scratch_shapes=[pltpu.VMEM((chains, HEAD_DIM, HEAD_DIM), F32),
                        pltpu.VMEM((chains, 1, HEAD_DIM), F32),
                        pltpu.VMEM((chains, 1, 1), F32)],
        compiler_params=_cparams(("arbitrary",)),
        name="mlstm_scan",
    )(qkvo, qkvo, g_rows, g_rows, g_cols, g_cols, bias)


def _mlstm_post_kernel(hf_ref, hb_ref, o_ref, gain_ref, out_ref):
    h = hf_ref[...] + hb_ref[...]
    h = jax.nn.sigmoid(o_ref[...]) * h
    for hd in range(HEADS):
        sl = slice(hd * HEAD_DIM, (hd + 1) * HEAD_DIM)
        hh = h[:, sl]
        mu = jnp.mean(hh, axis=1, keepdims=True)
        var = jnp.mean(jnp.square(hh - mu), axis=1, keepdims=True)
        out_ref[:, sl] = ((hh - mu) * lax.rsqrt(var + LN_EPS) * gain_ref[:, sl]).astype(out_ref.dtype)


def _mlstm_post(hdir, qkvo, gain, l, *, tm=512):
    return pl.pallas_call(
        _mlstm_post_kernel,
        grid=(SEQ // tm,),
        in_specs=[pl.BlockSpec((tm, BRANCH), lambda i: (i, 0)),
                  pl.BlockSpec((tm, BRANCH), lambda i: (i, 0)),
                  pl.BlockSpec((tm, BRANCH), lambda i: (i, 3)),
                  pl.BlockSpec((None, 1, BRANCH), lambda i: (l, 0, 0))],
        out_specs=pl.BlockSpec((tm, BRANCH), lambda i: (i, 0)),
        out_shape=jax.ShapeDtypeStruct((SEQ, BRANCH), BF16),
        compiler_params=_cparams(("parallel",)),
        name="mlstm_post",
    )(hdir[0], hdir[1], qkvo, gain.reshape(DEPTH, 1, BRANCH))


HY_CT = 32
HY_RB = 512
HY_UNROLL = 16


@functools.lru_cache(maxsize=None)
def _fft_constants():
    r = FFT_R
    idx = np.arange(r)
    th = 2.0 * np.pi * np.outer(idx, idx) / r
    cs, sn = np.cos(th), np.sin(th)
    ph = 2.0 * np.pi * np.outer(idx, idx) / FFT_N
    f1 = np.concatenate([cs, -sn], axis=1)
    m3 = np.block([[cs, -sn], [sn, cs]])
    m4 = np.block([[cs, sn], [-sn, cs]])
    m6 = np.concatenate([cs, -sn], axis=0) / FFT_N
    m6[:, r // 2:] = 0.0
    consts = {name: jnp.asarray(mat, F32).astype(BF16)
              for name, mat in (("f1", f1), ("m3", m3), ("m4", m4), ("m6", m6))}
    consts["tw"] = (jnp.asarray(np.cos(ph), F32), jnp.asarray(np.sin(ph), F32))
    return consts


def _rows(i, n):
    return pl.ds(pl.multiple_of(i * n, n), n)


def _for_each(n, body, unroll=1):
    def step(i, carry):
        body(i)
        return carry

    lax.fori_loop(0, n, step, 0, unroll=unroll)


def _fwd_stage1(sig_ref, hi_ref, ct, t_buf, a_buf, b_buf, f1_ref, twr_ref, twi_ref):
    r = FFT_R
    zeros = jnp.zeros((r // 2, r), F32)

    def tr_in(c):
        hi = zeros if hi_ref is None else hi_ref[c]
        t_buf[_rows(c, r), :] = jnp.concatenate([sig_ref[c], hi], axis=0).T.astype(BF16)

    _for_each(ct, tr_in, unroll=HY_UNROLL)

    def stage1(b):
        rows = _rows(b, HY_RB)
        a_buf[rows, :] = jnp.dot(t_buf[rows, :], f1_ref[...], preferred_element_type=F32)

    _for_each(ct * r // HY_RB, stage1, unroll=HY_UNROLL)

    def twiddle(c):
        rows = _rows(c, r)
        a = a_buf[rows, :]
        ar, ai = a[:, :r], a[:, r:]
        twr, twi = twr_ref[...], twi_ref[...]
        b_buf[rows, :r] = (ar * twr + ai * twi).T.astype(BF16)
        b_buf[rows, r:] = (ai * twr - ar * twi).T.astype(BF16)

    _for_each(ct, twiddle, unroll=HY_UNROLL)


def _hy_mlp_kernel(feat_ref, w1_ref, b1_ref, w2_ref, b2_ref, fr_ref, out_ref):
    h = _dot3_both(w1_ref[...], feat_ref[...]) + b1_ref[...]
    h = jnp.sin(fr_ref[0] * h)
    h = _dot3_both(w2_ref[...], h) + b2_ref[...]
    out_ref[...] = jnp.sin(fr_ref[1] * h)


def _hy_filter_kernel(w3_ref, h_ref, dec_ref, t_ref, out_ref):
    d = pl.program_id(0)
    filt = _dot3_both(w3_ref[...], h_ref[...])
    filt = filt * jnp.exp(-t_ref[...] * jnp.abs(dec_ref[...]))
    filt = filt / (jnp.sum(jnp.abs(filt), axis=1, keepdims=True) + 1e-6)
    lane = lax.broadcasted_iota(jnp.int32, filt.shape, 1)
    out_ref[...] = jnp.where((lane == 0) & (d == 1), 0.0, filt)


def _hy_spec_kernel(fa_ref, fb_ref, f1_ref, m3_ref, twr_ref, twi_ref, out_ref, t_buf, a_buf, b_buf, *, ct):
    r = FFT_R
    nblk = ct * r // HY_RB
    _fwd_stage1(fa_ref, fb_ref, ct, t_buf, a_buf, b_buf, f1_ref, twr_ref, twi_ref)

    def put(b):
        rows = _rows(b, HY_RB)
        out_ref[rows, :] = jnp.dot(b_buf[rows, :], m3_ref[...], preferred_element_type=F32)

    _for_each(nblk, put, unroll=HY_UNROLL)


def _shift_prev(p, n1, n2):
    r1 = pltpu.roll(p, 1, 2)
    r2 = pltpu.roll(r1, 1, 1)
    return jnp.where(n2 == 0, jnp.where(n1 == 0, 0.0, r2), r1)


def _shift_next(p, n1, n2):
    s1, s2 = p.shape[1], p.shape[2]
    r1 = pltpu.roll(p, s2 - 1, 2)
    r2 = pltpu.roll(r1, s1 - 1, 1)
    return jnp.where(n2 == s2 - 1, jnp.where(n1 == s1 - 1, 0.0, r2), r1)


def _hy_conv_kernel(p0_ref, p1_ref, p2_ref, cw_ref, cb_ref, skip_ref, h_ref,
                    f1_ref, m3_ref, m4_ref, m6_ref, twr_ref, twi_ref,
                    out_ref, z_buf, t_buf, a_buf, b_buf, *, ct):
    r = FFT_R
    n1_len = SEQ // r
    nblk = ct * r // HY_RB
    shp = (ct, n1_len, r)
    n1 = lax.broadcasted_iota(jnp.int32, shp, 1)
    n2 = lax.broadcasted_iota(jnp.int32, shp, 2)

    def short_conv(p_ref, g):
        p = p_ref[...].astype(F32)
        return (_shift_prev(p, n1, n2) * cw_ref[3 * g] + p * cw_ref[3 * g + 1]
                + _shift_next(p, n1, n2) * cw_ref[3 * g + 2] + cb_ref[g])

    z_buf[...] = short_conv(p1_ref, 1) * short_conv(p2_ref, 2)
    _fwd_stage1(z_buf, None, ct, t_buf, a_buf, b_buf, f1_ref, twr_ref, twi_ref)

    def freq_domain(b):
        rows = _rows(b, HY_RB)
        x = jnp.dot(b_buf[rows, :], m3_ref[...], preferred_element_type=F32)
        h = h_ref[rows, :]
        xr, xi = x[:, :r], x[:, r:]
        hr, hi = h[:, :r], h[:, r:]
        y = jnp.concatenate([xr * hr - xi * hi, xr * hi + xi * hr], axis=1).astype(BF16)
        a_buf[rows, :] = jnp.dot(y, m4_ref[...], preferred_element_type=F32)

    _for_each(nblk, freq_domain, unroll=HY_UNROLL)

    def inv_twiddle(c):
        rows = _rows(c, r)
        b = a_buf[rows, :]
        br, bi = b[:, :r], b[:, r:]
        twr, twi = twr_ref[...], twi_ref[...]
        b_buf[rows, :r] = (br * twr - bi * twi).T.astype(BF16)
        b_buf[rows, r:] = (br * twi + bi * twr).T.astype(BF16)

    _for_each(ct, inv_twiddle, unroll=HY_UNROLL)

    def stage6(b):
        rows = _rows(b, HY_RB)
        a_buf[rows, :r] = jnp.dot(b_buf[rows, :], m6_ref[...], preferred_element_type=F32)

    _for_each(nblk, stage6, unroll=HY_UNROLL)

    def finish(c):
        y_c = a_buf[_rows(c, r), :r].T[:n1_len, :]
        out_ref[c] = y_c + skip_ref[c] * z_buf[c]

    _for_each(ct, finish, unroll=HY_UNROLL)
    out_ref[...] = short_conv(p0_ref, 0) * out_ref[...]


def _hyena(hp_t, l, conv_w, conv_b, w1, b1, w2, b2, freq, w3, decay, skip, feat_t, t_row):
    r = FFT_R
    n1 = SEQ // r
    ct = HY_CT
    k = _fft_constants()
    f1, m3, m4, m6 = k["f1"], k["m3"], k["m4"], k["m6"]
    twr, twi = k["tw"]

    def full(a):
        return pl.BlockSpec(a.shape, lambda *_: (0,) * a.ndim)

    h2 = pl.pallas_call(
        _hy_mlp_kernel,
        grid=(2,),
        in_specs=[pl.BlockSpec((feat_t.shape[0], SEQ), lambda i: (0, i)),
                  pl.BlockSpec((None, HY_HID, feat_t.shape[0]), lambda i: (l, 0, 0)),
                  pl.BlockSpec((None, HY_HID, 1), lambda i: (l, 0, 0)),
                  pl.BlockSpec((None, HY_HID, HY_HID), lambda i: (l, 0, 0)),
                  pl.BlockSpec((None, HY_HID, 1), lambda i: (l, 0, 0)),
                  pl.BlockSpec((None, 2, HY_HID, 1), lambda i: (l, 0, 0, 0))],
        out_specs=pl.BlockSpec((HY_HID, SEQ), lambda i: (0, i)),
        out_shape=jax.ShapeDtypeStruct((HY_HID, 2 * SEQ), F32),
        compiler_params=_cparams(("parallel",)),
        name="hyena_filter_mlp",
    )(feat_t, jnp.pad(jnp.swapaxes(w1, 1, 2), ((0, 0), (0, 0), (0, feat_t.shape[0] - w1.shape[1]))),
      b1[:, :, None], jnp.swapaxes(w2, 1, 2), b2[:, :, None], freq[:, :, :, None])

    rt = 128
    filt = pl.pallas_call(
        _hy_filter_kernel,
        grid=(2, BRANCH // rt),
        in_specs=[pl.BlockSpec((None, rt, HY_HID), lambda d, i: (l, d * (BRANCH // rt) + i, 0)),
                  pl.BlockSpec((HY_HID, SEQ), lambda d, i: (0, d)),
                  pl.BlockSpec((None, None, rt, 1), lambda d, i: (l, d, i, 0)),
                  pl.BlockSpec((1, SEQ), lambda d, i: (0, d))],
        out_specs=pl.BlockSpec((None, rt, SEQ), lambda d, i: (d, i, 0)),
        out_shape=jax.ShapeDtypeStruct((2, BRANCH, SEQ), F32),
        compiler_params=_cparams(("parallel", "parallel")),
        name="hyena_filter",
    )(jnp.swapaxes(w3, 1, 2), h2, decay[:, :, :, None], t_row)
    filt = filt.reshape(2, BRANCH, n1, r)

    scratch = [pltpu.VMEM((ct * r, r), BF16), pltpu.VMEM((ct * r, 2 * r), F32), pltpu.VMEM((ct * r, 2 * r), BF16)]
    spec = pl.pallas_call(
        functools.partial(_hy_spec_kernel, ct=ct),
        grid=(BRANCH // ct,),
        in_specs=[pl.BlockSpec((None, ct, n1, r), lambda i: (0, i, 0, 0)),
                  pl.BlockSpec((None, ct, n1, r), lambda i: (1, i, 0, 0)),
                  full(f1), full(m3), full(twr), full(twi)],
        out_specs=pl.BlockSpec((ct * r, 2 * r), lambda i: (i, 0)),
        out_shape=jax.ShapeDtypeStruct((BRANCH * r, 2 * r), F32),
        scratch_shapes=scratch,
        compiler_params=_cparams(("parallel",)),
        name="hyena_filter_spectrum",
    )(filt, filt, f1, m3, twr, twi)

    hp3 = hp_t.reshape(3 * BRANCH, n1, r)
    nb = BRANCH // ct
    cw = conv_w.reshape(DEPTH, 3, 3, BRANCH).transpose(0, 2, 1, 3).reshape(DEPTH, 9, BRANCH, 1, 1)
    cb = conv_b.reshape(DEPTH, 3, BRANCH, 1, 1)
    consts = (f1, m3, m4, m6, twr, twi)
    return pl.pallas_call(
        functools.partial(_hy_conv_kernel, ct=ct),
        grid=(nb,),
        in_specs=[pl.BlockSpec((ct, n1, r), lambda i: (i, 0, 0)),
                  pl.BlockSpec((ct, n1, r), lambda i: (nb + i, 0, 0)),
                  pl.BlockSpec((ct, n1, r), lambda i: (2 * nb + i, 0, 0)),
                  pl.BlockSpec((None, 9, ct, 1, 1), lambda i: (l, 0, i, 0, 0)),
                  pl.BlockSpec((None, 3, ct, 1, 1), lambda i: (l, 0, i, 0, 0)),
                  pl.BlockSpec((None, ct, 1, 1), lambda i: (l, i, 0, 0)),
                  pl.BlockSpec((ct * r, 2 * r), lambda i: (i, 0)),
                  *[full(a) for a in consts]],
        out_specs=pl.BlockSpec((ct, n1, r), lambda i: (i, 0, 0)),
        out_shape=jax.ShapeDtypeStruct((BRANCH, n1, r), F32),
        scratch_shapes=[pltpu.VMEM((ct, n1, r), F32)] + scratch,
        compiler_params=_cparams(("parallel",)),
        name="hyena_conv",
    )(hp3, hp3, hp3, cw, cb, skip.reshape(DEPTH, BRANCH, 1, 1), spec, *consts).reshape(BRANCH, SEQ)


def _s5_chunk_matrices(lam_re, lam_im, log_step, b_re, b_im, c_re, c_im):
    t = S5_T
    lam = lax.complex(lam_re, lam_im)
    step = jnp.exp(log_step)[..., None]
    lam_dt = lam * step
    b_bar = ((jnp.exp(lam_dt) - 1.0) / lam)[..., None] * lax.complex(b_re, b_im)
    c_mat = lax.complex(c_re, c_im)
    pw = jnp.exp(lam_dt[None] * jnp.arange(t + 1, dtype=F32)[:, None, None, None])
    kern = jnp.einsum("dgvp,mdgp,dgpn->mdgvn", c_mat, pw[:t], b_bar, precision="highest").real
    lag = np.arange(t)[:, None] - np.arange(t)[None, :]
    pick = np.stack([lag[:, :, None] == np.arange(t), -lag[:, :, None] == np.arange(t)], axis=-1)
    kfb = jnp.einsum("ajmd,mdgvn->ajgvn", jnp.asarray(pick, F32), kern, precision="highest")
    w_intra = jnp.transpose(kfb, (2, 1, 4, 0, 3)).reshape(S5_GROUPS, t * S5_GROUP, t * S5_GROUP)
    in_f = pw[:t, 0][::-1][..., None] * b_bar[0][None]
    in_b = pw[:t, 1][..., None] * b_bar[1][None]

    def in_mat(m):
        m = jnp.transpose(m, (1, 0, 3, 2)).reshape(S5_GROUPS, t * S5_GROUP, S5_STATE)
        return jnp.concatenate([m.real, m.imag], axis=-1)

    w_in = jnp.concatenate([in_mat(in_f), in_mat(in_b)], axis=-1)
    out_f = c_mat[0][None] * pw[1:, 0][:, :, None, :]
    out_b = c_mat[1][None] * pw[1:, 1][::-1][:, :, None, :]

    def out_mat(m):
        m = jnp.transpose(m, (1, 3, 0, 2)).reshape(S5_GROUPS, S5_STATE, t * S5_GROUP)
        return jnp.concatenate([m.real, -m.imag], axis=1)

    w_out = jnp.concatenate([out_mat(out_f), out_mat(out_b)], axis=1)
    a_pow = jnp.exp(lam_dt[None] * (t * 2.0 ** jnp.arange(S5_LOG_CHUNKS, dtype=F32))[:, None, None, None])
    a_pow = jnp.transpose(a_pow, (2, 1, 0, 3))
    a_re = jnp.concatenate([a_pow.real, a_pow.real], axis=-1)
    a_im = jnp.concatenate([-a_pow.imag, a_pow.imag], axis=-1)
    return w_intra, w_in, w_out, a_re, a_im


S5_TILE_GROUPS = 128 // S5_GROUP


def _s5_group(ub, wi, win, wout, a_re, a_im):
    nc = S5_CHUNKS
    half = S5_STATE
    y = jnp.dot(ub, wi.astype(BF16), preferred_element_type=F32)
    s_in = jnp.dot(ub, win.astype(BF16), preferred_element_type=F32)
    row = lax.broadcasted_iota(jnp.int32, (nc, 2 * half), 0)

    def scan(s, d):
        for step in range(S5_LOG_CHUNKS):
            sh = 1 << step
            if d == 0:
                moved = jnp.where(row >= sh, pltpu.roll(s, sh, 0), 0.0)
            else:
                moved = jnp.where(row < nc - sh, pltpu.roll(s, nc - sh, 0), 0.0)
            swapped = pltpu.roll(moved, half, 1)
            s = s + a_re[d, step:step + 1, :] * moved + a_im[d, step:step + 1, :] * swapped
        if d == 0:
            return jnp.where(row >= 1, pltpu.roll(s, 1, 0), 0.0)
        return jnp.where(row < nc - 1, pltpu.roll(s, nc - 1, 0), 0.0)

    e = jnp.concatenate([scan(s_in[:, :2 * half], 0), scan(s_in[:, 2 * half:], 1)], axis=1)
    return y + jnp.dot(e.astype(BF16), wout.astype(BF16), preferred_element_type=F32)


@functools.lru_cache(maxsize=None)
def _s5_lane_permutation():
    t, tg, gs = S5_T, S5_TILE_GROUPS, S5_GROUP
    src = np.arange(t * tg * gs).reshape(t, tg, gs)
    dst = src.transpose(1, 0, 2).reshape(-1)
    perm = np.zeros((t * tg * gs, t * tg * gs), np.float32)
    perm[dst, np.arange(t * tg * gs)] = 1.0
    return jnp.asarray(perm, BF16)


def _s5_kernel(u_ref, perm_ref, wi_ref, win_ref, wout_ref, are_ref, aim_ref, skip_ref, y_ref, x_sc, y_sc):
    t, nc = S5_T, S5_CHUNKS
    lane = S5_TILE_GROUPS * S5_GROUP
    gw = t * S5_GROUP
    for tau in range(t):
        x_sc[:, tau * lane:(tau + 1) * lane] = u_ref[pl.ds(tau, nc, stride=t), :].astype(BF16)
    u_all = jnp.dot(x_sc[...], perm_ref[...], preferred_element_type=F32).astype(BF16)
    for g in range(S5_TILE_GROUPS):
        y_g = _s5_group(u_all[:, g * gw:(g + 1) * gw], wi_ref[g], win_ref[g], wout_ref[g], are_ref[g], aim_ref[g])
        y_sc[:, g * gw:(g + 1) * gw] = y_g.astype(BF16)
    y_nat = lax.dot_general(y_sc[...], perm_ref[...], (((1,), (1,)), ((), ())), preferred_element_type=F32)
    for tau in range(t):
        rows = pl.ds(tau, nc, stride=t)
        y_ref[rows, :] = y_nat[:, tau * lane:(tau + 1) * lane] + skip_ref[...] * u_ref[rows, :]


def _s5(u, mats, skip, l):
    w_intra, w_in, w_out, a_re, a_im = mats
    t = S5_T
    gw = t * S5_GROUP
    tg = S5_TILE_GROUPS
    lane = tg * S5_GROUP
    perm = _s5_lane_permutation()
    return pl.pallas_call(
        _s5_kernel,
        grid=(BRANCH // lane,),
        in_specs=[pl.BlockSpec((SEQ, lane), lambda j: (0, j)),
                  pl.BlockSpec(perm.shape, lambda j: (0, 0)),
                  pl.BlockSpec((None, tg, gw, gw), lambda j: (l, j, 0, 0)),
                  pl.BlockSpec((None, tg, gw, 4 * S5_STATE), lambda j: (l, j, 0, 0)),
                  pl.BlockSpec((None, tg, 4 * S5_STATE, gw), lambda j: (l, j, 0, 0)),
                  pl.BlockSpec((None, tg, 2, S5_LOG_CHUNKS, 2 * S5_STATE), lambda j: (l, j, 0, 0, 0)),
                  pl.BlockSpec((None, tg, 2, S5_LOG_CHUNKS, 2 * S5_STATE), lambda j: (l, j, 0, 0, 0)),
                  pl.BlockSpec((None, 1, lane), lambda j: (l, 0, j))],
        out_specs=pl.BlockSpec((SEQ, lane), lambda j: (0, j)),
        out_shape=jax.ShapeDtypeStruct((SEQ, BRANCH), F32),
        scratch_shapes=[pltpu.VMEM((S5_CHUNKS, t * lane), BF16), pltpu.VMEM((S5_CHUNKS, t * lane), BF16)],
        compiler_params=_cparams(("parallel",)),
        name="s5_scan",
    )(u, perm, w_intra, w_in, w_out, a_re, a_im, skip.reshape(DEPTH, 1, BRANCH))


def _merge_kernel(hm_ref, hyt_ref, s5_ref, x_ref, wa_ref, wb_ref, wga_ref, wgg_ref, wg0_ref, wg1_ref, wg2_ref,
                  out_ref, hy_sc):
    @pl.when(pl.program_id(1) == 0)
    def _():
        hy_sc[...] = hyt_ref[...].T.astype(BF16)

    def proj(a, w_ref):
        return jnp.dot(a, w_ref[...].astype(BF16), preferred_element_type=F32)

    def gate(w_ref):
        pre = lax.dot_general(x_ref[...], w_ref[...], (((1,), (1,)), ((), ())), preferred_element_type=F32)
        return jax.nn.sigmoid(pre)

    s5b = s5_ref[...].astype(BF16)
    out_a = proj(hm_ref[...], wa_ref)
    out_b = proj(hy_sc[...], wb_ref)
    out_c = proj(s5b, wga_ref) * jax.nn.sigmoid(proj(s5b, wgg_ref))
    merged = gate(wg0_ref) * out_a + gate(wg1_ref) * out_b + gate(wg2_ref) * out_c
    out_ref[...] = merged.astype(out_ref.dtype)


def _merge(hm, hy_t, s5y, x_bf, w_tail, w_a, w_b, w_glu, l, *, tm=512, tn=512):
    nj = D_MODEL // tn
    goff = (OFF_MIXG - OFF_HYENA) // tn

    def gate_spec(g):
        return pl.BlockSpec((None, tn, D_MODEL), lambda i, j: (l, goff + g * nj + j, 0))

    return pl.pallas_call(
        _merge_kernel,
        grid=(SEQ // tm, nj),
        in_specs=[pl.BlockSpec((tm, BRANCH), lambda i, j: (i, 0)),
                  pl.BlockSpec((BRANCH, tm), lambda i, j: (0, i)),
                  pl.BlockSpec((tm, BRANCH), lambda i, j: (i, 0)),
                  pl.BlockSpec((tm, D_MODEL), lambda i, j: (i, 0)),
                  pl.BlockSpec((None, BRANCH, tn), lambda i, j: (l, 0, j)),
                  pl.BlockSpec((None, BRANCH, tn), lambda i, j: (l, 0, j)),
                  pl.BlockSpec((None, BRANCH, tn), lambda i, j: (l, 0, j)),
                  pl.BlockSpec((None, BRANCH, tn), lambda i, j: (l, 0, nj + j)),
                  gate_spec(0), gate_spec(1), gate_spec(2)],
        out_specs=pl.BlockSpec((tm, tn), lambda i, j: (i, j)),
        out_shape=jax.ShapeDtypeStruct((SEQ, D_MODEL), BF16),
        scratch_shapes=[pltpu.VMEM((tm, BRANCH), BF16)],
        compiler_params=_cparams(("parallel", "arbitrary")),
        name="branch_merge",
    )(hm, hy_t, s5y, x_bf, w_a, w_b, w_glu, w_glu, w_tail, w_tail, w_tail)


def _layer_norm_rows(y, g, b):
    mu = jnp.mean(y, axis=1, keepdims=True)
    var = jnp.mean(jnp.square(y - mu), axis=1, keepdims=True)
    return (y - mu) * lax.rsqrt(var + LN_EPS) * g + b


def _res_ln_kernel(x_ref, f_ref, g_ref, b_ref, o_ref, ob_ref, *, transposed):
    f = f_ref[...]
    if transposed:
        f = f.T
    y = _layer_norm_rows(ALPHA * x_ref[...] + f, g_ref[...], b_ref[...])
    o_ref[...] = y
    ob_ref[...] = y.astype(BF16)


def _res_ln(x, f, g, b, l, *, transposed, tm=512):
    f_spec = (pl.BlockSpec((D_MODEL, tm), lambda i: (0, i)) if transposed
              else pl.BlockSpec((tm, D_MODEL), lambda i: (i, 0)))
    n_tok = x.shape[0]
    return pl.pallas_call(
        functools.partial(_res_ln_kernel, transposed=transposed),
        grid=(n_tok // tm,),
        in_specs=[pl.BlockSpec((tm, D_MODEL), lambda i: (i, 0)), f_spec,
                  pl.BlockSpec((None, 1, D_MODEL), lambda i: (l, 0, 0)),
                  pl.BlockSpec((None, 1, D_MODEL), lambda i: (l, 0, 0))],
        out_specs=[pl.BlockSpec((tm, D_MODEL), lambda i: (i, 0)),
                   pl.BlockSpec((tm, D_MODEL), lambda i: (i, 0))],
        out_shape=[jax.ShapeDtypeStruct((n_tok, D_MODEL), F32), jax.ShapeDtypeStruct((n_tok, D_MODEL), BF16)],
        compiler_params=_cparams(("parallel",)),
        name="residual_layernorm",
    )(x, f, g.reshape(DEPTH, 1, D_MODEL), b.reshape(DEPTH, 1, D_MODEL))


PEER_NO_RANK = 64.0


def _extract_sorted(s, n, out_ref, base, want_rank=False):
    rank = jnp.full_like(s, PEER_NO_RANK) if want_rank else None
    for r in range(n):
        m = jnp.max(s, axis=0, keepdims=True)
        out_ref[base + r:base + r + 1, :] = m
        hit = s == m
        if want_rank:
            rank = jnp.where(hit, float(r), rank)
        s = jnp.where(hit, -jnp.inf, s)
    return rank


def _peer_select_kernel(q_ref, keys_ref, cnt_ref, r2_ref, e1_ref, e2_ref, top_sc, kth_sc):
    k = PEER_TOPK
    qb = q_ref[...].astype(BF16)
    half = PEER_KEYS
    for h in range(PEER_HEADS):
        sc = []
        rank2 = None
        for c in range(2):
            qs = qb[:, (2 * h + c) * half:(2 * h + c + 1) * half]
            s = lax.dot_general(keys_ref[h, c].astype(BF16), qs, (((1,), (1,)), ((), ())),
                                preferred_element_type=F32)
            sc.append(s)
            rank2 = _extract_sorted(s, k, top_sc, c * k, want_rank=(c == 1))
        a = top_sc[0:k, :]
        b = top_sc[k:2 * k, :]
        sub = lax.broadcasted_iota(jnp.int32, (8, a.shape[1]), 0)
        groups = [a[0:1] + b[0:8], a[0:1] + b[8:16], a[1:2] + b[0:8]]
        for i in range(2, 8):
            groups.append(jnp.where(sub < k // (i + 1), a[i:i + 1] + b[0:8], -jnp.inf))
        groups.append(a[8:16] + b[0:1])
        cand = jnp.concatenate(groups, axis=0)
        _extract_sorted(cand, k, kth_sc, 0)
        tau = kth_sc[k - 1:k, :]
        top = a[0:1] + b[0:1]
        z = jnp.sum(jnp.where(cand >= tau, jnp.exp(cand - top), 0.0), axis=0, keepdims=True)
        cnt = jnp.zeros_like(sc[0])
        for j in range(k):
            bj = top_sc[k + j:k + j + 1, :]
            cnt = jnp.where(sc[0] + bj >= tau, j + 1.0, cnt)
        cnt_ref[h] = cnt
        r2_ref[h] = rank2.astype(BF16)
        e1_ref[h] = jnp.exp(sc[0] - a[0:1])
        e2_ref[h] = (jnp.exp(sc[1] - b[0:1]) / z).astype(BF16)


def _peer_select(q, keys, l, *, tt=128):
    n_tok = q.shape[0]
    big = jax.ShapeDtypeStruct((PEER_HEADS, PEER_KEYS, n_tok), F32)
    big_bf = jax.ShapeDtypeStruct((PEER_HEADS, PEER_KEYS, n_tok), BF16)
    big_spec = pl.BlockSpec((PEER_HEADS, PEER_KEYS, tt), lambda i: (0, 0, i))
    return pl.pallas_call(
        _peer_select_kernel,
        grid=(n_tok // tt,),
        in_specs=[pl.BlockSpec((tt, D_MODEL), lambda i: (i, 0)),
                  pl.BlockSpec((None, PEER_HEADS, 2, PEER_KEYS, PEER_KEYS), lambda i: (l, 0, 0, 0, 0))],
        out_specs=[big_spec, big_spec, big_spec, big_spec],
        out_shape=[big, big_bf, big, big_bf],
        scratch_shapes=[pltpu.VMEM((2 * PEER_TOPK, tt), F32), pltpu.VMEM((PEER_TOPK, tt), F32)],
        compiler_params=_cparams(("parallel",)),
        name="peer_select",
    )(q, keys)


def _peer_dense_kernel(x_ref, u_ref, vt_ref, cnt_ref, r2_ref, e1_ref, e2_ref, out_ref, act_sc, g_sc, *, rows):
    e = pl.program_id(1)

    @pl.when(e == 0)
    def _():
        out_ref[...] = jnp.zeros_like(out_ref)

    act_sc[...] = lax.dot_general(u_ref[...], x_ref[...], (((1,), (1,)), ((), ())),
                                  preferred_element_type=F32)
    kk = PEER_KEYS
    for tc in range(act_sc.shape[1] // kk):
        lanes = slice(tc * kk, (tc + 1) * kk)
        for i in range(rows):
            w = jnp.zeros((kk, kk), BF16)
            for h in range(PEER_HEADS):
                hit = r2_ref[h, :, lanes] < cnt_ref[h, i:i + 1, lanes].astype(BF16)
                val = e2_ref[h, :, lanes] * e1_ref[h, i:i + 1, lanes].astype(BF16)
                w = w + jnp.where(hit, val, jnp.zeros_like(val))
            g_sc[i * kk:(i + 1) * kk, lanes] = jax.nn.gelu(act_sc[i * kk:(i + 1) * kk, lanes].astype(BF16)) * w
    out_ref[...] += jnp.dot(vt_ref[...], g_sc[...], preferred_element_type=F32)


def _transpose_tile_kernel(v_ref, o_ref):
    o_ref[...] = v_ref[...].T.astype(o_ref.dtype)


def _peer_value_tiles(v, l, te):
    return pl.pallas_call(
        _transpose_tile_kernel,
        grid=(N_EXPERTS // te,),
        in_specs=[pl.BlockSpec((None, te, D_MODEL), lambda e: (l, e, 0))],
        out_specs=pl.BlockSpec((None, D_MODEL, te), lambda e: (e, 0, 0)),
        out_shape=jax.ShapeDtypeStruct((N_EXPERTS // te, D_MODEL, te), BF16),
        compiler_params=_cparams(("parallel",)),
        name="peer_value_tiles",
    )(v)


def _peer_dense(x_bf, u_bf, v, l, sel, *, tt=512, rows=8):
    cnt, r2, e1, e2 = sel
    te = rows * PEER_KEYS
    n_tok = x_bf.shape[0]
    vt_bf = _peer_value_tiles(v, l, te)
    row_spec = pl.BlockSpec((PEER_HEADS, rows, tt), lambda i, e: (0, e, i))
    all_spec = pl.BlockSpec((PEER_HEADS, PEER_KEYS, tt), lambda i, e: (0, 0, i))
    return pl.pallas_call(
        functools.partial(_peer_dense_kernel, rows=rows),
        grid=(n_tok // tt, N_EXPERTS // te),
        in_specs=[pl.BlockSpec((tt, D_MODEL), lambda i, e: (i, 0)),
                  pl.BlockSpec((None, te, D_MODEL), lambda i, e: (l, e, 0)),
                  pl.BlockSpec((None, D_MODEL, te), lambda i, e: (e, 0, 0)),
                  row_spec, all_spec, row_spec, all_spec],
        out_specs=pl.BlockSpec((D_MODEL, tt), lambda i, e: (0, i)),
        out_shape=jax.ShapeDtypeStruct((D_MODEL, n_tok), F32),
        scratch_shapes=[pltpu.VMEM((te, tt), F32), pltpu.VMEM((te, tt), BF16)],
        compiler_params=_cparams(("parallel", "arbitrary")),
        name="peer_dense",
    )(x_bf, u_bf, vt_bf, cnt, r2, e1, e2)


def _hyena_features():
    pos = jnp.arange(SEQ, dtype=F32)
    t = pos / (SEQ - 1)
    bands = jnp.linspace(1e-4, 16 - 1, 16, dtype=F32)
    ang = 2.0 * math.pi * pos[:, None] * bands[None, :] / SEQ
    feat = jnp.concatenate([t[:, None], jnp.cos(ang), -jnp.sin(ang)], axis=-1)
    feat_t = jnp.pad(feat.T, ((0, HY_EMB_PAD - feat.shape[1]), (0, 0)))
    t_row = t[None, :]

    def reversed_time(a):
        return jnp.roll(a[:, ::-1], 1, axis=1)

    return (jnp.concatenate([feat_t, reversed_time(feat_t)], axis=1),
            jnp.concatenate([t_row, reversed_time(t_row)], axis=1))


def kernel(x, w_in, mlstm_gate_bias, mlstm_norm_gain, w_mlstm_out, hyena_conv_w, hyena_conv_b, hyena_w1,
           hyena_b1, hyena_w2, hyena_b2, hyena_freq, hyena_w3, hyena_decay, hyena_skip, w_hyena_out,
           s5_lambda_re, s5_lambda_im, s5_log_step, s5_b_re, s5_b_im, s5_c_re, s5_c_im, s5_skip, w_s5_glu,
           w_out, ln1_g, ln1_b, peer_w_q, peer_subkeys, peer_u, peer_v, ln2_g, ln2_b):
    xf = x.reshape(SEQ, D_MODEL)
    xb = xf.astype(BF16)
    feat_t, t_row = _hyena_features()
    w_in_t = jnp.swapaxes(w_in, 1, 2)
    w_tail = _w_in_tail(w_in_t)
    u_bf = peer_u.astype(BF16)
    s5_mats = jax.vmap(_s5_chunk_matrices)(s5_lambda_re, s5_lambda_im, s5_log_step, s5_b_re, s5_b_im,
                                           s5_c_re, s5_c_im)
    for l in range(DEPTH):
        qkvo = _mm_wt(xb, w_in_t, l, 0, 4 * BRANCH, tm=2048, tn=512, name="proj_qkvo")
        gates_t = _mm_nt(w_in_t, l, OFF_GATES, N_GATES, xb, tr=N_GATES, tm=1024, name="proj_gates", out_dtype=F32)
        hp_t = _mm_nt(w_tail, l, 0, 3 * BRANCH, xb, tr=512, tm=2048, name="proj_hyena_t", out_dtype=BF16)
        s5_in = _mm_wt(xb, w_tail, l, 3 * BRANCH, BRANCH, tm=2048, tn=512, name="proj_s5")

        hdir = _mlstm(qkvo, gates_t, mlstm_gate_bias, l)
        hm = _mlstm_post(hdir, qkvo, mlstm_norm_gain, l)
        hy_t = _hyena(hp_t, l, hyena_conv_w, hyena_conv_b, hyena_w1, hyena_b1, hyena_w2, hyena_b2,
                      hyena_freq, hyena_w3, hyena_decay, hyena_skip, feat_t, t_row)
        s5y = _s5(s5_in, s5_mats, s5_skip, l)
        merged = _merge(hm, hy_t, s5y, xb, w_tail, w_mlstm_out, w_hyena_out, w_s5_glu, l)
        mix = _mm(merged, w_out, l, 0, D_MODEL, tm=2048, tn=512, name="proj_out")
        xf, xb = _res_ln(xf, mix, ln1_g, ln1_b, l, transposed=False)

        q = _mm(xb, peer_w_q, l, 0, PEER_HEADS * 2 * PEER_KEYS, tm=2048, tn=512, name="peer_query")
        sel = _peer_select(q, peer_subkeys, l)
        ffn_t = _peer_dense(xb, u_bf, peer_v, l, sel)
        xf, xb = _res_ln(xf, ffn_t, ln2_g, ln2_b, l, transposed=True)
    return xf.reshape(1, SEQ, D_MODEL)
```
